```python
import math
import jax, jax.numpy as jnp
from jax import lax
import numpy as np

D_MODEL = 1024
BATCH = 8
SEQ = 2048
DEPTH = 2
DEC_BATCH = 128
DEC_SEQ = 1
PAST_LEN = 16384
PAGE_SIZE = 128

N_AB_LAYERS = (DEPTH + 1) // 2
N_CONV_LAYERS = DEPTH // 2

DN_HEADS = D_MODEL // 256
DN_DK = 128
DN_DV = 128
DN_KEY = DN_HEADS * DN_DK
DN_VAL = DN_HEADS * DN_DV
DN_QKV = 2 * DN_KEY + DN_VAL
DN_CONV = 4
DN_CHUNK = 64

SSM_CH = D_MODEL // 2
SSM_GROUP_CH = 16
SSM_GROUPS = SSM_CH // SSM_GROUP_CH
SSM_STATE = 64
SSM_DT_MIN = 1e-3
SSM_DT_MAX = 1e-1

OFF_Z = DN_QKV
OFF_BETA = OFF_Z + DN_VAL
OFF_A = OFF_BETA + DN_HEADS
OFF_U = OFF_A + DN_HEADS
AB_IN = OFF_U + SSM_CH
AB_MIX = DN_VAL + SSM_CH

CONV_WIDTH = 31

MEM_TOKENS = 256
MEM_HEADS = 4
MEM_HEAD_DIM = D_MODEL // MEM_HEADS

MOE_GROUPS = 4
MOE_EXPERTS_PER_GROUP = 8
MOE_EXPERTS = MOE_GROUPS * MOE_EXPERTS_PER_GROUP
MOE_TOP_K = 2
MOE_D_FF = D_MODEL // 2

DEEPNORM_ALPHA = (2.0 * DEPTH) ** 0.25
DEEPNORM_BETA = (8.0 * DEPTH) ** -0.25
LN_EPS = 1e-5
RMS_EPS = 1e-6

kernel_name = 'hybrid_deltanet_s5_conformer_hmoe_step'

F32 = jnp.float32


def _layer_norm(x, g, b):
    xf = x.astype(F32)
    mu = jnp.mean(xf, -1, keepdims=True)
    var = jnp.mean(jnp.square(xf - mu), -1, keepdims=True)
    return ((xf - mu) * lax.rsqrt(var + LN_EPS) * g.astype(F32) + b.astype(F32)).astype(x.dtype)


def _post_norm(x, h, g, b):
    return _layer_norm(DEEPNORM_ALPHA * x + h.astype(x.dtype), g, b)


def _l2norm(x):
    return x * lax.rsqrt(jnp.sum(x * x, -1, keepdims=True) + RMS_EPS)


def _causal_dwconv(x, buf, w):
    xp = jnp.concatenate([buf.astype(x.dtype), x], axis=1)
    y = lax.conv_general_dilated(xp, w[:, None, :].astype(x.dtype), window_strides=(1,), padding='VALID',
                                 dimension_numbers=('NWC', 'WIO', 'NWC'), feature_group_count=x.shape[-1])
    return y, xp[:, -(w.shape[0] - 1):]


def _gated_delta_rule(q, k, v, g, beta, s0):
    bsz, t, h, dk = q.shape
    dv = v.shape[-1]
    c = min(DN_CHUNK, t)
    n = -(-t // c)
    pad = n * c - t

    def chunks(a):
        a = a.astype(F32)
        a = jnp.pad(a, [(0, 0), (0, pad)] + [(0, 0)] * (a.ndim - 2))
        a = a.reshape((bsz, n, c) + a.shape[2:])
        return jnp.moveaxis(a, 3, 2).swapaxes(0, 1)

    qc, kc, vc, gc, bc = chunks(q), chunks(k), chunks(v), chunks(g), chunks(beta)
    gc = jnp.cumsum(gc, axis=-1)
    incl = jnp.tril(jnp.ones((c, c), bool))
    strict = jnp.tril(jnp.ones((c, c), bool), -1)
    decay = jnp.exp(jnp.where(incl, gc[..., :, None] - gc[..., None, :], -jnp.inf))
    kb = kc * bc[..., None]
    lmat = jnp.where(strict, jnp.einsum('...id,...jd->...ij', kb, kc) * decay, 0.0)
    eye = jnp.broadcast_to(jnp.eye(c, dtype=F32), lmat.shape)
    tmat = lax.linalg.triangular_solve(lmat, eye, left_side=True, lower=True, unit_diagonal=True)
    u = tmat @ (vc * bc[..., None])
    w = tmat @ (kb * jnp.exp(gc)[..., None])
    qk = jnp.einsum('...id,...jd->...ij', qc, kc) * decay
    qg = qc * jnp.exp(gc)[..., None]
    kdec = kc * jnp.exp(gc[..., -1:] - gc)[..., None]
    glast = jnp.exp(gc[..., -1])

    def step(s, inp):
        u_i, w_i, qk_i, qg_i, kdec_i, gl_i = inp
        v_new = u_i - w_i @ s
        o = qg_i @ s + qk_i @ v_new
        s = s * gl_i[..., None, None] + jnp.einsum('bhcd,bhce->bhde', kdec_i, v_new)
        return s, o

    s, o = lax.scan(step, s0.astype(F32), (u, w, qk, qg, kdec, glast))
    o = jnp.moveaxis(o.swapaxes(0, 1), 2, 3).reshape(bsz, n * c, h, dv)[:, :t]
    return o, s


def _complex_affine_combine(e1, e2):
    a1r, a1i, b1r, b1i = e1
    a2r, a2i, b2r, b2i = e2
    return (a1r * a2r - a1i * a2i, a1r * a2i + a1i * a2r,
            a2r * b1r - a2i * b1i + b2r, a2r * b1i + a2i * b1r + b2i)


def _s5(u, h0_re, h0_im, lam_re, lam_im, log_dt, b_re, b_im, c_re, c_im, d_skip, w_glu, b_glu):
    bsz, t, _ = u.shape
    uf = u.astype(F32).reshape(bsz, t, SSM_GROUPS, SSM_GROUP_CH)
    lam_re = lam_re.astype(F32)
    lam_im = lam_im.astype(F32)
    dt = jnp.exp(log_dt.astype(F32))[:, None]
    mag = jnp.exp(lam_re * dt)
    ang = lam_im * dt
    lb_re, lb_im = mag * jnp.cos(ang), mag * jnp.sin(ang)
    den = lam_re * lam_re + lam_im * lam_im
    f_re = ((lb_re - 1.0) * lam_re + lb_im * lam_im) / den
    f_im = (lb_im * lam_re - (lb_re - 1.0) * lam_im) / den
    b_re = b_re.astype(F32)
    b_im = b_im.astype(F32)
    bb_re = f_re[..., None] * b_re - f_im[..., None] * b_im
    bb_im = f_re[..., None] * b_im + f_im[..., None] * b_re
    bu_re = jnp.einsum('btgp,gnp->btgn', uf, bb_re)
    bu_im = jnp.einsum('btgp,gnp->btgn', uf, bb_im)
    h0_re = h0_re.astype(F32)
    h0_im = h0_im.astype(F32)
    bu_re = bu_re.at[:, 0].add(lb_re * h0_re - lb_im * h0_im)
    bu_im = bu_im.at[:, 0].add(lb_re * h0_im + lb_im * h0_re)
    a_re = jnp.broadcast_to(lb_re, bu_re.shape)
    a_im = jnp.broadcast_to(lb_im, bu_im.shape)
    _, _, h_re, h_im = lax.associative_scan(_complex_affine_combine, (a_re, a_im, bu_re, bu_im), axis=1)
    y = (jnp.einsum('btgn,gpn->btgp', h_re, c_re.astype(F32))
         - jnp.einsum('btgn,gpn->btgp', h_im, c_im.astype(F32))
         + d_skip.astype(F32) * uf)
    y = jax.nn.gelu(y.reshape(bsz, t, SSM_CH))
    y = y * jax.nn.sigmoid(y @ w_glu.astype(F32) + b_glu.astype(F32))
    return y, h_re[:, -1], h_im[:, -1]


def _ab_mixer(x, s0, conv_buf, h0_re, h0_im, p, i):
    bsz, t, _ = x.shape
    proj = x @ p['ab_w_in'][i]
    qkv = proj[..., :DN_QKV]
    z = proj[..., OFF_Z:OFF_BETA]
    b_logit = proj[..., OFF_BETA:OFF_A]
    a_logit = proj[..., OFF_A:OFF_U]
    u = proj[..., OFF_U:]
    qkv, new_buf = _causal_dwconv(qkv, conv_buf, p['dn_conv_w'][i])
    qkv = jax.nn.silu(qkv.astype(F32))
    q = _l2norm(qkv[..., :DN_KEY].reshape(bsz, t, DN_HEADS, DN_DK)) * DN_DK ** -0.5
    k = _l2norm(qkv[..., DN_KEY:2 * DN_KEY].reshape(bsz, t, DN_HEADS, DN_DK))
    v = qkv[..., 2 * DN_KEY:].reshape(bsz, t, DN_HEADS, DN_DV)
    beta = jax.nn.sigmoid(b_logit.astype(F32))
    g = -jnp.exp(p['dn_a_log'][i].astype(F32)) * jax.nn.softplus(a_logit.astype(F32) + p['dn_dt_bias'][i].astype(F32))
    o, s_new = _gated_delta_rule(q, k, v, g, beta, s0)
    o = o * lax.rsqrt(jnp.mean(o * o, -1, keepdims=True) + RMS_EPS) * p['dn_norm_g'][i].astype(F32)
    o = o.reshape(bsz, t, DN_VAL) * jax.nn.silu(z.astype(F32))
    y_ssm, h_re, h_im = _s5(u, h0_re, h0_im, p['ssm_lambda_re'][i], p['ssm_lambda_im'][i], p['ssm_log_dt'][i],
                            p['ssm_b_re'][i], p['ssm_b_im'][i], p['ssm_c_re'][i], p['ssm_c_im'][i],
                            p['ssm_d'][i], p['ssm_w_glu'][i], p['ssm_b_glu'][i])
    mixed = jnp.concatenate([o.astype(x.dtype), y_ssm.astype(x.dtype)], axis=-1)
    return mixed @ p['ab_w_out'][i], s_new, new_buf, h_re, h_im


def _conv_module(x, buf, w_pw1, b_pw1, w_dw, b_dw, ln_g, ln_b, w_pw2, b_pw2):
    h = x @ w_pw1 + b_pw1
    h = h[..., :D_MODEL] * jax.nn.sigmoid(h[..., D_MODEL:])
    h, new_buf = _causal_dwconv(h, buf, w_dw)
    h = jax.nn.silu(_layer_norm(h + b_dw, ln_g, ln_b))
    return h @ w_pw2 + b_pw2, new_buf


def _mem_attn(x, mk, mv, wq, wo):
    bsz, t, _ = x.shape
    q = (x @ wq).reshape(bsz, t, MEM_HEADS, MEM_HEAD_DIM)
    s = jnp.einsum('bthd,bmhd->bhtm', q.astype(F32), mk.astype(F32)) * MEM_HEAD_DIM ** -0.5
    a = jax.nn.softmax(s, axis=-1).astype(x.dtype)
    o = jnp.einsum('bhtm,bmhd->bthd', a, mv.astype(x.dtype))
    return o.reshape(bsz, t, D_MODEL) @ wo


def _hier_moe(x, w_group, b_group, w_expert, b_expert, w_gate, w_up, w_down):
    bsz, t, d = x.shape
    xt = x.reshape(-1, d)
    ntok = xt.shape[0]
    gp = jax.nn.softmax((xt @ w_group + b_group).astype(F32), axis=-1)
    gw, gi = lax.top_k(gp, 1)
    el = (xt @ w_expert + b_expert).astype(F32).reshape(ntok, MOE_GROUPS, MOE_EXPERTS_PER_GROUP)
    el_sel = jnp.take_along_axis(el, gi[:, :, None], axis=1)[:, 0]
    ew, ei = lax.top_k(jax.nn.softmax(el_sel, axis=-1), MOE_TOP_K)
    wts = gw * ew / jnp.sum(ew, -1, keepdims=True)
    gate_in = jnp.sum(jax.nn.one_hot(ei, MOE_EXPERTS_PER_GROUP, dtype=F32) * wts[..., None], axis=1)
    gate = jax.nn.one_hot(gi[:, 0], MOE_GROUPS, dtype=F32)[:, :, None] * gate_in[:, None, :]
    wg = w_gate.reshape(MOE_GROUPS, MOE_EXPERTS_PER_GROUP, d, MOE_D_FF)
    wu = w_up.reshape(MOE_GROUPS, MOE_EXPERTS_PER_GROUP, d, MOE_D_FF)
    wd = w_down.reshape(MOE_GROUPS, MOE_EXPERTS_PER_GROUP, MOE_D_FF, d)
    y = jnp.zeros((ntok, d), x.dtype)
    for grp in range(MOE_GROUPS):
        hid = jax.nn.silu(jnp.einsum('nd,edf->nef', xt, wg[grp])) * jnp.einsum('nd,edf->nef', xt, wu[grp])
        hid = hid * gate[:, grp, :, None].astype(x.dtype)
        y = y + jnp.einsum('nef,efd->nd', hid, wd[grp])
    return y.reshape(bsz, t, d)


def _trunk(x, dn_s, dn_conv, ssm_re, ssm_im, cconv, mem_k, mem_v, p):
    out_s, out_conv, out_re, out_im, out_cc = [], [], [], [], []
    for layer in range(DEPTH):
        i = layer // 2
        if layer % 2 == 0:
            h, s_new, conv_new, re_new, im_new = _ab_mixer(x, dn_s[i], dn_conv[i], ssm_re[i], ssm_im[i], p, i)
            out_s.append(s_new)
            out_conv.append(conv_new)
            out_re.append(re_new)
            out_im.append(im_new)
        else:
            h, cc_new = _conv_module(x, cconv[i], p['cc_w_pw1'][i], p['cc_b_pw1'][i], p['cc_w_dw'][i], p['cc_b_dw'][i],
                                     p['cc_ln_g'][i], p['cc_ln_b'][i], p['cc_w_pw2'][i], p['cc_b_pw2'][i])
            out_cc.append(cc_new)
        x = _post_norm(x, h, p['ln_g'][layer, 0], p['ln_b'][layer, 0])
        h = _mem_attn(x, mem_k[layer], mem_v[layer], p['mem_wq'][layer], p['mem_wo'][layer])
        x = _post_norm(x, h, p['ln_g'][layer, 1], p['ln_b'][layer, 1])
        h = _hier_moe(x, p['moe_w_group'][layer], p['moe_b_group'][layer], p['moe_w_expert'][layer],
                      p['moe_b_expert'][layer], p['moe_w_gate'][layer], p['moe_w_up'][layer], p['moe_w_down'][layer])
        x = _post_norm(x, h, p['ln_g'][layer, 2], p['ln_b'][layer, 2])
    return x, jnp.stack(out_s), jnp.stack(out_conv), jnp.stack(out_re), jnp.stack(out_im), jnp.stack(out_cc)


def setup_inputs(seed: int = 0) -> dict:
    key = jax.random.key(seed)
    ks = iter(jax.random.split(key, 64))

    def nrm(shape, scale=1.0):
        return jax.random.normal(next(ks), shape, F32) * scale

    def uni(shape, lo, hi):
        return jax.random.uniform(next(ks), shape, F32, lo, hi)

    na, nc, d = N_AB_LAYERS, N_CONV_LAYERS, D_MODEL
    bt = DEEPNORM_BETA
    dt0 = jnp.exp(uni((na, DN_HEADS), math.log(1e-3), math.log(1e-1)))
    lam_n = jnp.arange(SSM_STATE, dtype=F32) * math.pi
    inp = {}
    inp['x_prompt'] = nrm((BATCH, SEQ, d))
    inp['x_sample'] = nrm((DEC_BATCH, DEC_SEQ, d))
    inp['state_dn_s'] = nrm((na, DEC_BATCH, DN_HEADS, DN_DK, DN_DV), 0.1)
    inp['state_dn_conv'] = nrm((na, DEC_BATCH, DN_CONV - 1, DN_QKV))
    inp['state_ssm_re'] = nrm((na, DEC_BATCH, SSM_GROUPS, SSM_STATE), 0.1)
    inp['state_ssm_im'] = nrm((na, DEC_BATCH, SSM_GROUPS, SSM_STATE), 0.1)
    inp['state_cconv'] = nrm((nc, DEC_BATCH, CONV_WIDTH - 1, d))
    inp['cache_mem_k'] = nrm((DEPTH, DEC_BATCH, MEM_TOKENS, MEM_HEADS, MEM_HEAD_DIM))
    inp['cache_mem_v'] = nrm((DEPTH, DEC_BATCH, MEM_TOKENS, MEM_HEADS, MEM_HEAD_DIM))
    inp['mem_prompt'] = nrm((BATCH, MEM_TOKENS, d))
    inp['ab_w_in'] = nrm((na, d, AB_IN), d ** -0.5)
    inp['dn_conv_w'] = nrm((na, DN_CONV, DN_QKV), DN_CONV ** -0.5)
    inp['dn_a_log'] = jnp.log(uni((na, DN_HEADS), 1.0, 16.0))
    inp['dn_dt_bias'] = dt0 + jnp.log(-jnp.expm1(-dt0))
    inp['dn_norm_g'] = 1.0 + nrm((na, DN_DV), 0.01)
    inp['ssm_lambda_re'] = -0.5 + nrm((na, SSM_GROUPS, SSM_STATE), 0.01)
    inp['ssm_lambda_im'] = lam_n + nrm((na, SSM_GROUPS, SSM_STATE), 0.01)
    inp['ssm_log_dt'] = uni((na, SSM_GROUPS), math.log(SSM_DT_MIN), math.log(SSM_DT_MAX))
    inp['ssm_b_re'] = nrm((na, SSM_GROUPS, SSM_STATE, SSM_GROUP_CH), (2 * SSM_GROUP_CH) ** -0.5)
    inp['ssm_b_im'] = nrm((na, SSM_GROUPS, SSM_STATE, SSM_GROUP_CH), (2 * SSM_GROUP_CH) ** -0.5)
    inp['ssm_c_re'] = nrm((na, SSM_GROUPS, SSM_GROUP_CH, SSM_STATE), SSM_STATE ** -0.5)
    inp['ssm_c_im'] = nrm((na, SSM_GROUPS, SSM_GROUP_CH, SSM_STATE), SSM_STATE ** -0.5)
    inp['ssm_d'] = nrm((na, SSM_GROUPS, SSM_GROUP_CH))
    inp['ssm_w_glu'] = nrm((na, SSM_CH, SSM_CH), SSM_CH ** -0.5)
    inp['ssm_b_glu'] = nrm((na, SSM_CH), 0.01)
    inp['ab_w_out'] = nrm((na, AB_MIX, d), AB_MIX ** -0.5 * bt)
    inp['cc_w_pw1'] = nrm((nc, d, 2 * d), d ** -0.5)
    inp['cc_b_pw1'] = nrm((nc, 2 * d), 0.01)
    inp['cc_w_dw'] = nrm((nc, CONV_WIDTH, d), CONV_WIDTH ** -0.5)
    inp['cc_b_dw'] = nrm((nc, d), 0.01)
    inp['cc_ln_g'] = 1.0 + nrm((nc, d), 0.01)
    inp['cc_ln_b'] = nrm((nc, d), 0.01)
    inp['cc_w_pw2'] = nrm((nc, d, d), d ** -0.5 * bt)
    inp['cc_b_pw2'] = nrm((nc, d), 0.01)
    inp['mem_wq'] = nrm((DEPTH, d, d), d ** -0.5)
    inp['mem_wk'] = nrm((DEPTH, d, d), d ** -0.5)
    inp['mem_wv'] = nrm((DEPTH, d, d), d ** -0.5 * bt)
    inp['mem_wo'] = nrm((DEPTH, d, d), d ** -0.5 * bt)
    inp['ln_g'] = 1.0 + nrm((DEPTH, 3, d), 0.01)
    inp['ln_b'] = nrm((DEPTH, 3, d), 0.01)
    inp['moe_w_group'] = nrm((DEPTH, d, MOE_GROUPS), d ** -0.5)
    inp['moe_b_group'] = nrm((DEPTH, MOE_GROUPS), 0.01)
    inp['moe_w_expert'] = nrm((DEPTH, d, MOE_EXPERTS), d ** -0.5)
    inp['moe_b_expert'] = nrm((DEPTH, MOE_EXPERTS), 0.01)
    inp['moe_w_gate'] = nrm((DEPTH, MOE_EXPERTS, d, MOE_D_FF), d ** -0.5)
    inp['moe_w_up'] = nrm((DEPTH, MOE_EXPERTS, d, MOE_D_FF), d ** -0.5 * bt)
    inp['moe_w_down'] = nrm((DEPTH, MOE_EXPERTS, MOE_D_FF, d), MOE_D_FF ** -0.5 * bt)
    return inp


def reference(x_prompt, x_sample, state_dn_s, state_dn_conv, state_ssm_re, state_ssm_im, state_cconv,
              cache_mem_k, cache_mem_v, mem_prompt, ab_w_in, dn_conv_w, dn_a_log, dn_dt_bias, dn_norm_g,
              ssm_lambda_re, ssm_lambda_im, ssm_log_dt, ssm_b_re, ssm_b_im, ssm_c_re, ssm_c_im, ssm_d,
              ssm_w_glu, ssm_b_glu, ab_w_out, cc_w_pw1, cc_b_pw1, cc_w_dw, cc_b_dw, cc_ln_g, cc_ln_b,
              cc_w_pw2, cc_b_pw2, mem_wq, mem_wk, mem_wv, mem_wo, ln_g, ln_b, moe_w_group, moe_b_group,
              moe_w_expert, moe_b_expert, moe_w_gate, moe_w_up, moe_w_down):
    p = dict(ab_w_in=ab_w_in, dn_conv_w=dn_conv_w, dn_a_log=dn_a_log, dn_dt_bias=dn_dt_bias,
             dn_norm_g=dn_norm_g, ssm_lambda_re=ssm_lambda_re, ssm_lambda_im=ssm_lambda_im,
             ssm_log_dt=ssm_log_dt, ssm_b_re=ssm_b_re, ssm_b_im=ssm_b_im, ssm_c_re=ssm_c_re,
             ssm_c_im=ssm_c_im, ssm_d=ssm_d, ssm_w_glu=ssm_w_glu, ssm_b_glu=ssm_b_glu, ab_w_out=ab_w_out,
             cc_w_pw1=cc_w_pw1, cc_b_pw1=cc_b_pw1, cc_w_dw=cc_w_dw, cc_b_dw=cc_b_dw, cc_ln_g=cc_ln_g,
             cc_ln_b=cc_ln_b, cc_w_pw2=cc_w_pw2, cc_b_pw2=cc_b_pw2, mem_wq=mem_wq, mem_wo=mem_wo,
             ln_g=ln_g, ln_b=ln_b, moe_w_group=moe_w_group, moe_b_group=moe_b_group,
             moe_w_expert=moe_w_expert, moe_b_expert=moe_b_expert, moe_w_gate=moe_w_gate,
             moe_w_up=moe_w_up, moe_w_down=moe_w_down)
    bsz = x_prompt.shape[0]
    n_mem = mem_prompt.shape[1]
    z_dn_s = jnp.zeros((N_AB_LAYERS, bsz, DN_HEADS, DN_DK, DN_DV), F32)
    z_dn_conv = jnp.zeros((N_AB_LAYERS, bsz, DN_CONV - 1, DN_QKV), x_prompt.dtype)
    z_ssm = jnp.zeros((N_AB_LAYERS, bsz, SSM_GROUPS, SSM_STATE), F32)
    z_cconv = jnp.zeros((N_CONV_LAYERS, bsz, CONV_WIDTH - 1, D_MODEL), x_prompt.dtype)
    p_mem_k = jnp.einsum('bmd,lde->lbme', mem_prompt, mem_wk).reshape(DEPTH, bsz, n_mem, MEM_HEADS, MEM_HEAD_DIM)
    p_mem_v = jnp.einsum('bmd,lde->lbme', mem_prompt, mem_wv).reshape(DEPTH, bsz, n_mem, MEM_HEADS, MEM_HEAD_DIM)
    y_prompt, p_dn_s, p_dn_conv, p_ssm_re, p_ssm_im, p_cconv = _trunk(
        x_prompt, z_dn_s, z_dn_conv, z_ssm, z_ssm, z_cconv, p_mem_k, p_mem_v, p)
    y_sample, s_dn_s, s_dn_conv, s_ssm_re, s_ssm_im, s_cconv = _trunk(
        x_sample, state_dn_s, state_dn_conv, state_ssm_re, state_ssm_im, state_cconv, cache_mem_k, cache_mem_v, p)
    return (y_prompt, y_sample, p_dn_s, p_dn_conv, p_ssm_re, p_ssm_im, p_cconv, p_mem_k, p_mem_v,
            s_dn_s, s_dn_conv, s_ssm_re, s_ssm_im, s_cconv)
```

```python
import functools
import math

import jax
import jax.numpy as jnp
from jax import lax
from jax.experimental import pallas as pl
from jax.experimental.pallas import tpu as pltpu

F32 = jnp.float32
BF16 = jnp.bfloat16

DN_HEADS = 4
DN_DK = 128
DN_DV = 128
DN_CONV = 4
DN_CHUNK = 64
SSM_GROUP_CH = 16
MEM_HEADS = 4
MOE_GROUPS = 4
MOE_EXPERTS_PER_GROUP = 8
MOE_TOP_K = 2
LN_EPS = 1e-5
RMS_EPS = 1e-6

V7X_LANES = 128
V7X_SUBLANES = 8
V7X_BF16_SUBLANES = 16
V7X_VMEM_LIMIT_BYTES = 52 * 1024 * 1024


def _cparams(*sem):
    return pltpu.CompilerParams(dimension_semantics=sem, vmem_limit_bytes=V7X_VMEM_LIMIT_BYTES)


def _bdot(a, b):
    return jnp.dot(a.astype(BF16), b.astype(BF16), preferred_element_type=F32)


def _bdot_nt(a, b):
    return lax.dot_general(a.astype(BF16), b.astype(BF16), (((1,), (1,)), ((), ())),
                           preferred_element_type=F32)


def _split3(a):
    hi = a.astype(BF16)
    r1 = a - hi.astype(F32)
    mid = r1.astype(BF16)
    lo = (r1 - mid.astype(F32)).astype(BF16)
    return hi, mid, lo


def _sigmoid(x):
    return 1.0 / (1.0 + jnp.exp(-x))


def _silu(x):
    return x * _sigmoid(x)


def _softplus(x):
    return jnp.maximum(x, 0.0) + jnp.log(1.0 + jnp.exp(-jnp.abs(x)))


def _layer_norm(v, g, b):
    mu = jnp.mean(v, axis=-1, keepdims=True)
    d = v - mu
    var = jnp.mean(d * d, axis=-1, keepdims=True)
    return d * lax.rsqrt(var + LN_EPS) * g + b


def _row_tile(n, pref):
    t = min(n, pref)
    assert n % t == 0, (n, t)
    return t


def _linear_kernel(x_ref, w_ref, b_ref, o_ref, *, glu):
    y = _bdot(x_ref[...], w_ref[...]) + b_ref[...]
    if glu:
        n = y.shape[-1] // 2
        y = y[:, :n] * _sigmoid(y[:, n:])
    o_ref[...] = y.astype(o_ref.dtype)


def _linear(x, w, bias=None, *, glu=False, tm=512, out_dtype=F32):
    m, k = x.shape
    n = w.shape[1]
    if bias is None:
        bias = jnp.zeros((n,), F32)
    tm = _row_tile(m, tm)
    n_out = n // 2 if glu else n
    return pl.pallas_call(
        functools.partial(_linear_kernel, glu=glu),
        grid=(m // tm,),
        in_specs=[pl.BlockSpec((tm, k), lambda i: (i, 0)),
                  pl.BlockSpec((k, n), lambda i: (0, 0)),
                  pl.BlockSpec((1, n), lambda i: (0, 0))],
        out_specs=pl.BlockSpec((tm, n_out), lambda i: (i, 0)),
        out_shape=jax.ShapeDtypeStruct((m, n_out), out_dtype),
        compiler_params=_cparams("parallel"),
        name="linear",
    )(x, w.astype(BF16), bias.reshape(1, n).astype(F32))


def _linear_postnorm_kernel(h_ref, w_ref, b_ref, x_ref, g_ref, beta_ref, o_ref, *, alpha):
    h = _bdot(h_ref[...], w_ref[...]) + b_ref[...]
    o_ref[...] = _layer_norm(alpha * x_ref[...] + h, g_ref[...], beta_ref[...])


def _linear_postnorm(h_in, w, bias, x_res, g, beta, *, alpha, tm=512):
    m, k = h_in.shape
    d = w.shape[1]
    if bias is None:
        bias = jnp.zeros((d,), F32)
    tm = _row_tile(m, tm)
    row = lambda a: a.reshape(1, d).astype(F32)
    return pl.pallas_call(
        functools.partial(_linear_postnorm_kernel, alpha=alpha),
        grid=(m // tm,),
        in_specs=[pl.BlockSpec((tm, k), lambda i: (i, 0)),
                  pl.BlockSpec((k, d), lambda i: (0, 0)),
                  pl.BlockSpec((1, d), lambda i: (0, 0)),
                  pl.BlockSpec((tm, d), lambda i: (i, 0)),
                  pl.BlockSpec((1, d), lambda i: (0, 0)),
                  pl.BlockSpec((1, d), lambda i: (0, 0))],
        out_specs=pl.BlockSpec((tm, d), lambda i: (i, 0)),
        out_shape=jax.ShapeDtypeStruct((m, d), F32),
        compiler_params=_cparams("parallel"),
        name="linear_postnorm",
    )(h_in, w.astype(BF16), row(bias), x_res, row(g), row(beta))


def _ab_in_kernel(x_ref, wqkvz_ref, wba_ref, wu_ref, qkv_ref, z_ref, ba_ref, u_ref, *, n_qkv):
    x = x_ref[...].astype(BF16)
    y = jnp.dot(x, wqkvz_ref[...], preferred_element_type=F32)
    qkv_ref[...] = y[:, :n_qkv]
    z_ref[...] = y[:, n_qkv:]
    ba_ref[...] = jnp.dot(x, wba_ref[...], preferred_element_type=F32)
    u_ref[...] = jnp.dot(x, wu_ref[...], preferred_element_type=F32)


def _ab_in_proj(x, w_in, *, tm=512):
    bx, tx, d = x.shape
    n_key = DN_HEADS * DN_DK
    n_val = DN_HEADS * DN_DV
    n_qkv = 2 * n_key + n_val
    off_beta = n_qkv + n_val
    off_u = off_beta + 2 * DN_HEADS
    n_u = w_in.shape[1] - off_u
    w_qkvz = w_in[:, :off_beta].astype(BF16)
    w_ba = jnp.pad(w_in[:, off_beta:off_u], ((0, 0), (0, V7X_LANES - 2 * DN_HEADS))).astype(BF16)
    w_u = w_in[:, off_u:].astype(BF16)
    tm = _row_tile(tx, tm)
    full = lambda a: pl.BlockSpec(a.shape, lambda b, i: (0, 0))
    return pl.pallas_call(
        functools.partial(_ab_in_kernel, n_qkv=n_qkv),
        grid=(bx, tx // tm),
        in_specs=[pl.BlockSpec((None, tm, d), lambda b, i: (b, i, 0)),
                  full(w_qkvz), full(w_ba), full(w_u)],
        out_specs=[pl.BlockSpec((None, tm, n_qkv), lambda b, i: (b, i, 0)),
                   pl.BlockSpec((None, tm, n_val), lambda b, i: (b, i, 0)),
                   pl.BlockSpec((None, tm, V7X_LANES), lambda b, i: (b, i, 0)),
                   pl.BlockSpec((tm, n_u), lambda b, i: (i, b))],
        out_shape=[jax.ShapeDtypeStruct((bx, tx, n_qkv), F32),
                   jax.ShapeDtypeStruct((bx, tx, n_val), F32),
                   jax.ShapeDtypeStruct((bx, tx, V7X_LANES), F32),
                   jax.ShapeDtypeStruct((tx, bx * n_u), F32)],
        compiler_params=_cparams("parallel", "parallel"),
        name="ab_in_proj",
    )(x, w_qkvz, w_ba, w_u)


def _ab_out_kernel(o_ref, y_ref, wt_ref, wb_ref, x_ref, g_ref, beta_ref, out_ref, *, alpha):
    h = _bdot(o_ref[...], wt_ref[...]) + _bdot(y_ref[...], wb_ref[...])
    out_ref[...] = _layer_norm(alpha * x_ref[...] + h, g_ref[...], beta_ref[...])


def _ab_out_proj(o, y_tm, w_out, x, g, beta, *, alpha, tm=512):
    bx, tx, d = x.shape
    n_o = o.shape[-1]
    n_y = y_tm.shape[1] // bx
    tm = _row_tile(tx, tm)
    wt = w_out[:n_o].astype(BF16)
    wb = w_out[n_o:].astype(BF16)
    row = lambda a: a.reshape(1, d).astype(F32)
    full = lambda a: pl.BlockSpec(a.shape, lambda b, i: (0, 0))
    return pl.pallas_call(
        functools.partial(_ab_out_kernel, alpha=alpha),
        grid=(bx, tx // tm),
        in_specs=[pl.BlockSpec((None, tm, n_o), lambda b, i: (b, i, 0)),
                  pl.BlockSpec((tm, n_y), lambda b, i: (i, b)),
                  full(wt), full(wb),
                  pl.BlockSpec((None, tm, d), lambda b, i: (b, i, 0)),
                  pl.BlockSpec((1, d), lambda b, i: (0, 0)),
                  pl.BlockSpec((1, d), lambda b, i: (0, 0))],
        out_specs=pl.BlockSpec((None, tm, d), lambda b, i: (b, i, 0)),
        out_shape=jax.ShapeDtypeStruct((bx, tx, d), F32),
        compiler_params=_cparams("parallel", "parallel"),
        name="ab_out_proj",
    )(o, y_tm, wt, wb, x, row(g), row(beta))


_DN_HALO = V7X_SUBLANES


def _dn_kernel(qkv_ref, ba_ref, z_ref, cbuf_ref, s0_ref, cw_ref, alog_ref, dtb_ref, ng_ref,
               o_ref, s_out_ref, cbuf_out_ref, xp_ref, s_ref, *, t_blk, c):
    j = pl.program_id(1)
    nj = pl.num_programs(1)
    hist = DN_CONV - 1
    n_key = DN_HEADS * DN_DK

    @pl.when(j == 0)
    def _():
        xp_ref[...] = jnp.zeros_like(xp_ref)
        xp_ref[_DN_HALO - hist:_DN_HALO, :] = cbuf_ref[...]
        s_ref[...] = s0_ref[...]

    xp_ref[_DN_HALO:_DN_HALO + t_blk, :] = qkv_ref[...]
    y = xp_ref[pl.ds(_DN_HALO - hist, c), :] * cw_ref[0:1, :]
    for tap in range(1, DN_CONV):
        y = y + xp_ref[pl.ds(_DN_HALO - hist + tap, c), :] * cw_ref[tap:tap + 1, :]
    new_tail = xp_ref[pl.ds(_DN_HALO + t_blk - hist, hist), :]
    xp_ref[_DN_HALO - hist:_DN_HALO, :] = new_tail
    y = _silu(y)
    rows = lax.broadcasted_iota(jnp.int32, (c, 1), 0)
    live = rows < t_blk
    y = jnp.where(live, y, 0.0)

    ii = lax.broadcasted_iota(jnp.int32, (c, c), 0)
    jj = lax.broadcasted_iota(jnp.int32, (c, c), 1)
    incl = ii >= jj
    strict = ii > jj
    tril = jnp.where(incl, 1.0, 0.0).astype(BF16)
    eye = jnp.where(ii == jj, 1.0, 0.0)
    ba = ba_ref[...]
    z = z_ref[...]
    n_double = int(math.log2(c))

    for h in range(DN_HEADS):
        lo, hi = h * DN_DK, (h + 1) * DN_DK
        qh = y[:, lo:hi]
        kh = y[:, n_key + lo:n_key + hi]
        vh = y[:, 2 * n_key + h * DN_DV:2 * n_key + (h + 1) * DN_DV]
        qh = qh * lax.rsqrt(jnp.sum(qh * qh, -1, keepdims=True) + RMS_EPS) * (DN_DK ** -0.5)
        kh = kh * lax.rsqrt(jnp.sum(kh * kh, -1, keepdims=True) + RMS_EPS)
        beta = jnp.where(live, _sigmoid(ba[:, h:h + 1]), 0.0)
        a_logit = ba[:, DN_HEADS + h:DN_HEADS + h + 1]
        g = -jnp.exp(alog_ref[0:1, lo:hi]) * _softplus(a_logit + dtb_ref[0:1, lo:hi])
        g = jnp.where(live, g, 0.0)
        g_hi, g_mid, g_lo = _split3(g)
        gc = (jnp.dot(tril, g_hi, preferred_element_type=F32)
              + jnp.dot(tril, g_mid, preferred_element_type=F32)
              + jnp.dot(tril, g_lo, preferred_element_type=F32))
        gc_row = jnp.transpose(gc)[0:1, :]
        diff = gc[:, :c] - gc_row
        decay = jnp.where(incl, jnp.exp(jnp.where(incl, diff, 0.0)), 0.0)
        eg = jnp.exp(gc)
        gc_last = gc[c - 1:c, :]
        kb = kh * beta
        lmat = jnp.where(strict, _bdot_nt(kb, kh) * decay, 0.0)
        tmat = eye - lmat
        lpow = _bdot(lmat, lmat)
        for step in range(1, n_double):
            tmat = tmat + _bdot(tmat, lpow)
            if step < n_double - 1:
                lpow = _bdot(lpow, lpow)
        uw = _bdot(tmat, jnp.concatenate([vh * beta, kb * eg], axis=1))
        u = uw[:, :DN_DV]
        w = uw[:, DN_DV:]
        qk = _bdot_nt(qh, kh) * decay
        qg = qh * eg
        kdec = kh * jnp.exp(gc_last - gc)
        glast = jnp.exp(gc_last)
        s = s_ref[h]
        v_new = u - _bdot(w, s)
        o = _bdot(qg, s) + _bdot(qk, v_new)
        s_ref[h] = s * glast + _bdot(jnp.transpose(kdec), v_new)
        o = o * lax.rsqrt(jnp.mean(o * o, -1, keepdims=True) + RMS_EPS) * ng_ref[...]
        o = o[:t_blk] * _silu(z[:, h * DN_DV:(h + 1) * DN_DV])
        o_ref[:, h * DN_DV:(h + 1) * DN_DV] = o

    @pl.when(j == nj - 1)
    def _():
        s_out_ref[...] = s_ref[...]
        cbuf_out_ref[...] = new_tail


def _deltanet(qkv, ba, z, conv_buf, s0, conv_w, a_log, dt_bias, norm_g):
    b, t, n_qkv = qkv.shape
    n_val = DN_HEADS * DN_DV
    hist = DN_CONV - 1
    if t >= DN_CHUNK:
        c = t_blk = DN_CHUNK
        assert t % c == 0
    else:
        t_blk = t
        c = max(V7X_BF16_SUBLANES, 1 << (t - 1).bit_length())
    assert t_blk >= hist or t_blk == t
    rep = lambda a: jnp.repeat(a.astype(F32), DN_DK).reshape(1, DN_HEADS * DN_DK)
    const = lambda a: pl.BlockSpec(a.shape, lambda i, j: (0,) * a.ndim)
    cw = conv_w.astype(F32)
    alog, dtb, ng = rep(a_log), rep(dt_bias), norm_g.reshape(1, DN_DV).astype(F32)
    return pl.pallas_call(
        functools.partial(_dn_kernel, t_blk=t_blk, c=c),
        grid=(b, t // t_blk),
        in_specs=[pl.BlockSpec((None, t_blk, n_qkv), lambda i, j: (i, j, 0)),
                  pl.BlockSpec((None, t_blk, V7X_LANES), lambda i, j: (i, j, 0)),
                  pl.BlockSpec((None, t_blk, n_val), lambda i, j: (i, j, 0)),
                  pl.BlockSpec((None, hist, n_qkv), lambda i, j: (i, 0, 0)),
                  pl.BlockSpec((None, DN_HEADS, DN_DK, DN_DV), lambda i, j: (i, 0, 0, 0)),
                  const(cw), const(alog), const(dtb), const(ng)],
        out_specs=[pl.BlockSpec((None, t_blk, n_val), lambda i, j: (i, j, 0)),
                   pl.BlockSpec((None, DN_HEADS, DN_DK, DN_DV), lambda i, j: (i, 0, 0, 0)),
                   pl.BlockSpec((None, hist, n_qkv), lambda i, j: (i, 0, 0))],
        out_shape=[jax.ShapeDtypeStruct((b, t, n_val), F32),
                   jax.ShapeDtypeStruct((b, DN_HEADS, DN_DK, DN_DV), F32),
                   jax.ShapeDtypeStruct((b, hist, n_qkv), F32)],
        scratch_shapes=[pltpu.VMEM((_DN_HALO + c, n_qkv), F32),
                        pltpu.VMEM((DN_HEADS, DN_DK, DN_DV), F32)],
        compiler_params=_cparams("parallel", "arbitrary"),
        name="deltanet",
    )(qkv, ba, z, conv_buf.astype(F32), s0.astype(F32), cw, alog, dtb, ng)


def _s5_param_kernel(lre_ref, lim_ref, ldt_ref, lbre_ref, lbim_ref, fre_ref, fim_ref):
    lam_re = lre_ref[...]
    lam_im = lim_ref[...]
    dt = jnp.exp(ldt_ref[...])
    mag = jnp.exp(lam_re * dt)
    ang = lam_im * dt
    lb_re = mag * jnp.cos(ang)
    lb_im = mag * jnp.sin(ang)
    den = lam_re * lam_re + lam_im * lam_im
    lbre_ref[...] = lb_re
    lbim_ref[...] = lb_im
    fre_ref[...] = ((lb_re - 1.0) * lam_re + lb_im * lam_im) / den
    fim_ref[...] = (lb_im * lam_re - (lb_re - 1.0) * lam_im) / den


def _s5_discretize(lam_re, lam_im, log_dt):
    g, n = lam_re.shape
    ldt = jnp.broadcast_to(log_dt.astype(F32)[:, None], (g, n))
    shp = jax.ShapeDtypeStruct((g, n), F32)
    return pl.pallas_call(_s5_param_kernel, out_shape=[shp] * 4, name="s5_discretize")(
        lam_re.astype(F32), lam_im.astype(F32), ldt)


def _s5_kernel(u_ref, h0re_ref, h0im_ref, lbre_ref, lbim_ref, bre_ref, bim_ref, cre_ref, cim_ref,
               d_ref, wglu_ref, bglu_ref, y_ref, hre_out_ref, him_out_ref,
               sre_ref, sim_ref, cre_s, cim_s, *, tb, bb, lane_chunk):
    j = pl.program_id(1)
    nj = pl.num_programs(1)
    n_ch = u_ref.shape[-1]
    n_st = sre_ref.shape[-1]
    halves = bre_ref.shape[0]
    ch_h = n_ch // halves
    st_h = n_st // halves

    @pl.when(j == 0)
    def _():
        cre_s[...] = h0re_ref[...]
        cim_s[...] = h0im_ref[...]

    u = u_ref[...].reshape(tb * bb, n_ch)
    ub = u.astype(BF16)
    for hf in range(halves):
        uh = ub[:, hf * ch_h:(hf + 1) * ch_h]
        sre_ref[:, hf * st_h:(hf + 1) * st_h] = jnp.dot(uh, bre_ref[hf], preferred_element_type=F32)
        sim_ref[:, hf * st_h:(hf + 1) * st_h] = jnp.dot(uh, bim_ref[hf], preferred_element_type=F32)

    for c0 in range(0, n_st, lane_chunk):
        cs = slice(c0, c0 + lane_chunk)
        lr = jnp.broadcast_to(lbre_ref[0:1, cs], (bb, lane_chunk))
        li = jnp.broadcast_to(lbim_ref[0:1, cs], (bb, lane_chunk))

        def body(t, carry, cs=cs, lr=lr, li=li):
            hr, hi = carry
            r = pl.multiple_of(t * bb, bb)
            nr = lr * hr - li * hi + sre_ref[pl.ds(r, bb), cs]
            ni = lr * hi + li * hr + sim_ref[pl.ds(r, bb), cs]
            sre_ref[pl.ds(r, bb), cs] = nr
            sim_ref[pl.ds(r, bb), cs] = ni
            return nr, ni

        hr, hi = lax.fori_loop(0, tb, body, (cre_s[:, cs], cim_s[:, cs]))
        cre_s[:, cs] = hr
        cim_s[:, cs] = hi

    ys = []
    for hf in range(halves):
        hre = sre_ref[:, hf * st_h:(hf + 1) * st_h].astype(BF16)
        him = sim_ref[:, hf * st_h:(hf + 1) * st_h].astype(BF16)
        ys.append(jnp.dot(hre, cre_ref[hf], preferred_element_type=F32)
                  - jnp.dot(him, cim_ref[hf], preferred_element_type=F32))
    y = jnp.concatenate(ys, axis=1) + d_ref[...] * u
    y = jax.nn.gelu(y)
    y = y * _sigmoid(_bdot(y, wglu_ref[...]) + bglu_ref[...])
    y_ref[...] = y.reshape(tb, bb, n_ch)

    @pl.when(j == nj - 1)
    def _():
        hre_out_ref[...] = cre_s[...]
        him_out_ref[...] = cim_s[...]


def _s5(u_tm, h0_re, h0_im, lam_re, lam_im, log_dt, b_re, b_im, c_re, c_im, d_skip, w_glu, b_glu, *, halves=2):
    t, b, n_ch = u_tm.shape
    g, n, p = b_re.shape
    n_st = g * n
    lb_re, lb_im, f_re, f_im = _s5_discretize(lam_re, lam_im, log_dt)
    b_re = b_re.astype(F32)
    b_im = b_im.astype(F32)
    bb_re = f_re[..., None] * b_re - f_im[..., None] * b_im
    bb_im = f_re[..., None] * b_im + f_im[..., None] * b_re
    gh = g // halves

    def in_blocks(a):
        a = a.reshape(halves, gh, n, p)
        eye = jnp.eye(gh, dtype=F32)
        return jnp.einsum('hgnp,gk->hgpkn', a, eye).reshape(halves, gh * p, gh * n).astype(BF16)

    def out_blocks(a):
        a = a.astype(F32).reshape(halves, gh, p, n)
        eye = jnp.eye(gh, dtype=F32)
        return jnp.einsum('hgpn,gk->hgnkp', a, eye).reshape(halves, gh * n, gh * p).astype(BF16)

    bre_m, bim_m = in_blocks(bb_re), in_blocks(bb_im)
    cre_m, cim_m = out_blocks(c_re), out_blocks(c_im)
    bb = b if b <= 128 else 128
    assert b % bb == 0
    tb = _row_tile(t, max(1, 512 // bb))
    lane_chunk = max(V7X_LANES, min(n_st, 8192 // bb))
    const = lambda a: pl.BlockSpec(a.shape, lambda i, j: (0,) * a.ndim)
    row = lambda a, m: a.reshape(1, m).astype(F32)
    args = (u_tm, h0_re.reshape(b, n_st).astype(F32), h0_im.reshape(b, n_st).astype(F32),
            row(lb_re, n_st), row(lb_im, n_st), bre_m, bim_m, cre_m, cim_m,
            row(d_skip, n_ch), w_glu.astype(BF16), row(b_glu, n_ch))
    y, hre, him = pl.pallas_call(
        functools.partial(_s5_kernel, tb=tb, bb=bb, lane_chunk=lane_chunk),
        grid=(b // bb, t // tb),
        in_specs=[pl.BlockSpec((tb, bb, n_ch), lambda i, j: (j, i, 0)),
                  pl.BlockSpec((bb, n_st), lambda i, j: (i, 0)),
                  pl.BlockSpec((bb, n_st), lambda i, j: (i, 0))] + [const(a) for a in args[3:]],
        out_specs=[pl.BlockSpec((tb, bb, n_ch), lambda i, j: (j, i, 0)),
                   pl.BlockSpec((bb, n_st), lambda i, j: (i, 0)),
                   pl.BlockSpec((bb, n_st), lambda i, j: (i, 0))],
        out_shape=[jax.ShapeDtypeStruct((t, b, n_ch), F32),
                   jax.ShapeDtypeStruct((b, n_st), F32),
                   jax.ShapeDtypeStruct((b, n_st), F32)],
        scratch_shapes=[pltpu.VMEM((tb * bb, n_st), F32), pltpu.VMEM((tb * bb, n_st), F32),
                        pltpu.VMEM((bb, n_st), F32), pltpu.VMEM((bb, n_st), F32)],
        compiler_params=_cparams("parallel", "arbitrary"),
        name="s5",
    )(*args)
    return y, hre.reshape(b, g, n), him.reshape(b, g, n)


def _ab_layer(x, dn_s, dn_conv, ssm_re, ssm_im, p, i, ln_g, ln_b, *, alpha):
    b, t, d = x.shape
    xr = x if t > 1 else x.reshape(1, b, d)
    bx, tx, _ = xr.shape
    qkv, z, ba, u_tm = _ab_in_proj(xr, p['ab_w_in'][i])
    n_ch = u_tm.shape[1] // bx
    shp = lambda a: a.reshape(b, t, a.shape[-1])
    o, s_new, conv_new = _deltanet(shp(qkv), shp(ba), shp(z), dn_conv, dn_s, p['dn_conv_w'][i],
                                   p['dn_a_log'][i], p['dn_dt_bias'][i], p['dn_norm_g'][i])
    y_tm, h_re, h_im = _s5(u_tm.reshape(t, b, n_ch), ssm_re, ssm_im, p['ssm_lambda_re'][i],
                           p['ssm_lambda_im'][i], p['ssm_log_dt'][i], p['ssm_b_re'][i], p['ssm_b_im'][i],
                           p['ssm_c_re'][i], p['ssm_c_im'][i], p['ssm_d'][i], p['ssm_w_glu'][i],
                           p['ssm_b_glu'][i])
    x_new = _ab_out_proj(o.reshape(bx, tx, -1), y_tm.reshape(tx, bx * n_ch), p['ab_w_out'][i], xr,
                         ln_g, ln_b, alpha=alpha)
    return x_new.reshape(b, t, d), s_new, conv_new, h_re, h_im


_CC_HALO = 32


def _cconv_seq_kernel(h_ref, buf_ref, w_ref, bdw_ref, g_ref, b_ref, o_ref, xp_ref, xs_ref, acc_ref, *, tt, width):
    j = pl.program_id(1)
    hist = width - 1
    d = h_ref.shape[-1]
    rows = V7X_SUBLANES

    @pl.when(j == 0)
    def _():
        xp_ref[0:_CC_HALO - hist, :] = jnp.zeros((_CC_HALO - hist, d), F32)
        xp_ref[_CC_HALO - hist:_CC_HALO, :] = buf_ref[...]

    xp_ref[_CC_HALO:_CC_HALO + tt, :] = h_ref[...]
    n_shift = xs_ref.shape[1]
    for s in range(1, rows):
        xs_ref[s - 1] = xp_ref[pl.ds(s, n_shift), :]
    base = _CC_HALO - hist
    for c0 in range(0, d, V7X_LANES):
        cs = slice(c0, c0 + V7X_LANES)
        taps = [jnp.broadcast_to(w_ref[k:k + 1, cs], (rows, V7X_LANES)) for k in range(width)]

        def window(r, k, cs=cs):
            a, s = divmod(base + k, rows)
            if s == 0:
                return xp_ref[pl.ds(r + a * rows, rows), cs]
            return xs_ref[s - 1, pl.ds(r + a * rows, rows), cs]

        def body(i, carry, taps=taps, window=window, cs=cs):
            r = pl.multiple_of(i * rows, rows)
            acc = window(r, 0) * taps[0]
            for k in range(1, width):
                acc = acc + window(r, k) * taps[k]
            acc_ref[pl.ds(r, rows), cs] = acc
            return carry

        lax.fori_loop(0, tt // rows, body, 0)
    xp_ref[0:_CC_HALO, :] = xp_ref[tt:tt + _CC_HALO, :]
    y = _layer_norm(acc_ref[...] + bdw_ref[...], g_ref[...], b_ref[...])
    o_ref[...] = _silu(y)


def _cconv_seq(h, buf, w_dw, b_dw, ln_g, ln_b, *, tt=256):
    b, t, d = h.shape
    width = w_dw.shape[0]
    tt = _row_tile(t, tt)
    assert tt >= _CC_HALO and width - 1 <= _CC_HALO
    row = lambda a: a.reshape(1, d).astype(F32)
    const = lambda a: pl.BlockSpec(a.shape, lambda i, j: (0, 0))
    args = (w_dw.astype(F32), row(b_dw), row(ln_g), row(ln_b))
    return pl.pallas_call(
        functools.partial(_cconv_seq_kernel, tt=tt, width=width),
        grid=(b, t // tt),
        in_specs=[pl.BlockSpec((None, tt, d), lambda i, j: (i, j, 0)),
                  pl.BlockSpec((None, width - 1, d), lambda i, j: (i, 0, 0))] + [const(a) for a in args],
        out_specs=pl.BlockSpec((None, tt, d), lambda i, j: (i, j, 0)),
        out_shape=jax.ShapeDtypeStruct((b, t, d), F32),
        scratch_shapes=[pltpu.VMEM((_CC_HALO + tt, d), F32),
                        pltpu.VMEM((V7X_SUBLANES - 1, _CC_HALO + tt - V7X_SUBLANES, d), F32),
                        pltpu.VMEM((tt, d), F32)],
        compiler_params=_cparams("parallel", "arbitrary"),
        name="cconv_seq",
    )(h, buf.astype(F32), *args)


def _cconv_step_kernel(h_ref, buf_ref, w_ref, bdw_ref, g_ref, b_ref, o_ref, *, width):
    hist = width - 1
    acc = jnp.sum(buf_ref[...] * w_ref[0:hist, :][None], axis=1) + h_ref[...] * w_ref[hist:width, :]
    o_ref[...] = _silu(_layer_norm(acc + bdw_ref[...], g_ref[...], b_ref[...]))


def _cconv_step(h, buf, w_dw, b_dw, ln_g, ln_b, *, bb=8):
    b, d = h.shape
    width = w_dw.shape[0]
    bb = _row_tile(b, bb)
    row = lambda a: a.reshape(1, d).astype(F32)
    const = lambda a: pl.BlockSpec(a.shape, lambda i: (0, 0))
    args = (w_dw.astype(F32), row(b_dw), row(ln_g), row(ln_b))
    return pl.pallas_call(
        functools.partial(_cconv_step_kernel, width=width),
        grid=(b // bb,),
        in_specs=[pl.BlockSpec((bb, d), lambda i: (i, 0)),
                  pl.BlockSpec((bb, width - 1, d), lambda i: (i, 0, 0))] + [const(a) for a in args],
        out_specs=pl.BlockSpec((bb, d), lambda i: (i, 0)),
        out_shape=jax.ShapeDtypeStruct((b, d), F32),
        compiler_params=_cparams("parallel"),
        name="cconv_step",
    )(h, buf.astype(F32), *args)


def _conv_layer(x, buf, p, i, ln_g, ln_b, *, alpha):
    b, t, d = x.shape
    x2 = x.reshape(b * t, d)
    h = _linear(x2, p['cc_w_pw1'][i], p['cc_b_pw1'][i], glu=True)
    args = (p['cc_w_dw'][i], p['cc_b_dw'][i], p['cc_ln_g'][i], p['cc_ln_b'][i])
    if t == 1:
        hc = _cconv_step(h, buf, *args)
    else:
        hc = _cconv_seq(h.reshape(b, t, d), buf, *args).reshape(b * t, d)
    new_buf = jnp.concatenate([buf.astype(F32), h.reshape(b, t, d)], axis=1)[:, t:]
    x_new = _linear_postnorm(hc, p['cc_w_pw2'][i], p['cc_b_pw2'][i], x2, ln_g, ln_b, alpha=alpha)
    return x_new.reshape(b, t, d), new_buf


def _mem_attn_seq_kernel(x_ref, wq_ref, k_ref, v_ref, wo_ref, g_ref, b_ref, o_ref, *, alpha, heads):
    x = x_ref[...]
    d = x.shape[-1]
    hd = d // heads
    q = _bdot(x, wq_ref[...]) * (hd ** -0.5)
    k = k_ref[...].astype(BF16)
    v = v_ref[...].astype(BF16)
    outs = []
    for h in range(heads):
        hs = slice(h * hd, (h + 1) * hd)
        s = _bdot_nt(q[:, hs], k[:, hs])
        s = s - jnp.max(s, axis=-1, keepdims=True)
        e = jnp.exp(s)
        a = e / jnp.sum(e, axis=-1, keepdims=True)
        outs.append(_bdot(a, v[:, hs]))
    o = jnp.concatenate(outs, axis=1)
    hres = _bdot(o, wo_ref[...])
    o_ref[...] = _layer_norm(alpha * x + hres, g_ref[...], b_ref[...])


def _mem_attn_seq(x, mk, mv, wq, wo, ln_g, ln_b, *, alpha, tq=512):
    b, t, d = x.shape
    m = mk.shape[1]
    tq = _row_tile(t, tq)
    row = lambda a: a.reshape(1, d).astype(F32)
    const = lambda a: pl.BlockSpec(a.shape, lambda i, j: (0, 0))
    wqb, wob = wq.astype(BF16), wo.astype(BF16)
    return pl.pallas_call(
        functools.partial(_mem_attn_seq_kernel, alpha=alpha, heads=MEM_HEADS),
        grid=(b, t // tq),
        in_specs=[pl.BlockSpec((None, tq, d), lambda i, j: (i, j, 0)),
                  const(wqb),
                  pl.BlockSpec((None, m, d), lambda i, j: (i, 0, 0)),
                  pl.BlockSpec((None, m, d), lambda i, j: (i, 0, 0)),
                  const(wob), pl.BlockSpec((1, d), lambda i, j: (0, 0)),
                  pl.BlockSpec((1, d), lambda i, j: (0, 0))],
        out_specs=pl.BlockSpec((None, tq, d), lambda i, j: (i, j, 0)),
        out_shape=jax.ShapeDtypeStruct((b, t, d), F32),
        compiler_params=_cparams("parallel", "parallel"),
        name="mem_attn_seq",
    )(x, wqb, mk, mv, wob, row(ln_g), row(ln_b))


def _mem_attn_step_kernel(q_ref, k_ref, v_ref, o_ref, *, heads, bb):
    i = pl.program_id(0)
    d = q_ref.shape[-1]
    hd = d // heads
    for r in range(bb):
        q = q_ref[pl.ds(i * bb + r, 1), :] * (hd ** -0.5)
        prod = k_ref[r] * q
        v = v_ref[r]
        for h in range(heads):
            hs = slice(h * hd, (h + 1) * hd)
            s = jnp.sum(prod[:, hs], axis=-1, keepdims=True)
            s = s - jnp.max(s, axis=0, keepdims=True)
            e = jnp.exp(s)
            a = e / jnp.sum(e, axis=0, keepdims=True)
            o_ref[pl.ds(i * bb + r, 1), hs] = jnp.sum(a * v[:, hs], axis=0, keepdims=True)


def _mem_attn_step(q, mk, mv, *, bb=4):
    b, d = q.shape
    m = mk.shape[1]
    bb = _row_tile(b, bb)
    return pl.pallas_call(
        functools.partial(_mem_attn_step_kernel, heads=MEM_HEADS, bb=bb),
        grid=(b // bb,),
        in_specs=[pl.BlockSpec((b, d), lambda i: (0, 0)),
                  pl.BlockSpec((bb, m, d), lambda i: (i, 0, 0)),
                  pl.BlockSpec((bb, m, d), lambda i: (i, 0, 0))],
        out_specs=pl.BlockSpec((b, d), lambda i: (0, 0)),
        out_shape=jax.ShapeDtypeStruct((b, d), F32),
        compiler_params=_cparams("arbitrary"),
        name="mem_attn_step",
    )(q, mk, mv)


def _mem_layer(x, mk, mv, wq, wo, ln_g, ln_b, *, alpha):
    b, t, d = x.shape
    m = mk.shape[1]
    mk = mk.reshape(b, m, d)
    mv = mv.reshape(b, m, d)
    if t == 1:
        x2 = x.reshape(b, d)
        q = _linear(x2, wq)
        o = _mem_attn_step(q, mk, mv)
        return _linear_postnorm(o, wo, None, x2, ln_g, ln_b, alpha=alpha).reshape(b, t, d)
    return _mem_attn_seq(x, mk, mv, wq, wo, ln_g, ln_b, alpha=alpha)


_NEG = -1e30


def _router_kernel(x_ref, w_ref, b_ref, ids_ref, wts_ref, *, groups, per_group):
    x = x_ref[...]
    w = w_ref[...]
    xh = x.astype(BF16)
    xl = (x - xh.astype(F32)).astype(BF16)
    wh = w.astype(BF16)
    wl = (w - wh.astype(F32)).astype(BF16)
    logits = (jnp.dot(xh, wh, preferred_element_type=F32) + jnp.dot(xh, wl, preferred_element_type=F32)
              + jnp.dot(xl, wh, preferred_element_type=F32)) + b_ref[...]
    lane = lax.broadcasted_iota(jnp.int32, logits.shape, 1)
    n_exp = groups * per_group
    is_g = lane < groups
    gl = jnp.where(is_g, logits, _NEG)
    gmax = jnp.max(gl, axis=-1, keepdims=True)
    gsum = jnp.sum(jnp.where(is_g, jnp.exp(gl - gmax), 0.0), axis=-1, keepdims=True)
    gw = 1.0 / gsum
    gi = jnp.min(jnp.where(gl == gmax, lane, V7X_LANES), axis=-1, keepdims=True)
    lane_grp = (lane - groups) // per_group
    sel = (lane >= groups) & (lane < groups + n_exp) & (lane_grp == gi)
    el = jnp.where(sel, logits, _NEG)
    emax = jnp.max(el, axis=-1, keepdims=True)
    ee = jnp.where(sel, jnp.exp(el - emax), 0.0)
    ep = jnp.where(sel, ee / jnp.sum(ee, axis=-1, keepdims=True), -1.0)
    p1 = jnp.max(ep, axis=-1, keepdims=True)
    i1 = jnp.min(jnp.where(ep == p1, lane, V7X_LANES), axis=-1, keepdims=True)
    ep2 = jnp.where(lane == i1, -1.0, ep)
    p2 = jnp.max(ep2, axis=-1, keepdims=True)
    i2 = jnp.min(jnp.where(ep2 == p2, lane, V7X_LANES), axis=-1, keepdims=True)
    denom = p1 + p2
    ids_ref[...] = jnp.where(lane == 0, i1 - groups, jnp.where(lane == 1, i2 - groups, 0))
    wts_ref[...] = jnp.where(lane == 0, gw * p1 / denom, jnp.where(lane == 1, gw * p2 / denom, 0.0))


def _router(x, w_group, b_group, w_expert, b_expert, *, tm=512):
    n, d = x.shape
    groups = w_group.shape[1]
    n_exp = w_expert.shape[1]
    pad = V7X_LANES - groups - n_exp
    w = jnp.pad(jnp.concatenate([w_group, w_expert], axis=1).astype(F32), ((0, 0), (0, pad)))
    b = jnp.pad(jnp.concatenate([b_group, b_expert]).astype(F32), (0, pad)).reshape(1, V7X_LANES)
    tm = _row_tile(n, tm)
    return pl.pallas_call(
        functools.partial(_router_kernel, groups=groups, per_group=n_exp // groups),
        grid=(n // tm,),
        in_specs=[pl.BlockSpec((tm, d), lambda i: (i, 0)),
                  pl.BlockSpec((d, V7X_LANES), lambda i: (0, 0)),
                  pl.BlockSpec((1, V7X_LANES), lambda i: (0, 0))],
        out_specs=[pl.BlockSpec((tm, V7X_LANES), lambda i: (i, 0)),
                   pl.BlockSpec((tm, V7X_LANES), lambda i: (i, 0))],
        out_shape=[jax.ShapeDtypeStruct((n, V7X_LANES), jnp.int32),
                   jax.ShapeDtypeStruct((n, V7X_LANES), F32)],
        compiler_params=_cparams("parallel"),
        name="moe_router",
    )(x, w, b)


def _expert_ffn_kernel(te_ref, nv_ref, x_ref, wg_ref, wu_ref, wd_ref, o_ref):
    i = pl.program_id(0)

    @pl.when(i < nv_ref[0])
    def _():
        x = x_ref[...]
        hid = _silu(_bdot(x, wg_ref[...])) * _bdot(x, wu_ref[...])
        o_ref[...] = _bdot(hid, wd_ref[...])

    @pl.when(i >= nv_ref[0])
    def _():
        o_ref[...] = jnp.zeros_like(o_ref)


def _expert_ffn(x_sorted, tile_expert, n_valid, w_gate, w_up, w_down, *, tm):
    r, d = x_sorted.shape
    f = w_gate.shape[-1]
    return pl.pallas_call(
        _expert_ffn_kernel,
        grid_spec=pltpu.PrefetchScalarGridSpec(
            num_scalar_prefetch=2,
            grid=(r // tm,),
            in_specs=[pl.BlockSpec((tm, d), lambda i, te, nv: (i, 0)),
                      pl.BlockSpec((None, d, f), lambda i, te, nv: (te[i], 0, 0)),
                      pl.BlockSpec((None, d, f), lambda i, te, nv: (te[i], 0, 0)),
                      pl.BlockSpec((None, f, d), lambda i, te, nv: (te[i], 0, 0))],
            out_specs=pl.BlockSpec((tm, d), lambda i, te, nv: (i, 0)),
        ),
        out_shape=jax.ShapeDtypeStruct((r, d), F32),
        compiler_params=_cparams("arbitrary"),
        name="moe_expert_ffn",
    )(tile_expert, n_valid, x_sorted, w_gate, w_up, w_down)


def _combine_kernel(y_ref, wts_ref, x_ref, g_ref, b_ref, o_ref, *, alpha):
    d = x_ref.shape[-1]
    wts = wts_ref[...]
    y = wts[:, 0:1] * y_ref[:, :d] + wts[:, 1:2] * y_ref[:, d:]
    o_ref[...] = _layer_norm(alpha * x_ref[...] + y, g_ref[...], b_ref[...])


def _combine_postnorm(y_pairs, wts, x, g, b, *, alpha, tm=512):
    n, d = x.shape
    tm = _row_tile(n, tm)
    row = lambda a: a.reshape(1, d).astype(F32)
    return pl.pallas_call(
        functools.partial(_combine_kernel, alpha=alpha),
        grid=(n // tm,),
        in_specs=[pl.BlockSpec((tm, 2 * d), lambda i: (i, 0)),
                  pl.BlockSpec((tm, V7X_LANES), lambda i: (i, 0)),
                  pl.BlockSpec((tm, d), lambda i: (i, 0)),
                  pl.BlockSpec((1, d), lambda i: (0, 0)),
                  pl.BlockSpec((1, d), lambda i: (0, 0))],
        out_specs=pl.BlockSpec((tm, d), lambda i: (i, 0)),
        out_shape=jax.ShapeDtypeStruct((n, d), F32),
        compiler_params=_cparams("parallel"),
        name="moe_combine",
    )(y_pairs, wts, x, row(g), row(b))


def _moe_layer(x, w_group, b_group, w_expert, b_expert, w_gate, w_up, w_down, ln_g, ln_b, *, alpha):
    b, t, d = x.shape
    n = b * t
    x2 = x.reshape(n, d)
    n_exp = w_gate.shape[0]
    ids, wts = _router(x2, w_group, b_group, w_expert, b_expert)
    tm = 256 if n >= 4096 else V7X_BF16_SUBLANES
    a = ids[:, :MOE_TOP_K].reshape(-1)
    onehot = (a[:, None] == jnp.arange(n_exp, dtype=jnp.int32)[None, :]).astype(jnp.int32)
    csum = jnp.cumsum(onehot, axis=0)
    rank = jnp.take_along_axis(csum, a[:, None], axis=1)[:, 0] - 1
    counts = csum[-1]
    padded = ((counts + tm - 1) // tm) * tm
    ends = jnp.cumsum(padded)
    slot = (ends - padded)[a] + rank
    n_tiles = (MOE_TOP_K * n + n_exp * (tm - 1)) // tm
    rows = n_tiles * tm
    row_token = jnp.zeros((rows,), jnp.int32).at[slot].set(jnp.arange(MOE_TOP_K * n, dtype=jnp.int32) // MOE_TOP_K)
    n_valid = (ends[-1] // tm).astype(jnp.int32)
    tile_start = jnp.minimum(jnp.arange(n_tiles, dtype=jnp.int32), n_valid - 1) * tm
    tile_expert = jnp.minimum(jnp.searchsorted(ends, tile_start, side='right'), n_exp - 1).astype(jnp.int32)
    x_sorted = jnp.take(x2.astype(BF16), row_token, axis=0)
    y_sorted = _expert_ffn(x_sorted, tile_expert, n_valid.reshape(1), w_gate, w_up, w_down, tm=tm)
    y_pairs = jnp.take(y_sorted, slot, axis=0).reshape(n, MOE_TOP_K * d)
    return _combine_postnorm(y_pairs, wts, x2, ln_g, ln_b, alpha=alpha).reshape(b, t, d)


def _trunk(x, dn_s, dn_conv, ssm_re, ssm_im, cconv, mem_k, mem_v, p):
    depth = p['ln_g'].shape[0]
    alpha = (2.0 * depth) ** 0.25
    out_s, out_conv, out_re, out_im, out_cc = [], [], [], [], []
    for layer in range(depth):
        i = layer // 2
        g, bta = p['ln_g'][layer], p['ln_b'][layer]
        if layer % 2 == 0:
            x, s_new, conv_new, re_new, im_new = _ab_layer(x, dn_s[i], dn_conv[i], ssm_re[i], ssm_im[i], p, i,
                                                           g[0], bta[0], alpha=alpha)
            out_s.append(s_new)
            out_conv.append(conv_new)
            out_re.append(re_new)
            out_im.append(im_new)
        else:
            x, cc_new = _conv_layer(x, cconv[i], p, i, g[0], bta[0], alpha=alpha)
            out_cc.append(cc_new)
        x = _mem_layer(x, mem_k[layer], mem_v[layer], p['mem_wq'][layer], p['mem_wo'][layer], g[1], bta[1],
                       alpha=alpha)
        x = _moe_layer(x, p['moe_w_group'][layer], p['moe_b_group'][layer], p['moe_w_expert'][layer],
                       p['moe_b_expert'][layer], p['moe_w_gate'][layer], p['moe_w_up'][layer],
                       p['moe_w_down'][layer], g[2], bta[2], alpha=alpha)
    return x, jnp.stack(out_s), jnp.stack(out_conv), jnp.stack(out_re), jnp.stack(out_im), jnp.stack(out_cc)


def kernel(x_prompt, x_sample, state_dn_s, state_dn_conv, state_ssm_re, state_ssm_im, state_cconv,
           cache_mem_k, cache_mem_v, mem_prompt, ab_w_in, dn_conv_w, dn_a_log, dn_dt_bias, dn_norm_g,
           ssm_lambda_re, ssm_lambda_im, ssm_log_dt, ssm_b_re, ssm_b_im, ssm_c_re, ssm_c_im, ssm_d,
           ssm_w_glu, ssm_b_glu, ab_w_out, cc_w_pw1, cc_b_pw1, cc_w_dw, cc_b_dw, cc_ln_g, cc_ln_b,
           cc_w_pw2, cc_b_pw2, mem_wq, mem_wk, mem_wv, mem_wo, ln_g, ln_b, moe_w_group, moe_b_group,
           moe_w_expert, moe_b_expert, moe_w_gate, moe_w_up, moe_w_down):
    p = dict(ab_w_in=ab_w_in, dn_conv_w=dn_conv_w, dn_a_log=dn_a_log, dn_dt_bias=dn_dt_bias,
             dn_norm_g=dn_norm_g, ssm_lambda_re=ssm_lambda_re, ssm_lambda_im=ssm_lambda_im,
             ssm_log_dt=ssm_log_dt, ssm_b_re=ssm_b_re, ssm_b_im=ssm_b_im, ssm_c_re=ssm_c_re,
             ssm_c_im=ssm_c_im, ssm_d=ssm_d, ssm_w_glu=ssm_w_glu, ssm_b_glu=ssm_b_glu, ab_w_out=ab_w_out,
             cc_w_pw1=cc_w_pw1, cc_b_pw1=cc_b_pw1, cc_w_dw=cc_w_dw, cc_b_dw=cc_b_dw, cc_ln_g=cc_ln_g,
             cc_ln_b=cc_ln_b, cc_w_pw2=cc_w_pw2, cc_b_pw2=cc_b_pw2, mem_wq=mem_wq, mem_wo=mem_wo,
             ln_g=ln_g, ln_b=ln_b, moe_w_group=moe_w_group, moe_b_group=moe_b_group,
             moe_w_expert=moe_w_expert, moe_b_expert=moe_b_expert, moe_w_gate=moe_w_gate,
             moe_w_up=moe_w_up, moe_w_down=moe_w_down)
    depth = ln_g.shape[0]
    n_ab = state_dn_s.shape[0]
    n_cc = state_cconv.shape[0]
    bsz, _, d = x_prompt.shape
    n_mem = mem_prompt.shape[1]
    hd = d // MEM_HEADS
    z_dn_s = jnp.zeros((n_ab, bsz) + state_dn_s.shape[2:], F32)
    z_dn_conv = jnp.zeros((n_ab, bsz) + state_dn_conv.shape[2:], F32)
    z_ssm = jnp.zeros((n_ab, bsz) + state_ssm_re.shape[2:], F32)
    z_cconv = jnp.zeros((n_cc, bsz) + state_cconv.shape[2:], F32)
    mem2 = mem_prompt.reshape(bsz * n_mem, d)
    p_mem_k = jnp.stack([_linear(mem2, mem_wk[l]) for l in range(depth)]).reshape(depth, bsz, n_mem, MEM_HEADS, hd)
    p_mem_v = jnp.stack([_linear(mem2, mem_wv[l]) for l in range(depth)]).reshape(depth, bsz, n_mem, MEM_HEADS, hd)
    y_prompt, p_dn_s, p_dn_conv, p_ssm_re, p_ssm_im, p_cconv = _trunk(
        x_prompt, z_dn_s, z_dn_conv, z_ssm, z_ssm, z_cconv, p_mem_k, p_mem_v, p)
    y_sample, s_dn_s, s_dn_conv, s_ssm_re, s_ssm_im, s_cconv = _trunk(
        x_sample, state_dn_s, state_dn_conv, state_ssm_re, state_ssm_im, state_cconv, cache_mem_k, cache_mem_v, p)
    return (y_prompt, y_sample, p_dn_s, p_dn_conv, p_ssm_re, p_ssm_im, p_cconv, p_mem_k, p_mem_v,
            s_dn_s, s_dn_conv, s_ssm_re, s_ssm_im, s_cconv)
```

```python
import functools
import math

import jax
import jax.numpy as jnp
from jax import lax
from jax.experimental import pallas as pl
from jax.experimental.pallas import tpu as pltpu

F32 = jnp.float32
BF16 = jnp.bfloat16

DN_HEADS = 4
DN_DK = 128
DN_DV = 128
DN_CONV = 4
DN_CHUNK = 64
SSM_GROUP_CH = 16
MEM_HEADS = 4
MOE_GROUPS = 4
MOE_EXPERTS_PER_GROUP = 8
MOE_TOP_K = 2
LN_EPS = 1e-5
RMS_EPS = 1e-6

V7X_LANES = 128
V7X_SUBLANES = 8
V7X_BF16_SUBLANES = 16
V7X_VMEM_LIMIT_BYTES = 52 * 1024 * 1024


def _cparams(*sem):
    return pltpu.CompilerParams(dimension_semantics=sem, vmem_limit_bytes=V7X_VMEM_LIMIT_BYTES)


def _bdot(a, b):
    return jnp.dot(a.astype(BF16), b.astype(BF16), preferred_element_type=F32)


def _bdot_nt(a, b):
    return lax.dot_general(a.astype(BF16), b.astype(BF16), (((1,), (1,)), ((), ())),
                           preferred_element_type=F32)


def _split3(a):
    hi = a.astype(BF16)
    r1 = a - hi.astype(F32)
    mid = r1.astype(BF16)
    lo = (r1 - mid.astype(F32)).astype(BF16)
    return hi, mid, lo


def _sigmoid(x):
    return 1.0 / (1.0 + jnp.exp(-x))


def _silu(x):
    return x * _sigmoid(x)


def _softplus(x):
    return jnp.maximum(x, 0.0) + jnp.log(1.0 + jnp.exp(-jnp.abs(x)))


def _layer_norm(v, g, b):
    mu = jnp.mean(v, axis=-1, keepdims=True)
    d = v - mu
    var = jnp.mean(d * d, axis=-1, keepdims=True)
    return d * lax.rsqrt(var + LN_EPS) * g + b


def _row_tile(n, pref):
    t = min(n, pref)
    assert n % t == 0, (n, t)
    return t


def _linear_kernel(x_ref, w_ref, b_ref, o_ref, *, glu):
    y = _bdot(x_ref[...], w_ref[...]) + b_ref[...]
    if glu:
        n = y.shape[-1] // 2
        y = y[:, :n] * _sigmoid(y[:, n:])
    o_ref[...] = y.astype(o_ref.dtype)


def _linear(x, w, bias=None, *, glu=False, tm=512, out_dtype=F32):
    m, k = x.shape
    n = w.shape[1]
    if bias is None:
        bias = jnp.zeros((n,), F32)
    tm = _row_tile(m, tm)
    n_out = n // 2 if glu else n
    return pl.pallas_call(
        functools.partial(_linear_kernel, glu=glu),
        grid=(m // tm,),
        in_specs=[pl.BlockSpec((tm, k), lambda i: (i, 0)),
                  pl.BlockSpec((k, n), lambda i: (0, 0)),
                  pl.BlockSpec((1, n), lambda i: (0, 0))],
        out_specs=pl.BlockSpec((tm, n_out), lambda i: (i, 0)),
        out_shape=jax.ShapeDtypeStruct((m, n_out), out_dtype),
        compiler_params=_cparams("parallel"),
        name="linear",
    )(x, w.astype(BF16), bias.reshape(1, n).astype(F32))


def _linear_postnorm_kernel(h_ref, w_ref, b_ref, x_ref, g_ref, beta_ref, o_ref, *, alpha):
    h = _bdot(h_ref[...], w_ref[...]) + b_ref[...]
    o_ref[...] = _layer_norm(alpha * x_ref[...] + h, g_ref[...], beta_ref[...])


def _linear_postnorm(h_in, w, bias, x_res, g, beta, *, alpha, tm=512):
    m, k = h_in.shape
    d = w.shape[1]
    if bias is None:
        bias = jnp.zeros((d,), F32)
    tm = _row_tile(m, tm)
    row = lambda a: a.reshape(1, d).astype(F32)
    return pl.pallas_call(
        functools.partial(_linear_postnorm_kernel, alpha=alpha),
        grid=(m // tm,),
        in_specs=[pl.BlockSpec((tm, k), lambda i: (i, 0)),
                  pl.BlockSpec((k, d), lambda i: (0, 0)),
                  pl.BlockSpec((1, d), lambda i: (0, 0)),
                  pl.BlockSpec((tm, d), lambda i: (i, 0)),
                  pl.BlockSpec((1, d), lambda i: (0, 0)),
                  pl.BlockSpec((1, d), lambda i: (0, 0))],
        out_specs=pl.BlockSpec((tm, d), lambda i: (i, 0)),
        out_shape=jax.ShapeDtypeStruct((m, d), F32),
        compiler_params=_cparams("parallel"),
        name="linear_postnorm",
    )(h_in, w.astype(BF16), row(bias), x_res, row(g), row(beta))


def _ab_in_kernel(x_ref, wqkvz_ref, wba_ref, wu_ref, qkv_ref, z_ref, ba_ref, u_ref, *, n_qkv):
    x = x_ref[...].astype(BF16)
    y = jnp.dot(x, wqkvz_ref[...], preferred_element_type=F32)
    qkv_ref[...] = y[:, :n_qkv]
    z_ref[...] = y[:, n_qkv:]
    ba_ref[...] = jnp.dot(x, wba_ref[...], preferred_element_type=F32)
    u_ref[...] = jnp.dot(x, wu_ref[...], preferred_element_type=F32)


def _ab_in_proj(x, w_in, *, tm=512):
    bx, tx, d = x.shape
    n_key = DN_HEADS * DN_DK
    n_val = DN_HEADS * DN_DV
    n_qkv = 2 * n_key + n_val
    off_beta = n_qkv + n_val
    off_u = off_beta + 2 * DN_HEADS
    n_u = w_in.shape[1] - off_u
    w_qkvz = w_in[:, :off_beta].astype(BF16)
    w_ba = jnp.pad(w_in[:, off_beta:off_u], ((0, 0), (0, V7X_LANES - 2 * DN_HEADS))).astype(BF16)
    w_u = w_in[:, off_u:].astype(BF16)
    tm = _row_tile(tx, tm)
    full = lambda a: pl.BlockSpec(a.shape, lambda b, i: (0, 0))
    return pl.pallas_call(
        functools.partial(_ab_in_kernel, n_qkv=n_qkv),
        grid=(bx, tx // tm),
        in_specs=[pl.BlockSpec((None, tm, d), lambda b, i: (b, i, 0)),
                  full(w_qkvz), full(w_ba), full(w_u)],
        out_specs=[pl.BlockSpec((None, tm, n_qkv), lambda b, i: (b, i, 0)),
                   pl.BlockSpec((None, tm, n_val), lambda b, i: (b, i, 0)),
                   pl.BlockSpec((None, tm, V7X_LANES), lambda b, i: (b, i, 0)),
                   pl.BlockSpec((tm, n_u), lambda b, i: (i, b))],
        out_shape=[jax.ShapeDtypeStruct((bx, tx, n_qkv), F32),
                   jax.ShapeDtypeStruct((bx, tx, n_val), F32),
                   jax.ShapeDtypeStruct((bx, tx, V7X_LANES), F32),
                   jax.ShapeDtypeStruct((tx, bx * n_u), F32)],
        compiler_params=_cparams("parallel", "parallel"),
        name="ab_in_proj",
    )(x, w_qkvz, w_ba, w_u)


def _ab_out_kernel(o_ref, y_ref, wt_ref, wb_ref, x_ref, g_ref, beta_ref, out_ref, *, alpha):
    h = _bdot(o_ref[...], wt_ref[...]) + _bdot(y_ref[...], wb_ref[...])
    out_ref[...] = _layer_norm(alpha * x_ref[...] + h, g_ref[...], beta_ref[...])


def _ab_out_proj(o, y_tm, w_out, x, g, beta, *, alpha, tm=512):
    bx, tx, d = x.shape
    n_o = o.shape[-1]
    n_y = y_tm.shape[1] // bx
    tm = _row_tile(tx, tm)
    wt = w_out[:n_o].astype(BF16)
    wb = w_out[n_o:].astype(BF16)
    row = lambda a: a.reshape(1, d).astype(F32)
    full = lambda a: pl.BlockSpec(a.shape, lambda b, i: (0, 0))
    return pl.pallas_call(
        functools.partial(_ab_out_kernel, alpha=alpha),
        grid=(bx, tx // tm),
        in_specs=[pl.BlockSpec((None, tm, n_o), lambda b, i: (b, i, 0)),
                  pl.BlockSpec((tm, n_y), lambda b, i: (i, b)),
                  full(wt), full(wb),
                  pl.BlockSpec((None, tm, d), lambda b, i: (b, i, 0)),
                  pl.BlockSpec((1, d), lambda b, i: (0, 0)),
                  pl.BlockSpec((1, d), lambda b, i: (0, 0))],
        out_specs=pl.BlockSpec((None, tm, d), lambda b, i: (b, i, 0)),
        out_shape=jax.ShapeDtypeStruct((bx, tx, d), F32),
        compiler_params=_cparams("parallel", "parallel"),
        name="ab_out_proj",
    )(o, y_tm, wt, wb, x, row(g), row(beta))


_DN_HALO = V7X_SUBLANES


def _dn_seq_kernel(qkv_ref, ba_ref, z_ref, cbuf_ref, s0_ref, cw_ref, alog_ref, dtb_ref, ng_ref,
                   o_ref, s_out_ref, cbuf_out_ref, xp_ref, s_ref, *, c, bb):
    j = pl.program_id(1)
    nj = pl.num_programs(1)
    for bi in range(bb):
        _dn_chunk(qkv_ref.at[bi], ba_ref.at[bi], z_ref.at[bi], cbuf_ref.at[bi], s0_ref.at[bi], cw_ref, alog_ref,
                  dtb_ref, ng_ref, o_ref.at[bi], s_out_ref.at[bi], cbuf_out_ref.at[bi], xp_ref.at[bi],
                  s_ref.at[bi], j=j, nj=nj, c=c)


def _dn_chunk(qkv_ref, ba_ref, z_ref, cbuf_ref, s0_ref, cw_ref, alog_ref, dtb_ref, ng_ref,
              o_ref, s_out_ref, cbuf_out_ref, xp_ref, s_ref, *, j, nj, c):
    hist = DN_CONV - 1
    n_key = DN_HEADS * DN_DK

    @pl.when(j == 0)
    def _():
        xp_ref[0:_DN_HALO - hist, :] = jnp.zeros((_DN_HALO - hist, xp_ref.shape[-1]), F32)
        xp_ref[_DN_HALO - hist:_DN_HALO, :] = cbuf_ref[...]
        s_ref[...] = s0_ref[...]

    xp_ref[_DN_HALO:_DN_HALO + c, :] = qkv_ref[...]
    y = xp_ref[pl.ds(_DN_HALO - hist, c), :] * cw_ref[0:1, :]
    for tap in range(1, DN_CONV):
        y = y + xp_ref[pl.ds(_DN_HALO - hist + tap, c), :] * cw_ref[tap:tap + 1, :]
    new_tail = xp_ref[pl.ds(_DN_HALO + c - hist, hist), :]
    xp_ref[_DN_HALO - hist:_DN_HALO, :] = new_tail
    y = _silu(y)

    ii = lax.broadcasted_iota(jnp.int32, (c, c), 0)
    jj = lax.broadcasted_iota(jnp.int32, (c, c), 1)
    incl = ii >= jj
    strict = ii > jj
    tril = jnp.where(incl, 1.0, 0.0).astype(BF16)
    eye = jnp.where(ii == jj, 1.0, 0.0)
    ba = ba_ref[...]
    z = z_ref[...]
    n_double = int(math.log2(c))

    for h in range(DN_HEADS):
        lo, hi = h * DN_DK, (h + 1) * DN_DK
        qh = y[:, lo:hi]
        kh = y[:, n_key + lo:n_key + hi]
        vh = y[:, 2 * n_key + h * DN_DV:2 * n_key + (h + 1) * DN_DV]
        qh = qh * lax.rsqrt(jnp.sum(qh * qh, -1, keepdims=True) + RMS_EPS) * (DN_DK ** -0.5)
        kh = kh * lax.rsqrt(jnp.sum(kh * kh, -1, keepdims=True) + RMS_EPS)
        beta = _sigmoid(ba[:, h:h + 1])
        a_logit = ba[:, DN_HEADS + h:DN_HEADS + h + 1]
        g = -jnp.exp(alog_ref[0:1, lo:hi]) * _softplus(a_logit + dtb_ref[0:1, lo:hi])
        g_hi, g_mid, g_lo = _split3(g)
        gc = (jnp.dot(tril, g_hi, preferred_element_type=F32)
              + jnp.dot(tril, g_mid, preferred_element_type=F32)
              + jnp.dot(tril, g_lo, preferred_element_type=F32))
        gc_row = jnp.transpose(gc)[0:1, :]
        diff = gc[:, :c] - gc_row
        decay = jnp.where(incl, jnp.exp(jnp.where(incl, diff, 0.0)), 0.0)
        eg = jnp.exp(gc)
        gc_last = gc[c - 1:c, :]
        kb = kh * beta
        lmat = jnp.where(strict, _bdot_nt(kb, kh) * decay, 0.0)
        tmat = eye - lmat
        lpow = _bdot(lmat, lmat)
        for step in range(1, n_double):
            tmat = tmat + _bdot(tmat, lpow)
            if step < n_double - 1:
                lpow = _bdot(lpow, lpow)
        uw = _bdot(tmat, jnp.concatenate([vh * beta, kb * eg], axis=1))
        u = uw[:, :DN_DV]
        w = uw[:, DN_DV:]
        qk = _bdot_nt(qh, kh) * decay
        qg = qh * eg
        kdec = kh * jnp.exp(gc_last - gc)
        glast = jnp.exp(gc_last)
        s = s_ref[h]
        v_new = u - _bdot(w, s)
        o = _bdot(qg, s) + _bdot(qk, v_new)
        s_ref[h] = s * glast + _bdot(jnp.transpose(kdec), v_new)
        o = o * lax.rsqrt(jnp.mean(o * o, -1, keepdims=True) + RMS_EPS) * ng_ref[...]
        o_ref[:, h * DN_DV:(h + 1) * DN_DV] = o * _silu(z[:, h * DN_DV:(h + 1) * DN_DV])

    @pl.when(j == nj - 1)
    def _():
        s_out_ref[...] = s_ref[...]
        cbuf_out_ref[...] = new_tail


def _dn_step_kernel(qkv_ref, ba_ref, z_ref, cbuf_ref, s0_ref, cw_ref, alog_ref, dtb_ref, ng_ref,
                    o_ref, s_out_ref, cbuf_out_ref, *, bb):
    hist = DN_CONV - 1
    n_key = DN_HEADS * DN_DK
    x = qkv_ref[...]
    y = x * cw_ref[hist:hist + 1, :]
    for tap in range(hist):
        y = y + cbuf_ref[:, tap, :] * cw_ref[tap:tap + 1, :]
    for tap in range(1, hist):
        cbuf_out_ref[:, tap - 1, :] = cbuf_ref[:, tap, :]
    cbuf_out_ref[:, hist - 1, :] = x
    y = _silu(y)
    ba = ba_ref[...]
    z = z_ref[...]
    row8 = lax.broadcasted_iota(jnp.int32, (V7X_SUBLANES, DN_DK), 0)
    row16 = lax.broadcasted_iota(jnp.int32, (V7X_BF16_SUBLANES, DN_DK), 0)
    for h in range(DN_HEADS):
        lo, hi = h * DN_DK, (h + 1) * DN_DK
        qh = y[:, lo:hi]
        kh = y[:, n_key + lo:n_key + hi]
        vh = y[:, 2 * n_key + h * DN_DV:2 * n_key + (h + 1) * DN_DV]
        qh = qh * lax.rsqrt(jnp.sum(qh * qh, -1, keepdims=True) + RMS_EPS) * (DN_DK ** -0.5)
        kh = kh * lax.rsqrt(jnp.sum(kh * kh, -1, keepdims=True) + RMS_EPS)
        beta = _sigmoid(ba[:, h:h + 1])
        a_logit = ba[:, DN_HEADS + h:DN_HEADS + h + 1]
        g = -jnp.exp(alog_ref[0:1, lo:hi]) * _softplus(a_logit + dtb_ref[0:1, lo:hi])
        eg = jnp.exp(g)
        w = kh * beta * eg
        qg = qh * eg
        u = vh * beta
        qk = jnp.sum(qh * kh, -1, keepdims=True)
        outs = []
        for r in range(bb):
            s = s0_ref[r, h]
            lhs = jnp.where(row8 == 0, w[r:r + 1], jnp.where(row8 == 1, qg[r:r + 1], 0.0))
            ws_qs = _bdot(lhs, s)
            v_new = u[r:r + 1] - ws_qs[0:1]
            outs.append(ws_qs[1:2] + qk[r:r + 1] * v_new)
            k16 = jnp.where(row16 == 0, kh[r:r + 1], 0.0)
            v16 = jnp.where(row16 == 0, v_new, 0.0)
            s_out_ref[r, h] = s * eg[r:r + 1] + _bdot(jnp.transpose(k16), v16)
        o = jnp.concatenate(outs, axis=0)
        o = o * lax.rsqrt(jnp.mean(o * o, -1, keepdims=True) + RMS_EPS) * ng_ref[...]
        o_ref[:, h * DN_DV:(h + 1) * DN_DV] = o * _silu(z[:, h * DN_DV:(h + 1) * DN_DV])


def _deltanet(qkv, ba, z, conv_buf, s0, conv_w, a_log, dt_bias, norm_g):
    b, t, n_qkv = qkv.shape
    n_val = DN_HEADS * DN_DV
    hist = DN_CONV - 1
    rep = lambda a: jnp.repeat(a.astype(F32), DN_DK).reshape(1, DN_HEADS * DN_DK)
    cw = conv_w.astype(F32)
    consts = (cw, rep(a_log), rep(dt_bias), norm_g.reshape(1, DN_DV).astype(F32))
    state_shape = (DN_HEADS, DN_DK, DN_DV)
    out_shape = [jax.ShapeDtypeStruct((b, t, n_val), F32),
                 jax.ShapeDtypeStruct((b,) + state_shape, F32),
                 jax.ShapeDtypeStruct((b, hist, n_qkv), F32)]
    if t == 1:
        bb = _row_tile(b, V7X_SUBLANES)
        const = lambda a: pl.BlockSpec(a.shape, lambda i: (0,) * a.ndim)
        o, s_new, cbuf_new = pl.pallas_call(
            functools.partial(_dn_step_kernel, bb=bb),
            grid=(b // bb,),
            in_specs=[pl.BlockSpec((bb, n_qkv), lambda i: (i, 0)),
                      pl.BlockSpec((bb, V7X_LANES), lambda i: (i, 0)),
                      pl.BlockSpec((bb, n_val), lambda i: (i, 0)),
                      pl.BlockSpec((bb, hist, n_qkv), lambda i: (i, 0, 0)),
                      pl.BlockSpec((bb,) + state_shape, lambda i: (i, 0, 0, 0))] + [const(a) for a in consts],
            out_specs=[pl.BlockSpec((bb, n_val), lambda i: (i, 0)),
                       pl.BlockSpec((bb,) + state_shape, lambda i: (i, 0, 0, 0)),
                       pl.BlockSpec((bb, hist, n_qkv), lambda i: (i, 0, 0))],
            out_shape=[jax.ShapeDtypeStruct((b, n_val), F32)] + out_shape[1:],
            compiler_params=_cparams("parallel"),
            name="deltanet_step",
        )(qkv.reshape(b, n_qkv), ba.reshape(b, V7X_LANES), z.reshape(b, n_val), conv_buf.astype(F32),
          s0.astype(F32), *consts)
        return o.reshape(b, 1, n_val), s_new, cbuf_new
    c = DN_CHUNK
    assert t % c == 0 and c >= hist
    bb = _row_tile(b, 4)
    const = lambda a: pl.BlockSpec(a.shape, lambda i, j: (0,) * a.ndim)
    return pl.pallas_call(
        functools.partial(_dn_seq_kernel, c=c, bb=bb),
        grid=(b // bb, t // c),
        in_specs=[pl.BlockSpec((bb, c, n_qkv), lambda i, j: (i, j, 0)),
                  pl.BlockSpec((bb, c, V7X_LANES), lambda i, j: (i, j, 0)),
                  pl.BlockSpec((bb, c, n_val), lambda i, j: (i, j, 0)),
                  pl.BlockSpec((bb, hist, n_qkv), lambda i, j: (i, 0, 0)),
                  pl.BlockSpec((bb,) + state_shape, lambda i, j: (i, 0, 0, 0))] + [const(a) for a in consts],
        out_specs=[pl.BlockSpec((bb, c, n_val), lambda i, j: (i, j, 0)),
                   pl.BlockSpec((bb,) + state_shape, lambda i, j: (i, 0, 0, 0)),
                   pl.BlockSpec((bb, hist, n_qkv), lambda i, j: (i, 0, 0))],
        out_shape=out_shape,
        scratch_shapes=[pltpu.VMEM((bb, _DN_HALO + c, n_qkv), F32),
                        pltpu.VMEM((bb,) + state_shape, F32)],
        compiler_params=_cparams("parallel", "arbitrary"),
        name="deltanet_seq",
    )(qkv, ba, z, conv_buf.astype(F32), s0.astype(F32), *consts)


def _s5_param_kernel(lre_ref, lim_ref, ldt_ref, lbre_ref, lbim_ref, fre_ref, fim_ref):
    lam_re = lre_ref[...]
    lam_im = lim_ref[...]
    dt = jnp.exp(ldt_ref[...])
    mag = jnp.exp(lam_re * dt)
    ang = lam_im * dt
    lb_re = mag * jnp.cos(ang)
    lb_im = mag * jnp.sin(ang)
    den = lam_re * lam_re + lam_im * lam_im
    lbre_ref[...] = lb_re
    lbim_ref[...] = lb_im
    fre_ref[...] = ((lb_re - 1.0) * lam_re + lb_im * lam_im) / den
    fim_ref[...] = (lb_im * lam_re - (lb_re - 1.0) * lam_im) / den


def _s5_discretize(lam_re, lam_im, log_dt):
    g, n = lam_re.shape
    ldt = jnp.broadcast_to(log_dt.astype(F32)[:, None], (g, n))
    shp = jax.ShapeDtypeStruct((g, n), F32)
    return pl.pallas_call(_s5_param_kernel, out_shape=[shp] * 4, name="s5_discretize")(
        lam_re.astype(F32), lam_im.astype(F32), ldt)


def _s5_kernel(u_ref, h0re_ref, h0im_ref, lbre_ref, lbim_ref, bre_ref, bim_ref, cre_ref, cim_ref,
               d_ref, wglu_ref, bglu_ref, y_ref, hre_out_ref, him_out_ref,
               sre_ref, sim_ref, cre_s, cim_s, *, tb, bb, lane_chunk):
    j = pl.program_id(1)
    nj = pl.num_programs(1)
    n_ch = u_ref.shape[-1]
    n_st = sre_ref.shape[-1]
    halves = bre_ref.shape[0]
    ch_h = n_ch // halves
    st_h = n_st // halves

    @pl.when(j == 0)
    def _():
        cre_s[...] = h0re_ref[...]
        cim_s[...] = h0im_ref[...]

    u = u_ref[...].reshape(tb * bb, n_ch)
    ub = u.astype(BF16)
    for hf in range(halves):
        uh = ub[:, hf * ch_h:(hf + 1) * ch_h]
        sre_ref[:, hf * st_h:(hf + 1) * st_h] = jnp.dot(uh, bre_ref[hf], preferred_element_type=F32)
        sim_ref[:, hf * st_h:(hf + 1) * st_h] = jnp.dot(uh, bim_ref[hf], preferred_element_type=F32)

    for c0 in range(0, n_st, lane_chunk):
        cs = slice(c0, c0 + lane_chunk)
        lr = jnp.broadcast_to(lbre_ref[0:1, cs], (bb, lane_chunk))
        li = jnp.broadcast_to(lbim_ref[0:1, cs], (bb, lane_chunk))

        def body(t, carry, cs=cs, lr=lr, li=li):
            hr, hi = carry
            r = pl.multiple_of(t * bb, bb)
            nr = lr * hr - li * hi + sre_ref[pl.ds(r, bb), cs]
            ni = lr * hi + li * hr + sim_ref[pl.ds(r, bb), cs]
            sre_ref[pl.ds(r, bb), cs] = nr
            sim_ref[pl.ds(r, bb), cs] = ni
            return nr, ni

        hr, hi = lax.fori_loop(0, tb, body, (cre_s[:, cs], cim_s[:, cs]))
        cre_s[:, cs] = hr
        cim_s[:, cs] = hi

    ys = []
    for hf in range(halves):
        hre = sre_ref[:, hf * st_h:(hf + 1) * st_h].astype(BF16)
        him = sim_ref[:, hf * st_h:(hf + 1) * st_h].astype(BF16)
        ys.append(jnp.dot(hre, cre_ref[hf], preferred_element_type=F32)
                  - jnp.dot(him, cim_ref[hf], preferred_element_type=F32))
    y = jnp.concatenate(ys, axis=1) + d_ref[...] * u
    y = jax.nn.gelu(y)
    y = y * _sigmoid(_bdot(y, wglu_ref[...]) + bglu_ref[...])
    y_ref[...] = y.reshape(tb, bb, n_ch)

    @pl.when(j == nj - 1)
    def _():
        hre_out_ref[...] = cre_s[...]
        him_out_ref[...] = cim_s[...]


def _s5(u_tm, h0_re, h0_im, lam_re, lam_im, log_dt, b_re, b_im, c_re, c_im, d_skip, w_glu, b_glu, *, halves=2):
    t, b, n_ch = u_tm.shape
    g, n, p = b_re.shape
    n_st = g * n
    lb_re, lb_im, f_re, f_im = _s5_discretize(lam_re, lam_im, log_dt)
    b_re = b_re.astype(F32)
    b_im = b_im.astype(F32)
    bb_re = f_re[..., None] * b_re - f_im[..., None] * b_im
    bb_im = f_re[..., None] * b_im + f_im[..., None] * b_re
    gh = g // halves

    def in_blocks(a):
        a = a.reshape(halves, gh, n, p)
        eye = jnp.eye(gh, dtype=F32)
        return jnp.einsum('hgnp,gk->hgpkn', a, eye).reshape(halves, gh * p, gh * n).astype(BF16)

    def out_blocks(a):
        a = a.astype(F32).reshape(halves, gh, p, n)
        eye = jnp.eye(gh, dtype=F32)
        return jnp.einsum('hgpn,gk->hgnkp', a, eye).reshape(halves, gh * n, gh * p).astype(BF16)

    bre_m, bim_m = in_blocks(bb_re), in_blocks(bb_im)
    cre_m, cim_m = out_blocks(c_re), out_blocks(c_im)
    bb = b if b <= 128 else 128
    assert b % bb == 0
    tb = _row_tile(t, max(1, 512 // bb))
    lane_chunk = max(V7X_LANES, min(n_st, 8192 // bb))
    const = lambda a: pl.BlockSpec(a.shape, lambda i, j: (0,) * a.ndim)
    row = lambda a, m: a.reshape(1, m).astype(F32)
    args = (u_tm, h0_re.reshape(b, n_st).astype(F32), h0_im.reshape(b, n_st).astype(F32),
            row(lb_re, n_st), row(lb_im, n_st), bre_m, bim_m, cre_m, cim_m,
            row(d_skip, n_ch), w_glu.astype(BF16), row(b_glu, n_ch))
    y, hre, him = pl.pallas_call(
        functools.partial(_s5_kernel, tb=tb, bb=bb, lane_chunk=lane_chunk),
        grid=(b // bb, t // tb),
        in_specs=[pl.BlockSpec((tb, bb, n_ch), lambda i, j: (j, i, 0)),
                  pl.BlockSpec((bb, n_st), lambda i, j: (i, 0)),
                  pl.BlockSpec((bb, n_st), lambda i, j: (i, 0))] + [const(a) for a in args[3:]],
        out_specs=[pl.BlockSpec((tb, bb, n_ch), lambda i, j: (j, i, 0)),
                   pl.BlockSpec((bb, n_st), lambda i, j: (i, 0)),
                   pl.BlockSpec((bb, n_st), lambda i, j: (i, 0))],
        out_shape=[jax.ShapeDtypeStruct((t, b, n_ch), F32),
                   jax.ShapeDtypeStruct((b, n_st), F32),
                   jax.ShapeDtypeStruct((b, n_st), F32)],
        scratch_shapes=[pltpu.VMEM((tb * bb, n_st), F32), pltpu.VMEM((tb * bb, n_st), F32),
                        pltpu.VMEM((bb, n_st), F32), pltpu.VMEM((bb, n_st), F32)],
        compiler_params=_cparams("parallel", "arbitrary"),
        name="s5",
    )(*args)
    return y, hre.reshape(b, g, n), him.reshape(b, g, n)


def _ab_layer(x, dn_s, dn_conv, ssm_re, ssm_im, p, i, ln_g, ln_b, *, alpha):
    b, t, d = x.shape
    xr = x if t > 1 else x.reshape(1, b, d)
    bx, tx, _ = xr.shape
    qkv, z, ba, u_tm = _ab_in_proj(xr, p['ab_w_in'][i])
    n_ch = u_tm.shape[1] // bx
    shp = lambda a: a.reshape(b, t, a.shape[-1])
    o, s_new, conv_new = _deltanet(shp(qkv), shp(ba), shp(z), dn_conv, dn_s, p['dn_conv_w'][i],
                                   p['dn_a_log'][i], p['dn_dt_bias'][i], p['dn_norm_g'][i])
    y_tm, h_re, h_im = _s5(u_tm.reshape(t, b, n_ch), ssm_re, ssm_im, p['ssm_lambda_re'][i],
                           p['ssm_lambda_im'][i], p['ssm_log_dt'][i], p['ssm_b_re'][i], p['ssm_b_im'][i],
                           p['ssm_c_re'][i], p['ssm_c_im'][i], p['ssm_d'][i], p['ssm_w_glu'][i],
                           p['ssm_b_glu'][i])
    x_new = _ab_out_proj(o.reshape(bx, tx, -1), y_tm.reshape(tx, bx * n_ch), p['ab_w_out'][i], xr,
                         ln_g, ln_b, alpha=alpha)
    return x_new.reshape(b, t, d), s_new, conv_new, h_re, h_im


_CC_HALO = 32
_CC_ROW_TILES = 4


def _cconv_seq_kernel(h_ref, buf_ref, w_ref, bdw_ref, g_ref, b_ref, o_ref, xp_ref, xs_ref, acc_ref, *, tt, width):
    j = pl.program_id(1)
    hist = width - 1
    d = h_ref.shape[-1]
    rows = V7X_SUBLANES

    @pl.when(j == 0)
    def _():
        xp_ref[0:_CC_HALO - hist, :] = jnp.zeros((_CC_HALO - hist, d), F32)
        xp_ref[_CC_HALO - hist:_CC_HALO, :] = buf_ref[...]

    xp_ref[_CC_HALO:_CC_HALO + tt, :] = h_ref[...]
    n_shift = xs_ref.shape[1]
    for s in range(1, rows):
        xs_ref[s - 1] = xp_ref[pl.ds(s, n_shift), :]
    base = _CC_HALO - hist
    for c0 in range(0, d, V7X_LANES):
        cs = slice(c0, c0 + V7X_LANES)
        taps = [jnp.broadcast_to(w_ref[k:k + 1, cs], (rows, V7X_LANES)) for k in range(width)]

        def window(r, k, cs=cs):
            a, s = divmod(base + k, rows)
            if s == 0:
                return xp_ref[pl.ds(r + a * rows, rows), cs]
            return xs_ref[s - 1, pl.ds(r + a * rows, rows), cs]

        def body(i, carry, taps=taps, window=window, cs=cs):
            for sub in range(_CC_ROW_TILES):
                r = pl.multiple_of((i * _CC_ROW_TILES + sub) * rows, rows)
                acc = window(r, 0) * taps[0]
                for k in range(1, width):
                    acc = acc + window(r, k) * taps[k]
                acc_ref[pl.ds(r, rows), cs] = acc
            return carry

        lax.fori_loop(0, tt // (rows * _CC_ROW_TILES), body, 0)
    xp_ref[0:_CC_HALO, :] = xp_ref[tt:tt + _CC_HALO, :]
    y = _layer_norm(acc_ref[...] + bdw_ref[...], g_ref[...], b_ref[...])
    o_ref[...] = _silu(y)


def _cconv_seq(h, buf, w_dw, b_dw, ln_g, ln_b, *, tt=256):
    b, t, d = h.shape
    width = w_dw.shape[0]
    tt = _row_tile(t, tt)
    assert tt >= _CC_HALO and width - 1 <= _CC_HALO
    row = lambda a: a.reshape(1, d).astype(F32)
    const = lambda a: pl.BlockSpec(a.shape, lambda i, j: (0, 0))
    args = (w_dw.astype(F32), row(b_dw), row(ln_g), row(ln_b))
    return pl.pallas_call(
        functools.partial(_cconv_seq_kernel, tt=tt, width=width),
        grid=(b, t // tt),
        in_specs=[pl.BlockSpec((None, tt, d), lambda i, j: (i, j, 0)),
                  pl.BlockSpec((None, width - 1, d), lambda i, j: (i, 0, 0))] + [const(a) for a in args],
        out_specs=pl.BlockSpec((None, tt, d), lambda i, j: (i, j, 0)),
        out_shape=jax.ShapeDtypeStruct((b, t, d), F32),
        scratch_shapes=[pltpu.VMEM((_CC_HALO + tt, d), F32),
                        pltpu.VMEM((V7X_SUBLANES - 1, _CC_HALO + tt - V7X_SUBLANES, d), F32),
                        pltpu.VMEM((tt, d), F32)],
        compiler_params=_cparams("parallel", "arbitrary"),
        name="cconv_seq",
    )(h, buf.astype(F32), *args)


def _cconv_step_kernel(h_ref, buf_ref, w_ref, bdw_ref, g_ref, b_ref, o_ref, *, width):
    hist = width - 1
    acc = jnp.sum(buf_ref[...] * w_ref[0:hist, :][None], axis=1) + h_ref[...] * w_ref[hist:width, :]
    o_ref[...] = _silu(_layer_norm(acc + bdw_ref[...], g_ref[...], b_ref[...]))


def _cconv_step(h, buf, w_dw, b_dw, ln_g, ln_b, *, bb=8):
    b, d = h.shape
    width = w_dw.shape[0]
    bb = _row_tile(b, bb)
    row = lambda a: a.reshape(1, d).astype(F32)
    const = lambda a: pl.BlockSpec(a.shape, lambda i: (0, 0))
    args = (w_dw.astype(F32), row(b_dw), row(ln_g), row(ln_b))
    return pl.pallas_call(
        functools.partial(_cconv_step_kernel, width=width),
        grid=(b // bb,),
        in_specs=[pl.BlockSpec((bb, d), lambda i: (i, 0)),
                  pl.BlockSpec((bb, width - 1, d), lambda i: (i, 0, 0))] + [const(a) for a in args],
        out_specs=pl.BlockSpec((bb, d), lambda i: (i, 0)),
        out_shape=jax.ShapeDtypeStruct((b, d), F32),
        compiler_params=_cparams("parallel"),
        name="cconv_step",
    )(h, buf.astype(F32), *args)


def _conv_layer(x, buf, p, i, ln_g, ln_b, *, alpha):
    b, t, d = x.shape
    x2 = x.reshape(b * t, d)
    h = _linear(x2, p['cc_w_pw1'][i], p['cc_b_pw1'][i], glu=True)
    args = (p['cc_w_dw'][i], p['cc_b_dw'][i], p['cc_ln_g'][i], p['cc_ln_b'][i])
    if t == 1:
        hc = _cconv_step(h, buf, *args)
    else:
        hc = _cconv_seq(h.reshape(b, t, d), buf, *args).reshape(b * t, d)
    new_buf = jnp.concatenate([buf.astype(F32), h.reshape(b, t, d)], axis=1)[:, t:]
    x_new = _linear_postnorm(hc, p['cc_w_pw2'][i], p['cc_b_pw2'][i], x2, ln_g, ln_b, alpha=alpha)
    return x_new.reshape(b, t, d), new_buf


def _mem_attn_seq_kernel(x_ref, wq_ref, k_ref, v_ref, wo_ref, g_ref, b_ref, o_ref, *, alpha, heads):
    x = x_ref[...]
    d = x.shape[-1]
    hd = d // heads
    q = _bdot(x, wq_ref[...]) * (hd ** -0.5)
    k = k_ref[...].astype(BF16)
    v = v_ref[...].astype(BF16)
    outs = []
    for h in range(heads):
        hs = slice(h * hd, (h + 1) * hd)
        s = _bdot_nt(q[:, hs], k[:, hs])
        s = s - jnp.max(s, axis=-1, keepdims=True)
        e = jnp.exp(s)
        a = e / jnp.sum(e, axis=-1, keepdims=True)
        outs.append(_bdot(a, v[:, hs]))
    o = jnp.concatenate(outs, axis=1)
    hres = _bdot(o, wo_ref[...])
    o_ref[...] = _layer_norm(alpha * x + hres, g_ref[...], b_ref[...])


def _mem_attn_seq(x, mk, mv, layer, wq, wo, ln_g, ln_b, *, alpha, tq=512):
    b, t, d = x.shape
    m = mk.shape[2]
    tq = _row_tile(t, tq)
    row = lambda a: a.reshape(1, d).astype(F32)
    const = lambda a: pl.BlockSpec(a.shape, lambda i, j: (0, 0))
    wqb, wob = wq.astype(BF16), wo.astype(BF16)
    return pl.pallas_call(
        functools.partial(_mem_attn_seq_kernel, alpha=alpha, heads=MEM_HEADS),
        grid=(b, t // tq),
        in_specs=[pl.BlockSpec((None, tq, d), lambda i, j: (i, j, 0)),
                  const(wqb),
                  pl.BlockSpec((None, None, m, d), lambda i, j: (layer, i, 0, 0)),
                  pl.BlockSpec((None, None, m, d), lambda i, j: (layer, i, 0, 0)),
                  const(wob), pl.BlockSpec((1, d), lambda i, j: (0, 0)),
                  pl.BlockSpec((1, d), lambda i, j: (0, 0))],
        out_specs=pl.BlockSpec((None, tq, d), lambda i, j: (i, j, 0)),
        out_shape=jax.ShapeDtypeStruct((b, t, d), F32),
        compiler_params=_cparams("parallel", "parallel"),
        name="mem_attn_seq",
    )(x, wqb, mk, mv, wob, row(ln_g), row(ln_b))


def _mem_attn_step_kernel(q_ref, k_ref, v_ref, o_ref, *, heads, bb):
    i = pl.program_id(0)
    d = q_ref.shape[-1]
    hd = d // heads
    for r in range(bb):
        q = q_ref[pl.ds(i * bb + r, 1), :] * (hd ** -0.5)
        prod = k_ref[r] * q
        v = v_ref[r]
        for h in range(heads):
            hs = slice(h * hd, (h + 1) * hd)
            s = jnp.sum(prod[:, hs], axis=-1, keepdims=True)
            s = s - jnp.max(s, axis=0, keepdims=True)
            e = jnp.exp(s)
            a = e / jnp.sum(e, axis=0, keepdims=True)
            o_ref[pl.ds(i * bb + r, 1), hs] = jnp.sum(a * v[:, hs], axis=0, keepdims=True)


def _mem_attn_step(q, mk_all, mv_all, layer, *, bb=4):
    b, d = q.shape
    m = mk_all.shape[2]
    bb = _row_tile(b, bb)
    return pl.pallas_call(
        functools.partial(_mem_attn_step_kernel, heads=MEM_HEADS, bb=bb),
        grid=(b // bb,),
        in_specs=[pl.BlockSpec((b, d), lambda i: (0, 0)),
                  pl.BlockSpec((None, bb, m, d), lambda i: (layer, i, 0, 0)),
                  pl.BlockSpec((None, bb, m, d), lambda i: (layer, i, 0, 0))],
        out_specs=pl.BlockSpec((b, d), lambda i: (0, 0)),
        out_shape=jax.ShapeDtypeStruct((b, d), F32),
        compiler_params=_cparams("arbitrary"),
        name="mem_attn_step",
    )(q, mk_all, mv_all)


def _mem_layer(x, mk_all, mv_all, layer, wq, wo, ln_g, ln_b, *, alpha):
    b, t, d = x.shape
    n_layers, _, m = mk_all.shape[:3]
    mk_all = mk_all.reshape(n_layers, b, m, d)
    mv_all = mv_all.reshape(n_layers, b, m, d)
    if t == 1:
        x2 = x.reshape(b, d)
        q = _linear(x2, wq)
        o = _mem_attn_step(q, mk_all, mv_all, layer)
        return _linear_postnorm(o, wo, None, x2, ln_g, ln_b, alpha=alpha).reshape(b, t, d)
    return _mem_attn_seq(x, mk_all, mv_all, layer, wq, wo, ln_g, ln_b, alpha=alpha)


_NEG = -1e30


def _router_kernel(x_ref, w_ref, b_ref, ids_ref, wts_ref, *, groups, per_group):
    x = x_ref[...]
    w = w_ref[...]
    xh = x.astype(BF16)
    xl = (x - xh.astype(F32)).astype(BF16)
    wh = w.astype(BF16)
    wl = (w - wh.astype(F32)).astype(BF16)
    logits = (jnp.dot(xh, wh, preferred_element_type=F32) + jnp.dot(xh, wl, preferred_element_type=F32)
              + jnp.dot(xl, wh, preferred_element_type=F32)) + b_ref[...]
    lane = lax.broadcasted_iota(jnp.int32, logits.shape, 1)
    n_exp = groups * per_group
    is_g = lane < groups
    gl = jnp.where(is_g, logits, _NEG)
    gmax = jnp.max(gl, axis=-1, keepdims=True)
    gsum = jnp.sum(jnp.where(is_g, jnp.exp(gl - gmax), 0.0), axis=-1, keepdims=True)
    gw = 1.0 / gsum
    gi = jnp.min(jnp.where(gl == gmax, lane, V7X_LANES), axis=-1, keepdims=True)
    lane_grp = (lane - groups) // per_group
    sel = (lane >= groups) & (lane < groups + n_exp) & (lane_grp == gi)
    el = jnp.where(sel, logits, _NEG)
    emax = jnp.max(el, axis=-1, keepdims=True)
    ee = jnp.where(sel, jnp.exp(el - emax), 0.0)
    ep = jnp.where(sel, ee / jnp.sum(ee, axis=-1, keepdims=True), -1.0)
    p1 = jnp.max(ep, axis=-1, keepdims=True)
    i1 = jnp.min(jnp.where(ep == p1, lane, V7X_LANES), axis=-1, keepdims=True)
    ep2 = jnp.where(lane == i1, -1.0, ep)
    p2 = jnp.max(ep2, axis=-1, keepdims=True)
    i2 = jnp.min(jnp.where(ep2 == p2, lane, V7X_LANES), axis=-1, keepdims=True)
    denom = p1 + p2
    ids_ref[...] = jnp.where(lane == 0, i1 - groups, jnp.where(lane == 1, i2 - groups, 0))
    wts_ref[...] = jnp.where(lane == 0, gw * p1 / denom, jnp.where(lane == 1, gw * p2 / denom, 0.0))


def _router(x, w_group, b_group, w_expert, b_expert, *, tm=512):
    n, d = x.shape
    groups = w_group.shape[1]
    n_exp = w_expert.shape[1]
    pad = V7X_LANES - groups - n_exp
    w = jnp.pad(jnp.concatenate([w_group, w_expert], axis=1).astype(F32), ((0, 0), (0, pad)))
    b = jnp.pad(jnp.concatenate([b_group, b_expert]).astype(F32), (0, pad)).reshape(1, V7X_LANES)
    tm = _row_tile(n, tm)
    return pl.pallas_call(
        functools.partial(_router_kernel, groups=groups, per_group=n_exp // groups),
        grid=(n // tm,),
        in_specs=[pl.BlockSpec((tm, d), lambda i: (i, 0)),
                  pl.BlockSpec((d, V7X_LANES), lambda i: (0, 0)),
                  pl.BlockSpec((1, V7X_LANES), lambda i: (0, 0))],
        out_specs=[pl.BlockSpec((tm, V7X_LANES), lambda i: (i, 0)),
                   pl.BlockSpec((tm, V7X_LANES), lambda i: (i, 0))],
        out_shape=[jax.ShapeDtypeStruct((n, V7X_LANES), jnp.int32),
                   jax.ShapeDtypeStruct((n, V7X_LANES), F32)],
        compiler_params=_cparams("parallel"),
        name="moe_router",
    )(x, w, b)


def _expert_ffn_kernel(te_ref, nv_ref, x_ref, wg_ref, wu_ref, wd_ref, o_ref):
    i = pl.program_id(0)

    @pl.when(i < nv_ref[0])
    def _():
        x = x_ref[...]
        hid = _silu(_bdot(x, wg_ref[...])) * _bdot(x, wu_ref[...])
        o_ref[...] = _bdot(hid, wd_ref[...])

    @pl.when(i >= nv_ref[0])
    def _():
        o_ref[...] = jnp.zeros_like(o_ref)


def _expert_ffn(x_sorted, tile_expert, n_valid, w_gate, w_up, w_down, layer, *, tm):
    r, d = x_sorted.shape
    f = w_gate.shape[-1]
    return pl.pallas_call(
        _expert_ffn_kernel,
        grid_spec=pltpu.PrefetchScalarGridSpec(
            num_scalar_prefetch=2,
            grid=(r // tm,),
            in_specs=[pl.BlockSpec((tm, d), lambda i, te, nv: (i, 0)),
                      pl.BlockSpec((None, None, d, f), lambda i, te, nv: (layer, te[i], 0, 0)),
                      pl.BlockSpec((None, None, d, f), lambda i, te, nv: (layer, te[i], 0, 0)),
                      pl.BlockSpec((None, None, f, d), lambda i, te, nv: (layer, te[i], 0, 0))],
            out_specs=pl.BlockSpec((tm, d), lambda i, te, nv: (i, 0)),
        ),
        out_shape=jax.ShapeDtypeStruct((r, d), F32),
        compiler_params=_cparams("arbitrary"),
        name="moe_expert_ffn",
    )(tile_expert, n_valid, x_sorted, w_gate, w_up, w_down)


def _rank_kernel(ids_ref, rank_ref, cnt_ref, carry_ref):
    i = pl.program_id(0)
    tm = ids_ref.shape[0]

    @pl.when(i == 0)
    def _():
        carry_ref[...] = jnp.zeros_like(carry_ref)

    ids = ids_ref[...]
    lane = lax.broadcasted_iota(jnp.int32, ids.shape, 1)
    oh0 = jnp.where(lane == ids[:, 0:1], 1.0, 0.0)
    oh1 = jnp.where(lane == ids[:, 1:2], 1.0, 0.0)
    both = oh0 + oh1
    ii = lax.broadcasted_iota(jnp.int32, (tm, tm), 0)
    jj = lax.broadcasted_iota(jnp.int32, (tm, tm), 1)
    earlier = jnp.where(ii > jj, 1.0, 0.0).astype(BF16)
    prefix = jnp.dot(earlier, both.astype(BF16), preferred_element_type=F32) + carry_ref[...]
    r0 = jnp.sum(oh0 * prefix, axis=-1, keepdims=True)
    r1 = jnp.sum(oh1 * prefix, axis=-1, keepdims=True)
    rank_ref[...] = jnp.where(lane == 0, r0, jnp.where(lane == 1, r1, 0.0)).astype(jnp.int32)
    total = carry_ref[...] + jnp.sum(both, axis=0, keepdims=True)
    carry_ref[...] = total
    cnt_ref[...] = total.astype(jnp.int32)


def _expert_ranks(ids, *, tm=512):
    n = ids.shape[0]
    tm = _row_tile(n, tm)
    return pl.pallas_call(
        _rank_kernel,
        grid=(n // tm,),
        in_specs=[pl.BlockSpec((tm, V7X_LANES), lambda i: (i, 0))],
        out_specs=[pl.BlockSpec((tm, V7X_LANES), lambda i: (i, 0)),
                   pl.BlockSpec((1, V7X_LANES), lambda i: (0, 0))],
        out_shape=[jax.ShapeDtypeStruct((n, V7X_LANES), jnp.int32),
                   jax.ShapeDtypeStruct((1, V7X_LANES), jnp.int32)],
        scratch_shapes=[pltpu.VMEM((1, V7X_LANES), F32)],
        compiler_params=_cparams("arbitrary"),
        name="moe_rank",
    )(ids)


def _dispatch_kernel(slot_ref, x_ref, xs_init_ref, xs_ref, sem):
    del xs_init_ref
    tm = x_ref.shape[0]

    def body(r, carry):
        for k in range(MOE_TOP_K):
            s = slot_ref[0, 0, MOE_TOP_K * r + k]
            pltpu.make_async_copy(x_ref.at[pl.ds(r, 1), :], xs_ref.at[pl.ds(s, 1), :], sem).start()
        return carry

    lax.fori_loop(0, tm, body, 0, unroll=8)
    for k in range(MOE_TOP_K):
        pltpu.make_async_copy(x_ref, xs_ref.at[pl.ds(0, tm), :], sem).wait()


def _dispatch(x, slot, rows, *, tm):
    n, d = x.shape
    slot3 = slot.reshape(n // tm, 1, MOE_TOP_K * tm)
    return pl.pallas_call(
        _dispatch_kernel,
        grid=(n // tm,),
        in_specs=[pl.BlockSpec((1, 1, MOE_TOP_K * tm), lambda i: (i, 0, 0), memory_space=pltpu.SMEM),
                  pl.BlockSpec((tm, d), lambda i: (i, 0)),
                  pl.BlockSpec(memory_space=pl.ANY)],
        out_specs=pl.BlockSpec(memory_space=pl.ANY),
        out_shape=jax.ShapeDtypeStruct((rows, d), F32),
        scratch_shapes=[pltpu.SemaphoreType.DMA(())],
        input_output_aliases={2: 0},
        compiler_params=_cparams("arbitrary"),
        name="moe_dispatch",
    )(slot3, x, jnp.zeros((rows, d), F32))


def _combine_kernel(slot_ref, slot_next_ref, wts_ref, x_ref, g_ref, b_ref, y_hbm, o_ref, ybuf, sem, *, alpha):
    i = pl.program_id(0)
    n = pl.num_programs(0)
    tm = x_ref.shape[0]
    cur = i % 2

    def issue(idx_ref, buf):
        def body(r, carry):
            for k in range(MOE_TOP_K):
                s = idx_ref[0, 0, MOE_TOP_K * r + k]
                pltpu.make_async_copy(y_hbm.at[pl.ds(s, 1), :], ybuf.at[buf, k, pl.ds(r, 1), :],
                                      sem.at[buf]).start()
            return carry

        lax.fori_loop(0, tm, body, 0, unroll=8)

    @pl.when(i == 0)
    def _():
        issue(slot_ref, 0)

    @pl.when(i + 1 < n)
    def _():
        issue(slot_next_ref, 1 - cur)

    for k in range(MOE_TOP_K):
        pltpu.make_async_copy(y_hbm.at[pl.ds(0, tm), :], ybuf.at[cur, k], sem.at[cur]).wait()
    wts = wts_ref[...]
    y = wts[:, 0:1] * ybuf[cur, 0] + wts[:, 1:2] * ybuf[cur, 1]
    o_ref[...] = _layer_norm(alpha * x_ref[...] + y, g_ref[...], b_ref[...])


def _combine_postnorm(y_sorted, slot, wts, x, g, b, *, alpha, tm):
    n, d = x.shape
    nt = n // tm
    slot3 = slot.reshape(nt, 1, MOE_TOP_K * tm)
    row = lambda a: a.reshape(1, d).astype(F32)
    smem = lambda f: pl.BlockSpec((1, 1, MOE_TOP_K * tm), f, memory_space=pltpu.SMEM)
    return pl.pallas_call(
        functools.partial(_combine_kernel, alpha=alpha),
        grid=(nt,),
        in_specs=[smem(lambda i: (i, 0, 0)),
                  smem(lambda i: (jnp.minimum(i + 1, nt - 1), 0, 0)),
                  pl.BlockSpec((tm, V7X_LANES), lambda i: (i, 0)),
                  pl.BlockSpec((tm, d), lambda i: (i, 0)),
                  pl.BlockSpec((1, d), lambda i: (0, 0)),
                  pl.BlockSpec((1, d), lambda i: (0, 0)),
                  pl.BlockSpec(memory_space=pl.ANY)],
        out_specs=pl.BlockSpec((tm, d), lambda i: (i, 0)),
        out_shape=jax.ShapeDtypeStruct((n, d), F32),
        scratch_shapes=[pltpu.VMEM((2, MOE_TOP_K, tm, d), F32), pltpu.SemaphoreType.DMA((2,))],
        compiler_params=_cparams("arbitrary"),
        name="moe_combine",
    )(slot3, slot3, wts, x, row(g), row(b), y_sorted)


def _moe_layer(x, p, layer, ln_g, ln_b, *, alpha):
    b, t, d = x.shape
    n = b * t
    x2 = x.reshape(n, d)
    w_gate, w_up, w_down = p['moe_w_gate'], p['moe_w_up'], p['moe_w_down']
    n_exp = w_gate.shape[1]
    ids, wts = _router(x2, p['moe_w_group'][layer], p['moe_b_group'][layer], p['moe_w_expert'][layer],
                       p['moe_b_expert'][layer])
    rank, counts = _expert_ranks(ids)
    tm = 256 if n >= 4096 else V7X_BF16_SUBLANES
    tm_tok = _row_tile(n, 256)
    counts = counts[0, :n_exp]
    padded = ((counts + tm - 1) // tm) * tm
    ends = jnp.cumsum(padded)
    starts = ends - padded
    n_tiles = (MOE_TOP_K * n + n_exp * (tm - 1)) // tm
    n_valid = (ends[-1] // tm).astype(jnp.int32)
    tile_start = jnp.minimum(jnp.arange(n_tiles, dtype=jnp.int32), n_valid - 1) * tm
    tile_expert = jnp.minimum(jnp.sum((ends[None, :] <= tile_start[:, None]).astype(jnp.int32), axis=1), n_exp - 1)
    choice = ids[:, :MOE_TOP_K]
    onehot = choice[:, :, None] == jnp.arange(n_exp, dtype=jnp.int32)[None, None, :]
    slot = jnp.sum(jnp.where(onehot, starts[None, None, :], 0), axis=-1) + rank[:, :MOE_TOP_K]
    x_sorted = _dispatch(x2, slot, n_tiles * tm, tm=tm_tok)
    y_sorted = _expert_ffn(x_sorted, tile_expert, n_valid.reshape(1), w_gate, w_up, w_down, layer, tm=tm)
    return _combine_postnorm(y_sorted, slot, wts, x2, ln_g, ln_b, alpha=alpha, tm=tm_tok).reshape(b, t, d)


def _trunk(x, dn_s, dn_conv, ssm_re, ssm_im, cconv, mem_k, mem_v, p):
    depth = p['ln_g'].shape[0]
    alpha = (2.0 * depth) ** 0.25
    out_s, out_conv, out_re, out_im, out_cc = [], [], [], [], []
    for layer in range(depth):
        i = layer // 2
        g, bta = p['ln_g'][layer], p['ln_b'][layer]
        if layer % 2 == 0:
            x, s_new, conv_new, re_new, im_new = _ab_layer(x, dn_s[i], dn_conv[i], ssm_re[i], ssm_im[i], p, i,
                                                           g[0], bta[0], alpha=alpha)
            out_s.append(s_new)
            out_conv.append(conv_new)
            out_re.append(re_new)
            out_im.append(im_new)
        else:
            x, cc_new = _conv_layer(x, cconv[i], p, i, g[0], bta[0], alpha=alpha)
            out_cc.append(cc_new)
        x = _mem_layer(x, mem_k, mem_v, layer, p['mem_wq'][layer], p['mem_wo'][layer], g[1], bta[1], alpha=alpha)
        x = _moe_layer(x, p, layer, g[2], bta[2], alpha=alpha)
    return x, jnp.stack(out_s), jnp.stack(out_conv), jnp.stack(out_re), jnp.stack(out_im), jnp.stack(out_cc)


def kernel(x_prompt, x_sample, state_dn_s, state_dn_conv, state_ssm_re, state_ssm_im, state_cconv,
           cache_mem_k, cache_mem_v, mem_prompt, ab_w_in, dn_conv_w, dn_a_log, dn_dt_bias, dn_norm_g,
           ssm_lambda_re, ssm_lambda_im, ssm_log_dt, ssm_b_re, ssm_b_im, ssm_c_re, ssm_c_im, ssm_d,
           ssm_w_glu, ssm_b_glu, ab_w_out, cc_w_pw1, cc_b_pw1, cc_w_dw, cc_b_dw, cc_ln_g, cc_ln_b,
           cc_w_pw2, cc_b_pw2, mem_wq, mem_wk, mem_wv, mem_wo, ln_g, ln_b, moe_w_group, moe_b_group,
           moe_w_expert, moe_b_expert, moe_w_gate, moe_w_up, moe_w_down):
    p = dict(ab_w_in=ab_w_in, dn_conv_w=dn_conv_w, dn_a_log=dn_a_log, dn_dt_bias=dn_dt_bias,
             dn_norm_g=dn_norm_g, ssm_lambda_re=ssm_lambda_re, ssm_lambda_im=ssm_lambda_im,
             ssm_log_dt=ssm_log_dt, ssm_b_re=ssm_b_re, ssm_b_im=ssm_b_im, ssm_c_re=ssm_c_re,
             ssm_c_im=ssm_c_im, ssm_d=ssm_d, ssm_w_glu=ssm_w_glu, ssm_b_glu=ssm_b_glu, ab_w_out=ab_w_out,
             cc_w_pw1=cc_w_pw1, cc_b_pw1=cc_b_pw1, cc_w_dw=cc_w_dw, cc_b_dw=cc_b_dw, cc_ln_g=cc_ln_g,
             cc_ln_b=cc_ln_b, cc_w_pw2=cc_w_pw2, cc_b_pw2=cc_b_pw2, mem_wq=mem_wq, mem_wo=mem_wo,
             ln_g=ln_g, ln_b=ln_b, moe_w_group=moe_w_group, moe_b_group=moe_b_group,
             moe_w_expert=moe_w_expert, moe_b_expert=moe_b_expert, moe_w_gate=moe_w_gate,
             moe_w_up=moe_w_up, moe_w_down=moe_w_down)
    depth = ln_g.shape[0]
    n_ab = state_dn_s.shape[0]
    n_cc = state_cconv.shape[0]
    bsz, _, d = x_prompt.shape
    n_mem = mem_prompt.shape[1]
    hd = d // MEM_HEADS
    z_dn_s = jnp.zeros((n_ab, bsz) + state_dn_s.shape[2:], F32)
    z_dn_conv = jnp.zeros((n_ab, bsz) + state_dn_conv.shape[2:], F32)
    z_ssm = jnp.zeros((n_ab, bsz) + state_ssm_re.shape[2:], F32)
    z_cconv = jnp.zeros((n_cc, bsz) + state_cconv.shape[2:], F32)
    mem2 = mem_prompt.reshape(bsz * n_mem, d)
    p_mem_k = jnp.stack([_linear(mem2, mem_wk[l]) for l in range(depth)]).reshape(depth, bsz, n_mem, MEM_HEADS, hd)
    p_mem_v = jnp.stack([_linear(mem2, mem_wv[l]) for l in range(depth)]).reshape(depth, bsz, n_mem, MEM_HEADS, hd)
    y_prompt, p_dn_s, p_dn_conv, p_ssm_re, p_ssm_im, p_cconv = _trunk(
        x_prompt, z_dn_s, z_dn_conv, z_ssm, z_ssm, z_cconv, p_mem_k, p_mem_v, p)
    y_sample, s_dn_s, s_dn_conv, s_ssm_re, s_ssm_im, s_cconv = _trunk(
        x_sample, state_dn_s, state_dn_conv, state_ssm_re, state_ssm_im, state_cconv, cache_mem_k, cache_mem_v, p)
    return (y_prompt, y_sample, p_dn_s, p_dn_conv, p_ssm_re, p_ssm_im, p_cconv, p_mem_k, p_mem_v,
            s_dn_s, s_dn_conv, s_ssm_re, s_ssm_im, s_cconv)
```

```python
import functools
import math

import jax
import jax.numpy as jnp
from jax import lax
from jax.experimental import pallas as pl
from jax.experimental.pallas import tpu as pltpu

F32 = jnp.float32
BF16 = jnp.bfloat16

DN_HEADS = 4
DN_DK = 128
DN_DV = 128
DN_CONV = 4
DN_CHUNK = 64
SSM_GROUP_CH = 16
MEM_HEADS = 4
MOE_GROUPS = 4
MOE_EXPERTS_PER_GROUP = 8
MOE_TOP_K = 2
LN_EPS = 1e-5
RMS_EPS = 1e-6

V7X_LANES = 128
V7X_SUBLANES = 8
V7X_BF16_SUBLANES = 16
V7X_VMEM_LIMIT_BYTES = 52 * 1024 * 1024


def _cparams(*sem):
    return pltpu.CompilerParams(dimension_semantics=sem, vmem_limit_bytes=V7X_VMEM_LIMIT_BYTES)


def _bdot(a, b):
    return jnp.dot(a.astype(BF16), b.astype(BF16), preferred_element_type=F32)


def _bdot_nt(a, b):
    return lax.dot_general(a.astype(BF16), b.astype(BF16), (((1,), (1,)), ((), ())),
                           preferred_element_type=F32)


def _split3(a):
    hi = a.astype(BF16)
    r1 = a - hi.astype(F32)
    mid = r1.astype(BF16)
    lo = (r1 - mid.astype(F32)).astype(BF16)
    return hi, mid, lo


def _sigmoid(x):
    return 1.0 / (1.0 + jnp.exp(-x))


def _silu(x):
    return x * _sigmoid(x)


def _softplus(x):
    return jnp.maximum(x, 0.0) + jnp.log(1.0 + jnp.exp(-jnp.abs(x)))


def _layer_norm(v, g, b):
    mu = jnp.mean(v, axis=-1, keepdims=True)
    d = v - mu
    var = jnp.mean(d * d, axis=-1, keepdims=True)
    return d * lax.rsqrt(var + LN_EPS) * g + b


def _row_tile(n, pref):
    t = min(n, pref)
    assert n % t == 0, (n, t)
    return t


def _linear_kernel(x_ref, w_ref, b_ref, o_ref, *, glu):
    y = _bdot(x_ref[...], w_ref[...]) + b_ref[...]
    if glu:
        n = y.shape[-1] // 2
        y = y[:, :n] * _sigmoid(y[:, n:])
    o_ref[...] = y.astype(o_ref.dtype)


def _linear(x, w, bias=None, *, glu=False, tm=512, out_dtype=F32):
    m, k = x.shape
    n = w.shape[1]
    if bias is None:
        bias = jnp.zeros((n,), F32)
    tm = _row_tile(m, tm)
    n_out = n // 2 if glu else n
    return pl.pallas_call(
        functools.partial(_linear_kernel, glu=glu),
        grid=(m // tm,),
        in_specs=[pl.BlockSpec((tm, k), lambda i: (i, 0)),
                  pl.BlockSpec((k, n), lambda i: (0, 0)),
                  pl.BlockSpec((1, n), lambda i: (0, 0))],
        out_specs=pl.BlockSpec((tm, n_out), lambda i: (i, 0)),
        out_shape=jax.ShapeDtypeStruct((m, n_out), out_dtype),
        compiler_params=_cparams("parallel"),
        name="linear",
    )(x, w.astype(BF16), bias.reshape(1, n).astype(F32))


def _linear_postnorm_kernel(h_ref, w_ref, b_ref, x_ref, g_ref, beta_ref, o_ref, *, alpha):
    h = _bdot(h_ref[...], w_ref[...]) + b_ref[...]
    o_ref[...] = _layer_norm(alpha * x_ref[...] + h, g_ref[...], beta_ref[...])


def _linear_postnorm(h_in, w, bias, x_res, g, beta, *, alpha, tm=512):
    m, k = h_in.shape
    d = w.shape[1]
    if bias is None:
        bias = jnp.zeros((d,), F32)
    tm = _row_tile(m, tm)
    row = lambda a: a.reshape(1, d).astype(F32)
    return pl.pallas_call(
        functools.partial(_linear_postnorm_kernel, alpha=alpha),
        grid=(m // tm,),
        in_specs=[pl.BlockSpec((tm, k), lambda i: (i, 0)),
                  pl.BlockSpec((k, d), lambda i: (0, 0)),
                  pl.BlockSpec((1, d), lambda i: (0, 0)),
                  pl.BlockSpec((tm, d), lambda i: (i, 0)),
                  pl.BlockSpec((1, d), lambda i: (0, 0)),
                  pl.BlockSpec((1, d), lambda i: (0, 0))],
        out_specs=pl.BlockSpec((tm, d), lambda i: (i, 0)),
        out_shape=jax.ShapeDtypeStruct((m, d), F32),
        compiler_params=_cparams("parallel"),
        name="linear_postnorm",
    )(h_in, w.astype(BF16), row(bias), x_res, row(g), row(beta))


def _ab_in_kernel(x_ref, wqkvz_ref, wba_ref, wu_ref, qkv_ref, z_ref, ba_ref, u_ref, *, n_qkv):
    x = x_ref[...].astype(BF16)
    y = jnp.dot(x, wqkvz_ref[...], preferred_element_type=F32)
    qkv_ref[...] = y[:, :n_qkv]
    z_ref[...] = y[:, n_qkv:]
    ba_ref[...] = jnp.dot(x, wba_ref[...], preferred_element_type=F32)
    u_ref[...] = jnp.dot(x, wu_ref[...], preferred_element_type=F32)


def _ab_in_proj(x, w_in, *, tm=512):
    bx, tx, d = x.shape
    n_key = DN_HEADS * DN_DK
    n_val = DN_HEADS * DN_DV
    n_qkv = 2 * n_key + n_val
    off_beta = n_qkv + n_val
    off_u = off_beta + 2 * DN_HEADS
    n_u = w_in.shape[1] - off_u
    w_qkvz = w_in[:, :off_beta].astype(BF16)
    w_ba = jnp.pad(w_in[:, off_beta:off_u], ((0, 0), (0, V7X_LANES - 2 * DN_HEADS))).astype(BF16)
    w_u = w_in[:, off_u:].astype(BF16)
    tm = _row_tile(tx, tm)
    full = lambda a: pl.BlockSpec(a.shape, lambda b, i: (0, 0))
    return pl.pallas_call(
        functools.partial(_ab_in_kernel, n_qkv=n_qkv),
        grid=(bx, tx // tm),
        in_specs=[pl.BlockSpec((None, tm, d), lambda b, i: (b, i, 0)),
                  full(w_qkvz), full(w_ba), full(w_u)],
        out_specs=[pl.BlockSpec((None, tm, n_qkv), lambda b, i: (b, i, 0)),
                   pl.BlockSpec((None, tm, n_val), lambda b, i: (b, i, 0)),
                   pl.BlockSpec((None, tm, V7X_LANES), lambda b, i: (b, i, 0)),
                   pl.BlockSpec((tm, n_u), lambda b, i: (i, b))],
        out_shape=[jax.ShapeDtypeStruct((bx, tx, n_qkv), F32),
                   jax.ShapeDtypeStruct((bx, tx, n_val), F32),
                   jax.ShapeDtypeStruct((bx, tx, V7X_LANES), F32),
                   jax.ShapeDtypeStruct((tx, bx * n_u), F32)],
        compiler_params=_cparams("parallel", "parallel"),
        name="ab_in_proj",
    )(x, w_qkvz, w_ba, w_u)


def _ab_out_kernel(o_ref, y_ref, wt_ref, wb_ref, x_ref, g_ref, beta_ref, out_ref, *, alpha):
    h = _bdot(o_ref[...], wt_ref[...]) + _bdot(y_ref[...], wb_ref[...])
    out_ref[...] = _layer_norm(alpha * x_ref[...] + h, g_ref[...], beta_ref[...])


def _ab_out_proj(o, y_tm, w_out, x, g, beta, *, alpha, tm=512):
    bx, tx, d = x.shape
    n_o = o.shape[-1]
    n_y = y_tm.shape[1] // bx
    tm = _row_tile(tx, tm)
    wt = w_out[:n_o].astype(BF16)
    wb = w_out[n_o:].astype(BF16)
    row = lambda a: a.reshape(1, d).astype(F32)
    full = lambda a: pl.BlockSpec(a.shape, lambda b, i: (0, 0))
    return pl.pallas_call(
        functools.partial(_ab_out_kernel, alpha=alpha),
        grid=(bx, tx // tm),
        in_specs=[pl.BlockSpec((None, tm, n_o), lambda b, i: (b, i, 0)),
                  pl.BlockSpec((tm, n_y), lambda b, i: (i, b)),
                  full(wt), full(wb),
                  pl.BlockSpec((None, tm, d), lambda b, i: (b, i, 0)),
                  pl.BlockSpec((1, d), lambda b, i: (0, 0)),
                  pl.BlockSpec((1, d), lambda b, i: (0, 0))],
        out_specs=pl.BlockSpec((None, tm, d), lambda b, i: (b, i, 0)),
        out_shape=jax.ShapeDtypeStruct((bx, tx, d), F32),
        compiler_params=_cparams("parallel", "parallel"),
        name="ab_out_proj",
    )(o, y_tm, wt, wb, x, row(g), row(beta))


_DN_HALO = V7X_SUBLANES


def _dn_seq_kernel(qkv_ref, ba_ref, z_ref, cbuf_ref, s0_ref, cw_ref, alog_ref, dtb_ref, ng_ref,
                   o_ref, s_out_ref, cbuf_out_ref, xp_ref, s_ref, *, c, bb):
    j = pl.program_id(1)
    nj = pl.num_programs(1)
    hist = DN_CONV - 1
    n_key = DN_HEADS * DN_DK
    units = [(bi, h) for bi in range(bb) for h in range(DN_HEADS)]
    each = lambda f: {u: f(u) for u in units}

    @pl.when(j == 0)
    def _():
        xp_ref[:, 0:_DN_HALO - hist, :] = jnp.zeros((bb, _DN_HALO - hist, xp_ref.shape[-1]), F32)
        xp_ref[:, _DN_HALO - hist:_DN_HALO, :] = cbuf_ref[...]
        s_ref[...] = s0_ref[...]

    ys, tails = [], []
    for bi in range(bb):
        xp_ref[bi, _DN_HALO:_DN_HALO + c, :] = qkv_ref[bi]
        y = xp_ref[bi, pl.ds(_DN_HALO - hist, c), :] * cw_ref[0:1, :]
        for tap in range(1, DN_CONV):
            y = y + xp_ref[bi, pl.ds(_DN_HALO - hist + tap, c), :] * cw_ref[tap:tap + 1, :]
        tail = xp_ref[bi, pl.ds(_DN_HALO + c - hist, hist), :]
        xp_ref[bi, _DN_HALO - hist:_DN_HALO, :] = tail
        tails.append(tail)
        ys.append(_silu(y))

    ii = lax.broadcasted_iota(jnp.int32, (c, c), 0)
    jj = lax.broadcasted_iota(jnp.int32, (c, c), 1)
    incl = ii >= jj
    strict = ii > jj
    tril = jnp.where(incl, 1.0, 0.0).astype(BF16)
    eye = jnp.where(ii == jj, 1.0, 0.0)
    n_double = int(math.log2(c))
    vsl = lambda h: slice(h * DN_DV, (h + 1) * DN_DV)

    def unit_inputs(u):
        bi, h = u
        lo, hi = h * DN_DK, (h + 1) * DN_DK
        y = ys[bi]
        qh = y[:, lo:hi]
        kh = y[:, n_key + lo:n_key + hi]
        vh = y[:, 2 * n_key + h * DN_DV:2 * n_key + (h + 1) * DN_DV]
        qh = qh * lax.rsqrt(jnp.sum(qh * qh, -1, keepdims=True) + RMS_EPS) * (DN_DK ** -0.5)
        kh = kh * lax.rsqrt(jnp.sum(kh * kh, -1, keepdims=True) + RMS_EPS)
        beta = _sigmoid(ba_ref[bi, :, h:h + 1])
        a_logit = ba_ref[bi, :, DN_HEADS + h:DN_HEADS + h + 1]
        g = -jnp.exp(alog_ref[0:1, lo:hi]) * _softplus(a_logit + dtb_ref[0:1, lo:hi])
        return qh, kh, vh, beta, g

    inp = each(unit_inputs)
    q = each(lambda u: inp[u][0])
    k = each(lambda u: inp[u][1])
    beta = each(lambda u: inp[u][3])
    parts = [p for u in units for p in _split3(inp[u][4])]
    gc_all = jnp.dot(tril, jnp.concatenate(parts, axis=1), preferred_element_type=F32)
    lanes = lambda i: slice(i * DN_DK, (i + 1) * DN_DK)
    gc = {u: gc_all[:, lanes(3 * i)] + gc_all[:, lanes(3 * i + 1)] + gc_all[:, lanes(3 * i + 2)]
          for i, u in enumerate(units)}

    def unit_decay(u):
        gc_row = jnp.transpose(gc[u])[0:1, :]
        diff = gc[u][:, :c] - gc_row
        return jnp.where(incl, jnp.exp(jnp.where(incl, diff, 0.0)), 0.0)

    decay = each(unit_decay)
    eg = each(lambda u: jnp.exp(gc[u]))
    kdec = each(lambda u: k[u] * jnp.exp(gc[u][c - 1:c, :] - gc[u]))
    kb = each(lambda u: k[u] * beta[u])
    akq = each(lambda u: _bdot_nt(jnp.concatenate([kb[u], q[u]], axis=0), k[u]))
    lmat = each(lambda u: jnp.where(strict, akq[u][:c] * decay[u], 0.0))
    qk = each(lambda u: akq[u][c:] * decay[u])
    tmat = each(lambda u: eye - lmat[u])
    lpow = each(lambda u: _bdot(lmat[u], lmat[u]))
    for step in range(1, n_double):
        if step < n_double - 1:
            prod = each(lambda u: _bdot(jnp.concatenate([lpow[u], tmat[u]], axis=0), lpow[u]))
            lpow = each(lambda u: prod[u][:c])
            tmat = each(lambda u: tmat[u] + prod[u][c:])
        else:
            prod = each(lambda u: _bdot(tmat[u], lpow[u]))
            tmat = each(lambda u: tmat[u] + prod[u])
    uw = each(lambda u: _bdot(tmat[u], jnp.concatenate([inp[u][2] * beta[u], kb[u] * eg[u]], axis=1)))
    s_old = each(lambda u: s_ref[u[0], u[1]])
    wq_s = each(lambda u: _bdot(jnp.concatenate([uw[u][:, DN_DV:], q[u] * eg[u]], axis=0), s_old[u]))
    v_new = each(lambda u: uw[u][:, :DN_DV] - wq_s[u][:c])
    o = each(lambda u: wq_s[u][c:] + _bdot(qk[u], v_new[u]))
    s_new = each(lambda u: s_old[u] * jnp.exp(gc[u][c - 1:c, :]) + _bdot(jnp.transpose(kdec[u]), v_new[u]))
    for u in units:
        bi, h = u
        s_ref[bi, h] = s_new[u]
        on = o[u] * lax.rsqrt(jnp.mean(o[u] * o[u], -1, keepdims=True) + RMS_EPS) * ng_ref[...]
        o_ref[bi, :, vsl(h)] = on * _silu(z_ref[bi, :, vsl(h)])

    @pl.when(j == nj - 1)
    def _():
        s_out_ref[...] = s_ref[...]
        for bi in range(bb):
            cbuf_out_ref[bi] = tails[bi]


def _dn_step_kernel(qkv_ref, ba_ref, z_ref, cbuf_ref, s0_ref, cw_ref, alog_ref, dtb_ref, ng_ref,
                    o_ref, s_out_ref, cbuf_out_ref, *, bb):
    hist = DN_CONV - 1
    n_key = DN_HEADS * DN_DK
    x = qkv_ref[...]
    y = x * cw_ref[hist:hist + 1, :]
    for tap in range(hist):
        y = y + cbuf_ref[:, tap, :] * cw_ref[tap:tap + 1, :]
    for tap in range(1, hist):
        cbuf_out_ref[:, tap - 1, :] = cbuf_ref[:, tap, :]
    cbuf_out_ref[:, hist - 1, :] = x
    y = _silu(y)
    ba = ba_ref[...]
    z = z_ref[...]
    row8 = lax.broadcasted_iota(jnp.int32, (V7X_SUBLANES, DN_DK), 0)
    row16 = lax.broadcasted_iota(jnp.int32, (V7X_BF16_SUBLANES, DN_DK), 0)
    heads = range(DN_HEADS)
    units = [(h, r) for h in heads for r in range(bb)]
    each = lambda f: {u: f(u) for u in units}
    one = lambda a, r: a[r:r + 1]

    def head_inputs(h):
        lo, hi = h * DN_DK, (h + 1) * DN_DK
        qh = y[:, lo:hi]
        kh = y[:, n_key + lo:n_key + hi]
        vh = y[:, 2 * n_key + h * DN_DV:2 * n_key + (h + 1) * DN_DV]
        qh = qh * lax.rsqrt(jnp.sum(qh * qh, -1, keepdims=True) + RMS_EPS) * (DN_DK ** -0.5)
        kh = kh * lax.rsqrt(jnp.sum(kh * kh, -1, keepdims=True) + RMS_EPS)
        beta = _sigmoid(ba[:, h:h + 1])
        a_logit = ba[:, DN_HEADS + h:DN_HEADS + h + 1]
        g = -jnp.exp(alog_ref[0:1, lo:hi]) * _softplus(a_logit + dtb_ref[0:1, lo:hi])
        eg = jnp.exp(g)
        return dict(k=kh, eg=eg, w=kh * beta * eg, qg=qh * eg, u=vh * beta, qk=jnp.sum(qh * kh, -1, keepdims=True))

    hd = [head_inputs(h) for h in heads]
    s_old = each(lambda u: s0_ref[u[1], u[0]])
    lhs = each(lambda u: jnp.where(row8 == 0, one(hd[u[0]]['w'], u[1]),
                                   jnp.where(row8 == 1, one(hd[u[0]]['qg'], u[1]), 0.0)))
    ws_qs = each(lambda u: _bdot(lhs[u], s_old[u]))
    v_new = each(lambda u: one(hd[u[0]]['u'], u[1]) - ws_qs[u][0:1])
    o_row = each(lambda u: ws_qs[u][1:2] + one(hd[u[0]]['qk'], u[1]) * v_new[u])
    k16 = each(lambda u: jnp.where(row16 == 0, one(hd[u[0]]['k'], u[1]), 0.0))
    v16 = each(lambda u: jnp.where(row16 == 0, v_new[u], 0.0))
    upd = each(lambda u: _bdot(jnp.transpose(k16[u]), v16[u]))
    for u in units:
        s_out_ref[u[1], u[0]] = s_old[u] * one(hd[u[0]]['eg'], u[1]) + upd[u]
    for h in heads:
        o = jnp.concatenate([o_row[(h, r)] for r in range(bb)], axis=0)
        o = o * lax.rsqrt(jnp.mean(o * o, -1, keepdims=True) + RMS_EPS) * ng_ref[...]
        o_ref[:, h * DN_DV:(h + 1) * DN_DV] = o * _silu(z[:, h * DN_DV:(h + 1) * DN_DV])


def _deltanet(qkv, ba, z, conv_buf, s0, conv_w, a_log, dt_bias, norm_g):
    b, t, n_qkv = qkv.shape
    n_val = DN_HEADS * DN_DV
    hist = DN_CONV - 1
    rep = lambda a: jnp.repeat(a.astype(F32), DN_DK).reshape(1, DN_HEADS * DN_DK)
    cw = conv_w.astype(F32)
    consts = (cw, rep(a_log), rep(dt_bias), norm_g.reshape(1, DN_DV).astype(F32))
    state_shape = (DN_HEADS, DN_DK, DN_DV)
    out_shape = [jax.ShapeDtypeStruct((b, t, n_val), F32),
                 jax.ShapeDtypeStruct((b,) + state_shape, F32),
                 jax.ShapeDtypeStruct((b, hist, n_qkv), F32)]
    if t == 1:
        bb = _row_tile(b, V7X_SUBLANES)
        const = lambda a: pl.BlockSpec(a.shape, lambda i: (0,) * a.ndim)
        o, s_new, cbuf_new = pl.pallas_call(
            functools.partial(_dn_step_kernel, bb=bb),
            grid=(b // bb,),
            in_specs=[pl.BlockSpec((bb, n_qkv), lambda i: (i, 0)),
                      pl.BlockSpec((bb, V7X_LANES), lambda i: (i, 0)),
                      pl.BlockSpec((bb, n_val), lambda i: (i, 0)),
                      pl.BlockSpec((bb, hist, n_qkv), lambda i: (i, 0, 0)),
                      pl.BlockSpec((bb,) + state_shape, lambda i: (i, 0, 0, 0))] + [const(a) for a in consts],
            out_specs=[pl.BlockSpec((bb, n_val), lambda i: (i, 0)),
                       pl.BlockSpec((bb,) + state_shape, lambda i: (i, 0, 0, 0)),
                       pl.BlockSpec((bb, hist, n_qkv), lambda i: (i, 0, 0))],
            out_shape=[jax.ShapeDtypeStruct((b, n_val), F32)] + out_shape[1:],
            compiler_params=_cparams("parallel"),
            name="deltanet_step",
        )(qkv.reshape(b, n_qkv), ba.reshape(b, V7X_LANES), z.reshape(b, n_val), conv_buf.astype(F32),
          s0.astype(F32), *consts)
        return o.reshape(b, 1, n_val), s_new, cbuf_new
    c = DN_CHUNK
    assert t % c == 0 and c >= hist
    bb = _row_tile(b, 4)
    const = lambda a: pl.BlockSpec(a.shape, lambda i, j: (0,) * a.ndim)
    return pl.pallas_call(
        functools.partial(_dn_seq_kernel, c=c, bb=bb),
        grid=(b // bb, t // c),
        in_specs=[pl.BlockSpec((bb, c, n_qkv), lambda i, j: (i, j, 0)),
                  pl.BlockSpec((bb, c, V7X_LANES), lambda i, j: (i, j, 0)),
                  pl.BlockSpec((bb, c, n_val), lambda i, j: (i, j, 0)),
                  pl.BlockSpec((bb, hist, n_qkv), lambda i, j: (i, 0, 0)),
                  pl.BlockSpec((bb,) + state_shape, lambda i, j: (i, 0, 0, 0))] + [const(a) for a in consts],
        out_specs=[pl.BlockSpec((bb, c, n_val), lambda i, j: (i, j, 0)),
                   pl.BlockSpec((bb,) + state_shape, lambda i, j: (i, 0, 0, 0)),
                   pl.BlockSpec((bb, hist, n_qkv), lambda i, j: (i, 0, 0))],
        out_shape=out_shape,
        scratch_shapes=[pltpu.VMEM((bb, _DN_HALO + c, n_qkv), F32),
                        pltpu.VMEM((bb,) + state_shape, F32)],
        compiler_params=_cparams("parallel", "arbitrary"),
        name="deltanet_seq",
    )(qkv, ba, z, conv_buf.astype(F32), s0.astype(F32), *consts)


def _s5_param_kernel(lre_ref, lim_ref, ldt_ref, lbre_ref, lbim_ref, fre_ref, fim_ref):
    lam_re = lre_ref[...]
    lam_im = lim_ref[...]
    dt = jnp.exp(ldt_ref[...])
    mag = jnp.exp(lam_re * dt)
    ang = lam_im * dt
    lb_re = mag * jnp.cos(ang)
    lb_im = mag * jnp.sin(ang)
    den = lam_re * lam_re + lam_im * lam_im
    lbre_ref[...] = lb_re
    lbim_ref[...] = lb_im
    fre_ref[...] = ((lb_re - 1.0) * lam_re + lb_im * lam_im) / den
    fim_ref[...] = (lb_im * lam_re - (lb_re - 1.0) * lam_im) / den


def _s5_discretize(lam_re, lam_im, log_dt):
    g, n = lam_re.shape
    ldt = jnp.broadcast_to(log_dt.astype(F32)[:, None], (g, n))
    shp = jax.ShapeDtypeStruct((g, n), F32)
    return pl.pallas_call(_s5_param_kernel, out_shape=[shp] * 4, name="s5_discretize")(
        lam_re.astype(F32), lam_im.astype(F32), ldt)


def _s5_kernel(u_ref, h0re_ref, h0im_ref, lbre_ref, lbim_ref, bre_ref, bim_ref, cre_ref, cim_ref,
               d_ref, wglu_ref, bglu_ref, y_ref, hre_out_ref, him_out_ref,
               sre_ref, sim_ref, cre_s, cim_s, *, tb, bb, lane_chunk):
    j = pl.program_id(1)
    nj = pl.num_programs(1)
    n_ch = u_ref.shape[-1]
    n_st = sre_ref.shape[-1]
    halves = bre_ref.shape[0]
    ch_h = n_ch // halves
    st_h = n_st // halves

    @pl.when(j == 0)
    def _():
        cre_s[...] = h0re_ref[...]
        cim_s[...] = h0im_ref[...]

    u = u_ref[...].reshape(tb * bb, n_ch)
    ub = u.astype(BF16)
    for hf in range(halves):
        uh = ub[:, hf * ch_h:(hf + 1) * ch_h]
        sre_ref[:, hf * st_h:(hf + 1) * st_h] = jnp.dot(uh, bre_ref[hf], preferred_element_type=F32)
        sim_ref[:, hf * st_h:(hf + 1) * st_h] = jnp.dot(uh, bim_ref[hf], preferred_element_type=F32)

    for c0 in range(0, n_st, lane_chunk):
        cs = slice(c0, c0 + lane_chunk)
        lr = jnp.broadcast_to(lbre_ref[0:1, cs], (bb, lane_chunk))
        li = jnp.broadcast_to(lbim_ref[0:1, cs], (bb, lane_chunk))

        def body(t, carry, cs=cs, lr=lr, li=li):
            hr, hi = carry
            r = pl.multiple_of(t * bb, bb)
            nr = lr * hr - li * hi + sre_ref[pl.ds(r, bb), cs]
            ni = lr * hi + li * hr + sim_ref[pl.ds(r, bb), cs]
            sre_ref[pl.ds(r, bb), cs] = nr
            sim_ref[pl.ds(r, bb), cs] = ni
            return nr, ni

        hr, hi = lax.fori_loop(0, tb, body, (cre_s[:, cs], cim_s[:, cs]))
        cre_s[:, cs] = hr
        cim_s[:, cs] = hi

    ys = []
    for hf in range(halves):
        hre = sre_ref[:, hf * st_h:(hf + 1) * st_h].astype(BF16)
        him = sim_ref[:, hf * st_h:(hf + 1) * st_h].astype(BF16)
        ys.append(jnp.dot(hre, cre_ref[hf], preferred_element_type=F32)
                  - jnp.dot(him, cim_ref[hf], preferred_element_type=F32))
    y = jnp.concatenate(ys, axis=1) + d_ref[...] * u
    y = jax.nn.gelu(y)
    y = y * _sigmoid(_bdot(y, wglu_ref[...]) + bglu_ref[...])
    y_ref[...] = y.reshape(tb, bb, n_ch)

    @pl.when(j == nj - 1)
    def _():
        hre_out_ref[...] = cre_s[...]
        him_out_ref[...] = cim_s[...]


def _s5(u_tm, h0_re, h0_im, lam_re, lam_im, log_dt, b_re, b_im, c_re, c_im, d_skip, w_glu, b_glu, *, halves=2):
    t, b, n_ch = u_tm.shape
    g, n, p = b_re.shape
    n_st = g * n
    lb_re, lb_im, f_re, f_im = _s5_discretize(lam_re, lam_im, log_dt)
    b_re = b_re.astype(F32)
    b_im = b_im.astype(F32)
    bb_re = f_re[..., None] * b_re - f_im[..., None] * b_im
    bb_im = f_re[..., None] * b_im + f_im[..., None] * b_re
    gh = g // halves

    def in_blocks(a):
        a = a.reshape(halves, gh, n, p)
        eye = jnp.eye(gh, dtype=F32)
        return jnp.einsum('hgnp,gk->hgpkn', a, eye).reshape(halves, gh * p, gh * n).astype(BF16)

    def out_blocks(a):
        a = a.astype(F32).reshape(halves, gh, p, n)
        eye = jnp.eye(gh, dtype=F32)
        return jnp.einsum('hgpn,gk->hgnkp', a, eye).reshape(halves, gh * n, gh * p).astype(BF16)

    bre_m, bim_m = in_blocks(bb_re), in_blocks(bb_im)
    cre_m, cim_m = out_blocks(c_re), out_blocks(c_im)
    bb = b if b <= 128 else 128
    assert b % bb == 0
    tb = _row_tile(t, max(1, 512 // bb))
    lane_chunk = max(V7X_LANES, min(n_st, 8192 // bb))
    const = lambda a: pl.BlockSpec(a.shape, lambda i, j: (0,) * a.ndim)
    row = lambda a, m: a.reshape(1, m).astype(F32)
    args = (u_tm, h0_re.reshape(b, n_st).astype(F32), h0_im.reshape(b, n_st).astype(F32),
            row(lb_re, n_st), row(lb_im, n_st), bre_m, bim_m, cre_m, cim_m,
            row(d_skip, n_ch), w_glu.astype(BF16), row(b_glu, n_ch))
    y, hre, him = pl.pallas_call(
        functools.partial(_s5_kernel, tb=tb, bb=bb, lane_chunk=lane_chunk),
        grid=(b // bb, t // tb),
        in_specs=[pl.BlockSpec((tb, bb, n_ch), lambda i, j: (j, i, 0)),
                  pl.BlockSpec((bb, n_st), lambda i, j: (i, 0)),
                  pl.BlockSpec((bb, n_st), lambda i, j: (i, 0))] + [const(a) for a in args[3:]],
        out_specs=[pl.BlockSpec((tb, bb, n_ch), lambda i, j: (j, i, 0)),
                   pl.BlockSpec((bb, n_st), lambda i, j: (i, 0)),
                   pl.BlockSpec((bb, n_st), lambda i, j: (i, 0))],
        out_shape=[jax.ShapeDtypeStruct((t, b, n_ch), F32),
                   jax.ShapeDtypeStruct((b, n_st), F32),
                   jax.ShapeDtypeStruct((b, n_st), F32)],
        scratch_shapes=[pltpu.VMEM((tb * bb, n_st), F32), pltpu.VMEM((tb * bb, n_st), F32),
                        pltpu.VMEM((bb, n_st), F32), pltpu.VMEM((bb, n_st), F32)],
        compiler_params=_cparams("parallel", "arbitrary"),
        name="s5",
    )(*args)
    return y, hre.reshape(b, g, n), him.reshape(b, g, n)


def _ab_layer(x, dn_s, dn_conv, ssm_re, ssm_im, p, i, ln_g, ln_b, *, alpha):
    b, t, d = x.shape
    xr = x if t > 1 else x.reshape(1, b, d)
    bx, tx, _ = xr.shape
    qkv, z, ba, u_tm = _ab_in_proj(xr, p['ab_w_in'][i])
    n_ch = u_tm.shape[1] // bx
    shp = lambda a: a.reshape(b, t, a.shape[-1])
    o, s_new, conv_new = _deltanet(shp(qkv), shp(ba), shp(z), dn_conv, dn_s, p['dn_conv_w'][i],
                                   p['dn_a_log'][i], p['dn_dt_bias'][i], p['dn_norm_g'][i])
    y_tm, h_re, h_im = _s5(u_tm.reshape(t, b, n_ch), ssm_re, ssm_im, p['ssm_lambda_re'][i],
                           p['ssm_lambda_im'][i], p['ssm_log_dt'][i], p['ssm_b_re'][i], p['ssm_b_im'][i],
                           p['ssm_c_re'][i], p['ssm_c_im'][i], p['ssm_d'][i], p['ssm_w_glu'][i],
                           p['ssm_b_glu'][i])
    x_new = _ab_out_proj(o.reshape(bx, tx, -1), y_tm.reshape(tx, bx * n_ch), p['ab_w_out'][i], xr,
                         ln_g, ln_b, alpha=alpha)
    return x_new.reshape(b, t, d), s_new, conv_new, h_re, h_im


_CC_HALO = 32
_CC_ROW_TILES = 4


def _cconv_seq_kernel(h_ref, buf_ref, w_ref, bdw_ref, g_ref, b_ref, o_ref, xp_ref, xs_ref, acc_ref, *, tt, width):
    j = pl.program_id(1)
    hist = width - 1
    d = h_ref.shape[-1]
    rows = V7X_SUBLANES

    @pl.when(j == 0)
    def _():
        xp_ref[0:_CC_HALO - hist, :] = jnp.zeros((_CC_HALO - hist, d), F32)
        xp_ref[_CC_HALO - hist:_CC_HALO, :] = buf_ref[...]

    xp_ref[_CC_HALO:_CC_HALO + tt, :] = h_ref[...]
    n_shift = xs_ref.shape[1]
    for s in range(1, rows):
        xs_ref[s - 1] = xp_ref[pl.ds(s, n_shift), :]
    base = _CC_HALO - hist
    for c0 in range(0, d, V7X_LANES):
        cs = slice(c0, c0 + V7X_LANES)
        taps = [jnp.broadcast_to(w_ref[k:k + 1, cs], (rows, V7X_LANES)) for k in range(width)]

        def window(r, k, cs=cs):
            a, s = divmod(base + k, rows)
            if s == 0:
                return xp_ref[pl.ds(r + a * rows, rows), cs]
            return xs_ref[s - 1, pl.ds(r + a * rows, rows), cs]

        def body(i, carry, taps=taps, window=window, cs=cs):
            for sub in range(_CC_ROW_TILES):
                r = pl.multiple_of((i * _CC_ROW_TILES + sub) * rows, rows)
                acc = window(r, 0) * taps[0]
                for k in range(1, width):
                    acc = acc + window(r, k) * taps[k]
                acc_ref[pl.ds(r, rows), cs] = acc
            return carry

        lax.fori_loop(0, tt // (rows * _CC_ROW_TILES), body, 0)
    xp_ref[0:_CC_HALO, :] = xp_ref[tt:tt + _CC_HALO, :]
    y = _layer_norm(acc_ref[...] + bdw_ref[...], g_ref[...], b_ref[...])
    o_ref[...] = _silu(y)


def _cconv_seq(h, buf, w_dw, b_dw, ln_g, ln_b, *, tt=256):
    b, t, d = h.shape
    width = w_dw.shape[0]
    tt = _row_tile(t, tt)
    assert tt >= _CC_HALO and width - 1 <= _CC_HALO
    row = lambda a: a.reshape(1, d).astype(F32)
    const = lambda a: pl.BlockSpec(a.shape, lambda i, j: (0, 0))
    args = (w_dw.astype(F32), row(b_dw), row(ln_g), row(ln_b))
    return pl.pallas_call(
        functools.partial(_cconv_seq_kernel, tt=tt, width=width),
        grid=(b, t // tt),
        in_specs=[pl.BlockSpec((None, tt, d), lambda i, j: (i, j, 0)),
                  pl.BlockSpec((None, width - 1, d), lambda i, j: (i, 0, 0))] + [const(a) for a in args],
        out_specs=pl.BlockSpec((None, tt, d), lambda i, j: (i, j, 0)),
        out_shape=jax.ShapeDtypeStruct((b, t, d), F32),
        scratch_shapes=[pltpu.VMEM((_CC_HALO + tt, d), F32),
                        pltpu.VMEM((V7X_SUBLANES - 1, _CC_HALO + tt - V7X_SUBLANES, d), F32),
                        pltpu.VMEM((tt, d), F32)],
        compiler_params=_cparams("parallel", "arbitrary"),
        name="cconv_seq",
    )(h, buf.astype(F32), *args)


def _cconv_step_kernel(h_ref, buf_ref, w_ref, bdw_ref, g_ref, b_ref, o_ref, *, width):
    hist = width - 1
    acc = jnp.sum(buf_ref[...] * w_ref[0:hist, :][None], axis=1) + h_ref[...] * w_ref[hist:width, :]
    o_ref[...] = _silu(_layer_norm(acc + bdw_ref[...], g_ref[...], b_ref[...]))


def _cconv_step(h, buf, w_dw, b_dw, ln_g, ln_b, *, bb=8):
    b, d = h.shape
    width = w_dw.shape[0]
    bb = _row_tile(b, bb)
    row = lambda a: a.reshape(1, d).astype(F32)
    const = lambda a: pl.BlockSpec(a.shape, lambda i: (0, 0))
    args = (w_dw.astype(F32), row(b_dw), row(ln_g), row(ln_b))
    return pl.pallas_call(
        functools.partial(_cconv_step_kernel, width=width),
        grid=(b // bb,),
        in_specs=[pl.BlockSpec((bb, d), lambda i: (i, 0)),
                  pl.BlockSpec((bb, width - 1, d), lambda i: (i, 0, 0))] + [const(a) for a in args],
        out_specs=pl.BlockSpec((bb, d), lambda i: (i, 0)),
        out_shape=jax.ShapeDtypeStruct((b, d), F32),
        compiler_params=_cparams("parallel"),
        name="cconv_step",
    )(h, buf.astype(F32), *args)


def _conv_layer(x, buf, p, i, ln_g, ln_b, *, alpha):
    b, t, d = x.shape
    x2 = x.reshape(b * t, d)
    h = _linear(x2, p['cc_w_pw1'][i], p['cc_b_pw1'][i], glu=True)
    args = (p['cc_w_dw'][i], p['cc_b_dw'][i], p['cc_ln_g'][i], p['cc_ln_b'][i])
    if t == 1:
        hc = _cconv_step(h, buf, *args)
    else:
        hc = _cconv_seq(h.reshape(b, t, d), buf, *args).reshape(b * t, d)
    new_buf = jnp.concatenate([buf.astype(F32), h.reshape(b, t, d)], axis=1)[:, t:]
    x_new = _linear_postnorm(hc, p['cc_w_pw2'][i], p['cc_b_pw2'][i], x2, ln_g, ln_b, alpha=alpha)
    return x_new.reshape(b, t, d), new_buf


def _mem_attn_seq_kernel(x_ref, wq_ref, k_ref, v_ref, wo_ref, g_ref, b_ref, o_ref, *, alpha, heads):
    x = x_ref[...]
    d = x.shape[-1]
    hd = d // heads
    q = _bdot(x, wq_ref[...]) * (hd ** -0.5)
    k = k_ref[...].astype(BF16)
    v = v_ref[...].astype(BF16)
    outs = []
    for h in range(heads):
        hs = slice(h * hd, (h + 1) * hd)
        s = _bdot_nt(q[:, hs], k[:, hs])
        s = s - jnp.max(s, axis=-1, keepdims=True)
        e = jnp.exp(s)
        a = e / jnp.sum(e, axis=-1, keepdims=True)
        outs.append(_bdot(a, v[:, hs]))
    o = jnp.concatenate(outs, axis=1)
    hres = _bdot(o, wo_ref[...])
    o_ref[...] = _layer_norm(alpha * x + hres, g_ref[...], b_ref[...])


def _mem_attn_seq(x, mk, mv, layer, wq, wo, ln_g, ln_b, *, alpha, tq=512):
    b, t, d = x.shape
    m = mk.shape[2]
    tq = _row_tile(t, tq)
    row = lambda a: a.reshape(1, d).astype(F32)
    const = lambda a: pl.BlockSpec(a.shape, lambda i, j: (0, 0))
    wqb, wob = wq.astype(BF16), wo.astype(BF16)
    return pl.pallas_call(
        functools.partial(_mem_attn_seq_kernel, alpha=alpha, heads=MEM_HEADS),
        grid=(b, t // tq),
        in_specs=[pl.BlockSpec((None, tq, d), lambda i, j: (i, j, 0)),
                  const(wqb),
                  pl.BlockSpec((None, None, m, d), lambda i, j: (layer, i, 0, 0)),
                  pl.BlockSpec((None, None, m, d), lambda i, j: (layer, i, 0, 0)),
                  const(wob), pl.BlockSpec((1, d), lambda i, j: (0, 0)),
                  pl.BlockSpec((1, d), lambda i, j: (0, 0))],
        out_specs=pl.BlockSpec((None, tq, d), lambda i, j: (i, j, 0)),
        out_shape=jax.ShapeDtypeStruct((b, t, d), F32),
        compiler_params=_cparams("parallel", "parallel"),
        name="mem_attn_seq",
    )(x, wqb, mk, mv, wob, row(ln_g), row(ln_b))


def _mem_attn_step_kernel(q_ref, k_ref, v_ref, o_ref, *, heads, bb):
    i = pl.program_id(0)
    hd = q_ref.shape[-1]
    for r in range(bb):
        q = q_ref[i * bb + r] * (hd ** -0.5)
        s = jnp.sum(k_ref[r] * q[None], axis=-1, keepdims=True)
        s = s - jnp.max(s, axis=0, keepdims=True)
        e = jnp.exp(s)
        a = e / jnp.sum(e, axis=0, keepdims=True)
        o_ref[i * bb + r] = jnp.sum(a * v_ref[r], axis=0)


def _mem_attn_step(q, mk_all, mv_all, layer, *, bb=4):
    b, heads, hd = q.shape
    m = mk_all.shape[2]
    bb = _row_tile(b, bb)
    return pl.pallas_call(
        functools.partial(_mem_attn_step_kernel, heads=heads, bb=bb),
        grid=(b // bb,),
        in_specs=[pl.BlockSpec((b, heads, hd), lambda i: (0, 0, 0)),
                  pl.BlockSpec((None, bb, m, heads, hd), lambda i: (layer, i, 0, 0, 0)),
                  pl.BlockSpec((None, bb, m, heads, hd), lambda i: (layer, i, 0, 0, 0))],
        out_specs=pl.BlockSpec((b, heads, hd), lambda i: (0, 0, 0)),
        out_shape=jax.ShapeDtypeStruct((b, heads, hd), F32),
        compiler_params=_cparams("arbitrary"),
        name="mem_attn_step",
    )(q, mk_all, mv_all)


def _mem_layer(x, mk_all, mv_all, layer, wq, wo, ln_g, ln_b, *, alpha):
    b, t, d = x.shape
    n_layers, _, m, heads, hd = mk_all.shape
    if t == 1:
        x2 = x.reshape(b, d)
        q = _linear(x2, wq).reshape(b, heads, hd)
        o = _mem_attn_step(q, mk_all, mv_all, layer).reshape(b, d)
        return _linear_postnorm(o, wo, None, x2, ln_g, ln_b, alpha=alpha).reshape(b, t, d)
    mk_all = mk_all.reshape(n_layers, b, m, d)
    mv_all = mv_all.reshape(n_layers, b, m, d)
    return _mem_attn_seq(x, mk_all, mv_all, layer, wq, wo, ln_g, ln_b, alpha=alpha)


_NEG = -1e30


def _router_kernel(x_ref, w_ref, b_ref, ids_ref, wts_ref, *, groups, per_group):
    x = x_ref[...]
    w = w_ref[...]
    xh = x.astype(BF16)
    xl = (x - xh.astype(F32)).astype(BF16)
    wh = w.astype(BF16)
    wl = (w - wh.astype(F32)).astype(BF16)
    logits = (jnp.dot(xh, wh, preferred_element_type=F32) + jnp.dot(xh, wl, preferred_element_type=F32)
              + jnp.dot(xl, wh, preferred_element_type=F32)) + b_ref[...]
    lane = lax.broadcasted_iota(jnp.int32, logits.shape, 1)
    n_exp = groups * per_group
    is_g = lane < groups
    gl = jnp.where(is_g, logits, _NEG)
    gmax = jnp.max(gl, axis=-1, keepdims=True)
    gsum = jnp.sum(jnp.where(is_g, jnp.exp(gl - gmax), 0.0), axis=-1, keepdims=True)
    gw = 1.0 / gsum
    gi = jnp.min(jnp.where(gl == gmax, lane, V7X_LANES), axis=-1, keepdims=True)
    lane_grp = (lane - groups) // per_group
    sel = (lane >= groups) & (lane < groups + n_exp) & (lane_grp == gi)
    el = jnp.where(sel, logits, _NEG)
    emax = jnp.max(el, axis=-1, keepdims=True)
    ee = jnp.where(sel, jnp.exp(el - emax), 0.0)
    ep = jnp.where(sel, ee / jnp.sum(ee, axis=-1, keepdims=True), -1.0)
    p1 = jnp.max(ep, axis=-1, keepdims=True)
    i1 = jnp.min(jnp.where(ep == p1, lane, V7X_LANES), axis=-1, keepdims=True)
    ep2 = jnp.where(lane == i1, -1.0, ep)
    p2 = jnp.max(ep2, axis=-1, keepdims=True)
    i2 = jnp.min(jnp.where(ep2 == p2, lane, V7X_LANES), axis=-1, keepdims=True)
    denom = p1 + p2
    ids_ref[...] = jnp.where(lane == 0, i1 - groups, jnp.where(lane == 1, i2 - groups, 0))
    wts_ref[...] = jnp.where(lane == 0, gw * p1 / denom, jnp.where(lane == 1, gw * p2 / denom, 0.0))


def _router(x, w_group, b_group, w_expert, b_expert, *, tm=512):
    n, d = x.shape
    groups = w_group.shape[1]
    n_exp = w_expert.shape[1]
    pad = V7X_LANES - groups - n_exp
    w = jnp.pad(jnp.concatenate([w_group, w_expert], axis=1).astype(F32), ((0, 0), (0, pad)))
    b = jnp.pad(jnp.concatenate([b_group, b_expert]).astype(F32), (0, pad)).reshape(1, V7X_LANES)
    tm = _row_tile(n, tm)
    return pl.pallas_call(
        functools.partial(_router_kernel, groups=groups, per_group=n_exp // groups),
        grid=(n // tm,),
        in_specs=[pl.BlockSpec((tm, d), lambda i: (i, 0)),
                  pl.BlockSpec((d, V7X_LANES), lambda i: (0, 0)),
                  pl.BlockSpec((1, V7X_LANES), lambda i: (0, 0))],
        out_specs=[pl.BlockSpec((tm, V7X_LANES), lambda i: (i, 0)),
                   pl.BlockSpec((tm, V7X_LANES), lambda i: (i, 0))],
        out_shape=[jax.ShapeDtypeStruct((n, V7X_LANES), jnp.int32),
                   jax.ShapeDtypeStruct((n, V7X_LANES), F32)],
        compiler_params=_cparams("parallel"),
        name="moe_router",
    )(x, w, b)


def _expert_ffn_kernel(te_ref, nv_ref, x_ref, wg_ref, wu_ref, wd_ref, o_ref):
    i = pl.program_id(0)

    @pl.when(i < nv_ref[0])
    def _():
        x = x_ref[...]
        hid = _silu(_bdot(x, wg_ref[...])) * _bdot(x, wu_ref[...])
        o_ref[...] = _bdot(hid, wd_ref[...])

    @pl.when(i >= nv_ref[0])
    def _():
        o_ref[...] = jnp.zeros_like(o_ref)


def _expert_ffn(x_sorted, tile_expert, n_valid, w_gate, w_up, w_down, layer, *, tm):
    r, d = x_sorted.shape
    f = w_gate.shape[-1]
    return pl.pallas_call(
        _expert_ffn_kernel,
        grid_spec=pltpu.PrefetchScalarGridSpec(
            num_scalar_prefetch=2,
            grid=(r // tm,),
            in_specs=[pl.BlockSpec((tm, d), lambda i, te, nv: (i, 0)),
                      pl.BlockSpec((None, None, d, f), lambda i, te, nv: (layer, te[i], 0, 0)),
                      pl.BlockSpec((None, None, d, f), lambda i, te, nv: (layer, te[i], 0, 0)),
                      pl.BlockSpec((None, None, f, d), lambda i, te, nv: (layer, te[i], 0, 0))],
            out_specs=pl.BlockSpec((tm, d), lambda i, te, nv: (i, 0)),
        ),
        out_shape=jax.ShapeDtypeStruct((r, d), F32),
        compiler_params=_cparams("arbitrary"),
        name="moe_expert_ffn",
    )(tile_expert, n_valid, x_sorted, w_gate, w_up, w_down)


def _rank_kernel(ids_ref, rank_ref, cnt_ref, carry_ref):
    i = pl.program_id(0)
    tm = ids_ref.shape[0]

    @pl.when(i == 0)
    def _():
        carry_ref[...] = jnp.zeros_like(carry_ref)

    ids = ids_ref[...]
    lane = lax.broadcasted_iota(jnp.int32, ids.shape, 1)
    oh0 = jnp.where(lane == ids[:, 0:1], 1.0, 0.0)
    oh1 = jnp.where(lane == ids[:, 1:2], 1.0, 0.0)
    both = oh0 + oh1
    ii = lax.broadcasted_iota(jnp.int32, (tm, tm), 0)
    jj = lax.broadcasted_iota(jnp.int32, (tm, tm), 1)
    earlier = jnp.where(ii > jj, 1.0, 0.0).astype(BF16)
    prefix = jnp.dot(earlier, both.astype(BF16), preferred_element_type=F32) + carry_ref[...]
    r0 = jnp.sum(oh0 * prefix, axis=-1, keepdims=True)
    r1 = jnp.sum(oh1 * prefix, axis=-1, keepdims=True)
    rank_ref[...] = jnp.where(lane == 0, r0, jnp.where(lane == 1, r1, 0.0)).astype(jnp.int32)
    total = carry_ref[...] + jnp.sum(both, axis=0, keepdims=True)
    carry_ref[...] = total
    cnt_ref[...] = total.astype(jnp.int32)


def _expert_ranks(ids, *, tm=512):
    n = ids.shape[0]
    tm = _row_tile(n, tm)
    return pl.pallas_call(
        _rank_kernel,
        grid=(n // tm,),
        in_specs=[pl.BlockSpec((tm, V7X_LANES), lambda i: (i, 0))],
        out_specs=[pl.BlockSpec((tm, V7X_LANES), lambda i: (i, 0)),
                   pl.BlockSpec((1, V7X_LANES), lambda i: (0, 0))],
        out_shape=[jax.ShapeDtypeStruct((n, V7X_LANES), jnp.int32),
                   jax.ShapeDtypeStruct((1, V7X_LANES), jnp.int32)],
        scratch_shapes=[pltpu.VMEM((1, V7X_LANES), F32)],
        compiler_params=_cparams("arbitrary"),
        name="moe_rank",
    )(ids)


def _dispatch_kernel(slot_ref, x_ref, xs_init_ref, xs_ref, sem):
    del xs_init_ref
    tm = x_ref.shape[0]

    def body(r, carry):
        for k in range(MOE_TOP_K):
            s = slot_ref[0, 0, MOE_TOP_K * r + k]
            pltpu.make_async_copy(x_ref.at[pl.ds(r, 1), :], xs_ref.at[pl.ds(s, 1), :], sem).start(priority=k)
        return carry

    lax.fori_loop(0, tm, body, 0, unroll=8)
    for k in range(MOE_TOP_K):
        pltpu.make_async_copy(x_ref, xs_ref.at[pl.ds(0, tm), :], sem).wait()


def _dispatch(x, slot, rows, *, tm):
    n, d = x.shape
    slot3 = slot.reshape(n // tm, 1, MOE_TOP_K * tm)
    return pl.pallas_call(
        _dispatch_kernel,
        grid=(n // tm,),
        in_specs=[pl.BlockSpec((1, 1, MOE_TOP_K * tm), lambda i: (i, 0, 0), memory_space=pltpu.SMEM),
                  pl.BlockSpec((tm, d), lambda i: (i, 0)),
                  pl.BlockSpec(memory_space=pl.ANY)],
        out_specs=pl.BlockSpec(memory_space=pl.ANY),
        out_shape=jax.ShapeDtypeStruct((rows, d), F32),
        scratch_shapes=[pltpu.SemaphoreType.DMA(())],
        input_output_aliases={2: 0},
        compiler_params=_cparams("arbitrary"),
        name="moe_dispatch",
    )(slot3, x, jnp.zeros((rows, d), F32))


def _combine_kernel(slot_ref, slot_next_ref, wts_ref, x_ref, g_ref, b_ref, y_hbm, o_ref, ybuf, sem, *, alpha):
    i = pl.program_id(0)
    n = pl.num_programs(0)
    tm = x_ref.shape[0]
    cur = i % 2

    def issue(idx_ref, buf):
        def body(r, carry):
            for k in range(MOE_TOP_K):
                s = idx_ref[0, 0, MOE_TOP_K * r + k]
                pltpu.make_async_copy(y_hbm.at[pl.ds(s, 1), :], ybuf.at[buf, k, pl.ds(r, 1), :],
                                      sem.at[buf]).start(priority=k)
            return carry

        lax.fori_loop(0, tm, body, 0, unroll=8)

    @pl.when(i == 0)
    def _():
        issue(slot_ref, 0)

    @pl.when(i + 1 < n)
    def _():
        issue(slot_next_ref, 1 - cur)

    for k in range(MOE_TOP_K):
        pltpu.make_async_copy(y_hbm.at[pl.ds(0, tm), :], ybuf.at[cur, k], sem.at[cur]).wait()
    wts = wts_ref[...]
    y = wts[:, 0:1] * ybuf[cur, 0] + wts[:, 1:2] * ybuf[cur, 1]
    o_ref[...] = _layer_norm(alpha * x_ref[...] + y, g_ref[...], b_ref[...])


def _combine_postnorm(y_sorted, slot, wts, x, g, b, *, alpha, tm):
    n, d = x.shape
    nt = n // tm
    slot3 = slot.reshape(nt, 1, MOE_TOP_K * tm)
    row = lambda a: a.reshape(1, d).astype(F32)
    smem = lambda f: pl.BlockSpec((1, 1, MOE_TOP_K * tm), f, memory_space=pltpu.SMEM)
    return pl.pallas_call(
        functools.partial(_combine_kernel, alpha=alpha),
        grid=(nt,),
        in_specs=[smem(lambda i: (i, 0, 0)),
                  smem(lambda i: (jnp.minimum(i + 1, nt - 1), 0, 0)),
                  pl.BlockSpec((tm, V7X_LANES), lambda i: (i, 0)),
                  pl.BlockSpec((tm, d), lambda i: (i, 0)),
                  pl.BlockSpec((1, d), lambda i: (0, 0)),
                  pl.BlockSpec((1, d), lambda i: (0, 0)),
                  pl.BlockSpec(memory_space=pl.ANY)],
        out_specs=pl.BlockSpec((tm, d), lambda i: (i, 0)),
        out_shape=jax.ShapeDtypeStruct((n, d), F32),
        scratch_shapes=[pltpu.VMEM((2, MOE_TOP_K, tm, d), F32), pltpu.SemaphoreType.DMA((2,))],
        compiler_params=_cparams("arbitrary"),
        name="moe_combine",
    )(slot3, slot3, wts, x, row(g), row(b), y_sorted)


def _moe_layer(x, p, layer, ln_g, ln_b, *, alpha):
    b, t, d = x.shape
    n = b * t
    x2 = x.reshape(n, d)
    w_gate, w_up, w_down = p['moe_w_gate'], p['moe_w_up'], p['moe_w_down']
    n_exp = w_gate.shape[1]
    ids, wts = _router(x2, p['moe_w_group'][layer], p['moe_b_group'][layer], p['moe_w_expert'][layer],
                       p['moe_b_expert'][layer])
    rank, counts = _expert_ranks(ids)
    tm = 256 if n >= 4096 else V7X_BF16_SUBLANES
    tm_tok = _row_tile(n, 256)
    counts = counts[0, :n_exp]
    padded = ((counts + tm - 1) // tm) * tm
    ends = jnp.cumsum(padded)
    starts = ends - padded
    n_tiles = (MOE_TOP_K * n + n_exp * (tm - 1)) // tm
    n_valid = (ends[-1] // tm).astype(jnp.int32)
    tile_start = jnp.minimum(jnp.arange(n_tiles, dtype=jnp.int32), n_valid - 1) * tm
    tile_expert = jnp.minimum(jnp.sum((ends[None, :] <= tile_start[:, None]).astype(jnp.int32), axis=1), n_exp - 1)
    choice = ids[:, :MOE_TOP_K]
    onehot = choice[:, :, None] == jnp.arange(n_exp, dtype=jnp.int32)[None, None, :]
    slot = jnp.sum(jnp.where(onehot, starts[None, None, :], 0), axis=-1) + rank[:, :MOE_TOP_K]
    x_sorted = _dispatch(x2, slot, n_tiles * tm, tm=tm_tok)
    y_sorted = _expert_ffn(x_sorted, tile_expert, n_valid.reshape(1), w_gate, w_up, w_down, layer, tm=tm)
    return _combine_postnorm(y_sorted, slot, wts, x2, ln_g, ln_b, alpha=alpha, tm=tm_tok).reshape(b, t, d)


def _trunk(x, dn_s, dn_conv, ssm_re, ssm_im, cconv, mem_k, mem_v, p):
    depth = p['ln_g'].shape[0]
    alpha = (2.0 * depth) ** 0.25
    out_s, out_conv, out_re, out_im, out_cc = [], [], [], [], []
    for layer in range(depth):
        i = layer // 2
        g, bta = p['ln_g'][layer], p['ln_b'][layer]
        if layer % 2 == 0:
            x, s_new, conv_new, re_new, im_new = _ab_layer(x, dn_s[i], dn_conv[i], ssm_re[i], ssm_im[i], p, i,
                                                           g[0], bta[0], alpha=alpha)
            out_s.append(s_new)
            out_conv.append(conv_new)
            out_re.append(re_new)
            out_im.append(im_new)
        else:
            x, cc_new = _conv_layer(x, cconv[i], p, i, g[0], bta[0], alpha=alpha)
            out_cc.append(cc_new)
        x = _mem_layer(x, mem_k, mem_v, layer, p['mem_wq'][layer], p['mem_wo'][layer], g[1], bta[1], alpha=alpha)
        x = _moe_layer(x, p, layer, g[2], bta[2], alpha=alpha)
    return x, jnp.stack(out_s), jnp.stack(out_conv), jnp.stack(out_re), jnp.stack(out_im), jnp.stack(out_cc)


def kernel(x_prompt, x_sample, state_dn_s, state_dn_conv, state_ssm_re, state_ssm_im, state_cconv,
           cache_mem_k, cache_mem_v, mem_prompt, ab_w_in, dn_conv_w, dn_a_log, dn_dt_bias, dn_norm_g,
           ssm_lambda_re, ssm_lambda_im, ssm_log_dt, ssm_b_re, ssm_b_im, ssm_c_re, ssm_c_im, ssm_d,
           ssm_w_glu, ssm_b_glu, ab_w_out, cc_w_pw1, cc_b_pw1, cc_w_dw, cc_b_dw, cc_ln_g, cc_ln_b,
           cc_w_pw2, cc_b_pw2, mem_wq, mem_wk, mem_wv, mem_wo, ln_g, ln_b, moe_w_group, moe_b_group,
           moe_w_expert, moe_b_expert, moe_w_gate, moe_w_up, moe_w_down):
    p = dict(ab_w_in=ab_w_in, dn_conv_w=dn_conv_w, dn_a_log=dn_a_log, dn_dt_bias=dn_dt_bias,
             dn_norm_g=dn_norm_g, ssm_lambda_re=ssm_lambda_re, ssm_lambda_im=ssm_lambda_im,
             ssm_log_dt=ssm_log_dt, ssm_b_re=ssm_b_re, ssm_b_im=ssm_b_im, ssm_c_re=ssm_c_re,
             ssm_c_im=ssm_c_im, ssm_d=ssm_d, ssm_w_glu=ssm_w_glu, ssm_b_glu=ssm_b_glu, ab_w_out=ab_w_out,
             cc_w_pw1=cc_w_pw1, cc_b_pw1=cc_b_pw1, cc_w_dw=cc_w_dw, cc_b_dw=cc_b_dw, cc_ln_g=cc_ln_g,
             cc_ln_b=cc_ln_b, cc_w_pw2=cc_w_pw2, cc_b_pw2=cc_b_pw2, mem_wq=mem_wq, mem_wo=mem_wo,
             ln_g=ln_g, ln_b=ln_b, moe_w_group=moe_w_group, moe_b_group=moe_b_group,
             moe_w_expert=moe_w_expert, moe_b_expert=moe_b_expert, moe_w_gate=moe_w_gate,
             moe_w_up=moe_w_up, moe_w_down=moe_w_down)
    depth = ln_g.shape[0]
    n_ab = state_dn_s.shape[0]
    n_cc = state_cconv.shape[0]
    bsz, _, d = x_prompt.shape
    n_mem = mem_prompt.shape[1]
    hd = d // MEM_HEADS
    z_dn_s = jnp.zeros((n_ab, bsz) + state_dn_s.shape[2:], F32)
    z_dn_conv = jnp.zeros((n_ab, bsz) + state_dn_conv.shape[2:], F32)
    z_ssm = jnp.zeros((n_ab, bsz) + state_ssm_re.shape[2:], F32)
    z_cconv = jnp.zeros((n_cc, bsz) + state_cconv.shape[2:], F32)
    mem2 = mem_prompt.reshape(bsz * n_mem, d)
    p_mem_k = jnp.stack([_linear(mem2, mem_wk[l]) for l in range(depth)]).reshape(depth, bsz, n_mem, MEM_HEADS, hd)
    p_mem_v = jnp.stack([_linear(mem2, mem_wv[l]) for l in range(depth)]).reshape(depth, bsz, n_mem, MEM_HEADS, hd)
    y_prompt, p_dn_s, p_dn_conv, p_ssm_re, p_ssm_im, p_cconv = _trunk(
        x_prompt, z_dn_s, z_dn_conv, z_ssm, z_ssm, z_cconv, p_mem_k, p_mem_v, p)
    y_sample, s_dn_s, s_dn_conv, s_ssm_re, s_ssm_im, s_cconv = _trunk(
        x_sample, state_dn_s, state_dn_conv, state_ssm_re, state_ssm_im, state_cconv, cache_mem_k, cache_mem_v, p)
    return (y_prompt, y_sample, p_dn_s, p_dn_conv, p_ssm_re, p_ssm_im, p_cconv, p_mem_k, p_mem_v,
            s_dn_s, s_dn_conv, s_ssm_re, s_ssm_im, s_cconv)
```

```python
import functools
import math

import jax
import jax.numpy as jnp
from jax import lax
from jax.experimental import pallas as pl
from jax.experimental.pallas import tpu as pltpu

F32 = jnp.float32
BF16 = jnp.bfloat16

DN_HEADS = 4
DN_DK = 128
DN_DV = 128
DN_CONV = 4
DN_CHUNK = 64
SSM_GROUP_CH = 16
MEM_HEADS = 4
MOE_GROUPS = 4
MOE_EXPERTS_PER_GROUP = 8
MOE_TOP_K = 2
LN_EPS = 1e-5
RMS_EPS = 1e-6

V7X_LANES = 128
V7X_SUBLANES = 8
V7X_BF16_SUBLANES = 16
V7X_VMEM_LIMIT_BYTES = 52 * 1024 * 1024


def _cparams(*sem):
    return pltpu.CompilerParams(dimension_semantics=sem, vmem_limit_bytes=V7X_VMEM_LIMIT_BYTES)


def _bdot(a, b):
    return jnp.dot(a.astype(BF16), b.astype(BF16), preferred_element_type=F32)


def _bdot_nt(a, b):
    return lax.dot_general(a.astype(BF16), b.astype(BF16), (((1,), (1,)), ((), ())),
                           preferred_element_type=F32)


def _split3(a):
    hi = a.astype(BF16)
    r1 = a - hi.astype(F32)
    mid = r1.astype(BF16)
    lo = (r1 - mid.astype(F32)).astype(BF16)
    return hi, mid, lo


def _sigmoid(x):
    return 1.0 / (1.0 + jnp.exp(-x))


def _silu(x):
    return x * _sigmoid(x)


def _softplus(x):
    return jnp.maximum(x, 0.0) + jnp.log(1.0 + jnp.exp(-jnp.abs(x)))


def _layer_norm(v, g, b):
    mu = jnp.mean(v, axis=-1, keepdims=True)
    d = v - mu
    var = jnp.mean(d * d, axis=-1, keepdims=True)
    return d * lax.rsqrt(var + LN_EPS) * g + b


def _row_tile(n, pref):
    t = min(n, pref)
    assert n % t == 0, (n, t)
    return t


def _linear_kernel(x_ref, w_ref, b_ref, o_ref, *, glu):
    y = _bdot(x_ref[...], w_ref[...]) + b_ref[...]
    if glu:
        n = y.shape[-1] // 2
        y = y[:, :n] * _sigmoid(y[:, n:])
    o_ref[...] = y.astype(o_ref.dtype)


def _linear(x, w, bias=None, *, glu=False, tm=512, out_dtype=F32):
    m, k = x.shape
    n = w.shape[1]
    if bias is None:
        bias = jnp.zeros((n,), F32)
    tm = _row_tile(m, tm)
    n_out = n // 2 if glu else n
    return pl.pallas_call(
        functools.partial(_linear_kernel, glu=glu),
        grid=(m // tm,),
        in_specs=[pl.BlockSpec((tm, k), lambda i: (i, 0)),
                  pl.BlockSpec((k, n), lambda i: (0, 0)),
                  pl.BlockSpec((1, n), lambda i: (0, 0))],
        out_specs=pl.BlockSpec((tm, n_out), lambda i: (i, 0)),
        out_shape=jax.ShapeDtypeStruct((m, n_out), out_dtype),
        compiler_params=_cparams("parallel"),
        name="linear",
    )(x, w.astype(BF16), bias.reshape(1, n).astype(F32))


def _linear_postnorm_kernel(h_ref, w_ref, b_ref, x_ref, g_ref, beta_ref, o_ref, *, alpha):
    h = _bdot(h_ref[...], w_ref[...]) + b_ref[...]
    o_ref[...] = _layer_norm(alpha * x_ref[...] + h, g_ref[...], beta_ref[...])


def _linear_postnorm(h_in, w, bias, x_res, g, beta, *, alpha, tm=512):
    m, k = h_in.shape
    d = w.shape[1]
    if bias is None:
        bias = jnp.zeros((d,), F32)
    tm = _row_tile(m, tm)
    row = lambda a: a.reshape(1, d).astype(F32)
    return pl.pallas_call(
        functools.partial(_linear_postnorm_kernel, alpha=alpha),
        grid=(m // tm,),
        in_specs=[pl.BlockSpec((tm, k), lambda i: (i, 0)),
                  pl.BlockSpec((k, d), lambda i: (0, 0)),
                  pl.BlockSpec((1, d), lambda i: (0, 0)),
                  pl.BlockSpec((tm, d), lambda i: (i, 0)),
                  pl.BlockSpec((1, d), lambda i: (0, 0)),
                  pl.BlockSpec((1, d), lambda i: (0, 0))],
        out_specs=pl.BlockSpec((tm, d), lambda i: (i, 0)),
        out_shape=jax.ShapeDtypeStruct((m, d), F32),
        compiler_params=_cparams("parallel"),
        name="linear_postnorm",
    )(h_in, w.astype(BF16), row(bias), x_res, row(g), row(beta))


def _ab_in_kernel(x_ref, wqkvz_ref, wba_ref, wu_ref, qkv_ref, z_ref, ba_ref, u_ref, *, n_qkv):
    x = x_ref[...].astype(BF16)
    y = jnp.dot(x, wqkvz_ref[...], preferred_element_type=F32)
    qkv_ref[...] = y[:, :n_qkv]
    z_ref[...] = y[:, n_qkv:]
    ba_ref[...] = jnp.dot(x, wba_ref[...], preferred_element_type=F32)
    u_ref[...] = jnp.dot(x, wu_ref[...], preferred_element_type=F32)


def _ab_in_proj(x, w_in, *, tm=512):
    bx, tx, d = x.shape
    n_key = DN_HEADS * DN_DK
    n_val = DN_HEADS * DN_DV
    n_qkv = 2 * n_key + n_val
    off_beta = n_qkv + n_val
    off_u = off_beta + 2 * DN_HEADS
    n_u = w_in.shape[1] - off_u
    w_qkvz = w_in[:, :off_beta].astype(BF16)
    w_ba = jnp.pad(w_in[:, off_beta:off_u], ((0, 0), (0, V7X_LANES - 2 * DN_HEADS))).astype(BF16)
    w_u = w_in[:, off_u:].astype(BF16)
    tm = _row_tile(tx, tm)
    full = lambda a: pl.BlockSpec(a.shape, lambda b, i: (0, 0))
    return pl.pallas_call(
        functools.partial(_ab_in_kernel, n_qkv=n_qkv),
        grid=(bx, tx // tm),
        in_specs=[pl.BlockSpec((None, tm, d), lambda b, i: (b, i, 0)),
                  full(w_qkvz), full(w_ba), full(w_u)],
        out_specs=[pl.BlockSpec((None, tm, n_qkv), lambda b, i: (b, i, 0)),
                   pl.BlockSpec((None, tm, n_val), lambda b, i: (b, i, 0)),
                   pl.BlockSpec((None, tm, V7X_LANES), lambda b, i: (b, i, 0)),
                   pl.BlockSpec((tm, n_u), lambda b, i: (i, b))],
        out_shape=[jax.ShapeDtypeStruct((bx, tx, n_qkv), F32),
                   jax.ShapeDtypeStruct((bx, tx, n_val), F32),
                   jax.ShapeDtypeStruct((bx, tx, V7X_LANES), F32),
                   jax.ShapeDtypeStruct((tx, bx * n_u), F32)],
        compiler_params=_cparams("parallel", "parallel"),
        name="ab_in_proj",
    )(x, w_qkvz, w_ba, w_u)


def _ab_out_kernel(o_ref, y_ref, wt_ref, wb_ref, x_ref, g_ref, beta_ref, out_ref, *, alpha):
    h = _bdot(o_ref[...], wt_ref[...]) + _bdot(y_ref[...], wb_ref[...])
    out_ref[...] = _layer_norm(alpha * x_ref[...] + h, g_ref[...], beta_ref[...])


def _ab_out_proj(o, y_tm, w_out, x, g, beta, *, alpha, tm=512):
    bx, tx, d = x.shape
    n_o = o.shape[-1]
    n_y = y_tm.shape[1] // bx
    tm = _row_tile(tx, tm)
    wt = w_out[:n_o].astype(BF16)
    wb = w_out[n_o:].astype(BF16)
    row = lambda a: a.reshape(1, d).astype(F32)
    full = lambda a: pl.BlockSpec(a.shape, lambda b, i: (0, 0))
    return pl.pallas_call(
        functools.partial(_ab_out_kernel, alpha=alpha),
        grid=(bx, tx // tm),
        in_specs=[pl.BlockSpec((None, tm, n_o), lambda b, i: (b, i, 0)),
                  pl.BlockSpec((tm, n_y), lambda b, i: (i, b)),
                  full(wt), full(wb),
                  pl.BlockSpec((None, tm, d), lambda b, i: (b, i, 0)),
                  pl.BlockSpec((1, d), lambda b, i: (0, 0)),
                  pl.BlockSpec((1, d), lambda b, i: (0, 0))],
        out_specs=pl.BlockSpec((None, tm, d), lambda b, i: (b, i, 0)),
        out_shape=jax.ShapeDtypeStruct((bx, tx, d), F32),
        compiler_params=_cparams("parallel", "parallel"),
        name="ab_out_proj",
    )(o, y_tm, wt, wb, x, row(g), row(beta))


_DN_HALO = V7X_SUBLANES


def _dn_seq_kernel(qkv_ref, ba_ref, z_ref, cbuf_ref, s0_ref, cw_ref, alog_ref, dtb_ref, ng_ref,
                   o_ref, s_out_ref, cbuf_out_ref, xp_ref, s_ref, *, c, bb):
    j = pl.program_id(1)
    nj = pl.num_programs(1)
    hist = DN_CONV - 1
    n_key = DN_HEADS * DN_DK
    units = [(bi, h) for bi in range(bb) for h in range(DN_HEADS)]
    each = lambda f: {u: f(u) for u in units}

    @pl.when(j == 0)
    def _():
        xp_ref[:, 0:_DN_HALO - hist, :] = jnp.zeros((bb, _DN_HALO - hist, xp_ref.shape[-1]), F32)
        xp_ref[:, _DN_HALO - hist:_DN_HALO, :] = cbuf_ref[...]
        s_ref[...] = s0_ref[...]

    ys, tails = [], []
    for bi in range(bb):
        xp_ref[bi, _DN_HALO:_DN_HALO + c, :] = qkv_ref[bi]
        y = xp_ref[bi, pl.ds(_DN_HALO - hist, c), :] * cw_ref[0:1, :]
        for tap in range(1, DN_CONV):
            y = y + xp_ref[bi, pl.ds(_DN_HALO - hist + tap, c), :] * cw_ref[tap:tap + 1, :]
        tail = xp_ref[bi, pl.ds(_DN_HALO + c - hist, hist), :]
        xp_ref[bi, _DN_HALO - hist:_DN_HALO, :] = tail
        tails.append(tail)
        ys.append(_silu(y))

    ii = lax.broadcasted_iota(jnp.int32, (c, c), 0)
    jj = lax.broadcasted_iota(jnp.int32, (c, c), 1)
    incl = ii >= jj
    strict = ii > jj
    tril = jnp.where(incl, 1.0, 0.0).astype(BF16)
    eye = jnp.where(ii == jj, 1.0, 0.0)
    n_double = int(math.log2(c))
    vsl = lambda h: slice(h * DN_DV, (h + 1) * DN_DV)

    def unit_inputs(u):
        bi, h = u
        lo, hi = h * DN_DK, (h + 1) * DN_DK
        y = ys[bi]
        qh = y[:, lo:hi]
        kh = y[:, n_key + lo:n_key + hi]
        vh = y[:, 2 * n_key + h * DN_DV:2 * n_key + (h + 1) * DN_DV]
        qh = qh * lax.rsqrt(jnp.sum(qh * qh, -1, keepdims=True) + RMS_EPS) * (DN_DK ** -0.5)
        kh = kh * lax.rsqrt(jnp.sum(kh * kh, -1, keepdims=True) + RMS_EPS)
        beta = _sigmoid(ba_ref[bi, :, h:h + 1])
        a_logit = ba_ref[bi, :, DN_HEADS + h:DN_HEADS + h + 1]
        g = -jnp.exp(alog_ref[0:1, lo:hi]) * _softplus(a_logit + dtb_ref[0:1, lo:hi])
        return qh, kh, vh, beta, g

    inp = each(unit_inputs)
    q = each(lambda u: inp[u][0])
    k = each(lambda u: inp[u][1])
    beta = each(lambda u: inp[u][3])
    parts = [p for u in units for p in _split3(inp[u][4])]
    gc_all = jnp.dot(tril, jnp.concatenate(parts, axis=1), preferred_element_type=F32)
    lanes = lambda i: slice(i * DN_DK, (i + 1) * DN_DK)
    gc = {u: gc_all[:, lanes(3 * i)] + gc_all[:, lanes(3 * i + 1)] + gc_all[:, lanes(3 * i + 2)]
          for i, u in enumerate(units)}

    def unit_decay(u):
        gc_row = jnp.transpose(gc[u])[0:1, :]
        diff = gc[u][:, :c] - gc_row
        return jnp.where(incl, jnp.exp(jnp.where(incl, diff, 0.0)), 0.0)

    decay = each(unit_decay)
    eg = each(lambda u: jnp.exp(gc[u]))
    kdec = each(lambda u: k[u] * jnp.exp(gc[u][c - 1:c, :] - gc[u]))
    kb = each(lambda u: k[u] * beta[u])
    akq = each(lambda u: _bdot_nt(jnp.concatenate([kb[u], q[u]], axis=0), k[u]))
    lmat = each(lambda u: jnp.where(strict, akq[u][:c] * decay[u], 0.0))
    qk = each(lambda u: akq[u][c:] * decay[u])
    tmat = each(lambda u: eye - lmat[u])
    lpow = each(lambda u: _bdot(lmat[u], lmat[u]))
    for step in range(1, n_double):
        if step < n_double - 1:
            prod = each(lambda u: _bdot(jnp.concatenate([lpow[u], tmat[u]], axis=0), lpow[u]))
            lpow = each(lambda u: prod[u][:c])
            tmat = each(lambda u: tmat[u] + prod[u][c:])
        else:
            prod = each(lambda u: _bdot(tmat[u], lpow[u]))
            tmat = each(lambda u: tmat[u] + prod[u])
    uw = each(lambda u: _bdot(tmat[u], jnp.concatenate([inp[u][2] * beta[u], kb[u] * eg[u]], axis=1)))
    s_old = each(lambda u: s_ref[u[0], u[1]])
    wq_s = each(lambda u: _bdot(jnp.concatenate([uw[u][:, DN_DV:], q[u] * eg[u]], axis=0), s_old[u]))
    v_new = each(lambda u: uw[u][:, :DN_DV] - wq_s[u][:c])
    o = each(lambda u: wq_s[u][c:] + _bdot(qk[u], v_new[u]))
    s_new = each(lambda u: s_old[u] * jnp.exp(gc[u][c - 1:c, :]) + _bdot(jnp.transpose(kdec[u]), v_new[u]))
    for u in units:
        bi, h = u
        s_ref[bi, h] = s_new[u]
        on = o[u] * lax.rsqrt(jnp.mean(o[u] * o[u], -1, keepdims=True) + RMS_EPS) * ng_ref[...]
        o_ref[bi, :, vsl(h)] = on * _silu(z_ref[bi, :, vsl(h)])

    @pl.when(j == nj - 1)
    def _():
        s_out_ref[...] = s_ref[...]
        for bi in range(bb):
            cbuf_out_ref[bi] = tails[bi]


def _dn_step_kernel(qkv_ref, ba_ref, z_ref, cbuf_ref, s0_ref, cw_ref, alog_ref, dtb_ref, ng_ref,
                    o_ref, s_out_ref, cbuf_out_ref, *, bb):
    hist = DN_CONV - 1
    n_key = DN_HEADS * DN_DK
    x = qkv_ref[...]
    y = x * cw_ref[hist:hist + 1, :]
    for tap in range(hist):
        y = y + cbuf_ref[:, tap, :] * cw_ref[tap:tap + 1, :]
    for tap in range(1, hist):
        cbuf_out_ref[:, tap - 1, :] = cbuf_ref[:, tap, :]
    cbuf_out_ref[:, hist - 1, :] = x
    y = _silu(y)
    ba = ba_ref[...]
    z = z_ref[...]
    row8 = lax.broadcasted_iota(jnp.int32, (V7X_SUBLANES, DN_DK), 0)
    row16 = lax.broadcasted_iota(jnp.int32, (V7X_BF16_SUBLANES, DN_DK), 0)
    heads = range(DN_HEADS)
    units = [(h, r) for h in heads for r in range(bb)]
    each = lambda f: {u: f(u) for u in units}
    one = lambda a, r: a[r:r + 1]

    def head_inputs(h):
        lo, hi = h * DN_DK, (h + 1) * DN_DK
        qh = y[:, lo:hi]
        kh = y[:, n_key + lo:n_key + hi]
        vh = y[:, 2 * n_key + h * DN_DV:2 * n_key + (h + 1) * DN_DV]
        qh = qh * lax.rsqrt(jnp.sum(qh * qh, -1, keepdims=True) + RMS_EPS) * (DN_DK ** -0.5)
        kh = kh * lax.rsqrt(jnp.sum(kh * kh, -1, keepdims=True) + RMS_EPS)
        beta = _sigmoid(ba[:, h:h + 1])
        a_logit = ba[:, DN_HEADS + h:DN_HEADS + h + 1]
        g = -jnp.exp(alog_ref[0:1, lo:hi]) * _softplus(a_logit + dtb_ref[0:1, lo:hi])
        eg = jnp.exp(g)
        return dict(k=kh, eg=eg, w=kh * beta * eg, qg=qh * eg, u=vh * beta, qk=jnp.sum(qh * kh, -1, keepdims=True))

    hd = [head_inputs(h) for h in heads]
    s_old = each(lambda u: s0_ref[u[1], u[0]])
    lhs = each(lambda u: jnp.where(row8 == 0, one(hd[u[0]]['w'], u[1]),
                                   jnp.where(row8 == 1, one(hd[u[0]]['qg'], u[1]), 0.0)))
    ws_qs = each(lambda u: _bdot(lhs[u], s_old[u]))
    v_new = each(lambda u: one(hd[u[0]]['u'], u[1]) - ws_qs[u][0:1])
    o_row = each(lambda u: ws_qs[u][1:2] + one(hd[u[0]]['qk'], u[1]) * v_new[u])
    k16 = each(lambda u: jnp.where(row16 == 0, one(hd[u[0]]['k'], u[1]), 0.0))
    v16 = each(lambda u: jnp.where(row16 == 0, v_new[u], 0.0))
    upd = each(lambda u: _bdot(jnp.transpose(k16[u]), v16[u]))
    for u in units:
        s_out_ref[u[1], u[0]] = s_old[u] * one(hd[u[0]]['eg'], u[1]) + upd[u]
    for h in heads:
        o = jnp.concatenate([o_row[(h, r)] for r in range(bb)], axis=0)
        o = o * lax.rsqrt(jnp.mean(o * o, -1, keepdims=True) + RMS_EPS) * ng_ref[...]
        o_ref[:, h * DN_DV:(h + 1) * DN_DV] = o * _silu(z[:, h * DN_DV:(h + 1) * DN_DV])


def _deltanet(qkv, ba, z, conv_buf, s0, conv_w, a_log, dt_bias, norm_g):
    b, t, n_qkv = qkv.shape
    n_val = DN_HEADS * DN_DV
    hist = DN_CONV - 1
    rep = lambda a: jnp.repeat(a.astype(F32), DN_DK).reshape(1, DN_HEADS * DN_DK)
    cw = conv_w.astype(F32)
    consts = (cw, rep(a_log), rep(dt_bias), norm_g.reshape(1, DN_DV).astype(F32))
    state_shape = (DN_HEADS, DN_DK, DN_DV)
    out_shape = [jax.ShapeDtypeStruct((b, t, n_val), F32),
                 jax.ShapeDtypeStruct((b,) + state_shape, F32),
                 jax.ShapeDtypeStruct((b, hist, n_qkv), F32)]
    if t == 1:
        bb = _row_tile(b, V7X_SUBLANES)
        const = lambda a: pl.BlockSpec(a.shape, lambda i: (0,) * a.ndim)
        o, s_new, cbuf_new = pl.pallas_call(
            functools.partial(_dn_step_kernel, bb=bb),
            grid=(b // bb,),
            in_specs=[pl.BlockSpec((bb, n_qkv), lambda i: (i, 0)),
                      pl.BlockSpec((bb, V7X_LANES), lambda i: (i, 0)),
                      pl.BlockSpec((bb, n_val), lambda i: (i, 0)),
                      pl.BlockSpec((bb, hist, n_qkv), lambda i: (i, 0, 0)),
                      pl.BlockSpec((bb,) + state_shape, lambda i: (i, 0, 0, 0))] + [const(a) for a in consts],
            out_specs=[pl.BlockSpec((bb, n_val), lambda i: (i, 0)),
                       pl.BlockSpec((bb,) + state_shape, lambda i: (i, 0, 0, 0)),
                       pl.BlockSpec((bb, hist, n_qkv), lambda i: (i, 0, 0))],
            out_shape=[jax.ShapeDtypeStruct((b, n_val), F32)] + out_shape[1:],
            compiler_params=_cparams("parallel"),
            name="deltanet_step",
        )(qkv.reshape(b, n_qkv), ba.reshape(b, V7X_LANES), z.reshape(b, n_val), conv_buf.astype(F32),
          s0.astype(F32), *consts)
        return o.reshape(b, 1, n_val), s_new, cbuf_new
    c = DN_CHUNK
    assert t % c == 0 and c >= hist
    bb = _row_tile(b, 4)
    const = lambda a: pl.BlockSpec(a.shape, lambda i, j: (0,) * a.ndim)
    return pl.pallas_call(
        functools.partial(_dn_seq_kernel, c=c, bb=bb),
        grid=(b // bb, t // c),
        in_specs=[pl.BlockSpec((bb, c, n_qkv), lambda i, j: (i, j, 0)),
                  pl.BlockSpec((bb, c, V7X_LANES), lambda i, j: (i, j, 0)),
                  pl.BlockSpec((bb, c, n_val), lambda i, j: (i, j, 0)),
                  pl.BlockSpec((bb, hist, n_qkv), lambda i, j: (i, 0, 0)),
                  pl.BlockSpec((bb,) + state_shape, lambda i, j: (i, 0, 0, 0))] + [const(a) for a in consts],
        out_specs=[pl.BlockSpec((bb, c, n_val), lambda i, j: (i, j, 0)),
                   pl.BlockSpec((bb,) + state_shape, lambda i, j: (i, 0, 0, 0)),
                   pl.BlockSpec((bb, hist, n_qkv), lambda i, j: (i, 0, 0))],
        out_shape=out_shape,
        scratch_shapes=[pltpu.VMEM((bb, _DN_HALO + c, n_qkv), F32),
                        pltpu.VMEM((bb,) + state_shape, F32)],
        compiler_params=_cparams("parallel", "arbitrary"),
        name="deltanet_seq",
    )(qkv, ba, z, conv_buf.astype(F32), s0.astype(F32), *consts)


def _s5_param_kernel(lre_ref, lim_ref, ldt_ref, lbre_ref, lbim_ref, fre_ref, fim_ref):
    lam_re = lre_ref[...]
    lam_im = lim_ref[...]
    dt = jnp.exp(ldt_ref[...])
    mag = jnp.exp(lam_re * dt)
    ang = lam_im * dt
    lb_re = mag * jnp.cos(ang)
    lb_im = mag * jnp.sin(ang)
    den = lam_re * lam_re + lam_im * lam_im
    lbre_ref[...] = lb_re
    lbim_ref[...] = lb_im
    fre_ref[...] = ((lb_re - 1.0) * lam_re + lb_im * lam_im) / den
    fim_ref[...] = (lb_im * lam_re - (lb_re - 1.0) * lam_im) / den


def _s5_discretize(lam_re, lam_im, log_dt):
    g, n = lam_re.shape
    ldt = jnp.broadcast_to(log_dt.astype(F32)[:, None], (g, n))
    shp = jax.ShapeDtypeStruct((g, n), F32)
    return pl.pallas_call(_s5_param_kernel, out_shape=[shp] * 4, name="s5_discretize")(
        lam_re.astype(F32), lam_im.astype(F32), ldt)


def _s5_kernel(u_ref, h0re_ref, h0im_ref, lbre_ref, lbim_ref, bre_ref, bim_ref, cre_ref, cim_ref,
               d_ref, wglu_ref, bglu_ref, y_ref, hre_out_ref, him_out_ref,
               sre_ref, sim_ref, cre_s, cim_s, *, tb, bb, lane_chunk):
    j = pl.program_id(1)
    nj = pl.num_programs(1)
    n_ch = u_ref.shape[-1]
    n_st = sre_ref.shape[-1]
    halves = bre_ref.shape[0]
    ch_h = n_ch // halves
    st_h = n_st // halves

    @pl.when(j == 0)
    def _():
        cre_s[...] = h0re_ref[...]
        cim_s[...] = h0im_ref[...]

    u = u_ref[...].reshape(tb * bb, n_ch)
    ub = u.astype(BF16)
    for hf in range(halves):
        uh = ub[:, hf * ch_h:(hf + 1) * ch_h]
        sre_ref[:, hf * st_h:(hf + 1) * st_h] = jnp.dot(uh, bre_ref[hf], preferred_element_type=F32)
        sim_ref[:, hf * st_h:(hf + 1) * st_h] = jnp.dot(uh, bim_ref[hf], preferred_element_type=F32)

    for c0 in range(0, n_st, lane_chunk):
        cs = slice(c0, c0 + lane_chunk)
        lr = jnp.broadcast_to(lbre_ref[0:1, cs], (bb, lane_chunk))
        li = jnp.broadcast_to(lbim_ref[0:1, cs], (bb, lane_chunk))

        def body(t, carry, cs=cs, lr=lr, li=li):
            hr, hi = carry
            r = pl.multiple_of(t * bb, bb)
            nr = lr * hr - li * hi + sre_ref[pl.ds(r, bb), cs]
            ni = lr * hi + li * hr + sim_ref[pl.ds(r, bb), cs]
            sre_ref[pl.ds(r, bb), cs] = nr
            sim_ref[pl.ds(r, bb), cs] = ni
            return nr, ni

        hr, hi = lax.fori_loop(0, tb, body, (cre_s[:, cs], cim_s[:, cs]))
        cre_s[:, cs] = hr
        cim_s[:, cs] = hi

    ys = []
    for hf in range(halves):
        hre = sre_ref[:, hf * st_h:(hf + 1) * st_h].astype(BF16)
        him = sim_ref[:, hf * st_h:(hf + 1) * st_h].astype(BF16)
        ys.append(jnp.dot(hre, cre_ref[hf], preferred_element_type=F32)
                  - jnp.dot(him, cim_ref[hf], preferred_element_type=F32))
    y = jnp.concatenate(ys, axis=1) + d_ref[...] * u
    y = jax.nn.gelu(y)
    y = y * _sigmoid(_bdot(y, wglu_ref[...]) + bglu_ref[...])
    y_ref[...] = y.reshape(tb, bb, n_ch)

    @pl.when(j == nj - 1)
    def _():
        hre_out_ref[...] = cre_s[...]
        him_out_ref[...] = cim_s[...]


def _s5(u_tm, h0_re, h0_im, lam_re, lam_im, log_dt, b_re, b_im, c_re, c_im, d_skip, w_glu, b_glu, *, halves=2):
    t, b, n_ch = u_tm.shape
    g, n, p = b_re.shape
    n_st = g * n
    lb_re, lb_im, f_re, f_im = _s5_discretize(lam_re, lam_im, log_dt)
    b_re = b_re.astype(F32)
    b_im = b_im.astype(F32)
    bb_re = f_re[..., None] * b_re - f_im[..., None] * b_im
    bb_im = f_re[..., None] * b_im + f_im[..., None] * b_re
    gh = g // halves

    def in_blocks(a):
        a = a.reshape(halves, gh, n, p)
        eye = jnp.eye(gh, dtype=F32)
        return jnp.einsum('hgnp,gk->hgpkn', a, eye).reshape(halves, gh * p, gh * n).astype(BF16)

    def out_blocks(a):
        a = a.astype(F32).reshape(halves, gh, p, n)
        eye = jnp.eye(gh, dtype=F32)
        return jnp.einsum('hgpn,gk->hgnkp', a, eye).reshape(halves, gh * n, gh * p).astype(BF16)

    bre_m, bim_m = in_blocks(bb_re), in_blocks(bb_im)
    cre_m, cim_m = out_blocks(c_re), out_blocks(c_im)
    bb = b if b <= 128 else 128
    assert b % bb == 0
    tb = _row_tile(t, max(1, 512 // bb))
    lane_chunk = max(V7X_LANES, min(n_st, 8192 // bb))
    const = lambda a: pl.BlockSpec(a.shape, lambda i, j: (0,) * a.ndim)
    row = lambda a, m: a.reshape(1, m).astype(F32)
    args = (u_tm, h0_re.reshape(b, n_st).astype(F32), h0_im.reshape(b, n_st).astype(F32),
            row(lb_re, n_st), row(lb_im, n_st), bre_m, bim_m, cre_m, cim_m,
            row(d_skip, n_ch), w_glu.astype(BF16), row(b_glu, n_ch))
    y, hre, him = pl.pallas_call(
        functools.partial(_s5_kernel, tb=tb, bb=bb, lane_chunk=lane_chunk),
        grid=(b // bb, t // tb),
        in_specs=[pl.BlockSpec((tb, bb, n_ch), lambda i, j: (j, i, 0)),
                  pl.BlockSpec((bb, n_st), lambda i, j: (i, 0)),
                  pl.BlockSpec((bb, n_st), lambda i, j: (i, 0))] + [const(a) for a in args[3:]],
        out_specs=[pl.BlockSpec((tb, bb, n_ch), lambda i, j: (j, i, 0)),
                   pl.BlockSpec((bb, n_st), lambda i, j: (i, 0)),
                   pl.BlockSpec((bb, n_st), lambda i, j: (i, 0))],
        out_shape=[jax.ShapeDtypeStruct((t, b, n_ch), F32),
                   jax.ShapeDtypeStruct((b, n_st), F32),
                   jax.ShapeDtypeStruct((b, n_st), F32)],
        scratch_shapes=[pltpu.VMEM((tb * bb, n_st), F32), pltpu.VMEM((tb * bb, n_st), F32),
                        pltpu.VMEM((bb, n_st), F32), pltpu.VMEM((bb, n_st), F32)],
        compiler_params=_cparams("parallel", "arbitrary"),
        name="s5",
    )(*args)
    return y, hre.reshape(b, g, n), him.reshape(b, g, n)


def _ab_layer(x, dn_s, dn_conv, ssm_re, ssm_im, p, i, ln_g, ln_b, *, alpha):
    b, t, d = x.shape
    xr = x if t > 1 else x.reshape(1, b, d)
    bx, tx, _ = xr.shape
    qkv, z, ba, u_tm = _ab_in_proj(xr, p['ab_w_in'][i])
    n_ch = u_tm.shape[1] // bx
    shp = lambda a: a.reshape(b, t, a.shape[-1])
    o, s_new, conv_new = _deltanet(shp(qkv), shp(ba), shp(z), dn_conv, dn_s, p['dn_conv_w'][i],
                                   p['dn_a_log'][i], p['dn_dt_bias'][i], p['dn_norm_g'][i])
    y_tm, h_re, h_im = _s5(u_tm.reshape(t, b, n_ch), ssm_re, ssm_im, p['ssm_lambda_re'][i],
                           p['ssm_lambda_im'][i], p['ssm_log_dt'][i], p['ssm_b_re'][i], p['ssm_b_im'][i],
                           p['ssm_c_re'][i], p['ssm_c_im'][i], p['ssm_d'][i], p['ssm_w_glu'][i],
                           p['ssm_b_glu'][i])
    x_new = _ab_out_proj(o.reshape(bx, tx, -1), y_tm.reshape(tx, bx * n_ch), p['ab_w_out'][i], xr,
                         ln_g, ln_b, alpha=alpha)
    return x_new.reshape(b, t, d), s_new, conv_new, h_re, h_im


_CC_HALO = 32
_CC_ROW_TILES = 8


def _cconv_seq_kernel(h_ref, buf_ref, w_ref, bdw_ref, g_ref, b_ref, o_ref, xp_ref, xs_ref, acc_ref, *, tt, width):
    j = pl.program_id(1)
    hist = width - 1
    d = h_ref.shape[-1]
    rows = V7X_SUBLANES

    @pl.when(j == 0)
    def _():
        xp_ref[0:_CC_HALO - hist, :] = jnp.zeros((_CC_HALO - hist, d), F32)
        xp_ref[_CC_HALO - hist:_CC_HALO, :] = buf_ref[...]

    xp_ref[_CC_HALO:_CC_HALO + tt, :] = h_ref[...]
    n_shift = xs_ref.shape[1]
    for s in range(1, rows):
        xs_ref[s - 1] = xp_ref[pl.ds(s, n_shift), :]
    base = _CC_HALO - hist
    for c0 in range(0, d, V7X_LANES):
        cs = slice(c0, c0 + V7X_LANES)
        taps = [jnp.broadcast_to(w_ref[k:k + 1, cs], (rows, V7X_LANES)) for k in range(width)]

        def body(i, carry, taps=taps, cs=cs):
            r = pl.multiple_of(i * (rows * _CC_ROW_TILES), rows * _CC_ROW_TILES)
            accs = [None] * _CC_ROW_TILES
            for s in range(rows):
                ks = [k for k in range(width) if (base + k) % rows == s]
                if not ks:
                    continue
                n_rows = rows * (_CC_ROW_TILES + max((base + k) // rows for k in ks))
                big = xp_ref[pl.ds(r, n_rows), cs] if s == 0 else xs_ref[s - 1, pl.ds(r, n_rows), cs]
                for k in ks:
                    a = (base + k) // rows
                    for sub in range(_CC_ROW_TILES):
                        term = big[(sub + a) * rows:(sub + a + 1) * rows] * taps[k]
                        accs[sub] = term if accs[sub] is None else accs[sub] + term
            for sub in range(_CC_ROW_TILES):
                acc_ref[pl.ds(r + sub * rows, rows), cs] = accs[sub]
            return carry

        lax.fori_loop(0, tt // (rows * _CC_ROW_TILES), body, 0)
    xp_ref[0:_CC_HALO, :] = xp_ref[tt:tt + _CC_HALO, :]
    y = _layer_norm(acc_ref[...] + bdw_ref[...], g_ref[...], b_ref[...])
    o_ref[...] = _silu(y)


def _cconv_seq(h, buf, w_dw, b_dw, ln_g, ln_b, *, tt=256):
    b, t, d = h.shape
    width = w_dw.shape[0]
    tt = _row_tile(t, tt)
    assert tt >= _CC_HALO and width - 1 <= _CC_HALO
    row = lambda a: a.reshape(1, d).astype(F32)
    const = lambda a: pl.BlockSpec(a.shape, lambda i, j: (0, 0))
    args = (w_dw.astype(F32), row(b_dw), row(ln_g), row(ln_b))
    return pl.pallas_call(
        functools.partial(_cconv_seq_kernel, tt=tt, width=width),
        grid=(b, t // tt),
        in_specs=[pl.BlockSpec((None, tt, d), lambda i, j: (i, j, 0)),
                  pl.BlockSpec((None, width - 1, d), lambda i, j: (i, 0, 0))] + [const(a) for a in args],
        out_specs=pl.BlockSpec((None, tt, d), lambda i, j: (i, j, 0)),
        out_shape=jax.ShapeDtypeStruct((b, t, d), F32),
        scratch_shapes=[pltpu.VMEM((_CC_HALO + tt, d), F32),
                        pltpu.VMEM((V7X_SUBLANES - 1, _CC_HALO + tt - V7X_SUBLANES, d), F32),
                        pltpu.VMEM((tt, d), F32)],
        compiler_params=_cparams("parallel", "arbitrary"),
        name="cconv_seq",
    )(h, buf.astype(F32), *args)


def _cconv_step_kernel(h_ref, buf_ref, w_ref, bdw_ref, g_ref, b_ref, o_ref, *, width):
    hist = width - 1
    acc = jnp.sum(buf_ref[...] * w_ref[0:hist, :][None], axis=1) + h_ref[...] * w_ref[hist:width, :]
    o_ref[...] = _silu(_layer_norm(acc + bdw_ref[...], g_ref[...], b_ref[...]))


def _cconv_step(h, buf, w_dw, b_dw, ln_g, ln_b, *, bb=8):
    b, d = h.shape
    width = w_dw.shape[0]
    bb = _row_tile(b, bb)
    row = lambda a: a.reshape(1, d).astype(F32)
    const = lambda a: pl.BlockSpec(a.shape, lambda i: (0, 0))
    args = (w_dw.astype(F32), row(b_dw), row(ln_g), row(ln_b))
    return pl.pallas_call(
        functools.partial(_cconv_step_kernel, width=width),
        grid=(b // bb,),
        in_specs=[pl.BlockSpec((bb, d), lambda i: (i, 0)),
                  pl.BlockSpec((bb, width - 1, d), lambda i: (i, 0, 0))] + [const(a) for a in args],
        out_specs=pl.BlockSpec((bb, d), lambda i: (i, 0)),
        out_shape=jax.ShapeDtypeStruct((b, d), F32),
        compiler_params=_cparams("parallel"),
        name="cconv_step",
    )(h, buf.astype(F32), *args)


def _conv_layer(x, buf, p, i, ln_g, ln_b, *, alpha):
    b, t, d = x.shape
    x2 = x.reshape(b * t, d)
    h = _linear(x2, p['cc_w_pw1'][i], p['cc_b_pw1'][i], glu=True)
    args = (p['cc_w_dw'][i], p['cc_b_dw'][i], p['cc_ln_g'][i], p['cc_ln_b'][i])
    if t == 1:
        hc = _cconv_step(h, buf, *args)
    else:
        hc = _cconv_seq(h.reshape(b, t, d), buf, *args).reshape(b * t, d)
    new_buf = jnp.concatenate([buf.astype(F32), h.reshape(b, t, d)], axis=1)[:, t:]
    x_new = _linear_postnorm(hc, p['cc_w_pw2'][i], p['cc_b_pw2'][i], x2, ln_g, ln_b, alpha=alpha)
    return x_new.reshape(b, t, d), new_buf


def _mem_attn_seq_kernel(x_ref, wq_ref, k_ref, v_ref, wo_ref, g_ref, b_ref, o_ref, *, alpha, heads):
    x = x_ref[...]
    d = x.shape[-1]
    hd = d // heads
    q = _bdot(x, wq_ref[...]) * (hd ** -0.5)
    k = k_ref[...].astype(BF16)
    v = v_ref[...].astype(BF16)
    hsl = [slice(h * hd, (h + 1) * hd) for h in range(heads)]
    ss = [_bdot_nt(q[:, hs], k[:, hs]) for hs in hsl]
    es = [jnp.exp(s - jnp.max(s, axis=-1, keepdims=True)) for s in ss]
    ps = [e / jnp.sum(e, axis=-1, keepdims=True) for e in es]
    o = jnp.concatenate([_bdot(a, v[:, hs]) for a, hs in zip(ps, hsl)], axis=1)
    hres = _bdot(o, wo_ref[...])
    o_ref[...] = _layer_norm(alpha * x + hres, g_ref[...], b_ref[...])


def _mem_attn_seq(x, mk, mv, layer, wq, wo, ln_g, ln_b, *, alpha, tq=512):
    b, t, d = x.shape
    m = mk.shape[2]
    tq = _row_tile(t, tq)
    row = lambda a: a.reshape(1, d).astype(F32)
    const = lambda a: pl.BlockSpec(a.shape, lambda i, j: (0, 0))
    wqb, wob = wq.astype(BF16), wo.astype(BF16)
    return pl.pallas_call(
        functools.partial(_mem_attn_seq_kernel, alpha=alpha, heads=MEM_HEADS),
        grid=(b, t // tq),
        in_specs=[pl.BlockSpec((None, tq, d), lambda i, j: (i, j, 0)),
                  const(wqb),
                  pl.BlockSpec((None, None, m, d), lambda i, j: (layer, i, 0, 0)),
                  pl.BlockSpec((None, None, m, d), lambda i, j: (layer, i, 0, 0)),
                  const(wob), pl.BlockSpec((1, d), lambda i, j: (0, 0)),
                  pl.BlockSpec((1, d), lambda i, j: (0, 0))],
        out_specs=pl.BlockSpec((None, tq, d), lambda i, j: (i, j, 0)),
        out_shape=jax.ShapeDtypeStruct((b, t, d), F32),
        compiler_params=_cparams("parallel", "parallel"),
        name="mem_attn_seq",
    )(x, wqb, mk, mv, wob, row(ln_g), row(ln_b))


def _mem_attn_step_kernel(q_ref, k_ref, v_ref, o_ref, *, heads, bb):
    i = pl.program_id(0)
    hd = q_ref.shape[-1]
    for r in range(bb):
        q = q_ref[i * bb + r] * (hd ** -0.5)
        s = jnp.sum(k_ref[r] * q[None], axis=-1, keepdims=True)
        s = s - jnp.max(s, axis=0, keepdims=True)
        e = jnp.exp(s)
        a = e / jnp.sum(e, axis=0, keepdims=True)
        o_ref[i * bb + r] = jnp.sum(a * v_ref[r], axis=0)


def _mem_attn_step(q, mk_all, mv_all, layer, *, bb=4):
    b, heads, hd = q.shape
    m = mk_all.shape[2]
    bb = _row_tile(b, bb)
    return pl.pallas_call(
        functools.partial(_mem_attn_step_kernel, heads=heads, bb=bb),
        grid=(b // bb,),
        in_specs=[pl.BlockSpec((b, heads, hd), lambda i: (0, 0, 0)),
                  pl.BlockSpec((None, bb, m, heads, hd), lambda i: (layer, i, 0, 0, 0)),
                  pl.BlockSpec((None, bb, m, heads, hd), lambda i: (layer, i, 0, 0, 0))],
        out_specs=pl.BlockSpec((b, heads, hd), lambda i: (0, 0, 0)),
        out_shape=jax.ShapeDtypeStruct((b, heads, hd), F32),
        compiler_params=_cparams("arbitrary"),
        name="mem_attn_step",
    )(q, mk_all, mv_all)


def _mem_layer(x, mk_all, mv_all, layer, wq, wo, ln_g, ln_b, *, alpha):
    b, t, d = x.shape
    n_layers, _, m, heads, hd = mk_all.shape
    if t == 1:
        x2 = x.reshape(b, d)
        q = _linear(x2, wq).reshape(b, heads, hd)
        o = _mem_attn_step(q, mk_all, mv_all, layer).reshape(b, d)
        return _linear_postnorm(o, wo, None, x2, ln_g, ln_b, alpha=alpha).reshape(b, t, d)
    mk_all = mk_all.reshape(n_layers, b, m, d)
    mv_all = mv_all.reshape(n_layers, b, m, d)
    return _mem_attn_seq(x, mk_all, mv_all, layer, wq, wo, ln_g, ln_b, alpha=alpha)


_NEG = -1e30


def _router_kernel(x_ref, w_ref, b_ref, ids_ref, wts_ref, *, groups, per_group):
    x = x_ref[...]
    w = w_ref[...]
    xh = x.astype(BF16)
    xl = (x - xh.astype(F32)).astype(BF16)
    wh = w.astype(BF16)
    wl = (w - wh.astype(F32)).astype(BF16)
    logits = (jnp.dot(xh, wh, preferred_element_type=F32) + jnp.dot(xh, wl, preferred_element_type=F32)
              + jnp.dot(xl, wh, preferred_element_type=F32)) + b_ref[...]
    lane = lax.broadcasted_iota(jnp.int32, logits.shape, 1)
    n_exp = groups * per_group
    is_g = lane < groups
    gl = jnp.where(is_g, logits, _NEG)
    gmax = jnp.max(gl, axis=-1, keepdims=True)
    gsum = jnp.sum(jnp.where(is_g, jnp.exp(gl - gmax), 0.0), axis=-1, keepdims=True)
    gw = 1.0 / gsum
    gi = jnp.min(jnp.where(gl == gmax, lane, V7X_LANES), axis=-1, keepdims=True)
    lane_grp = (lane - groups) // per_group
    sel = (lane >= groups) & (lane < groups + n_exp) & (lane_grp == gi)
    el = jnp.where(sel, logits, _NEG)
    emax = jnp.max(el, axis=-1, keepdims=True)
    ee = jnp.where(sel, jnp.exp(el - emax), 0.0)
    ep = jnp.where(sel, ee / jnp.sum(ee, axis=-1, keepdims=True), -1.0)
    p1 = jnp.max(ep, axis=-1, keepdims=True)
    i1 = jnp.min(jnp.where(ep == p1, lane, V7X_LANES), axis=-1, keepdims=True)
    ep2 = jnp.where(lane == i1, -1.0, ep)
    p2 = jnp.max(ep2, axis=-1, keepdims=True)
    i2 = jnp.min(jnp.where(ep2 == p2, lane, V7X_LANES), axis=-1, keepdims=True)
    denom = p1 + p2
    ids_ref[...] = jnp.where(lane == 0, i1 - groups, jnp.where(lane == 1, i2 - groups, 0))
    wts_ref[...] = jnp.where(lane == 0, gw * p1 / denom, jnp.where(lane == 1, gw * p2 / denom, 0.0))


def _router(x, w_group, b_group, w_expert, b_expert, *, tm=512):
    n, d = x.shape
    groups = w_group.shape[1]
    n_exp = w_expert.shape[1]
    pad = V7X_LANES - groups - n_exp
    w = jnp.pad(jnp.concatenate([w_group, w_expert], axis=1).astype(F32), ((0, 0), (0, pad)))
    b = jnp.pad(jnp.concatenate([b_group, b_expert]).astype(F32), (0, pad)).reshape(1, V7X_LANES)
    tm = _row_tile(n, tm)
    return pl.pallas_call(
        functools.partial(_router_kernel, groups=groups, per_group=n_exp // groups),
        grid=(n // tm,),
        in_specs=[pl.BlockSpec((tm, d), lambda i: (i, 0)),
                  pl.BlockSpec((d, V7X_LANES), lambda i: (0, 0)),
                  pl.BlockSpec((1, V7X_LANES), lambda i: (0, 0))],
        out_specs=[pl.BlockSpec((tm, V7X_LANES), lambda i: (i, 0)),
                   pl.BlockSpec((tm, V7X_LANES), lambda i: (i, 0))],
        out_shape=[jax.ShapeDtypeStruct((n, V7X_LANES), jnp.int32),
                   jax.ShapeDtypeStruct((n, V7X_LANES), F32)],
        compiler_params=_cparams("parallel"),
        name="moe_router",
    )(x, w, b)


def _expert_ffn_kernel(te_ref, nv_ref, x_ref, wg_ref, wu_ref, wd_ref, o_ref):
    i = pl.program_id(0)

    @pl.when(i < nv_ref[0])
    def _():
        x = x_ref[...]
        hid = _silu(_bdot(x, wg_ref[...])) * _bdot(x, wu_ref[...])
        o_ref[...] = _bdot(hid, wd_ref[...])

    @pl.when(i >= nv_ref[0])
    def _():
        o_ref[...] = jnp.zeros_like(o_ref)


def _expert_ffn(x_sorted, tile_expert, n_valid, w_gate, w_up, w_down, layer, *, tm):
    r, d = x_sorted.shape
    f = w_gate.shape[-1]
    return pl.pallas_call(
        _expert_ffn_kernel,
        grid_spec=pltpu.PrefetchScalarGridSpec(
            num_scalar_prefetch=2,
            grid=(r // tm,),
            in_specs=[pl.BlockSpec((tm, d), lambda i, te, nv: (i, 0)),
                      pl.BlockSpec((None, None, d, f), lambda i, te, nv: (layer, te[i], 0, 0)),
                      pl.BlockSpec((None, None, d, f), lambda i, te, nv: (layer, te[i], 0, 0)),
                      pl.BlockSpec((None, None, f, d), lambda i, te, nv: (layer, te[i], 0, 0))],
            out_specs=pl.BlockSpec((tm, d), lambda i, te, nv: (i, 0)),
        ),
        out_shape=jax.ShapeDtypeStruct((r, d), F32),
        compiler_params=_cparams("arbitrary"),
        name="moe_expert_ffn",
    )(tile_expert, n_valid, x_sorted, w_gate, w_up, w_down)


def _rank_kernel(ids_ref, cnt_in_ref, rank_ref, cnt_ref, carry_ref):
    i = pl.program_id(0)
    tm = ids_ref.shape[0]

    @pl.when(i == 0)
    def _():
        carry_ref[...] = cnt_in_ref[...].astype(F32)

    ids = ids_ref[...]
    lane = lax.broadcasted_iota(jnp.int32, ids.shape, 1)
    oh0 = jnp.where(lane == ids[:, 0:1], 1.0, 0.0)
    oh1 = jnp.where(lane == ids[:, 1:2], 1.0, 0.0)
    both = oh0 + oh1
    ii = lax.broadcasted_iota(jnp.int32, (tm, tm), 0)
    jj = lax.broadcasted_iota(jnp.int32, (tm, tm), 1)
    earlier = jnp.where(ii > jj, 1.0, 0.0).astype(BF16)
    prefix = jnp.dot(earlier, both.astype(BF16), preferred_element_type=F32) + carry_ref[...]
    r0 = jnp.sum(oh0 * prefix, axis=-1, keepdims=True)
    r1 = jnp.sum(oh1 * prefix, axis=-1, keepdims=True)
    rank_ref[...] = jnp.where(lane == 0, r0, jnp.where(lane == 1, r1, 0.0)).astype(jnp.int32)
    total = carry_ref[...] + jnp.sum(both, axis=0, keepdims=True)
    carry_ref[...] = total
    cnt_ref[...] = total.astype(jnp.int32)


def _expert_ranks(ids, counts_in, *, tm=512):
    n = ids.shape[0]
    tm = _row_tile(n, tm)
    return pl.pallas_call(
        _rank_kernel,
        grid=(n // tm,),
        in_specs=[pl.BlockSpec((tm, V7X_LANES), lambda i: (i, 0)),
                  pl.BlockSpec((1, V7X_LANES), lambda i: (0, 0))],
        out_specs=[pl.BlockSpec((tm, V7X_LANES), lambda i: (i, 0)),
                   pl.BlockSpec((1, V7X_LANES), lambda i: (0, 0))],
        out_shape=[jax.ShapeDtypeStruct((n, V7X_LANES), jnp.int32),
                   jax.ShapeDtypeStruct((1, V7X_LANES), jnp.int32)],
        scratch_shapes=[pltpu.VMEM((1, V7X_LANES), F32)],
        compiler_params=_cparams("arbitrary"),
        name="moe_rank",
    )(ids, counts_in)


def _for_each_row(tm, fn):
    def body(i, carry):
        base = pl.multiple_of(i * V7X_SUBLANES, V7X_SUBLANES)
        for sub in range(V7X_SUBLANES):
            fn(base, sub)
        return carry

    lax.fori_loop(0, tm // V7X_SUBLANES, body, 0)


def _dispatch_kernel(slot_ref, x_ref, xs_init_ref, xs_ref, sem):
    del xs_init_ref
    tm = x_ref.shape[0]

    def issue(base, sub):
        for k in range(MOE_TOP_K):
            s = slot_ref[0, 0, MOE_TOP_K * (base + sub) + k]
            pltpu.make_async_copy(x_ref.at[pl.ds(base + sub, 1), :], xs_ref.at[pl.ds(s, 1), :],
                                  sem).start(priority=k)

    _for_each_row(tm, issue)
    for k in range(MOE_TOP_K):
        pltpu.make_async_copy(x_ref, xs_ref.at[pl.ds(0, tm), :], sem).wait()


def _dispatch(x, slot, xs_init, *, tm):
    n, d = x.shape
    rows = xs_init.shape[0]
    slot3 = slot.reshape(n // tm, 1, MOE_TOP_K * tm)
    return pl.pallas_call(
        _dispatch_kernel,
        grid=(n // tm,),
        in_specs=[pl.BlockSpec((1, 1, MOE_TOP_K * tm), lambda i: (i, 0, 0), memory_space=pltpu.SMEM),
                  pl.BlockSpec((tm, d), lambda i: (i, 0)),
                  pl.BlockSpec(memory_space=pl.ANY)],
        out_specs=pl.BlockSpec(memory_space=pl.ANY),
        out_shape=jax.ShapeDtypeStruct((rows, d), F32),
        scratch_shapes=[pltpu.SemaphoreType.DMA(())],
        input_output_aliases={2: 0},
        compiler_params=_cparams("arbitrary"),
        name="moe_dispatch",
    )(slot3, x, xs_init)


def _combine_kernel(slot_ref, slot_next_ref, wts_ref, x_ref, g_ref, b_ref, y_hbm, o_ref, ybuf, sem, *, alpha):
    i = pl.program_id(0)
    n = pl.num_programs(0)
    tm = x_ref.shape[0]
    cur = i % 2

    def issue(idx_ref, buf):
        def row(base, sub):
            for k in range(MOE_TOP_K):
                s = idx_ref[0, 0, MOE_TOP_K * (base + sub) + k]
                pltpu.make_async_copy(y_hbm.at[pl.ds(s, 1), :], ybuf.at[buf, k, pl.ds(base + sub, 1), :],
                                      sem.at[buf]).start(priority=k)

        _for_each_row(tm, row)

    @pl.when(i == 0)
    def _():
        issue(slot_ref, 0)

    @pl.when(i + 1 < n)
    def _():
        issue(slot_next_ref, 1 - cur)

    for k in range(MOE_TOP_K):
        pltpu.make_async_copy(y_hbm.at[pl.ds(0, tm), :], ybuf.at[cur, k], sem.at[cur]).wait()
    wts = wts_ref[...]
    y = wts[:, 0:1] * ybuf[cur, 0] + wts[:, 1:2] * ybuf[cur, 1]
    o_ref[...] = _layer_norm(alpha * x_ref[...] + y, g_ref[...], b_ref[...])


def _combine_postnorm(y_sorted, slot, wts, x, g, b, *, alpha, tm):
    n, d = x.shape
    nt = n // tm
    slot3 = slot.reshape(nt, 1, MOE_TOP_K * tm)
    row = lambda a: a.reshape(1, d).astype(F32)
    smem = lambda f: pl.BlockSpec((1, 1, MOE_TOP_K * tm), f, memory_space=pltpu.SMEM)
    return pl.pallas_call(
        functools.partial(_combine_kernel, alpha=alpha),
        grid=(nt,),
        in_specs=[smem(lambda i: (i, 0, 0)),
                  smem(lambda i: (jnp.minimum(i + 1, nt - 1), 0, 0)),
                  pl.BlockSpec((tm, V7X_LANES), lambda i: (i, 0)),
                  pl.BlockSpec((tm, d), lambda i: (i, 0)),
                  pl.BlockSpec((1, d), lambda i: (0, 0)),
                  pl.BlockSpec((1, d), lambda i: (0, 0)),
                  pl.BlockSpec(memory_space=pl.ANY)],
        out_specs=pl.BlockSpec((tm, d), lambda i: (i, 0)),
        out_shape=jax.ShapeDtypeStruct((n, d), F32),
        scratch_shapes=[pltpu.VMEM((2, MOE_TOP_K, tm, d), F32), pltpu.SemaphoreType.DMA((2,))],
        compiler_params=_cparams("arbitrary"),
        name="moe_combine",
    )(slot3, slot3, wts, x, row(g), row(b), y_sorted)


def _moe_layer(xs, p, layer, ln_g, ln_b, *, alpha):
    w_gate, w_up, w_down = p['moe_w_gate'], p['moe_w_up'], p['moe_w_down']
    n_exp = w_gate.shape[1]
    d = xs[0].shape[-1]
    x2s = [x.reshape(-1, d) for x in xs]
    n_total = sum(x2.shape[0] for x2 in x2s)
    tm = 256 if n_total >= 4096 else V7X_BF16_SUBLANES
    routed = [_router(x2, p['moe_w_group'][layer], p['moe_b_group'][layer], p['moe_w_expert'][layer],
                      p['moe_b_expert'][layer]) for x2 in x2s]
    counts = jnp.zeros((1, V7X_LANES), jnp.int32)
    ranks = []
    for ids, _ in routed:
        rank, counts = _expert_ranks(ids, counts)
        ranks.append(rank)
    counts = counts[0, :n_exp]
    padded = ((counts + tm - 1) // tm) * tm
    ends = jnp.cumsum(padded)
    starts = ends - padded
    n_tiles = (MOE_TOP_K * n_total + n_exp * (tm - 1)) // tm
    n_valid = (ends[-1] // tm).astype(jnp.int32)
    tile_start = jnp.minimum(jnp.arange(n_tiles, dtype=jnp.int32), n_valid - 1) * tm
    tile_expert = jnp.minimum(jnp.sum((ends[None, :] <= tile_start[:, None]).astype(jnp.int32), axis=1), n_exp - 1)
    x_sorted = jnp.zeros((n_tiles * tm, d), F32)
    slots = []
    for x2, (ids, _), rank in zip(x2s, routed, ranks):
        choice = ids[:, :MOE_TOP_K]
        onehot = choice[:, :, None] == jnp.arange(n_exp, dtype=jnp.int32)[None, None, :]
        slot = jnp.sum(jnp.where(onehot, starts[None, None, :], 0), axis=-1) + rank[:, :MOE_TOP_K]
        x_sorted = _dispatch(x2, slot, x_sorted, tm=_row_tile(x2.shape[0], 256))
        slots.append(slot)
    y_sorted = _expert_ffn(x_sorted, tile_expert, n_valid.reshape(1), w_gate, w_up, w_down, layer, tm=tm)
    return [_combine_postnorm(y_sorted, slot, wts, x2, ln_g, ln_b, alpha=alpha,
                              tm=_row_tile(x2.shape[0], 256)).reshape(x.shape)
            for x, x2, (_, wts), slot in zip(xs, x2s, routed, slots)]


def _trunks(groups, p):
    depth = p['ln_g'].shape[0]
    alpha = (2.0 * depth) ** 0.25
    xs = [grp[0] for grp in groups]
    outs = [([], [], [], [], []) for _ in groups]
    for layer in range(depth):
        i = layer // 2
        g, bta = p['ln_g'][layer], p['ln_b'][layer]
        for gi, (_, dn_s, dn_conv, ssm_re, ssm_im, cconv, mem_k, mem_v) in enumerate(groups):
            out_s, out_conv, out_re, out_im, out_cc = outs[gi]
            x = xs[gi]
            if layer % 2 == 0:
                x, s_new, conv_new, re_new, im_new = _ab_layer(x, dn_s[i], dn_conv[i], ssm_re[i], ssm_im[i], p, i,
                                                               g[0], bta[0], alpha=alpha)
                out_s.append(s_new)
                out_conv.append(conv_new)
                out_re.append(re_new)
                out_im.append(im_new)
            else:
                x, cc_new = _conv_layer(x, cconv[i], p, i, g[0], bta[0], alpha=alpha)
                out_cc.append(cc_new)
            xs[gi] = _mem_layer(x, mem_k, mem_v, layer, p['mem_wq'][layer], p['mem_wo'][layer], g[1], bta[1],
                                alpha=alpha)
        xs = _moe_layer(xs, p, layer, g[2], bta[2], alpha=alpha)
    return [(x,) + tuple(jnp.stack(o) for o in out) for x, out in zip(xs, outs)]


def kernel(x_prompt, x_sample, state_dn_s, state_dn_conv, state_ssm_re, state_ssm_im, state_cconv,
           cache_mem_k, cache_mem_v, mem_prompt, ab_w_in, dn_conv_w, dn_a_log, dn_dt_bias, dn_norm_g,
           ssm_lambda_re, ssm_lambda_im, ssm_log_dt, ssm_b_re, ssm_b_im, ssm_c_re, ssm_c_im, ssm_d,
           ssm_w_glu, ssm_b_glu, ab_w_out, cc_w_pw1, cc_b_pw1, cc_w_dw, cc_b_dw, cc_ln_g, cc_ln_b,
           cc_w_pw2, cc_b_pw2, mem_wq, mem_wk, mem_wv, mem_wo, ln_g, ln_b, moe_w_group, moe_b_group,
           moe_w_expert, moe_b_expert, moe_w_gate, moe_w_up, moe_w_down):
    p = dict(ab_w_in=ab_w_in, dn_conv_w=dn_conv_w, dn_a_log=dn_a_log, dn_dt_bias=dn_dt_bias,
             dn_norm_g=dn_norm_g, ssm_lambda_re=ssm_lambda_re, ssm_lambda_im=ssm_lambda_im,
             ssm_log_dt=ssm_log_dt, ssm_b_re=ssm_b_re, ssm_b_im=ssm_b_im, ssm_c_re=ssm_c_re,
             ssm_c_im=ssm_c_im, ssm_d=ssm_d, ssm_w_glu=ssm_w_glu, ssm_b_glu=ssm_b_glu, ab_w_out=ab_w_out,
             cc_w_pw1=cc_w_pw1, cc_b_pw1=cc_b_pw1, cc_w_dw=cc_w_dw, cc_b_dw=cc_b_dw, cc_ln_g=cc_ln_g,
             cc_ln_b=cc_ln_b, cc_w_pw2=cc_w_pw2, cc_b_pw2=cc_b_pw2, mem_wq=mem_wq, mem_wo=mem_wo,
             ln_g=ln_g, ln_b=ln_b, moe_w_group=moe_w_group, moe_b_group=moe_b_group,
             moe_w_expert=moe_w_expert, moe_b_expert=moe_b_expert, moe_w_gate=moe_w_gate,
             moe_w_up=moe_w_up, moe_w_down=moe_w_down)
    depth = ln_g.shape[0]
    n_ab = state_dn_s.shape[0]
    n_cc = state_cconv.shape[0]
    bsz, _, d = x_prompt.shape
    n_mem = mem_prompt.shape[1]
    hd = d // MEM_HEADS
    z_dn_s = jnp.zeros((n_ab, bsz) + state_dn_s.shape[2:], F32)
    z_dn_conv = jnp.zeros((n_ab, bsz) + state_dn_conv.shape[2:], F32)
    z_ssm = jnp.zeros((n_ab, bsz) + state_ssm_re.shape[2:], F32)
    z_cconv = jnp.zeros((n_cc, bsz) + state_cconv.shape[2:], F32)
    mem2 = mem_prompt.reshape(bsz * n_mem, d)
    p_mem_k = jnp.stack([_linear(mem2, mem_wk[l]) for l in range(depth)]).reshape(depth, bsz, n_mem, MEM_HEADS, hd)
    p_mem_v = jnp.stack([_linear(mem2, mem_wv[l]) for l in range(depth)]).reshape(depth, bsz, n_mem, MEM_HEADS, hd)
    (y_prompt, p_dn_s, p_dn_conv, p_ssm_re, p_ssm_im, p_cconv), \
        (y_sample, s_dn_s, s_dn_conv, s_ssm_re, s_ssm_im, s_cconv) = _trunks(
            [(x_prompt, z_dn_s, z_dn_conv, z_ssm, z_ssm, z_cconv, p_mem_k, p_mem_v),
             (x_sample, state_dn_s, state_dn_conv, state_ssm_re, state_ssm_im, state_cconv, cache_mem_k,
              cache_mem_v)], p)
    return (y_prompt, y_sample, p_dn_s, p_dn_conv, p_ssm_re, p_ssm_im, p_cconv, p_mem_k, p_mem_v,
            s_dn_s, s_dn_conv, s_ssm_re, s_ssm_im, s_cconv)
```

```python
import functools
import math

import jax
import jax.numpy as jnp
from jax import lax
from jax.experimental import pallas as pl
from jax.experimental.pallas import tpu as pltpu

F32 = jnp.float32
BF16 = jnp.bfloat16

DN_HEADS = 4
DN_DK = 128
DN_DV = 128
DN_CONV = 4
DN_CHUNK = 64
SSM_GROUP_CH = 16
MEM_HEADS = 4
MOE_GROUPS = 4
MOE_EXPERTS_PER_GROUP = 8
MOE_TOP_K = 2
LN_EPS = 1e-5
RMS_EPS = 1e-6

V7X_LANES = 128
V7X_SUBLANES = 8
V7X_BF16_SUBLANES = 16
V7X_VMEM_LIMIT_BYTES = 52 * 1024 * 1024


def _cparams(*sem):
    return pltpu.CompilerParams(dimension_semantics=sem, vmem_limit_bytes=V7X_VMEM_LIMIT_BYTES)


def _bdot(a, b):
    return jnp.dot(a.astype(BF16), b.astype(BF16), preferred_element_type=F32)


def _bdot_nt(a, b):
    return lax.dot_general(a.astype(BF16), b.astype(BF16), (((1,), (1,)), ((), ())),
                           preferred_element_type=F32)


def _split3(a):
    hi = a.astype(BF16)
    r1 = a - hi.astype(F32)
    mid = r1.astype(BF16)
    lo = (r1 - mid.astype(F32)).astype(BF16)
    return hi, mid, lo


def _sigmoid(x):
    return 1.0 / (1.0 + jnp.exp(-x))


def _silu(x):
    return x * _sigmoid(x)


def _softplus(x):
    return jnp.maximum(x, 0.0) + jnp.log(1.0 + jnp.exp(-jnp.abs(x)))


def _layer_norm(v, g, b):
    mu = jnp.mean(v, axis=-1, keepdims=True)
    d = v - mu
    var = jnp.mean(d * d, axis=-1, keepdims=True)
    return d * lax.rsqrt(var + LN_EPS) * g + b


def _row_tile(n, pref):
    t = min(n, pref)
    assert n % t == 0, (n, t)
    return t


def _linear_kernel(x_ref, w_ref, b_ref, o_ref, *, glu):
    y = _bdot(x_ref[...], w_ref[...]) + b_ref[...]
    if glu:
        n = y.shape[-1] // 2
        y = y[:, :n] * _sigmoid(y[:, n:])
    o_ref[...] = y.astype(o_ref.dtype)


def _linear(x, w, bias=None, *, glu=False, tm=512, out_dtype=F32):
    m, k = x.shape
    n = w.shape[1]
    if bias is None:
        bias = jnp.zeros((n,), F32)
    tm = _row_tile(m, tm)
    n_out = n // 2 if glu else n
    return pl.pallas_call(
        functools.partial(_linear_kernel, glu=glu),
        grid=(m // tm,),
        in_specs=[pl.BlockSpec((tm, k), lambda i: (i, 0)),
                  pl.BlockSpec((k, n), lambda i: (0, 0)),
                  pl.BlockSpec((1, n), lambda i: (0, 0))],
        out_specs=pl.BlockSpec((tm, n_out), lambda i: (i, 0)),
        out_shape=jax.ShapeDtypeStruct((m, n_out), out_dtype),
        compiler_params=_cparams("parallel"),
        name="linear",
    )(x, w.astype(BF16), bias.reshape(1, n).astype(F32))


def _linear_postnorm_kernel(h_ref, w_ref, b_ref, x_ref, g_ref, beta_ref, o_ref, *, alpha):
    h = _bdot(h_ref[...], w_ref[...]) + b_ref[...]
    o_ref[...] = _layer_norm(alpha * x_ref[...] + h, g_ref[...], beta_ref[...])


def _linear_postnorm(h_in, w, bias, x_res, g, beta, *, alpha, tm=512):
    m, k = h_in.shape
    d = w.shape[1]
    if bias is None:
        bias = jnp.zeros((d,), F32)
    tm = _row_tile(m, tm)
    row = lambda a: a.reshape(1, d).astype(F32)
    return pl.pallas_call(
        functools.partial(_linear_postnorm_kernel, alpha=alpha),
        grid=(m // tm,),
        in_specs=[pl.BlockSpec((tm, k), lambda i: (i, 0)),
                  pl.BlockSpec((k, d), lambda i: (0, 0)),
                  pl.BlockSpec((1, d), lambda i: (0, 0)),
                  pl.BlockSpec((tm, d), lambda i: (i, 0)),
                  pl.BlockSpec((1, d), lambda i: (0, 0)),
                  pl.BlockSpec((1, d), lambda i: (0, 0))],
        out_specs=pl.BlockSpec((tm, d), lambda i: (i, 0)),
        out_shape=jax.ShapeDtypeStruct((m, d), F32),
        compiler_params=_cparams("parallel"),
        name="linear_postnorm",
    )(h_in, w.astype(BF16), row(bias), x_res, row(g), row(beta))


def _ab_in_kernel(x_ref, wqkvz_ref, wba_ref, wu_ref, qkv_ref, z_ref, ba_ref, u_ref, *, n_qkv):
    x = x_ref[...].astype(BF16)
    y = jnp.dot(x, wqkvz_ref[...], preferred_element_type=F32)
    qkv_ref[...] = y[:, :n_qkv]
    z_ref[...] = y[:, n_qkv:]
    ba_ref[...] = jnp.dot(x, wba_ref[...], preferred_element_type=F32)
    u_ref[...] = jnp.dot(x, wu_ref[...], preferred_element_type=F32)


def _ab_in_proj(x, w_in, *, tm=512):
    bx, tx, d = x.shape
    n_key = DN_HEADS * DN_DK
    n_val = DN_HEADS * DN_DV
    n_qkv = 2 * n_key + n_val
    off_beta = n_qkv + n_val
    off_u = off_beta + 2 * DN_HEADS
    n_u = w_in.shape[1] - off_u
    w_qkvz = w_in[:, :off_beta].astype(BF16)
    w_ba = jnp.pad(w_in[:, off_beta:off_u], ((0, 0), (0, V7X_LANES - 2 * DN_HEADS))).astype(BF16)
    w_u = w_in[:, off_u:].astype(BF16)
    tm = _row_tile(tx, tm)
    full = lambda a: pl.BlockSpec(a.shape, lambda b, i: (0, 0))
    return pl.pallas_call(
        functools.partial(_ab_in_kernel, n_qkv=n_qkv),
        grid=(bx, tx // tm),
        in_specs=[pl.BlockSpec((None, tm, d), lambda b, i: (b, i, 0)),
                  full(w_qkvz), full(w_ba), full(w_u)],
        out_specs=[pl.BlockSpec((None, tm, n_qkv), lambda b, i: (b, i, 0)),
                   pl.BlockSpec((None, tm, n_val), lambda b, i: (b, i, 0)),
                   pl.BlockSpec((None, tm, V7X_LANES), lambda b, i: (b, i, 0)),
                   pl.BlockSpec((tm, n_u), lambda b, i: (i, b))],
        out_shape=[jax.ShapeDtypeStruct((bx, tx, n_qkv), F32),
                   jax.ShapeDtypeStruct((bx, tx, n_val), F32),
                   jax.ShapeDtypeStruct((bx, tx, V7X_LANES), F32),
                   jax.ShapeDtypeStruct((tx, bx * n_u), F32)],
        compiler_params=_cparams("parallel", "parallel"),
        name="ab_in_proj",
    )(x, w_qkvz, w_ba, w_u)


def _ab_out_kernel(o_ref, y_ref, wt_ref, wb_ref, x_ref, g_ref, beta_ref, out_ref, *, alpha):
    h = _bdot(o_ref[...], wt_ref[...]) + _bdot(y_ref[...], wb_ref[...])
    out_ref[...] = _layer_norm(alpha * x_ref[...] + h, g_ref[...], beta_ref[...])


def _ab_out_proj(o, y_tm, w_out, x, g, beta, *, alpha, tm=512):
    bx, tx, d = x.shape
    n_o = o.shape[-1]
    n_y = y_tm.shape[1] // bx
    tm = _row_tile(tx, tm)
    wt = w_out[:n_o].astype(BF16)
    wb = w_out[n_o:].astype(BF16)
    row = lambda a: a.reshape(1, d).astype(F32)
    full = lambda a: pl.BlockSpec(a.shape, lambda b, i: (0, 0))
    return pl.pallas_call(
        functools.partial(_ab_out_kernel, alpha=alpha),
        grid=(bx, tx // tm),
        in_specs=[pl.BlockSpec((None, tm, n_o), lambda b, i: (b, i, 0)),
                  pl.BlockSpec((tm, n_y), lambda b, i: (i, b)),
                  full(wt), full(wb),
                  pl.BlockSpec((None, tm, d), lambda b, i: (b, i, 0)),
                  pl.BlockSpec((1, d), lambda b, i: (0, 0)),
                  pl.BlockSpec((1, d), lambda b, i: (0, 0))],
        out_specs=pl.BlockSpec((None, tm, d), lambda b, i: (b, i, 0)),
        out_shape=jax.ShapeDtypeStruct((bx, tx, d), F32),
        compiler_params=_cparams("parallel", "parallel"),
        name="ab_out_proj",
    )(o, y_tm, wt, wb, x, row(g), row(beta))


_DN_HALO = V7X_SUBLANES


def _dn_seq_kernel(qkv_ref, ba_ref, z_ref, cbuf_ref, s0_ref, cw_ref, alog_ref, dtb_ref, ng_ref,
                   o_ref, s_out_ref, cbuf_out_ref, xp_ref, s_ref, *, c, bb):
    j = pl.program_id(1)
    nj = pl.num_programs(1)
    hist = DN_CONV - 1
    n_key = DN_HEADS * DN_DK
    units = [(bi, h) for bi in range(bb) for h in range(DN_HEADS)]
    each = lambda f: {u: f(u) for u in units}

    @pl.when(j == 0)
    def _():
        xp_ref[:, 0:_DN_HALO - hist, :] = jnp.zeros((bb, _DN_HALO - hist, xp_ref.shape[-1]), F32)
        xp_ref[:, _DN_HALO - hist:_DN_HALO, :] = cbuf_ref[...]
        s_ref[...] = s0_ref[...]

    ys, tails = [], []
    for bi in range(bb):
        xp_ref[bi, _DN_HALO:_DN_HALO + c, :] = qkv_ref[bi]
        y = xp_ref[bi, pl.ds(_DN_HALO - hist, c), :] * cw_ref[0:1, :]
        for tap in range(1, DN_CONV):
            y = y + xp_ref[bi, pl.ds(_DN_HALO - hist + tap, c), :] * cw_ref[tap:tap + 1, :]
        tail = xp_ref[bi, pl.ds(_DN_HALO + c - hist, hist), :]
        xp_ref[bi, _DN_HALO - hist:_DN_HALO, :] = tail
        tails.append(tail)
        ys.append(_silu(y))

    ii = lax.broadcasted_iota(jnp.int32, (c, c), 0)
    jj = lax.broadcasted_iota(jnp.int32, (c, c), 1)
    incl = ii >= jj
    strict = ii > jj
    tril = jnp.where(incl, 1.0, 0.0).astype(BF16)
    eye = jnp.where(ii == jj, 1.0, 0.0)
    n_double = int(math.log2(c))
    vsl = lambda h: slice(h * DN_DV, (h + 1) * DN_DV)

    def unit_inputs(u):
        bi, h = u
        lo, hi = h * DN_DK, (h + 1) * DN_DK
        y = ys[bi]
        qh = y[:, lo:hi]
        kh = y[:, n_key + lo:n_key + hi]
        vh = y[:, 2 * n_key + h * DN_DV:2 * n_key + (h + 1) * DN_DV]
        qh = qh * lax.rsqrt(jnp.sum(qh * qh, -1, keepdims=True) + RMS_EPS) * (DN_DK ** -0.5)
        kh = kh * lax.rsqrt(jnp.sum(kh * kh, -1, keepdims=True) + RMS_EPS)
        beta = _sigmoid(ba_ref[bi, :, h:h + 1])
        a_logit = ba_ref[bi, :, DN_HEADS + h:DN_HEADS + h + 1]
        g = -jnp.exp(alog_ref[0:1, lo:hi]) * _softplus(a_logit + dtb_ref[0:1, lo:hi])
        return qh, kh, vh, beta, g

    inp = each(unit_inputs)
    q = each(lambda u: inp[u][0])
    k = each(lambda u: inp[u][1])
    beta = each(lambda u: inp[u][3])
    parts = [p for u in units for p in _split3(inp[u][4])]
    gc_all = jnp.dot(tril, jnp.concatenate(parts, axis=1), preferred_element_type=F32)
    lanes = lambda i: slice(i * DN_DK, (i + 1) * DN_DK)
    gc = {u: gc_all[:, lanes(3 * i)] + gc_all[:, lanes(3 * i + 1)] + gc_all[:, lanes(3 * i + 2)]
          for i, u in enumerate(units)}

    def unit_decay(u):
        gc_row = jnp.transpose(gc[u])[0:1, :]
        diff = gc[u][:, :c] - gc_row
        return jnp.where(incl, jnp.exp(jnp.where(incl, diff, 0.0)), 0.0)

    decay = each(unit_decay)
    eg = each(lambda u: jnp.exp(gc[u]))
    kdec = each(lambda u: k[u] * jnp.exp(gc[u][c - 1:c, :] - gc[u]))
    kb = each(lambda u: k[u] * beta[u])
    akq = each(lambda u: _bdot_nt(jnp.concatenate([kb[u], q[u]], axis=0), k[u]))
    lmat = each(lambda u: jnp.where(strict, akq[u][:c] * decay[u], 0.0))
    qk = each(lambda u: akq[u][c:] * decay[u])
    tmat = each(lambda u: eye - lmat[u])
    lpow = each(lambda u: _bdot(lmat[u], lmat[u]))
    for step in range(1, n_double):
        if step < n_double - 1:
            prod = each(lambda u: _bdot(jnp.concatenate([lpow[u], tmat[u]], axis=0), lpow[u]))
            lpow = each(lambda u: prod[u][:c])
            tmat = each(lambda u: tmat[u] + prod[u][c:])
        else:
            prod = each(lambda u: _bdot(tmat[u], lpow[u]))
            tmat = each(lambda u: tmat[u] + prod[u])
    uw = each(lambda u: _bdot(tmat[u], jnp.concatenate([inp[u][2] * beta[u], kb[u] * eg[u]], axis=1)))
    s_old = each(lambda u: s_ref[u[0], u[1]])
    wq_s = each(lambda u: _bdot(jnp.concatenate([uw[u][:, DN_DV:], q[u] * eg[u]], axis=0), s_old[u]))
    v_new = each(lambda u: uw[u][:, :DN_DV] - wq_s[u][:c])
    o = each(lambda u: wq_s[u][c:] + _bdot(qk[u], v_new[u]))
    s_new = each(lambda u: s_old[u] * jnp.exp(gc[u][c - 1:c, :]) + _bdot(jnp.transpose(kdec[u]), v_new[u]))
    for u in units:
        bi, h = u
        s_ref[bi, h] = s_new[u]
        on = o[u] * lax.rsqrt(jnp.mean(o[u] * o[u], -1, keepdims=True) + RMS_EPS) * ng_ref[...]
        o_ref[bi, :, vsl(h)] = on * _silu(z_ref[bi, :, vsl(h)])

    @pl.when(j == nj - 1)
    def _():
        s_out_ref[...] = s_ref[...]
        for bi in range(bb):
            cbuf_out_ref[bi] = tails[bi]


def _dn_step_kernel(qkv_ref, ba_ref, z_ref, cbuf_ref, s0_ref, cw_ref, alog_ref, dtb_ref, ng_ref,
                    o_ref, s_out_ref, cbuf_out_ref, *, bb):
    hist = DN_CONV - 1
    n_key = DN_HEADS * DN_DK
    x = qkv_ref[...]
    y = x * cw_ref[hist:hist + 1, :]
    for tap in range(hist):
        y = y + cbuf_ref[:, tap, :] * cw_ref[tap:tap + 1, :]
    for tap in range(1, hist):
        cbuf_out_ref[:, tap - 1, :] = cbuf_ref[:, tap, :]
    cbuf_out_ref[:, hist - 1, :] = x
    y = _silu(y)
    ba = ba_ref[...]
    z = z_ref[...]
    row8 = lax.broadcasted_iota(jnp.int32, (V7X_SUBLANES, DN_DK), 0)
    row16 = lax.broadcasted_iota(jnp.int32, (V7X_BF16_SUBLANES, DN_DK), 0)
    heads = range(DN_HEADS)
    units = [(h, r) for h in heads for r in range(bb)]
    each = lambda f: {u: f(u) for u in units}
    one = lambda a, r: a[r:r + 1]

    def head_inputs(h):
        lo, hi = h * DN_DK, (h + 1) * DN_DK
        qh = y[:, lo:hi]
        kh = y[:, n_key + lo:n_key + hi]
        vh = y[:, 2 * n_key + h * DN_DV:2 * n_key + (h + 1) * DN_DV]
        qh = qh * lax.rsqrt(jnp.sum(qh * qh, -1, keepdims=True) + RMS_EPS) * (DN_DK ** -0.5)
        kh = kh * lax.rsqrt(jnp.sum(kh * kh, -1, keepdims=True) + RMS_EPS)
        beta = _sigmoid(ba[:, h:h + 1])
        a_logit = ba[:, DN_HEADS + h:DN_HEADS + h + 1]
        g = -jnp.exp(alog_ref[0:1, lo:hi]) * _softplus(a_logit + dtb_ref[0:1, lo:hi])
        eg = jnp.exp(g)
        return dict(k=kh, eg=eg, w=kh * beta * eg, qg=qh * eg, u=vh * beta, qk=jnp.sum(qh * kh, -1, keepdims=True))

    hd = [head_inputs(h) for h in heads]
    s_old = each(lambda u: s0_ref[u[1], u[0]])
    lhs = each(lambda u: jnp.where(row8 == 0, one(hd[u[0]]['w'], u[1]),
                                   jnp.where(row8 == 1, one(hd[u[0]]['qg'], u[1]), 0.0)))
    ws_qs = each(lambda u: _bdot(lhs[u], s_old[u]))
    v_new = each(lambda u: one(hd[u[0]]['u'], u[1]) - ws_qs[u][0:1])
    o_row = each(lambda u: ws_qs[u][1:2] + one(hd[u[0]]['qk'], u[1]) * v_new[u])
    k16 = each(lambda u: jnp.where(row16 == 0, one(hd[u[0]]['k'], u[1]), 0.0))
    v16 = each(lambda u: jnp.where(row16 == 0, v_new[u], 0.0))
    upd = each(lambda u: _bdot(jnp.transpose(k16[u]), v16[u]))
    for u in units:
        s_out_ref[u[1], u[0]] = s_old[u] * one(hd[u[0]]['eg'], u[1]) + upd[u]
    for h in heads:
        o = jnp.concatenate([o_row[(h, r)] for r in range(bb)], axis=0)
        o = o * lax.rsqrt(jnp.mean(o * o, -1, keepdims=True) + RMS_EPS) * ng_ref[...]
        o_ref[:, h * DN_DV:(h + 1) * DN_DV] = o * _silu(z[:, h * DN_DV:(h + 1) * DN_DV])


def _deltanet(qkv, ba, z, conv_buf, s0, conv_w, a_log, dt_bias, norm_g):
    b, t, n_qkv = qkv.shape
    n_val = DN_HEADS * DN_DV
    hist = DN_CONV - 1
    rep = lambda a: jnp.repeat(a.astype(F32), DN_DK).reshape(1, DN_HEADS * DN_DK)
    cw = conv_w.astype(F32)
    consts = (cw, rep(a_log), rep(dt_bias), norm_g.reshape(1, DN_DV).astype(F32))
    state_shape = (DN_HEADS, DN_DK, DN_DV)
    out_shape = [jax.ShapeDtypeStruct((b, t, n_val), F32),
                 jax.ShapeDtypeStruct((b,) + state_shape, F32),
                 jax.ShapeDtypeStruct((b, hist, n_qkv), F32)]
    if t == 1:
        bb = _row_tile(b, V7X_SUBLANES)
        const = lambda a: pl.BlockSpec(a.shape, lambda i: (0,) * a.ndim)
        o, s_new, cbuf_new = pl.pallas_call(
            functools.partial(_dn_step_kernel, bb=bb),
            grid=(b // bb,),
            in_specs=[pl.BlockSpec((bb, n_qkv), lambda i: (i, 0)),
                      pl.BlockSpec((bb, V7X_LANES), lambda i: (i, 0)),
                      pl.BlockSpec((bb, n_val), lambda i: (i, 0)),
                      pl.BlockSpec((bb, hist, n_qkv), lambda i: (i, 0, 0)),
                      pl.BlockSpec((bb,) + state_shape, lambda i: (i, 0, 0, 0))] + [const(a) for a in consts],
            out_specs=[pl.BlockSpec((bb, n_val), lambda i: (i, 0)),
                       pl.BlockSpec((bb,) + state_shape, lambda i: (i, 0, 0, 0)),
                       pl.BlockSpec((bb, hist, n_qkv), lambda i: (i, 0, 0))],
            out_shape=[jax.ShapeDtypeStruct((b, n_val), F32)] + out_shape[1:],
            compiler_params=_cparams("parallel"),
            name="deltanet_step",
        )(qkv.reshape(b, n_qkv), ba.reshape(b, V7X_LANES), z.reshape(b, n_val), conv_buf.astype(F32),
          s0.astype(F32), *consts)
        return o.reshape(b, 1, n_val), s_new, cbuf_new
    c = DN_CHUNK
    assert t % c == 0 and c >= hist
    bb = _row_tile(b, 4)
    const = lambda a: pl.BlockSpec(a.shape, lambda i, j: (0,) * a.ndim)
    return pl.pallas_call(
        functools.partial(_dn_seq_kernel, c=c, bb=bb),
        grid=(b // bb, t // c),
        in_specs=[pl.BlockSpec((bb, c, n_qkv), lambda i, j: (i, j, 0)),
                  pl.BlockSpec((bb, c, V7X_LANES), lambda i, j: (i, j, 0)),
                  pl.BlockSpec((bb, c, n_val), lambda i, j: (i, j, 0)),
                  pl.BlockSpec((bb, hist, n_qkv), lambda i, j: (i, 0, 0)),
                  pl.BlockSpec((bb,) + state_shape, lambda i, j: (i, 0, 0, 0))] + [const(a) for a in consts],
        out_specs=[pl.BlockSpec((bb, c, n_val), lambda i, j: (i, j, 0)),
                   pl.BlockSpec((bb,) + state_shape, lambda i, j: (i, 0, 0, 0)),
                   pl.BlockSpec((bb, hist, n_qkv), lambda i, j: (i, 0, 0))],
        out_shape=out_shape,
        scratch_shapes=[pltpu.VMEM((bb, _DN_HALO + c, n_qkv), F32),
                        pltpu.VMEM((bb,) + state_shape, F32)],
        compiler_params=_cparams("parallel", "arbitrary"),
        name="deltanet_seq",
    )(qkv, ba, z, conv_buf.astype(F32), s0.astype(F32), *consts)


def _s5_param_kernel(lre_ref, lim_ref, ldt_ref, lbre_ref, lbim_ref, fre_ref, fim_ref):
    lam_re = lre_ref[...]
    lam_im = lim_ref[...]
    dt = jnp.exp(ldt_ref[...])
    mag = jnp.exp(lam_re * dt)
    ang = lam_im * dt
    lb_re = mag * jnp.cos(ang)
    lb_im = mag * jnp.sin(ang)
    den = lam_re * lam_re + lam_im * lam_im
    lbre_ref[...] = lb_re
    lbim_ref[...] = lb_im
    fre_ref[...] = ((lb_re - 1.0) * lam_re + lb_im * lam_im) / den
    fim_ref[...] = (lb_im * lam_re - (lb_re - 1.0) * lam_im) / den


def _s5_discretize(lam_re, lam_im, log_dt):
    g, n = lam_re.shape
    ldt = jnp.broadcast_to(log_dt.astype(F32)[:, None], (g, n))
    shp = jax.ShapeDtypeStruct((g, n), F32)
    return pl.pallas_call(_s5_param_kernel, out_shape=[shp] * 4, name="s5_discretize")(
        lam_re.astype(F32), lam_im.astype(F32), ldt)


def _s5_kernel(u_ref, h0re_ref, h0im_ref, lbre_ref, lbim_ref, bre_ref, bim_ref, cre_ref, cim_ref,
               d_ref, wglu_ref, bglu_ref, y_ref, hre_out_ref, him_out_ref,
               sre_ref, sim_ref, cre_s, cim_s, *, tb, bb, lane_chunk):
    j = pl.program_id(1)
    nj = pl.num_programs(1)
    n_ch = u_ref.shape[-1]
    n_st = sre_ref.shape[-1]
    halves = bre_ref.shape[0]
    ch_h = n_ch // halves
    st_h = n_st // halves

    @pl.when(j == 0)
    def _():
        cre_s[...] = h0re_ref[...]
        cim_s[...] = h0im_ref[...]

    u = u_ref[...].reshape(tb * bb, n_ch)
    ub = u.astype(BF16)
    for hf in range(halves):
        uh = ub[:, hf * ch_h:(hf + 1) * ch_h]
        sre_ref[:, hf * st_h:(hf + 1) * st_h] = jnp.dot(uh, bre_ref[hf], preferred_element_type=F32)
        sim_ref[:, hf * st_h:(hf + 1) * st_h] = jnp.dot(uh, bim_ref[hf], preferred_element_type=F32)

    for c0 in range(0, n_st, lane_chunk):
        cs = slice(c0, c0 + lane_chunk)
        lr = jnp.broadcast_to(lbre_ref[0:1, cs], (bb, lane_chunk))
        li = jnp.broadcast_to(lbim_ref[0:1, cs], (bb, lane_chunk))

        def body(t, carry, cs=cs, lr=lr, li=li):
            hr, hi = carry
            r = pl.multiple_of(t * bb, bb)
            nr = lr * hr - li * hi + sre_ref[pl.ds(r, bb), cs]
            ni = lr * hi + li * hr + sim_ref[pl.ds(r, bb), cs]
            sre_ref[pl.ds(r, bb), cs] = nr
            sim_ref[pl.ds(r, bb), cs] = ni
            return nr, ni

        hr, hi = lax.fori_loop(0, tb, body, (cre_s[:, cs], cim_s[:, cs]))
        cre_s[:, cs] = hr
        cim_s[:, cs] = hi

    ys = []
    for hf in range(halves):
        hre = sre_ref[:, hf * st_h:(hf + 1) * st_h].astype(BF16)
        him = sim_ref[:, hf * st_h:(hf + 1) * st_h].astype(BF16)
        ys.append(jnp.dot(hre, cre_ref[hf], preferred_element_type=F32)
                  - jnp.dot(him, cim_ref[hf], preferred_element_type=F32))
    y = jnp.concatenate(ys, axis=1) + d_ref[...] * u
    y = jax.nn.gelu(y)
    y = y * _sigmoid(_bdot(y, wglu_ref[...]) + bglu_ref[...])
    y_ref[...] = y.reshape(tb, bb, n_ch)

    @pl.when(j == nj - 1)
    def _():
        hre_out_ref[...] = cre_s[...]
        him_out_ref[...] = cim_s[...]


def _s5(u_tm, h0_re, h0_im, lam_re, lam_im, log_dt, b_re, b_im, c_re, c_im, d_skip, w_glu, b_glu, *, halves=2):
    t, b, n_ch = u_tm.shape
    g, n, p = b_re.shape
    n_st = g * n
    lb_re, lb_im, f_re, f_im = _s5_discretize(lam_re, lam_im, log_dt)
    b_re = b_re.astype(F32)
    b_im = b_im.astype(F32)
    bb_re = f_re[..., None] * b_re - f_im[..., None] * b_im
    bb_im = f_re[..., None] * b_im + f_im[..., None] * b_re
    gh = g // halves

    def in_blocks(a):
        a = a.reshape(halves, gh, n, p)
        eye = jnp.eye(gh, dtype=F32)
        return jnp.einsum('hgnp,gk->hgpkn', a, eye).reshape(halves, gh * p, gh * n).astype(BF16)

    def out_blocks(a):
        a = a.astype(F32).reshape(halves, gh, p, n)
        eye = jnp.eye(gh, dtype=F32)
        return jnp.einsum('hgpn,gk->hgnkp', a, eye).reshape(halves, gh * n, gh * p).astype(BF16)

    bre_m, bim_m = in_blocks(bb_re), in_blocks(bb_im)
    cre_m, cim_m = out_blocks(c_re), out_blocks(c_im)
    bb = b if b <= 128 else 128
    assert b % bb == 0
    tb = _row_tile(t, max(1, 512 // bb))
    lane_chunk = max(V7X_LANES, min(n_st, 8192 // bb))
    const = lambda a: pl.BlockSpec(a.shape, lambda i, j: (0,) * a.ndim)
    row = lambda a, m: a.reshape(1, m).astype(F32)
    args = (u_tm, h0_re.reshape(b, n_st).astype(F32), h0_im.reshape(b, n_st).astype(F32),
            row(lb_re, n_st), row(lb_im, n_st), bre_m, bim_m, cre_m, cim_m,
            row(d_skip, n_ch), w_glu.astype(BF16), row(b_glu, n_ch))
    y, hre, him = pl.pallas_call(
        functools.partial(_s5_kernel, tb=tb, bb=bb, lane_chunk=lane_chunk),
        grid=(b // bb, t // tb),
        in_specs=[pl.BlockSpec((tb, bb, n_ch), lambda i, j: (j, i, 0)),
                  pl.BlockSpec((bb, n_st), lambda i, j: (i, 0)),
                  pl.BlockSpec((bb, n_st), lambda i, j: (i, 0))] + [const(a) for a in args[3:]],
        out_specs=[pl.BlockSpec((tb, bb, n_ch), lambda i, j: (j, i, 0)),
                   pl.BlockSpec((bb, n_st), lambda i, j: (i, 0)),
                   pl.BlockSpec((bb, n_st), lambda i, j: (i, 0))],
        out_shape=[jax.ShapeDtypeStruct((t, b, n_ch), F32),
                   jax.ShapeDtypeStruct((b, n_st), F32),
                   jax.ShapeDtypeStruct((b, n_st), F32)],
        scratch_shapes=[pltpu.VMEM((tb * bb, n_st), F32), pltpu.VMEM((tb * bb, n_st), F32),
                        pltpu.VMEM((bb, n_st), F32), pltpu.VMEM((bb, n_st), F32)],
        compiler_params=_cparams("parallel", "arbitrary"),
        name="s5",
    )(*args)
    return y, hre.reshape(b, g, n), him.reshape(b, g, n)


def _ab_mixers(x, dn_s, dn_conv, ssm_re, ssm_im, p, i):
    b, t, d = x.shape
    xr = x if t > 1 else x.reshape(1, b, d)
    bx, tx, _ = xr.shape
    qkv, z, ba, u_tm = _ab_in_proj(xr, p['ab_w_in'][i])
    n_ch = u_tm.shape[1] // bx
    shp = lambda a: a.reshape(b, t, a.shape[-1])
    o, s_new, conv_new = _deltanet(shp(qkv), shp(ba), shp(z), dn_conv, dn_s, p['dn_conv_w'][i],
                                   p['dn_a_log'][i], p['dn_dt_bias'][i], p['dn_norm_g'][i])
    y_tm, h_re, h_im = _s5(u_tm.reshape(t, b, n_ch), ssm_re, ssm_im, p['ssm_lambda_re'][i],
                           p['ssm_lambda_im'][i], p['ssm_log_dt'][i], p['ssm_b_re'][i], p['ssm_b_im'][i],
                           p['ssm_c_re'][i], p['ssm_c_im'][i], p['ssm_d'][i], p['ssm_w_glu'][i],
                           p['ssm_b_glu'][i])
    return o.reshape(bx, tx, -1), y_tm.reshape(tx, bx * n_ch), (s_new, conv_new, h_re, h_im)


def _ab_layer(x, dn_s, dn_conv, ssm_re, ssm_im, p, i, ln_g, ln_b, *, alpha):
    b, t, d = x.shape
    o, y_tm, states = _ab_mixers(x, dn_s, dn_conv, ssm_re, ssm_im, p, i)
    xr = x.reshape(o.shape[0], o.shape[1], d)
    x_new = _ab_out_proj(o, y_tm, p['ab_w_out'][i], xr, ln_g, ln_b, alpha=alpha)
    return (x_new.reshape(b, t, d),) + states


_CC_HALO = 32
_CC_ROW_TILES = 8


def _cconv_seq_kernel(h_ref, buf_ref, w_ref, bdw_ref, g_ref, b_ref, o_ref, xp_ref, xs_ref, acc_ref, *, tt, width):
    j = pl.program_id(1)
    hist = width - 1
    d = h_ref.shape[-1]
    rows = V7X_SUBLANES

    @pl.when(j == 0)
    def _():
        xp_ref[0:_CC_HALO - hist, :] = jnp.zeros((_CC_HALO - hist, d), F32)
        xp_ref[_CC_HALO - hist:_CC_HALO, :] = buf_ref[...]

    xp_ref[_CC_HALO:_CC_HALO + tt, :] = h_ref[...]
    n_shift = xs_ref.shape[1]
    for s in range(1, rows):
        xs_ref[s - 1] = xp_ref[pl.ds(s, n_shift), :]
    base = _CC_HALO - hist
    for c0 in range(0, d, V7X_LANES):
        cs = slice(c0, c0 + V7X_LANES)
        taps = [jnp.broadcast_to(w_ref[k:k + 1, cs], (rows, V7X_LANES)) for k in range(width)]

        def body(i, carry, taps=taps, cs=cs):
            r = pl.multiple_of(i * (rows * _CC_ROW_TILES), rows * _CC_ROW_TILES)
            accs = [None] * _CC_ROW_TILES
            for s in range(rows):
                ks = [k for k in range(width) if (base + k) % rows == s]
                if not ks:
                    continue
                n_rows = rows * (_CC_ROW_TILES + max((base + k) // rows for k in ks))
                big = xp_ref[pl.ds(r, n_rows), cs] if s == 0 else xs_ref[s - 1, pl.ds(r, n_rows), cs]
                for k in ks:
                    a = (base + k) // rows
                    for sub in range(_CC_ROW_TILES):
                        term = big[(sub + a) * rows:(sub + a + 1) * rows] * taps[k]
                        accs[sub] = term if accs[sub] is None else accs[sub] + term
            for sub in range(_CC_ROW_TILES):
                acc_ref[pl.ds(r + sub * rows, rows), cs] = accs[sub]
            return carry

        lax.fori_loop(0, tt // (rows * _CC_ROW_TILES), body, 0)
    xp_ref[0:_CC_HALO, :] = xp_ref[tt:tt + _CC_HALO, :]
    y = _layer_norm(acc_ref[...] + bdw_ref[...], g_ref[...], b_ref[...])
    o_ref[...] = _silu(y)


def _cconv_seq(h, buf, w_dw, b_dw, ln_g, ln_b, *, tt=256):
    b, t, d = h.shape
    width = w_dw.shape[0]
    tt = _row_tile(t, tt)
    assert tt >= _CC_HALO and width - 1 <= _CC_HALO
    row = lambda a: a.reshape(1, d).astype(F32)
    const = lambda a: pl.BlockSpec(a.shape, lambda i, j: (0, 0))
    args = (w_dw.astype(F32), row(b_dw), row(ln_g), row(ln_b))
    return pl.pallas_call(
        functools.partial(_cconv_seq_kernel, tt=tt, width=width),
        grid=(b, t // tt),
        in_specs=[pl.BlockSpec((None, tt, d), lambda i, j: (i, j, 0)),
                  pl.BlockSpec((None, width - 1, d), lambda i, j: (i, 0, 0))] + [const(a) for a in args],
        out_specs=pl.BlockSpec((None, tt, d), lambda i, j: (i, j, 0)),
        out_shape=jax.ShapeDtypeStruct((b, t, d), F32),
        scratch_shapes=[pltpu.VMEM((_CC_HALO + tt, d), F32),
                        pltpu.VMEM((V7X_SUBLANES - 1, _CC_HALO + tt - V7X_SUBLANES, d), F32),
                        pltpu.VMEM((tt, d), F32)],
        compiler_params=_cparams("parallel", "arbitrary"),
        name="cconv_seq",
    )(h, buf.astype(F32), *args)


def _cconv_step_kernel(h_ref, buf_ref, w_ref, bdw_ref, g_ref, b_ref, o_ref, *, width):
    hist = width - 1
    acc = jnp.sum(buf_ref[...] * w_ref[0:hist, :][None], axis=1) + h_ref[...] * w_ref[hist:width, :]
    o_ref[...] = _silu(_layer_norm(acc + bdw_ref[...], g_ref[...], b_ref[...]))


def _cconv_step(h, buf, w_dw, b_dw, ln_g, ln_b, *, bb=8):
    b, d = h.shape
    width = w_dw.shape[0]
    bb = _row_tile(b, bb)
    row = lambda a: a.reshape(1, d).astype(F32)
    const = lambda a: pl.BlockSpec(a.shape, lambda i: (0, 0))
    args = (w_dw.astype(F32), row(b_dw), row(ln_g), row(ln_b))
    return pl.pallas_call(
        functools.partial(_cconv_step_kernel, width=width),
        grid=(b // bb,),
        in_specs=[pl.BlockSpec((bb, d), lambda i: (i, 0)),
                  pl.BlockSpec((bb, width - 1, d), lambda i: (i, 0, 0))] + [const(a) for a in args],
        out_specs=pl.BlockSpec((bb, d), lambda i: (i, 0)),
        out_shape=jax.ShapeDtypeStruct((b, d), F32),
        compiler_params=_cparams("parallel"),
        name="cconv_step",
    )(h, buf.astype(F32), *args)


def _conv_mixer(x, buf, p, i):
    b, t, d = x.shape
    h = _linear(x.reshape(b * t, d), p['cc_w_pw1'][i], p['cc_b_pw1'][i], glu=True)
    args = (p['cc_w_dw'][i], p['cc_b_dw'][i], p['cc_ln_g'][i], p['cc_ln_b'][i])
    if t == 1:
        hc = _cconv_step(h, buf, *args)
    else:
        hc = _cconv_seq(h.reshape(b, t, d), buf, *args).reshape(b * t, d)
    new_buf = jnp.concatenate([buf.astype(F32), h.reshape(b, t, d)], axis=1)[:, t:]
    return hc, new_buf


def _conv_layer(x, buf, p, i, ln_g, ln_b, *, alpha):
    b, t, d = x.shape
    hc, new_buf = _conv_mixer(x, buf, p, i)
    x_new = _linear_postnorm(hc, p['cc_w_pw2'][i], p['cc_b_pw2'][i], x.reshape(b * t, d), ln_g, ln_b, alpha=alpha)
    return x_new.reshape(b, t, d), new_buf


def _mem_attn_seq_kernel(x_ref, wq_ref, k_ref, v_ref, wo_ref, g_ref, b_ref, o_ref, *, alpha, heads):
    o_ref[...] = _mem_attn_math(x_ref[...], wq_ref, k_ref, v_ref, wo_ref, g_ref, b_ref, alpha=alpha, heads=heads)


def _mem_attn_math(x, wq_ref, k_ref, v_ref, wo_ref, g_ref, b_ref, *, alpha, heads):
    d = x.shape[-1]
    hd = d // heads
    q = _bdot(x, wq_ref[...]) * (hd ** -0.5)
    k = k_ref[...].astype(BF16)
    v = v_ref[...].astype(BF16)
    hsl = [slice(h * hd, (h + 1) * hd) for h in range(heads)]
    ss = [_bdot_nt(q[:, hs], k[:, hs]) for hs in hsl]
    es = [jnp.exp(s - jnp.max(s, axis=-1, keepdims=True)) for s in ss]
    ps = [e / jnp.sum(e, axis=-1, keepdims=True) for e in es]
    o = jnp.concatenate([_bdot(a, v[:, hs]) for a, hs in zip(ps, hsl)], axis=1)
    hres = _bdot(o, wo_ref[...])
    return _layer_norm(alpha * x + hres, g_ref[...], b_ref[...])


def _proj_mem_router_kernel(*refs, n_proj, alpha, heads, groups, per_group):
    proj = refs[:2 * n_proj]
    (bias_ref, xres_ref, g0_ref, b0_ref, wq_ref, k_ref, v_ref, wo_ref, g1_ref, b1_ref, rw_ref, rb_ref,
     x_out_ref, ids_ref, wts_ref) = refs[2 * n_proj:]
    h = bias_ref[...]
    for a in range(n_proj):
        h = h + _bdot(proj[2 * a][...], proj[2 * a + 1][...])
    x = _layer_norm(alpha * xres_ref[...] + h, g0_ref[...], b0_ref[...])
    x = _mem_attn_math(x, wq_ref, k_ref, v_ref, wo_ref, g1_ref, b1_ref, alpha=alpha, heads=heads)
    x_out_ref[...] = x
    ids_ref[...], wts_ref[...] = _router_math(x, rw_ref[...], rb_ref[...], groups=groups, per_group=per_group)


def _proj_mem_router(proj, bias, x_res, ln0, mk, mv, layer, wq, wo, ln1, router_w, router_b, *, alpha, tq=512):
    b, t, d = x_res.shape
    m = mk.shape[2]
    nt = t // tq
    groups = MOE_GROUPS
    row = lambda a: a.reshape(1, d).astype(F32)
    const = lambda a: pl.BlockSpec(a.shape, lambda i, j: (0,) * a.ndim)
    xblk = pl.BlockSpec((None, tq, d), lambda i, j: (i, j, 0))
    rowblk = pl.BlockSpec((1, d), lambda i, j: (0, 0))
    proj_args, proj_specs = [], []
    for arr, spec, w in proj:
        wb = w.astype(BF16)
        proj_args += [arr, wb]
        proj_specs += [spec, const(wb)]
    if bias is None:
        bias = jnp.zeros((d,), F32)
    wqb, wob = wq.astype(BF16), wo.astype(BF16)
    lane_blk = pl.BlockSpec((tq, V7X_LANES), lambda i, j: (i * nt + j, 0))
    x_out, ids, wts = pl.pallas_call(
        functools.partial(_proj_mem_router_kernel, n_proj=len(proj), alpha=alpha, heads=MEM_HEADS,
                          groups=groups, per_group=MOE_EXPERTS_PER_GROUP),
        grid=(b, nt),
        in_specs=proj_specs + [rowblk, xblk, rowblk, rowblk, const(wqb),
                               pl.BlockSpec((None, None, m, d), lambda i, j: (layer, i, 0, 0)),
                               pl.BlockSpec((None, None, m, d), lambda i, j: (layer, i, 0, 0)),
                               const(wob), rowblk, rowblk, const(router_w), const(router_b)],
        out_specs=[xblk, lane_blk, lane_blk],
        out_shape=[jax.ShapeDtypeStruct((b, t, d), F32),
                   jax.ShapeDtypeStruct((b * t, V7X_LANES), jnp.int32),
                   jax.ShapeDtypeStruct((b * t, V7X_LANES), F32)],
        compiler_params=_cparams("parallel", "parallel"),
        name="proj_mem_router",
    )(*proj_args, row(bias), x_res, row(ln0[0]), row(ln0[1]), wqb, mk, mv, wob, row(ln1[0]), row(ln1[1]),
      router_w, router_b)
    return x_out, ids, wts


def _mem_attn_seq(x, mk, mv, layer, wq, wo, ln_g, ln_b, *, alpha, tq=512):
    b, t, d = x.shape
    m = mk.shape[2]
    tq = _row_tile(t, tq)
    row = lambda a: a.reshape(1, d).astype(F32)
    const = lambda a: pl.BlockSpec(a.shape, lambda i, j: (0, 0))
    wqb, wob = wq.astype(BF16), wo.astype(BF16)
    return pl.pallas_call(
        functools.partial(_mem_attn_seq_kernel, alpha=alpha, heads=MEM_HEADS),
        grid=(b, t // tq),
        in_specs=[pl.BlockSpec((None, tq, d), lambda i, j: (i, j, 0)),
                  const(wqb),
                  pl.BlockSpec((None, None, m, d), lambda i, j: (layer, i, 0, 0)),
                  pl.BlockSpec((None, None, m, d), lambda i, j: (layer, i, 0, 0)),
                  const(wob), pl.BlockSpec((1, d), lambda i, j: (0, 0)),
                  pl.BlockSpec((1, d), lambda i, j: (0, 0))],
        out_specs=pl.BlockSpec((None, tq, d), lambda i, j: (i, j, 0)),
        out_shape=jax.ShapeDtypeStruct((b, t, d), F32),
        compiler_params=_cparams("parallel", "parallel"),
        name="mem_attn_seq",
    )(x, wqb, mk, mv, wob, row(ln_g), row(ln_b))


def _mem_attn_step_kernel(q_ref, k_ref, v_ref, o_ref, *, heads, bb):
    i = pl.program_id(0)
    hd = q_ref.shape[-1]
    for r in range(bb):
        q = q_ref[i * bb + r] * (hd ** -0.5)
        s = jnp.sum(k_ref[r] * q[None], axis=-1, keepdims=True)
        s = s - jnp.max(s, axis=0, keepdims=True)
        e = jnp.exp(s)
        o_ref[i * bb + r] = jnp.sum(e * v_ref[r], axis=0) / jnp.sum(e, axis=0)


def _mem_attn_step(q, mk_all, mv_all, layer, *, bb=4):
    b, heads, hd = q.shape
    m = mk_all.shape[2]
    bb = _row_tile(b, bb)
    return pl.pallas_call(
        functools.partial(_mem_attn_step_kernel, heads=heads, bb=bb),
        grid=(b // bb,),
        in_specs=[pl.BlockSpec((b, heads, hd), lambda i: (0, 0, 0)),
                  pl.BlockSpec((None, bb, m, heads, hd), lambda i: (layer, i, 0, 0, 0)),
                  pl.BlockSpec((None, bb, m, heads, hd), lambda i: (layer, i, 0, 0, 0))],
        out_specs=pl.BlockSpec((b, heads, hd), lambda i: (0, 0, 0)),
        out_shape=jax.ShapeDtypeStruct((b, heads, hd), F32),
        compiler_params=_cparams("arbitrary"),
        name="mem_attn_step",
    )(q, mk_all, mv_all)


def _mem_layer(x, mk_all, mv_all, layer, wq, wo, ln_g, ln_b, *, alpha):
    b, t, d = x.shape
    n_layers, _, m, heads, hd = mk_all.shape
    if t == 1:
        x2 = x.reshape(b, d)
        q = _linear(x2, wq).reshape(b, heads, hd)
        o = _mem_attn_step(q, mk_all, mv_all, layer).reshape(b, d)
        return _linear_postnorm(o, wo, None, x2, ln_g, ln_b, alpha=alpha).reshape(b, t, d)
    mk_all = mk_all.reshape(n_layers, b, m, d)
    mv_all = mv_all.reshape(n_layers, b, m, d)
    return _mem_attn_seq(x, mk_all, mv_all, layer, wq, wo, ln_g, ln_b, alpha=alpha)


_NEG = -1e30


def _router_kernel(x_ref, w_ref, b_ref, ids_ref, wts_ref, *, groups, per_group):
    ids_ref[...], wts_ref[...] = _router_math(x_ref[...], w_ref[...], b_ref[...], groups=groups,
                                              per_group=per_group)


def _router_math(x, w, bias, *, groups, per_group):
    xh = x.astype(BF16)
    xl = (x - xh.astype(F32)).astype(BF16)
    wh = w.astype(BF16)
    wl = (w - wh.astype(F32)).astype(BF16)
    logits = (jnp.dot(xh, wh, preferred_element_type=F32) + jnp.dot(xh, wl, preferred_element_type=F32)
              + jnp.dot(xl, wh, preferred_element_type=F32)) + bias
    lane = lax.broadcasted_iota(jnp.int32, logits.shape, 1)
    n_exp = groups * per_group
    is_g = lane < groups
    gl = jnp.where(is_g, logits, _NEG)
    gmax = jnp.max(gl, axis=-1, keepdims=True)
    gsum = jnp.sum(jnp.where(is_g, jnp.exp(gl - gmax), 0.0), axis=-1, keepdims=True)
    gw = 1.0 / gsum
    gi = jnp.min(jnp.where(gl == gmax, lane, V7X_LANES), axis=-1, keepdims=True)
    lane_grp = (lane - groups) // per_group
    sel = (lane >= groups) & (lane < groups + n_exp) & (lane_grp == gi)
    el = jnp.where(sel, logits, _NEG)
    emax = jnp.max(el, axis=-1, keepdims=True)
    ee = jnp.where(sel, jnp.exp(el - emax), 0.0)
    ep = jnp.where(sel, ee / jnp.sum(ee, axis=-1, keepdims=True), -1.0)
    p1 = jnp.max(ep, axis=-1, keepdims=True)
    i1 = jnp.min(jnp.where(ep == p1, lane, V7X_LANES), axis=-1, keepdims=True)
    ep2 = jnp.where(lane == i1, -1.0, ep)
    p2 = jnp.max(ep2, axis=-1, keepdims=True)
    i2 = jnp.min(jnp.where(ep2 == p2, lane, V7X_LANES), axis=-1, keepdims=True)
    denom = p1 + p2
    ids = jnp.where(lane == 0, i1 - groups, jnp.where(lane == 1, i2 - groups, 0))
    wts = jnp.where(lane == 0, gw * p1 / denom, jnp.where(lane == 1, gw * p2 / denom, 0.0))
    return ids, wts


def _router_params(p, layer):
    w_group, w_expert = p['moe_w_group'][layer], p['moe_w_expert'][layer]
    pad = V7X_LANES - w_group.shape[1] - w_expert.shape[1]
    w = jnp.pad(jnp.concatenate([w_group, w_expert], axis=1).astype(F32), ((0, 0), (0, pad)))
    b = jnp.pad(jnp.concatenate([p['moe_b_group'][layer], p['moe_b_expert'][layer]]).astype(F32), (0, pad))
    return w, b.reshape(1, V7X_LANES)


def _router(x, w, b, *, tm=512):
    n, d = x.shape
    tm = _row_tile(n, tm)
    return pl.pallas_call(
        functools.partial(_router_kernel, groups=MOE_GROUPS, per_group=MOE_EXPERTS_PER_GROUP),
        grid=(n // tm,),
        in_specs=[pl.BlockSpec((tm, d), lambda i: (i, 0)),
                  pl.BlockSpec((d, V7X_LANES), lambda i: (0, 0)),
                  pl.BlockSpec((1, V7X_LANES), lambda i: (0, 0))],
        out_specs=[pl.BlockSpec((tm, V7X_LANES), lambda i: (i, 0)),
                   pl.BlockSpec((tm, V7X_LANES), lambda i: (i, 0))],
        out_shape=[jax.ShapeDtypeStruct((n, V7X_LANES), jnp.int32),
                   jax.ShapeDtypeStruct((n, V7X_LANES), F32)],
        compiler_params=_cparams("parallel"),
        name="moe_router",
    )(x, w, b)


def _expert_ffn_kernel(te_ref, nv_ref, x_ref, wg_ref, wu_ref, wd_ref, o_ref):
    i = pl.program_id(0)

    @pl.when(i < nv_ref[0])
    def _():
        x = x_ref[...]
        hid = _silu(_bdot(x, wg_ref[...])) * _bdot(x, wu_ref[...])
        o_ref[...] = _bdot(hid, wd_ref[...])

    @pl.when(i >= nv_ref[0])
    def _():
        o_ref[...] = jnp.zeros_like(o_ref)


def _expert_ffn(x_sorted, tile_expert, n_valid, w_gate, w_up, w_down, layer, *, tm):
    r, d = x_sorted.shape
    f = w_gate.shape[-1]
    return pl.pallas_call(
        _expert_ffn_kernel,
        grid_spec=pltpu.PrefetchScalarGridSpec(
            num_scalar_prefetch=2,
            grid=(r // tm,),
            in_specs=[pl.BlockSpec((tm, d), lambda i, te, nv: (i, 0)),
                      pl.BlockSpec((None, None, d, f), lambda i, te, nv: (layer, te[i], 0, 0)),
                      pl.BlockSpec((None, None, d, f), lambda i, te, nv: (layer, te[i], 0, 0)),
                      pl.BlockSpec((None, None, f, d), lambda i, te, nv: (layer, te[i], 0, 0))],
            out_specs=pl.BlockSpec((tm, d), lambda i, te, nv: (i, 0)),
        ),
        out_shape=jax.ShapeDtypeStruct((r, d), F32),
        compiler_params=_cparams("arbitrary"),
        name="moe_expert_ffn",
    )(tile_expert, n_valid, x_sorted, w_gate, w_up, w_down)


def _rank_kernel(ids_ref, cnt_in_ref, rank_ref, cnt_ref, carry_ref):
    i = pl.program_id(0)
    tm = ids_ref.shape[0]

    @pl.when(i == 0)
    def _():
        carry_ref[...] = cnt_in_ref[...].astype(F32)

    ids = ids_ref[...]
    lane = lax.broadcasted_iota(jnp.int32, ids.shape, 1)
    oh0 = jnp.where(lane == ids[:, 0:1], 1.0, 0.0)
    oh1 = jnp.where(lane == ids[:, 1:2], 1.0, 0.0)
    both = oh0 + oh1
    ii = lax.broadcasted_iota(jnp.int32, (tm, tm), 0)
    jj = lax.broadcasted_iota(jnp.int32, (tm, tm), 1)
    earlier = jnp.where(ii > jj, 1.0, 0.0).astype(BF16)
    prefix = jnp.dot(earlier, both.astype(BF16), preferred_element_type=F32) + carry_ref[...]
    r0 = jnp.sum(oh0 * prefix, axis=-1, keepdims=True)
    r1 = jnp.sum(oh1 * prefix, axis=-1, keepdims=True)
    rank_ref[...] = jnp.where(lane == 0, r0, jnp.where(lane == 1, r1, 0.0)).astype(jnp.int32)
    total = carry_ref[...] + jnp.sum(both, axis=0, keepdims=True)
    carry_ref[...] = total
    cnt_ref[...] = total.astype(jnp.int32)


def _expert_ranks(ids, counts_in, *, tm=512):
    n = ids.shape[0]
    tm = _row_tile(n, tm)
    return pl.pallas_call(
        _rank_kernel,
        grid=(n // tm,),
        in_specs=[pl.BlockSpec((tm, V7X_LANES), lambda i: (i, 0)),
                  pl.BlockSpec((1, V7X_LANES), lambda i: (0, 0))],
        out_specs=[pl.BlockSpec((tm, V7X_LANES), lambda i: (i, 0)),
                   pl.BlockSpec((1, V7X_LANES), lambda i: (0, 0))],
        out_shape=[jax.ShapeDtypeStruct((n, V7X_LANES), jnp.int32),
                   jax.ShapeDtypeStruct((1, V7X_LANES), jnp.int32)],
        scratch_shapes=[pltpu.VMEM((1, V7X_LANES), F32)],
        compiler_params=_cparams("arbitrary"),
        name="moe_rank",
    )(ids, counts_in)


def _for_each_row(tm, fn):
    def body(i, carry):
        base = pl.multiple_of(i * V7X_SUBLANES, V7X_SUBLANES)
        for sub in range(V7X_SUBLANES):
            fn(base, sub)
        return carry

    lax.fori_loop(0, tm // V7X_SUBLANES, body, 0)


def _dispatch_kernel(slot_ref, x_ref, xs_init_ref, xs_ref, sem):
    del xs_init_ref
    tm = x_ref.shape[0]

    def issue(base, sub):
        for k in range(MOE_TOP_K):
            s = slot_ref[0, 0, MOE_TOP_K * (base + sub) + k]
            pltpu.make_async_copy(x_ref.at[pl.ds(base + sub, 1), :], xs_ref.at[pl.ds(s, 1), :],
                                  sem).start(priority=k)

    _for_each_row(tm, issue)
    for k in range(MOE_TOP_K):
        pltpu.make_async_copy(x_ref, xs_ref.at[pl.ds(0, tm), :], sem).wait()


def _dispatch(x, slot, xs_init, *, tm):
    n, d = x.shape
    rows = xs_init.shape[0]
    slot3 = slot.reshape(n // tm, 1, MOE_TOP_K * tm)
    return pl.pallas_call(
        _dispatch_kernel,
        grid=(n // tm,),
        in_specs=[pl.BlockSpec((1, 1, MOE_TOP_K * tm), lambda i: (i, 0, 0), memory_space=pltpu.SMEM),
                  pl.BlockSpec((tm, d), lambda i: (i, 0)),
                  pl.BlockSpec(memory_space=pl.ANY)],
        out_specs=pl.BlockSpec(memory_space=pl.ANY),
        out_shape=jax.ShapeDtypeStruct((rows, d), F32),
        scratch_shapes=[pltpu.SemaphoreType.DMA(())],
        input_output_aliases={2: 0},
        compiler_params=_cparams("arbitrary"),
        name="moe_dispatch",
    )(slot3, x, xs_init)


def _combine_kernel(slot_ref, slot_next_ref, wts_ref, x_ref, g_ref, b_ref, y_hbm, o_ref, ybuf, sem, *, alpha):
    i = pl.program_id(0)
    n = pl.num_programs(0)
    tm = x_ref.shape[0]
    cur = i % 2

    def issue(idx_ref, buf):
        def row(base, sub):
            for k in range(MOE_TOP_K):
                s = idx_ref[0, 0, MOE_TOP_K * (base + sub) + k]
                pltpu.make_async_copy(y_hbm.at[pl.ds(s, 1), :], ybuf.at[buf, k, pl.ds(base + sub, 1), :],
                                      sem.at[buf]).start(priority=k)

        _for_each_row(tm, row)

    @pl.when(i == 0)
    def _():
        issue(slot_ref, 0)

    @pl.when(i + 1 < n)
    def _():
        issue(slot_next_ref, 1 - cur)

    for k in range(MOE_TOP_K):
        pltpu.make_async_copy(y_hbm.at[pl.ds(0, tm), :], ybuf.at[cur, k], sem.at[cur]).wait()
    wts = wts_ref[...]
    y = wts[:, 0:1] * ybuf[cur, 0] + wts[:, 1:2] * ybuf[cur, 1]
    o_ref[...] = _layer_norm(alpha * x_ref[...] + y, g_ref[...], b_ref[...])


def _combine_postnorm(y_sorted, slot, wts, x, g, b, *, alpha, tm):
    n, d = x.shape
    nt = n // tm
    slot3 = slot.reshape(nt, 1, MOE_TOP_K * tm)
    row = lambda a: a.reshape(1, d).astype(F32)
    smem = lambda f: pl.BlockSpec((1, 1, MOE_TOP_K * tm), f, memory_space=pltpu.SMEM)
    return pl.pallas_call(
        functools.partial(_combine_kernel, alpha=alpha),
        grid=(nt,),
        in_specs=[smem(lambda i: (i, 0, 0)),
                  smem(lambda i: (jnp.minimum(i + 1, nt - 1), 0, 0)),
                  pl.BlockSpec((tm, V7X_LANES), lambda i: (i, 0)),
                  pl.BlockSpec((tm, d), lambda i: (i, 0)),
                  pl.BlockSpec((1, d), lambda i: (0, 0)),
                  pl.BlockSpec((1, d), lambda i: (0, 0)),
                  pl.BlockSpec(memory_space=pl.ANY)],
        out_specs=pl.BlockSpec((tm, d), lambda i: (i, 0)),
        out_shape=jax.ShapeDtypeStruct((n, d), F32),
        scratch_shapes=[pltpu.VMEM((2, MOE_TOP_K, tm, d), F32), pltpu.SemaphoreType.DMA((2,))],
        compiler_params=_cparams("arbitrary"),
        name="moe_combine",
    )(slot3, slot3, wts, x, row(g), row(b), y_sorted)


def _moe_layer(xs, p, layer, ln_g, ln_b, *, alpha, routed=None):
    w_gate, w_up, w_down = p['moe_w_gate'], p['moe_w_up'], p['moe_w_down']
    n_exp = w_gate.shape[1]
    d = xs[0].shape[-1]
    x2s = [x.reshape(-1, d) for x in xs]
    n_total = sum(x2.shape[0] for x2 in x2s)
    tm = 512 if n_total >= 8192 else V7X_BF16_SUBLANES
    if routed is None:
        routed = [_router(x2, *_router_params(p, layer)) for x2 in x2s]
    counts = jnp.zeros((1, V7X_LANES), jnp.int32)
    ranks = []
    for ids, _ in routed:
        rank, counts = _expert_ranks(ids, counts)
        ranks.append(rank)
    counts = counts[0, :n_exp]
    padded = ((counts + tm - 1) // tm) * tm
    ends = jnp.cumsum(padded)
    starts = ends - padded
    n_tiles = (MOE_TOP_K * n_total + n_exp * (tm - 1)) // tm
    n_valid = (ends[-1] // tm).astype(jnp.int32)
    tile_start = jnp.minimum(jnp.arange(n_tiles, dtype=jnp.int32), n_valid - 1) * tm
    tile_expert = jnp.minimum(jnp.sum((ends[None, :] <= tile_start[:, None]).astype(jnp.int32), axis=1), n_exp - 1)
    x_sorted = jnp.zeros((n_tiles * tm, d), F32)
    slots = []
    for x2, (ids, _), rank in zip(x2s, routed, ranks):
        choice = ids[:, :MOE_TOP_K]
        onehot = choice[:, :, None] == jnp.arange(n_exp, dtype=jnp.int32)[None, None, :]
        slot = jnp.sum(jnp.where(onehot, starts[None, None, :], 0), axis=-1) + rank[:, :MOE_TOP_K]
        x_sorted = _dispatch(x2, slot, x_sorted, tm=_row_tile(x2.shape[0], 256))
        slots.append(slot)
    y_sorted = _expert_ffn(x_sorted, tile_expert, n_valid.reshape(1), w_gate, w_up, w_down, layer, tm=tm)
    return [_combine_postnorm(y_sorted, slot, wts, x2, ln_g, ln_b, alpha=alpha,
                              tm=_row_tile(x2.shape[0], 256)).reshape(x.shape)
            for x, x2, (_, wts), slot in zip(xs, x2s, routed, slots)]


def _trunks(groups, p):
    depth = p['ln_g'].shape[0]
    alpha = (2.0 * depth) ** 0.25
    xs = [grp[0] for grp in groups]
    outs = [([], [], [], [], []) for _ in groups]
    for layer in range(depth):
        i = layer // 2
        g, bta = p['ln_g'][layer], p['ln_b'][layer]
        rw, rb = _router_params(p, layer)
        wq, wo = p['mem_wq'][layer], p['mem_wo'][layer]
        routed = []
        for gi, (_, dn_s, dn_conv, ssm_re, ssm_im, cconv, mem_k, mem_v) in enumerate(groups):
            out_s, out_conv, out_re, out_im, out_cc = outs[gi]
            x = xs[gi]
            b, t, d = x.shape
            if t == 1:
                if layer % 2 == 0:
                    x, s_new, conv_new, re_new, im_new = _ab_layer(x, dn_s[i], dn_conv[i], ssm_re[i], ssm_im[i],
                                                                   p, i, g[0], bta[0], alpha=alpha)
                else:
                    x, cc_new = _conv_layer(x, cconv[i], p, i, g[0], bta[0], alpha=alpha)
                x = _mem_layer(x, mem_k, mem_v, layer, wq, wo, g[1], bta[1], alpha=alpha)
                routed.append(_router(x.reshape(b * t, d), rw, rb))
            else:
                tq = _row_tile(t, 512)
                if layer % 2 == 0:
                    o, y_tm, (s_new, conv_new, re_new, im_new) = _ab_mixers(x, dn_s[i], dn_conv[i], ssm_re[i],
                                                                            ssm_im[i], p, i)
                    n_o, n_y = o.shape[-1], y_tm.shape[1] // b
                    w_out = p['ab_w_out'][i]
                    proj = [(o, pl.BlockSpec((None, tq, n_o), lambda bi, j: (bi, j, 0)), w_out[:n_o]),
                            (y_tm, pl.BlockSpec((tq, n_y), lambda bi, j: (j, bi)), w_out[n_o:])]
                    bias = None
                else:
                    hc, cc_new = _conv_mixer(x, cconv[i], p, i)
                    proj = [(hc.reshape(b, t, d), pl.BlockSpec((None, tq, d), lambda bi, j: (bi, j, 0)),
                             p['cc_w_pw2'][i])]
                    bias = p['cc_b_pw2'][i]
                n_mem = mem_k.shape[2]
                x, ids, wts = _proj_mem_router(
                    proj, bias, x, (g[0], bta[0]), mem_k.reshape(depth, b, n_mem, d),
                    mem_v.reshape(depth, b, n_mem, d), layer, wq, wo, (g[1], bta[1]), rw, rb, alpha=alpha, tq=tq)
                routed.append((ids, wts))
            if layer % 2 == 0:
                out_s.append(s_new)
                out_conv.append(conv_new)
                out_re.append(re_new)
                out_im.append(im_new)
            else:
                out_cc.append(cc_new)
            xs[gi] = x
        xs = _moe_layer(xs, p, layer, g[2], bta[2], alpha=alpha, routed=routed)
    return [(x,) + tuple(jnp.stack(o) for o in out) for x, out in zip(xs, outs)]


def kernel(x_prompt, x_sample, state_dn_s, state_dn_conv, state_ssm_re, state_ssm_im, state_cconv,
           cache_mem_k, cache_mem_v, mem_prompt, ab_w_in, dn_conv_w, dn_a_log, dn_dt_bias, dn_norm_g,
           ssm_lambda_re, ssm_lambda_im, ssm_log_dt, ssm_b_re, ssm_b_im, ssm_c_re, ssm_c_im, ssm_d,
           ssm_w_glu, ssm_b_glu, ab_w_out, cc_w_pw1, cc_b_pw1, cc_w_dw, cc_b_dw, cc_ln_g, cc_ln_b,
           cc_w_pw2, cc_b_pw2, mem_wq, mem_wk, mem_wv, mem_wo, ln_g, ln_b, moe_w_group, moe_b_group,
           moe_w_expert, moe_b_expert, moe_w_gate, moe_w_up, moe_w_down):
    p = dict(ab_w_in=ab_w_in, dn_conv_w=dn_conv_w, dn_a_log=dn_a_log, dn_dt_bias=dn_dt_bias,
             dn_norm_g=dn_norm_g, ssm_lambda_re=ssm_lambda_re, ssm_lambda_im=ssm_lambda_im,
             ssm_log_dt=ssm_log_dt, ssm_b_re=ssm_b_re, ssm_b_im=ssm_b_im, ssm_c_re=ssm_c_re,
             ssm_c_im=ssm_c_im, ssm_d=ssm_d, ssm_w_glu=ssm_w_glu, ssm_b_glu=ssm_b_glu, ab_w_out=ab_w_out,
             cc_w_pw1=cc_w_pw1, cc_b_pw1=cc_b_pw1, cc_w_dw=cc_w_dw, cc_b_dw=cc_b_dw, cc_ln_g=cc_ln_g,
             cc_ln_b=cc_ln_b, cc_w_pw2=cc_w_pw2, cc_b_pw2=cc_b_pw2, mem_wq=mem_wq, mem_wo=mem_wo,
             ln_g=ln_g, ln_b=ln_b, moe_w_group=moe_w_group, moe_b_group=moe_b_group,
             moe_w_expert=moe_w_expert, moe_b_expert=moe_b_expert, moe_w_gate=moe_w_gate,
             moe_w_up=moe_w_up, moe_w_down=moe_w_down)
    depth = ln_g.shape[0]
    n_ab = state_dn_s.shape[0]
    n_cc = state_cconv.shape[0]
    bsz, _, d = x_prompt.shape
    n_mem = mem_prompt.shape[1]
    hd = d // MEM_HEADS
    z_dn_s = jnp.zeros((n_ab, bsz) + state_dn_s.shape[2:], F32)
    z_dn_conv = jnp.zeros((n_ab, bsz) + state_dn_conv.shape[2:], F32)
    z_ssm = jnp.zeros((n_ab, bsz) + state_ssm_re.shape[2:], F32)
    z_cconv = jnp.zeros((n_cc, bsz) + state_cconv.shape[2:], F32)
    mem2 = mem_prompt.reshape(bsz * n_mem, d)
    p_mem_k = jnp.stack([_linear(mem2, mem_wk[l]) for l in range(depth)]).reshape(depth, bsz, n_mem, MEM_HEADS, hd)
    p_mem_v = jnp.stack([_linear(mem2, mem_wv[l]) for l in range(depth)]).reshape(depth, bsz, n_mem, MEM_HEADS, hd)
    (y_prompt, p_dn_s, p_dn_conv, p_ssm_re, p_ssm_im, p_cconv), \
        (y_sample, s_dn_s, s_dn_conv, s_ssm_re, s_ssm_im, s_cconv) = _trunks(
            [(x_prompt, z_dn_s, z_dn_conv, z_ssm, z_ssm, z_cconv, p_mem_k, p_mem_v),
             (x_sample, state_dn_s, state_dn_conv, state_ssm_re, state_ssm_im, state_cconv, cache_mem_k,
              cache_mem_v)], p)
    return (y_prompt, y_sample, p_dn_s, p_dn_conv, p_ssm_re, p_ssm_im, p_cconv, p_mem_k, p_mem_v,
            s_dn_s, s_dn_conv, s_ssm_re, s_ssm_im, s_cconv)
```

```python
import functools
import math

import jax
import jax.numpy as jnp
from jax import lax
from jax.experimental import pallas as pl
from jax.experimental.pallas import tpu as pltpu

F32 = jnp.float32
BF16 = jnp.bfloat16

DN_HEADS = 4
DN_DK = 128
DN_DV = 128
DN_CONV = 4
DN_CHUNK = 64
SSM_GROUP_CH = 16
MEM_HEADS = 4
MOE_GROUPS = 4
MOE_EXPERTS_PER_GROUP = 8
MOE_TOP_K = 2
LN_EPS = 1e-5
RMS_EPS = 1e-6

V7X_LANES = 128
V7X_SUBLANES = 8
V7X_BF16_SUBLANES = 16
V7X_VMEM_LIMIT_BYTES = 52 * 1024 * 1024


def _cparams(*sem):
    return pltpu.CompilerParams(dimension_semantics=sem, vmem_limit_bytes=V7X_VMEM_LIMIT_BYTES)


def _bdot(a, b):
    return jnp.dot(a.astype(BF16), b.astype(BF16), preferred_element_type=F32)


def _bdot_nt(a, b):
    return lax.dot_general(a.astype(BF16), b.astype(BF16), (((1,), (1,)), ((), ())),
                           preferred_element_type=F32)


def _split3(a):
    hi = a.astype(BF16)
    r1 = a - hi.astype(F32)
    mid = r1.astype(BF16)
    lo = (r1 - mid.astype(F32)).astype(BF16)
    return hi, mid, lo


def _sigmoid(x):
    return 1.0 / (1.0 + jnp.exp(-x))


def _silu(x):
    return x * _sigmoid(x)


def _softplus(x):
    return jnp.maximum(x, 0.0) + jnp.log(1.0 + jnp.exp(-jnp.abs(x)))


def _layer_norm(v, g, b):
    mu = jnp.mean(v, axis=-1, keepdims=True)
    d = v - mu
    var = jnp.mean(d * d, axis=-1, keepdims=True)
    return d * lax.rsqrt(var + LN_EPS) * g + b


def _row_tile(n, pref):
    t = min(n, pref)
    assert n % t == 0, (n, t)
    return t


def _linear_kernel(x_ref, w_ref, b_ref, o_ref, *, glu):
    y = _bdot(x_ref[...], w_ref[...]) + b_ref[...]
    if glu:
        n = y.shape[-1] // 2
        y = y[:, :n] * _sigmoid(y[:, n:])
    o_ref[...] = y.astype(o_ref.dtype)


def _linear(x, w, bias=None, *, glu=False, tm=512, out_dtype=F32):
    m, k = x.shape
    n = w.shape[1]
    if bias is None:
        bias = jnp.zeros((n,), F32)
    tm = _row_tile(m, tm)
    n_out = n // 2 if glu else n
    return pl.pallas_call(
        functools.partial(_linear_kernel, glu=glu),
        grid=(m // tm,),
        in_specs=[pl.BlockSpec((tm, k), lambda i: (i, 0)),
                  pl.BlockSpec((k, n), lambda i: (0, 0)),
                  pl.BlockSpec((1, n), lambda i: (0, 0))],
        out_specs=pl.BlockSpec((tm, n_out), lambda i: (i, 0)),
        out_shape=jax.ShapeDtypeStruct((m, n_out), out_dtype),
        compiler_params=_cparams("parallel"),
        name="linear",
    )(x, w.astype(BF16), bias.reshape(1, n).astype(F32))


def _linear_postnorm_kernel(h_ref, w_ref, b_ref, x_ref, g_ref, beta_ref, o_ref, *, alpha):
    h = _bdot(h_ref[...], w_ref[...]) + b_ref[...]
    o_ref[...] = _layer_norm(alpha * x_ref[...] + h, g_ref[...], beta_ref[...])


def _linear_postnorm(h_in, w, bias, x_res, g, beta, *, alpha, tm=512):
    m, k = h_in.shape
    d = w.shape[1]
    if bias is None:
        bias = jnp.zeros((d,), F32)
    tm = _row_tile(m, tm)
    row = lambda a: a.reshape(1, d).astype(F32)
    return pl.pallas_call(
        functools.partial(_linear_postnorm_kernel, alpha=alpha),
        grid=(m // tm,),
        in_specs=[pl.BlockSpec((tm, k), lambda i: (i, 0)),
                  pl.BlockSpec((k, d), lambda i: (0, 0)),
                  pl.BlockSpec((1, d), lambda i: (0, 0)),
                  pl.BlockSpec((tm, d), lambda i: (i, 0)),
                  pl.BlockSpec((1, d), lambda i: (0, 0)),
                  pl.BlockSpec((1, d), lambda i: (0, 0))],
        out_specs=pl.BlockSpec((tm, d), lambda i: (i, 0)),
        out_shape=jax.ShapeDtypeStruct((m, d), F32),
        compiler_params=_cparams("parallel"),
        name="linear_postnorm",
    )(h_in, w.astype(BF16), row(bias), x_res, row(g), row(beta))


def _ab_in_kernel(x_ref, wqkvz_ref, wba_ref, wu_ref, qkv_ref, z_ref, ba_ref, u_ref, *, n_qkv):
    x = x_ref[...].astype(BF16)
    y = jnp.dot(x, wqkvz_ref[...], preferred_element_type=F32)
    qkv_ref[...] = y[:, :n_qkv]
    z_ref[...] = y[:, n_qkv:]
    ba_ref[...] = jnp.dot(x, wba_ref[...], preferred_element_type=F32)
    u_ref[...] = jnp.dot(x, wu_ref[...], preferred_element_type=F32)


def _ab_in_proj(x, w_in, *, tm=512):
    bx, tx, d = x.shape
    n_key = DN_HEADS * DN_DK
    n_val = DN_HEADS * DN_DV
    n_qkv = 2 * n_key + n_val
    off_beta = n_qkv + n_val
    off_u = off_beta + 2 * DN_HEADS
    n_u = w_in.shape[1] - off_u
    w_qkvz = w_in[:, :off_beta].astype(BF16)
    w_ba = jnp.pad(w_in[:, off_beta:off_u], ((0, 0), (0, V7X_LANES - 2 * DN_HEADS))).astype(BF16)
    w_u = w_in[:, off_u:].astype(BF16)
    tm = _row_tile(tx, tm)
    full = lambda a: pl.BlockSpec(a.shape, lambda b, i: (0, 0))
    return pl.pallas_call(
        functools.partial(_ab_in_kernel, n_qkv=n_qkv),
        grid=(bx, tx // tm),
        in_specs=[pl.BlockSpec((None, tm, d), lambda b, i: (b, i, 0)),
                  full(w_qkvz), full(w_ba), full(w_u)],
        out_specs=[pl.BlockSpec((None, tm, n_qkv), lambda b, i: (b, i, 0)),
                   pl.BlockSpec((None, tm, n_val), lambda b, i: (b, i, 0)),
                   pl.BlockSpec((None, tm, V7X_LANES), lambda b, i: (b, i, 0)),
                   pl.BlockSpec((tm, n_u), lambda b, i: (i, b))],
        out_shape=[jax.ShapeDtypeStruct((bx, tx, n_qkv), F32),
                   jax.ShapeDtypeStruct((bx, tx, n_val), F32),
                   jax.ShapeDtypeStruct((bx, tx, V7X_LANES), F32),
                   jax.ShapeDtypeStruct((tx, bx * n_u), F32)],
        compiler_params=_cparams("parallel", "parallel"),
        name="ab_in_proj",
    )(x, w_qkvz, w_ba, w_u)


def _ab_out_kernel(o_ref, y_ref, wt_ref, wb_ref, x_ref, g_ref, beta_ref, out_ref, *, alpha):
    h = _bdot(o_ref[...], wt_ref[...]) + _bdot(y_ref[...], wb_ref[...])
    out_ref[...] = _layer_norm(alpha * x_ref[...] + h, g_ref[...], beta_ref[...])


def _ab_out_proj(o, y_tm, w_out, x, g, beta, *, alpha, tm=512):
    bx, tx, d = x.shape
    n_o = o.shape[-1]
    n_y = y_tm.shape[1] // bx
    tm = _row_tile(tx, tm)
    wt = w_out[:n_o].astype(BF16)
    wb = w_out[n_o:].astype(BF16)
    row = lambda a: a.reshape(1, d).astype(F32)
    full = lambda a: pl.BlockSpec(a.shape, lambda b, i: (0, 0))
    return pl.pallas_call(
        functools.partial(_ab_out_kernel, alpha=alpha),
        grid=(bx, tx // tm),
        in_specs=[pl.BlockSpec((None, tm, n_o), lambda b, i: (b, i, 0)),
                  pl.BlockSpec((tm, n_y), lambda b, i: (i, b)),
                  full(wt), full(wb),
                  pl.BlockSpec((None, tm, d), lambda b, i: (b, i, 0)),
                  pl.BlockSpec((1, d), lambda b, i: (0, 0)),
                  pl.BlockSpec((1, d), lambda b, i: (0, 0))],
        out_specs=pl.BlockSpec((None, tm, d), lambda b, i: (b, i, 0)),
        out_shape=jax.ShapeDtypeStruct((bx, tx, d), F32),
        compiler_params=_cparams("parallel", "parallel"),
        name="ab_out_proj",
    )(o, y_tm, wt, wb, x, row(g), row(beta))


_DN_HALO = V7X_SUBLANES


def _dn_seq_kernel(qkv_ref, ba_ref, z_ref, cbuf_ref, s0_ref, cw_ref, alog_ref, dtb_ref, ng_ref,
                   o_ref, s_out_ref, cbuf_out_ref, xp_ref, s_ref, *, c, bb):
    j = pl.program_id(1)
    nj = pl.num_programs(1)
    hist = DN_CONV - 1
    n_key = DN_HEADS * DN_DK
    units = [(bi, h) for bi in range(bb) for h in range(DN_HEADS)]
    each = lambda f: {u: f(u) for u in units}

    @pl.when(j == 0)
    def _():
        xp_ref[:, 0:_DN_HALO - hist, :] = jnp.zeros((bb, _DN_HALO - hist, xp_ref.shape[-1]), F32)
        xp_ref[:, _DN_HALO - hist:_DN_HALO, :] = cbuf_ref[...]
        s_ref[...] = s0_ref[...]

    ys, tails = [], []
    for bi in range(bb):
        xp_ref[bi, _DN_HALO:_DN_HALO + c, :] = qkv_ref[bi]
        y = xp_ref[bi, pl.ds(_DN_HALO - hist, c), :] * cw_ref[0:1, :]
        for tap in range(1, DN_CONV):
            y = y + xp_ref[bi, pl.ds(_DN_HALO - hist + tap, c), :] * cw_ref[tap:tap + 1, :]
        tail = xp_ref[bi, pl.ds(_DN_HALO + c - hist, hist), :]
        xp_ref[bi, _DN_HALO - hist:_DN_HALO, :] = tail
        tails.append(tail)
        ys.append(_silu(y))

    ii = lax.broadcasted_iota(jnp.int32, (c, c), 0)
    jj = lax.broadcasted_iota(jnp.int32, (c, c), 1)
    incl = ii >= jj
    strict = ii > jj
    tril = jnp.where(incl, 1.0, 0.0).astype(BF16)
    eye = jnp.where(ii == jj, 1.0, 0.0)
    n_double = int(math.log2(c))
    vsl = lambda h: slice(h * DN_DV, (h + 1) * DN_DV)

    def unit_inputs(u):
        bi, h = u
        lo, hi = h * DN_DK, (h + 1) * DN_DK
        y = ys[bi]
        qh = y[:, lo:hi]
        kh = y[:, n_key + lo:n_key + hi]
        vh = y[:, 2 * n_key + h * DN_DV:2 * n_key + (h + 1) * DN_DV]
        qh = qh * lax.rsqrt(jnp.sum(qh * qh, -1, keepdims=True) + RMS_EPS) * (DN_DK ** -0.5)
        kh = kh * lax.rsqrt(jnp.sum(kh * kh, -1, keepdims=True) + RMS_EPS)
        beta = _sigmoid(ba_ref[bi, :, h:h + 1])
        a_logit = ba_ref[bi, :, DN_HEADS + h:DN_HEADS + h + 1]
        g = -jnp.exp(alog_ref[0:1, lo:hi]) * _softplus(a_logit + dtb_ref[0:1, lo:hi])
        return qh, kh, vh, beta, g

    inp = each(unit_inputs)
    q = each(lambda u: inp[u][0])
    k = each(lambda u: inp[u][1])
    beta = each(lambda u: inp[u][3])
    parts = [p for u in units for p in _split3(inp[u][4])]
    gc_all = jnp.dot(tril, jnp.concatenate(parts, axis=1), preferred_element_type=F32)
    lanes = lambda i: slice(i * DN_DK, (i + 1) * DN_DK)
    gc = {u: gc_all[:, lanes(3 * i)] + gc_all[:, lanes(3 * i + 1)] + gc_all[:, lanes(3 * i + 2)]
          for i, u in enumerate(units)}

    def unit_decay(u):
        gc_row = jnp.transpose(gc[u])[0:1, :]
        diff = gc[u][:, :c] - gc_row
        return jnp.where(incl, jnp.exp(jnp.where(incl, diff, 0.0)), 0.0)

    decay = each(unit_decay)
    eg = each(lambda u: jnp.exp(gc[u]))
    kdec = each(lambda u: k[u] * jnp.exp(gc[u][c - 1:c, :] - gc[u]))
    kb = each(lambda u: k[u] * beta[u])
    akq = each(lambda u: _bdot_nt(jnp.concatenate([kb[u], q[u]], axis=0), k[u]))
    lmat = each(lambda u: jnp.where(strict, akq[u][:c] * decay[u], 0.0))
    qk = each(lambda u: akq[u][c:] * decay[u])
    tmat = each(lambda u: eye - lmat[u])
    lpow = each(lambda u: _bdot(lmat[u], lmat[u]))
    for step in range(1, n_double):
        if step < n_double - 1:
            prod = each(lambda u: _bdot(jnp.concatenate([lpow[u], tmat[u]], axis=0), lpow[u]))
            lpow = each(lambda u: prod[u][:c])
            tmat = each(lambda u: tmat[u] + prod[u][c:])
        else:
            prod = each(lambda u: _bdot(tmat[u], lpow[u]))
            tmat = each(lambda u: tmat[u] + prod[u])
    uw = each(lambda u: _bdot(tmat[u], jnp.concatenate([inp[u][2] * beta[u], kb[u] * eg[u]], axis=1)))
    s_old = each(lambda u: s_ref[u[0], u[1]])
    wq_s = each(lambda u: _bdot(jnp.concatenate([uw[u][:, DN_DV:], q[u] * eg[u]], axis=0), s_old[u]))
    v_new = each(lambda u: uw[u][:, :DN_DV] - wq_s[u][:c])
    o = each(lambda u: wq_s[u][c:] + _bdot(qk[u], v_new[u]))
    s_new = each(lambda u: s_old[u] * jnp.exp(gc[u][c - 1:c, :]) + _bdot(jnp.transpose(kdec[u]), v_new[u]))
    for u in units:
        bi, h = u
        s_ref[bi, h] = s_new[u]
        on = o[u] * lax.rsqrt(jnp.mean(o[u] * o[u], -1, keepdims=True) + RMS_EPS) * ng_ref[...]
        o_ref[bi, :, vsl(h)] = on * _silu(z_ref[bi, :, vsl(h)])

    @pl.when(j == nj - 1)
    def _():
        s_out_ref[...] = s_ref[...]
        for bi in range(bb):
            cbuf_out_ref[bi] = tails[bi]


def _dn_step_kernel(qkv_ref, ba_ref, z_ref, cbuf_ref, s0_ref, cw_ref, alog_ref, dtb_ref, ng_ref,
                    o_ref, s_out_ref, cbuf_out_ref, *, bb):
    hist = DN_CONV - 1
    n_key = DN_HEADS * DN_DK
    x = qkv_ref[...]
    y = x * cw_ref[hist:hist + 1, :]
    for tap in range(hist):
        y = y + cbuf_ref[:, tap, :] * cw_ref[tap:tap + 1, :]
    for tap in range(1, hist):
        cbuf_out_ref[:, tap - 1, :] = cbuf_ref[:, tap, :]
    cbuf_out_ref[:, hist - 1, :] = x
    y = _silu(y)
    ba = ba_ref[...]
    z = z_ref[...]
    row8 = lax.broadcasted_iota(jnp.int32, (V7X_SUBLANES, DN_DK), 0)
    row16 = lax.broadcasted_iota(jnp.int32, (V7X_BF16_SUBLANES, DN_DK), 0)
    heads = range(DN_HEADS)
    units = [(h, r) for h in heads for r in range(bb)]
    each = lambda f: {u: f(u) for u in units}
    one = lambda a, r: a[r:r + 1]

    def head_inputs(h):
        lo, hi = h * DN_DK, (h + 1) * DN_DK
        qh = y[:, lo:hi]
        kh = y[:, n_key + lo:n_key + hi]
        vh = y[:, 2 * n_key + h * DN_DV:2 * n_key + (h + 1) * DN_DV]
        qh = qh * lax.rsqrt(jnp.sum(qh * qh, -1, keepdims=True) + RMS_EPS) * (DN_DK ** -0.5)
        kh = kh * lax.rsqrt(jnp.sum(kh * kh, -1, keepdims=True) + RMS_EPS)
        beta = _sigmoid(ba[:, h:h + 1])
        a_logit = ba[:, DN_HEADS + h:DN_HEADS + h + 1]
        g = -jnp.exp(alog_ref[0:1, lo:hi]) * _softplus(a_logit + dtb_ref[0:1, lo:hi])
        eg = jnp.exp(g)
        return dict(k=kh, eg=eg, w=kh * beta * eg, qg=qh * eg, u=vh * beta, qk=jnp.sum(qh * kh, -1, keepdims=True))

    hd = [head_inputs(h) for h in heads]
    s_old = each(lambda u: s0_ref[u[1], u[0]])
    lhs = each(lambda u: jnp.where(row8 == 0, one(hd[u[0]]['w'], u[1]),
                                   jnp.where(row8 == 1, one(hd[u[0]]['qg'], u[1]), 0.0)))
    ws_qs = each(lambda u: _bdot(lhs[u], s_old[u]))
    v_new = each(lambda u: one(hd[u[0]]['u'], u[1]) - ws_qs[u][0:1])
    o_row = each(lambda u: ws_qs[u][1:2] + one(hd[u[0]]['qk'], u[1]) * v_new[u])
    k16 = each(lambda u: jnp.where(row16 == 0, one(hd[u[0]]['k'], u[1]), 0.0))
    v16 = each(lambda u: jnp.where(row16 == 0, v_new[u], 0.0))
    upd = each(lambda u: _bdot(jnp.transpose(k16[u]), v16[u]))
    for u in units:
        s_out_ref[u[1], u[0]] = s_old[u] * one(hd[u[0]]['eg'], u[1]) + upd[u]
    for h in heads:
        o = jnp.concatenate([o_row[(h, r)] for r in range(bb)], axis=0)
        o = o * lax.rsqrt(jnp.mean(o * o, -1, keepdims=True) + RMS_EPS) * ng_ref[...]
        o_ref[:, h * DN_DV:(h + 1) * DN_DV] = o * _silu(z[:, h * DN_DV:(h + 1) * DN_DV])


def _deltanet(qkv, ba, z, conv_buf, s0, conv_w, a_log, dt_bias, norm_g):
    b, t, n_qkv = qkv.shape
    n_val = DN_HEADS * DN_DV
    hist = DN_CONV - 1
    rep = lambda a: jnp.repeat(a.astype(F32), DN_DK).reshape(1, DN_HEADS * DN_DK)
    cw = conv_w.astype(F32)
    consts = (cw, rep(a_log), rep(dt_bias), norm_g.reshape(1, DN_DV).astype(F32))
    state_shape = (DN_HEADS, DN_DK, DN_DV)
    out_shape = [jax.ShapeDtypeStruct((b, t, n_val), F32),
                 jax.ShapeDtypeStruct((b,) + state_shape, F32),
                 jax.ShapeDtypeStruct((b, hist, n_qkv), F32)]
    if t == 1:
        bb = _row_tile(b, V7X_SUBLANES)
        const = lambda a: pl.BlockSpec(a.shape, lambda i: (0,) * a.ndim)
        o, s_new, cbuf_new = pl.pallas_call(
            functools.partial(_dn_step_kernel, bb=bb),
            grid=(b // bb,),
            in_specs=[pl.BlockSpec((bb, n_qkv), lambda i: (i, 0)),
                      pl.BlockSpec((bb, V7X_LANES), lambda i: (i, 0)),
                      pl.BlockSpec((bb, n_val), lambda i: (i, 0)),
                      pl.BlockSpec((bb, hist, n_qkv), lambda i: (i, 0, 0)),
                      pl.BlockSpec((bb,) + state_shape, lambda i: (i, 0, 0, 0))] + [const(a) for a in consts],
            out_specs=[pl.BlockSpec((bb, n_val), lambda i: (i, 0)),
                       pl.BlockSpec((bb,) + state_shape, lambda i: (i, 0, 0, 0)),
                       pl.BlockSpec((bb, hist, n_qkv), lambda i: (i, 0, 0))],
            out_shape=[jax.ShapeDtypeStruct((b, n_val), F32)] + out_shape[1:],
            compiler_params=_cparams("parallel"),
            name="deltanet_step",
        )(qkv.reshape(b, n_qkv), ba.reshape(b, V7X_LANES), z.reshape(b, n_val), conv_buf.astype(F32),
          s0.astype(F32), *consts)
        return o.reshape(b, 1, n_val), s_new, cbuf_new
    c = DN_CHUNK
    assert t % c == 0 and c >= hist
    bb = _row_tile(b, 4)
    const = lambda a: pl.BlockSpec(a.shape, lambda i, j: (0,) * a.ndim)
    return pl.pallas_call(
        functools.partial(_dn_seq_kernel, c=c, bb=bb),
        grid=(b // bb, t // c),
        in_specs=[pl.BlockSpec((bb, c, n_qkv), lambda i, j: (i, j, 0)),
                  pl.BlockSpec((bb, c, V7X_LANES), lambda i, j: (i, j, 0)),
                  pl.BlockSpec((bb, c, n_val), lambda i, j: (i, j, 0)),
                  pl.BlockSpec((bb, hist, n_qkv), lambda i, j: (i, 0, 0)),
                  pl.BlockSpec((bb,) + state_shape, lambda i, j: (i, 0, 0, 0))] + [const(a) for a in consts],
        out_specs=[pl.BlockSpec((bb, c, n_val), lambda i, j: (i, j, 0)),
                   pl.BlockSpec((bb,) + state_shape, lambda i, j: (i, 0, 0, 0)),
                   pl.BlockSpec((bb, hist, n_qkv), lambda i, j: (i, 0, 0))],
        out_shape=out_shape,
        scratch_shapes=[pltpu.VMEM((bb, _DN_HALO + c, n_qkv), F32),
                        pltpu.VMEM((bb,) + state_shape, F32)],
        compiler_params=_cparams("parallel", "arbitrary"),
        name="deltanet_seq",
    )(qkv, ba, z, conv_buf.astype(F32), s0.astype(F32), *consts)


def _s5_param_kernel(lre_ref, lim_ref, ldt_ref, lbre_ref, lbim_ref, fre_ref, fim_ref):
    lam_re = lre_ref[...]
    lam_im = lim_ref[...]
    dt = jnp.exp(ldt_ref[...])
    mag = jnp.exp(lam_re * dt)
    ang = lam_im * dt
    lb_re = mag * jnp.cos(ang)
    lb_im = mag * jnp.sin(ang)
    den = lam_re * lam_re + lam_im * lam_im
    lbre_ref[...] = lb_re
    lbim_ref[...] = lb_im
    fre_ref[...] = ((lb_re - 1.0) * lam_re + lb_im * lam_im) / den
    fim_ref[...] = (lb_im * lam_re - (lb_re - 1.0) * lam_im) / den


def _s5_discretize(lam_re, lam_im, log_dt):
    g, n = lam_re.shape
    ldt = jnp.broadcast_to(log_dt.astype(F32)[:, None], (g, n))
    shp = jax.ShapeDtypeStruct((g, n), F32)
    return pl.pallas_call(_s5_param_kernel, out_shape=[shp] * 4, name="s5_discretize")(
        lam_re.astype(F32), lam_im.astype(F32), ldt)


def _s5_kernel(u_ref, h0re_ref, h0im_ref, lbre_ref, lbim_ref, bre_ref, bim_ref, cre_ref, cim_ref,
               d_ref, wglu_ref, bglu_ref, y_ref, hre_out_ref, him_out_ref,
               sre_ref, sim_ref, cre_s, cim_s, *, tb, bb, lane_chunk):
    j = pl.program_id(1)
    nj = pl.num_programs(1)
    n_ch = u_ref.shape[-1]
    n_st = sre_ref.shape[-1]
    halves = bre_ref.shape[0]
    ch_h = n_ch // halves
    st_h = n_st // halves

    @pl.when(j == 0)
    def _():
        cre_s[...] = h0re_ref[...]
        cim_s[...] = h0im_ref[...]

    u = u_ref[...].reshape(tb * bb, n_ch)
    ub = u.astype(BF16)
    for hf in range(halves):
        uh = ub[:, hf * ch_h:(hf + 1) * ch_h]
        sre_ref[:, hf * st_h:(hf + 1) * st_h] = jnp.dot(uh, bre_ref[hf], preferred_element_type=F32)
        sim_ref[:, hf * st_h:(hf + 1) * st_h] = jnp.dot(uh, bim_ref[hf], preferred_element_type=F32)

    for c0 in range(0, n_st, lane_chunk):
        cs = slice(c0, c0 + lane_chunk)
        lr = jnp.broadcast_to(lbre_ref[0:1, cs], (bb, lane_chunk))
        li = jnp.broadcast_to(lbim_ref[0:1, cs], (bb, lane_chunk))

        def body(t, carry, cs=cs, lr=lr, li=li):
            hr, hi = carry
            r = pl.multiple_of(t * bb, bb)
            nr = lr * hr - li * hi + sre_ref[pl.ds(r, bb), cs]
            ni = lr * hi + li * hr + sim_ref[pl.ds(r, bb), cs]
            sre_ref[pl.ds(r, bb), cs] = nr
            sim_ref[pl.ds(r, bb), cs] = ni
            return nr, ni

        hr, hi = lax.fori_loop(0, tb, body, (cre_s[:, cs], cim_s[:, cs]))
        cre_s[:, cs] = hr
        cim_s[:, cs] = hi

    ys = []
    for hf in range(halves):
        hre = sre_ref[:, hf * st_h:(hf + 1) * st_h].astype(BF16)
        him = sim_ref[:, hf * st_h:(hf + 1) * st_h].astype(BF16)
        ys.append(jnp.dot(hre, cre_ref[hf], preferred_element_type=F32)
                  - jnp.dot(him, cim_ref[hf], preferred_element_type=F32))
    y = jnp.concatenate(ys, axis=1) + d_ref[...] * u
    y = jax.nn.gelu(y)
    y = y * _sigmoid(_bdot(y, wglu_ref[...]) + bglu_ref[...])
    y_ref[...] = y.reshape(tb, bb, n_ch)

    @pl.when(j == nj - 1)
    def _():
        hre_out_ref[...] = cre_s[...]
        him_out_ref[...] = cim_s[...]


def _s5(u_tm, h0_re, h0_im, lam_re, lam_im, log_dt, b_re, b_im, c_re, c_im, d_skip, w_glu, b_glu, *, halves=2):
    t, b, n_ch = u_tm.shape
    g, n, p = b_re.shape
    n_st = g * n
    lb_re, lb_im, f_re, f_im = _s5_discretize(lam_re, lam_im, log_dt)
    b_re = b_re.astype(F32)
    b_im = b_im.astype(F32)
    bb_re = f_re[..., None] * b_re - f_im[..., None] * b_im
    bb_im = f_re[..., None] * b_im + f_im[..., None] * b_re
    gh = g // halves

    def in_blocks(a):
        a = a.reshape(halves, gh, n, p)
        eye = jnp.eye(gh, dtype=F32)
        return jnp.einsum('hgnp,gk->hgpkn', a, eye).reshape(halves, gh * p, gh * n).astype(BF16)

    def out_blocks(a):
        a = a.astype(F32).reshape(halves, gh, p, n)
        eye = jnp.eye(gh, dtype=F32)
        return jnp.einsum('hgpn,gk->hgnkp', a, eye).reshape(halves, gh * n, gh * p).astype(BF16)

    bre_m, bim_m = in_blocks(bb_re), in_blocks(bb_im)
    cre_m, cim_m = out_blocks(c_re), out_blocks(c_im)
    bb = b if b <= 128 else 128
    assert b % bb == 0
    tb = _row_tile(t, max(1, 512 // bb))
    lane_chunk = max(V7X_LANES, min(n_st, 8192 // bb))
    const = lambda a: pl.BlockSpec(a.shape, lambda i, j: (0,) * a.ndim)
    row = lambda a, m: a.reshape(1, m).astype(F32)
    args = (u_tm, h0_re.reshape(b, n_st).astype(F32), h0_im.reshape(b, n_st).astype(F32),
            row(lb_re, n_st), row(lb_im, n_st), bre_m, bim_m, cre_m, cim_m,
            row(d_skip, n_ch), w_glu.astype(BF16), row(b_glu, n_ch))
    y, hre, him = pl.pallas_call(
        functools.partial(_s5_kernel, tb=tb, bb=bb, lane_chunk=lane_chunk),
        grid=(b // bb, t // tb),
        in_specs=[pl.BlockSpec((tb, bb, n_ch), lambda i, j: (j, i, 0)),
                  pl.BlockSpec((bb, n_st), lambda i, j: (i, 0)),
                  pl.BlockSpec((bb, n_st), lambda i, j: (i, 0))] + [const(a) for a in args[3:]],
        out_specs=[pl.BlockSpec((tb, bb, n_ch), lambda i, j: (j, i, 0)),
                   pl.BlockSpec((bb, n_st), lambda i, j: (i, 0)),
                   pl.BlockSpec((bb, n_st), lambda i, j: (i, 0))],
        out_shape=[jax.ShapeDtypeStruct((t, b, n_ch), F32),
                   jax.ShapeDtypeStruct((b, n_st), F32),
                   jax.ShapeDtypeStruct((b, n_st), F32)],
        scratch_shapes=[pltpu.VMEM((tb * bb, n_st), F32), pltpu.VMEM((tb * bb, n_st), F32),
                        pltpu.VMEM((bb, n_st), F32), pltpu.VMEM((bb, n_st), F32)],
        compiler_params=_cparams("parallel", "arbitrary"),
        name="s5",
    )(*args)
    return y, hre.reshape(b, g, n), him.reshape(b, g, n)


def _ab_mixers(x, dn_s, dn_conv, ssm_re, ssm_im, p, i):
    b, t, d = x.shape
    xr = x if t > 1 else x.reshape(1, b, d)
    bx, tx, _ = xr.shape
    qkv, z, ba, u_tm = _ab_in_proj(xr, p['ab_w_in'][i])
    n_ch = u_tm.shape[1] // bx
    shp = lambda a: a.reshape(b, t, a.shape[-1])
    o, s_new, conv_new = _deltanet(shp(qkv), shp(ba), shp(z), dn_conv, dn_s, p['dn_conv_w'][i],
                                   p['dn_a_log'][i], p['dn_dt_bias'][i], p['dn_norm_g'][i])
    y_tm, h_re, h_im = _s5(u_tm.reshape(t, b, n_ch), ssm_re, ssm_im, p['ssm_lambda_re'][i],
                           p['ssm_lambda_im'][i], p['ssm_log_dt'][i], p['ssm_b_re'][i], p['ssm_b_im'][i],
                           p['ssm_c_re'][i], p['ssm_c_im'][i], p['ssm_d'][i], p['ssm_w_glu'][i],
                           p['ssm_b_glu'][i])
    return o.reshape(bx, tx, -1), y_tm.reshape(tx, bx * n_ch), (s_new, conv_new, h_re, h_im)


def _ab_layer(x, dn_s, dn_conv, ssm_re, ssm_im, p, i, ln_g, ln_b, *, alpha):
    b, t, d = x.shape
    o, y_tm, states = _ab_mixers(x, dn_s, dn_conv, ssm_re, ssm_im, p, i)
    xr = x.reshape(o.shape[0], o.shape[1], d)
    x_new = _ab_out_proj(o, y_tm, p['ab_w_out'][i], xr, ln_g, ln_b, alpha=alpha)
    return (x_new.reshape(b, t, d),) + states


_CC_HALO = 32
_CC_ROW_TILES = 8


def _cconv_seq_kernel(h_ref, buf_ref, w_ref, bdw_ref, g_ref, b_ref, o_ref, xp_ref, xs_ref, acc_ref, *, tt, width):
    j = pl.program_id(1)
    hist = width - 1
    d = h_ref.shape[-1]
    rows = V7X_SUBLANES

    @pl.when(j == 0)
    def _():
        xp_ref[0:_CC_HALO - hist, :] = jnp.zeros((_CC_HALO - hist, d), F32)
        xp_ref[_CC_HALO - hist:_CC_HALO, :] = buf_ref[...]

    xp_ref[_CC_HALO:_CC_HALO + tt, :] = h_ref[...]
    n_shift = xs_ref.shape[1]
    for s in range(1, rows):
        xs_ref[s - 1] = xp_ref[pl.ds(s, n_shift), :]
    base = _CC_HALO - hist
    for c0 in range(0, d, V7X_LANES):
        cs = slice(c0, c0 + V7X_LANES)
        taps = [jnp.broadcast_to(w_ref[k:k + 1, cs], (rows, V7X_LANES)) for k in range(width)]

        def body(i, carry, taps=taps, cs=cs):
            r = pl.multiple_of(i * (rows * _CC_ROW_TILES), rows * _CC_ROW_TILES)
            accs = [None] * _CC_ROW_TILES
            for s in range(rows):
                ks = [k for k in range(width) if (base + k) % rows == s]
                if not ks:
                    continue
                n_rows = rows * (_CC_ROW_TILES + max((base + k) // rows for k in ks))
                big = xp_ref[pl.ds(r, n_rows), cs] if s == 0 else xs_ref[s - 1, pl.ds(r, n_rows), cs]
                for k in ks:
                    a = (base + k) // rows
                    for sub in range(_CC_ROW_TILES):
                        term = big[(sub + a) * rows:(sub + a + 1) * rows] * taps[k]
                        accs[sub] = term if accs[sub] is None else accs[sub] + term
            for sub in range(_CC_ROW_TILES):
                acc_ref[pl.ds(r + sub * rows, rows), cs] = accs[sub]
            return carry

        lax.fori_loop(0, tt // (rows * _CC_ROW_TILES), body, 0)
    xp_ref[0:_CC_HALO, :] = xp_ref[tt:tt + _CC_HALO, :]
    y = _layer_norm(acc_ref[...] + bdw_ref[...], g_ref[...], b_ref[...])
    o_ref[...] = _silu(y)


def _cconv_seq(h, buf, w_dw, b_dw, ln_g, ln_b, *, tt=256):
    b, t, d = h.shape
    width = w_dw.shape[0]
    tt = _row_tile(t, tt)
    assert tt >= _CC_HALO and width - 1 <= _CC_HALO
    row = lambda a: a.reshape(1, d).astype(F32)
    const = lambda a: pl.BlockSpec(a.shape, lambda i, j: (0, 0))
    args = (w_dw.astype(F32), row(b_dw), row(ln_g), row(ln_b))
    return pl.pallas_call(
        functools.partial(_cconv_seq_kernel, tt=tt, width=width),
        grid=(b, t // tt),
        in_specs=[pl.BlockSpec((None, tt, d), lambda i, j: (i, j, 0)),
                  pl.BlockSpec((None, width - 1, d), lambda i, j: (i, 0, 0))] + [const(a) for a in args],
        out_specs=pl.BlockSpec((None, tt, d), lambda i, j: (i, j, 0)),
        out_shape=jax.ShapeDtypeStruct((b, t, d), F32),
        scratch_shapes=[pltpu.VMEM((_CC_HALO + tt, d), F32),
                        pltpu.VMEM((V7X_SUBLANES - 1, _CC_HALO + tt - V7X_SUBLANES, d), F32),
                        pltpu.VMEM((tt, d), F32)],
        compiler_params=_cparams("parallel", "arbitrary"),
        name="cconv_seq",
    )(h, buf.astype(F32), *args)


def _cconv_step_kernel(h_ref, buf_ref, w_ref, bdw_ref, g_ref, b_ref, o_ref, *, width):
    hist = width - 1
    acc = jnp.sum(buf_ref[...] * w_ref[0:hist, :][None], axis=1) + h_ref[...] * w_ref[hist:width, :]
    o_ref[...] = _silu(_layer_norm(acc + bdw_ref[...], g_ref[...], b_ref[...]))


def _cconv_step(h, buf, w_dw, b_dw, ln_g, ln_b, *, bb=8):
    b, d = h.shape
    width = w_dw.shape[0]
    bb = _row_tile(b, bb)
    row = lambda a: a.reshape(1, d).astype(F32)
    const = lambda a: pl.BlockSpec(a.shape, lambda i: (0, 0))
    args = (w_dw.astype(F32), row(b_dw), row(ln_g), row(ln_b))
    return pl.pallas_call(
        functools.partial(_cconv_step_kernel, width=width),
        grid=(b // bb,),
        in_specs=[pl.BlockSpec((bb, d), lambda i: (i, 0)),
                  pl.BlockSpec((bb, width - 1, d), lambda i: (i, 0, 0))] + [const(a) for a in args],
        out_specs=pl.BlockSpec((bb, d), lambda i: (i, 0)),
        out_shape=jax.ShapeDtypeStruct((b, d), F32),
        compiler_params=_cparams("parallel"),
        name="cconv_step",
    )(h, buf.astype(F32), *args)


def _conv_mixer(x, buf, p, i):
    b, t, d = x.shape
    h = _linear(x.reshape(b * t, d), p['cc_w_pw1'][i], p['cc_b_pw1'][i], glu=True)
    args = (p['cc_w_dw'][i], p['cc_b_dw'][i], p['cc_ln_g'][i], p['cc_ln_b'][i])
    if t == 1:
        hc = _cconv_step(h, buf, *args)
    else:
        hc = _cconv_seq(h.reshape(b, t, d), buf, *args).reshape(b * t, d)
    new_buf = jnp.concatenate([buf.astype(F32), h.reshape(b, t, d)], axis=1)[:, t:]
    return hc, new_buf


def _conv_layer(x, buf, p, i, ln_g, ln_b, *, alpha):
    b, t, d = x.shape
    hc, new_buf = _conv_mixer(x, buf, p, i)
    x_new = _linear_postnorm(hc, p['cc_w_pw2'][i], p['cc_b_pw2'][i], x.reshape(b * t, d), ln_g, ln_b, alpha=alpha)
    return x_new.reshape(b, t, d), new_buf


def _mem_attn_seq_kernel(x_ref, wq_ref, k_ref, v_ref, wo_ref, g_ref, b_ref, o_ref, *, alpha, heads):
    o_ref[...] = _mem_attn_math(x_ref[...], wq_ref, k_ref, v_ref, wo_ref, g_ref, b_ref, alpha=alpha, heads=heads)


def _mem_attn_math(x, wq_ref, k_ref, v_ref, wo_ref, g_ref, b_ref, *, alpha, heads):
    d = x.shape[-1]
    hd = d // heads
    q = _bdot(x, wq_ref[...]) * (hd ** -0.5)
    k = k_ref[...].astype(BF16)
    v = v_ref[...].astype(BF16)
    hsl = [slice(h * hd, (h + 1) * hd) for h in range(heads)]
    ss = [_bdot_nt(q[:, hs], k[:, hs]) for hs in hsl]
    es = [jnp.exp(s - jnp.max(s, axis=-1, keepdims=True)) for s in ss]
    ps = [e / jnp.sum(e, axis=-1, keepdims=True) for e in es]
    o = jnp.concatenate([_bdot(a, v[:, hs]) for a, hs in zip(ps, hsl)], axis=1)
    hres = _bdot(o, wo_ref[...])
    return _layer_norm(alpha * x + hres, g_ref[...], b_ref[...])


def _proj_mem_router_kernel(*refs, n_proj, alpha, heads, groups, per_group):
    proj = refs[:2 * n_proj]
    (bias_ref, xres_ref, g0_ref, b0_ref, wq_ref, k_ref, v_ref, wo_ref, g1_ref, b1_ref, rw_ref, rb_ref,
     x_out_ref, ids_ref, wts_ref) = refs[2 * n_proj:]
    h = bias_ref[...]
    for a in range(n_proj):
        h = h + _bdot(proj[2 * a][...], proj[2 * a + 1][...])
    x = _layer_norm(alpha * xres_ref[...] + h, g0_ref[...], b0_ref[...])
    x = _mem_attn_math(x, wq_ref, k_ref, v_ref, wo_ref, g1_ref, b1_ref, alpha=alpha, heads=heads)
    x_out_ref[...] = x
    ids_ref[...], wts_ref[...] = _router_math(x, rw_ref[...], rb_ref[...], groups=groups, per_group=per_group)


def _proj_mem_router(proj, bias, x_res, ln0, mk, mv, layer, wq, wo, ln1, router_w, router_b, *, alpha, tq=512):
    b, t, d = x_res.shape
    m = mk.shape[2]
    nt = t // tq
    groups = MOE_GROUPS
    row = lambda a: a.reshape(1, d).astype(F32)
    const = lambda a: pl.BlockSpec(a.shape, lambda i, j: (0,) * a.ndim)
    xblk = pl.BlockSpec((None, tq, d), lambda i, j: (i, j, 0))
    rowblk = pl.BlockSpec((1, d), lambda i, j: (0, 0))
    proj_args, proj_specs = [], []
    for arr, spec, w in proj:
        wb = w.astype(BF16)
        proj_args += [arr, wb]
        proj_specs += [spec, const(wb)]
    if bias is None:
        bias = jnp.zeros((d,), F32)
    wqb, wob = wq.astype(BF16), wo.astype(BF16)
    lane_blk = pl.BlockSpec((tq, V7X_LANES), lambda i, j: (i * nt + j, 0))
    x_out, ids, wts = pl.pallas_call(
        functools.partial(_proj_mem_router_kernel, n_proj=len(proj), alpha=alpha, heads=MEM_HEADS,
                          groups=groups, per_group=MOE_EXPERTS_PER_GROUP),
        grid=(b, nt),
        in_specs=proj_specs + [rowblk, xblk, rowblk, rowblk, const(wqb),
                               pl.BlockSpec((None, None, m, d), lambda i, j: (layer, i, 0, 0)),
                               pl.BlockSpec((None, None, m, d), lambda i, j: (layer, i, 0, 0)),
                               const(wob), rowblk, rowblk, const(router_w), const(router_b)],
        out_specs=[xblk, lane_blk, lane_blk],
        out_shape=[jax.ShapeDtypeStruct((b, t, d), F32),
                   jax.ShapeDtypeStruct((b * t, V7X_LANES), jnp.int32),
                   jax.ShapeDtypeStruct((b * t, V7X_LANES), F32)],
        compiler_params=_cparams("parallel", "parallel"),
        name="proj_mem_router",
    )(*proj_args, row(bias), x_res, row(ln0[0]), row(ln0[1]), wqb, mk, mv, wob, row(ln1[0]), row(ln1[1]),
      router_w, router_b)
    return x_out, ids, wts


def _mem_attn_seq(x, mk, mv, layer, wq, wo, ln_g, ln_b, *, alpha, tq=512):
    b, t, d = x.shape
    m = mk.shape[2]
    tq = _row_tile(t, tq)
    row = lambda a: a.reshape(1, d).astype(F32)
    const = lambda a: pl.BlockSpec(a.shape, lambda i, j: (0, 0))
    wqb, wob = wq.astype(BF16), wo.astype(BF16)
    return pl.pallas_call(
        functools.partial(_mem_attn_seq_kernel, alpha=alpha, heads=MEM_HEADS),
        grid=(b, t // tq),
        in_specs=[pl.BlockSpec((None, tq, d), lambda i, j: (i, j, 0)),
                  const(wqb),
                  pl.BlockSpec((None, None, m, d), lambda i, j: (layer, i, 0, 0)),
                  pl.BlockSpec((None, None, m, d), lambda i, j: (layer, i, 0, 0)),
                  const(wob), pl.BlockSpec((1, d), lambda i, j: (0, 0)),
                  pl.BlockSpec((1, d), lambda i, j: (0, 0))],
        out_specs=pl.BlockSpec((None, tq, d), lambda i, j: (i, j, 0)),
        out_shape=jax.ShapeDtypeStruct((b, t, d), F32),
        compiler_params=_cparams("parallel", "parallel"),
        name="mem_attn_seq",
    )(x, wqb, mk, mv, wob, row(ln_g), row(ln_b))


def _mem_attn_step_kernel(q_ref, k_ref, v_ref, o_ref, *, heads, bb):
    i = pl.program_id(0)
    hd = q_ref.shape[-1]
    for r in range(bb):
        q = q_ref[i * bb + r] * (hd ** -0.5)
        s = jnp.sum(k_ref[r] * q[None], axis=-1, keepdims=True)
        s = s - jnp.max(s, axis=0, keepdims=True)
        e = jnp.exp(s)
        o_ref[i * bb + r] = jnp.sum(e * v_ref[r], axis=0) / jnp.sum(e, axis=0)


def _mem_attn_step(q, mk_all, mv_all, layer, *, bb=4):
    b, heads, hd = q.shape
    m = mk_all.shape[2]
    bb = _row_tile(b, bb)
    return pl.pallas_call(
        functools.partial(_mem_attn_step_kernel, heads=heads, bb=bb),
        grid=(b // bb,),
        in_specs=[pl.BlockSpec((b, heads, hd), lambda i: (0, 0, 0)),
                  pl.BlockSpec((None, bb, m, heads, hd), lambda i: (layer, i, 0, 0, 0)),
                  pl.BlockSpec((None, bb, m, heads, hd), lambda i: (layer, i, 0, 0, 0))],
        out_specs=pl.BlockSpec((b, heads, hd), lambda i: (0, 0, 0)),
        out_shape=jax.ShapeDtypeStruct((b, heads, hd), F32),
        compiler_params=_cparams("arbitrary"),
        name="mem_attn_step",
    )(q, mk_all, mv_all)


def _mem_layer(x, mk_all, mv_all, layer, wq, wo, ln_g, ln_b, *, alpha):
    b, t, d = x.shape
    n_layers, _, m, heads, hd = mk_all.shape
    if t == 1:
        x2 = x.reshape(b, d)
        q = _linear(x2, wq).reshape(b, heads, hd)
        o = _mem_attn_step(q, mk_all, mv_all, layer).reshape(b, d)
        return _linear_postnorm(o, wo, None, x2, ln_g, ln_b, alpha=alpha).reshape(b, t, d)
    mk_all = mk_all.reshape(n_layers, b, m, d)
    mv_all = mv_all.reshape(n_layers, b, m, d)
    return _mem_attn_seq(x, mk_all, mv_all, layer, wq, wo, ln_g, ln_b, alpha=alpha)


_NEG = -1e30


def _router_kernel(x_ref, w_ref, b_ref, ids_ref, wts_ref, *, groups, per_group):
    ids_ref[...], wts_ref[...] = _router_math(x_ref[...], w_ref[...], b_ref[...], groups=groups,
                                              per_group=per_group)


def _router_math(x, w, bias, *, groups, per_group):
    xh = x.astype(BF16)
    xl = (x - xh.astype(F32)).astype(BF16)
    wh = w.astype(BF16)
    wl = (w - wh.astype(F32)).astype(BF16)
    logits = (jnp.dot(xh, wh, preferred_element_type=F32) + jnp.dot(xh, wl, preferred_element_type=F32)
              + jnp.dot(xl, wh, preferred_element_type=F32)) + bias
    lane = lax.broadcasted_iota(jnp.int32, logits.shape, 1)
    n_exp = groups * per_group
    is_g = lane < groups
    gl = jnp.where(is_g, logits, _NEG)
    gmax = jnp.max(gl, axis=-1, keepdims=True)
    gsum = jnp.sum(jnp.where(is_g, jnp.exp(gl - gmax), 0.0), axis=-1, keepdims=True)
    gw = 1.0 / gsum
    gi = jnp.min(jnp.where(gl == gmax, lane, V7X_LANES), axis=-1, keepdims=True)
    lane_grp = (lane - groups) // per_group
    sel = (lane >= groups) & (lane < groups + n_exp) & (lane_grp == gi)
    el = jnp.where(sel, logits, _NEG)
    emax = jnp.max(el, axis=-1, keepdims=True)
    ee = jnp.where(sel, jnp.exp(el - emax), 0.0)
    ep = jnp.where(sel, ee / jnp.sum(ee, axis=-1, keepdims=True), -1.0)
    p1 = jnp.max(ep, axis=-1, keepdims=True)
    i1 = jnp.min(jnp.where(ep == p1, lane, V7X_LANES), axis=-1, keepdims=True)
    ep2 = jnp.where(lane == i1, -1.0, ep)
    p2 = jnp.max(ep2, axis=-1, keepdims=True)
    i2 = jnp.min(jnp.where(ep2 == p2, lane, V7X_LANES), axis=-1, keepdims=True)
    denom = p1 + p2
    ids = jnp.where(lane == 0, i1 - groups, jnp.where(lane == 1, i2 - groups, 0))
    wts = jnp.where(lane == 0, gw * p1 / denom, jnp.where(lane == 1, gw * p2 / denom, 0.0))
    return ids, wts


def _router_params(p, layer):
    w_group, w_expert = p['moe_w_group'][layer], p['moe_w_expert'][layer]
    pad = V7X_LANES - w_group.shape[1] - w_expert.shape[1]
    w = jnp.pad(jnp.concatenate([w_group, w_expert], axis=1).astype(F32), ((0, 0), (0, pad)))
    b = jnp.pad(jnp.concatenate([p['moe_b_group'][layer], p['moe_b_expert'][layer]]).astype(F32), (0, pad))
    return w, b.reshape(1, V7X_LANES)


def _router(x, w, b, *, tm=512):
    n, d = x.shape
    tm = _row_tile(n, tm)
    return pl.pallas_call(
        functools.partial(_router_kernel, groups=MOE_GROUPS, per_group=MOE_EXPERTS_PER_GROUP),
        grid=(n // tm,),
        in_specs=[pl.BlockSpec((tm, d), lambda i: (i, 0)),
                  pl.BlockSpec((d, V7X_LANES), lambda i: (0, 0)),
                  pl.BlockSpec((1, V7X_LANES), lambda i: (0, 0))],
        out_specs=[pl.BlockSpec((tm, V7X_LANES), lambda i: (i, 0)),
                   pl.BlockSpec((tm, V7X_LANES), lambda i: (i, 0))],
        out_shape=[jax.ShapeDtypeStruct((n, V7X_LANES), jnp.int32),
                   jax.ShapeDtypeStruct((n, V7X_LANES), F32)],
        compiler_params=_cparams("parallel"),
        name="moe_router",
    )(x, w, b)


def _expert_ffn_kernel(te_ref, nv_ref, x_ref, wg_ref, wu_ref, wd_ref, o_ref):
    i = pl.program_id(0)

    @pl.when(i < nv_ref[0])
    def _():
        x = x_ref[...]
        hid = _silu(_bdot(x, wg_ref[...])) * _bdot(x, wu_ref[...])
        o_ref[...] = _bdot(hid, wd_ref[...])

    @pl.when(i >= nv_ref[0])
    def _():
        o_ref[...] = jnp.zeros_like(o_ref)


def _expert_ffn(x_sorted, tile_expert, n_valid, w_gate, w_up, w_down, layer, *, tm):
    r, d = x_sorted.shape
    f = w_gate.shape[-1]
    return pl.pallas_call(
        _expert_ffn_kernel,
        grid_spec=pltpu.PrefetchScalarGridSpec(
            num_scalar_prefetch=2,
            grid=(r // tm,),
            in_specs=[pl.BlockSpec((tm, d), lambda i, te, nv: (jnp.minimum(i, nv[0] - 1), 0)),
                      pl.BlockSpec((None, None, d, f), lambda i, te, nv: (layer, te[i], 0, 0)),
                      pl.BlockSpec((None, None, d, f), lambda i, te, nv: (layer, te[i], 0, 0)),
                      pl.BlockSpec((None, None, f, d), lambda i, te, nv: (layer, te[i], 0, 0))],
            out_specs=pl.BlockSpec((tm, d), lambda i, te, nv: (i, 0)),
        ),
        out_shape=jax.ShapeDtypeStruct((r, d), F32),
        compiler_params=_cparams("arbitrary"),
        name="moe_expert_ffn",
    )(tile_expert, n_valid, x_sorted, w_gate, w_up, w_down)


def _rank_kernel(ids_ref, cnt_in_ref, rank_ref, cnt_ref, carry_ref):
    i = pl.program_id(0)
    tm = ids_ref.shape[0]

    @pl.when(i == 0)
    def _():
        carry_ref[...] = cnt_in_ref[...].astype(F32)

    ids = ids_ref[...]
    lane = lax.broadcasted_iota(jnp.int32, ids.shape, 1)
    oh0 = jnp.where(lane == ids[:, 0:1], 1.0, 0.0)
    oh1 = jnp.where(lane == ids[:, 1:2], 1.0, 0.0)
    both = oh0 + oh1
    ii = lax.broadcasted_iota(jnp.int32, (tm, tm), 0)
    jj = lax.broadcasted_iota(jnp.int32, (tm, tm), 1)
    earlier = jnp.where(ii > jj, 1.0, 0.0).astype(BF16)
    prefix = jnp.dot(earlier, both.astype(BF16), preferred_element_type=F32) + carry_ref[...]
    r0 = jnp.sum(oh0 * prefix, axis=-1, keepdims=True)
    r1 = jnp.sum(oh1 * prefix, axis=-1, keepdims=True)
    rank_ref[...] = jnp.where(lane == 0, r0, jnp.where(lane == 1, r1, 0.0)).astype(jnp.int32)
    total = carry_ref[...] + jnp.sum(both, axis=0, keepdims=True)
    carry_ref[...] = total
    cnt_ref[...] = total.astype(jnp.int32)


def _expert_ranks(ids, counts_in, *, tm=512):
    n = ids.shape[0]
    tm = _row_tile(n, tm)
    return pl.pallas_call(
        _rank_kernel,
        grid=(n // tm,),
        in_specs=[pl.BlockSpec((tm, V7X_LANES), lambda i: (i, 0)),
                  pl.BlockSpec((1, V7X_LANES), lambda i: (0, 0))],
        out_specs=[pl.BlockSpec((tm, V7X_LANES), lambda i: (i, 0)),
                   pl.BlockSpec((1, V7X_LANES), lambda i: (0, 0))],
        out_shape=[jax.ShapeDtypeStruct((n, V7X_LANES), jnp.int32),
                   jax.ShapeDtypeStruct((1, V7X_LANES), jnp.int32)],
        scratch_shapes=[pltpu.VMEM((1, V7X_LANES), F32)],
        compiler_params=_cparams("arbitrary"),
        name="moe_rank",
    )(ids, counts_in)


def _for_each_row(tm, fn):
    def body(i, carry):
        base = pl.multiple_of(i * V7X_SUBLANES, V7X_SUBLANES)
        for sub in range(V7X_SUBLANES):
            fn(base, sub)
        return carry

    lax.fori_loop(0, tm // V7X_SUBLANES, body, 0)


def _dispatch_kernel(fill_ref, slot_ref, x_ref, *rest, tile, fill_tiles):
    tm = x_ref.shape[0]
    if fill_tiles:
        xs_ref, sem, zbuf, zsem = rest

        @pl.when(pl.program_id(0) == 0)
        def _():
            zbuf[...] = jnp.zeros_like(zbuf)
            n_exp = fill_ref.shape[1]
            n_tiles = xs_ref.shape[0] // tile
            total_end = fill_ref[1, n_exp - 1]
            starts, conds = [], []
            for e in range(n_exp):
                for j in range(fill_tiles):
                    starts.append(fill_ref[0, e] + j * tile)
                    conds.append(starts[-1] < fill_ref[1, e])
            for t in range(n_tiles):
                starts.append(t * tile)
                conds.append(t * tile >= total_end)

            def fill(start):
                return pltpu.make_async_copy(zbuf, xs_ref.at[pl.ds(pl.multiple_of(start, tile), tile), :], zsem)

            for start, c in zip(starts, conds):
                pl.when(c)(lambda start=start: fill(start).start())
            for start, c in zip(starts, conds):
                pl.when(c)(lambda start=start: fill(start).wait())
    else:
        _, xs_ref, sem = rest

    def issue(base, sub):
        for k in range(MOE_TOP_K):
            s = slot_ref[0, 0, MOE_TOP_K * (base + sub) + k]
            pltpu.make_async_copy(x_ref.at[pl.ds(base + sub, 1), :], xs_ref.at[pl.ds(s, 1), :],
                                  sem).start(priority=k)

    _for_each_row(tm, issue)
    for k in range(MOE_TOP_K):
        pltpu.make_async_copy(x_ref, xs_ref.at[pl.ds(0, tm), :], sem).wait()


def _dispatch(x, slot, fill, *, rows, tile, fill_tiles, xs=None, tm):
    n, d = x.shape
    slot3 = slot.reshape(n // tm, 1, MOE_TOP_K * tm)
    in_specs = [pl.BlockSpec(memory_space=pltpu.SMEM),
                pl.BlockSpec((1, 1, MOE_TOP_K * tm), lambda i: (i, 0, 0), memory_space=pltpu.SMEM),
                pl.BlockSpec((tm, d), lambda i: (i, 0))]
    scratch = [pltpu.SemaphoreType.DMA(())]
    fresh = xs is None
    if fresh:
        args, aliases = (fill, slot3, x), {}
        scratch += [pltpu.VMEM((tile, d), F32), pltpu.SemaphoreType.DMA(())]
    else:
        args, aliases = (fill, slot3, x, xs), {3: 0}
        in_specs.append(pl.BlockSpec(memory_space=pl.ANY))
    return pl.pallas_call(
        functools.partial(_dispatch_kernel, tile=tile, fill_tiles=fill_tiles if fresh else 0),
        grid=(n // tm,),
        in_specs=in_specs,
        out_specs=pl.BlockSpec(memory_space=pl.ANY),
        out_shape=jax.ShapeDtypeStruct((rows, d), F32),
        scratch_shapes=scratch,
        input_output_aliases=aliases,
        compiler_params=_cparams("arbitrary"),
        name="moe_dispatch",
    )(*args)


def _combine_kernel(slot_ref, slot_next_ref, wts_ref, x_ref, g_ref, b_ref, y_hbm, o_ref, ybuf, sem, *, alpha):
    i = pl.program_id(0)
    n = pl.num_programs(0)
    tm = x_ref.shape[0]
    cur = i % 2

    def issue(idx_ref, buf):
        def row(base, sub):
            for k in range(MOE_TOP_K):
                s = idx_ref[0, 0, MOE_TOP_K * (base + sub) + k]
                pltpu.make_async_copy(y_hbm.at[pl.ds(s, 1), :], ybuf.at[buf, k, pl.ds(base + sub, 1), :],
                                      sem.at[buf]).start(priority=k)

        _for_each_row(tm, row)

    @pl.when(i == 0)
    def _():
        issue(slot_ref, 0)

    @pl.when(i + 1 < n)
    def _():
        issue(slot_next_ref, 1 - cur)

    for k in range(MOE_TOP_K):
        pltpu.make_async_copy(y_hbm.at[pl.ds(0, tm), :], ybuf.at[cur, k], sem.at[cur]).wait()
    wts = wts_ref[...]
    y = wts[:, 0:1] * ybuf[cur, 0] + wts[:, 1:2] * ybuf[cur, 1]
    o_ref[...] = _layer_norm(alpha * x_ref[...] + y, g_ref[...], b_ref[...])


def _combine_postnorm(y_sorted, slot, wts, x, g, b, *, alpha, tm):
    n, d = x.shape
    nt = n // tm
    slot3 = slot.reshape(nt, 1, MOE_TOP_K * tm)
    row = lambda a: a.reshape(1, d).astype(F32)
    smem = lambda f: pl.BlockSpec((1, 1, MOE_TOP_K * tm), f, memory_space=pltpu.SMEM)
    return pl.pallas_call(
        functools.partial(_combine_kernel, alpha=alpha),
        grid=(nt,),
        in_specs=[smem(lambda i: (i, 0, 0)),
                  smem(lambda i: (jnp.minimum(i + 1, nt - 1), 0, 0)),
                  pl.BlockSpec((tm, V7X_LANES), lambda i: (i, 0)),
                  pl.BlockSpec((tm, d), lambda i: (i, 0)),
                  pl.BlockSpec((1, d), lambda i: (0, 0)),
                  pl.BlockSpec((1, d), lambda i: (0, 0)),
                  pl.BlockSpec(memory_space=pl.ANY)],
        out_specs=pl.BlockSpec((tm, d), lambda i: (i, 0)),
        out_shape=jax.ShapeDtypeStruct((n, d), F32),
        scratch_shapes=[pltpu.VMEM((2, MOE_TOP_K, tm, d), F32), pltpu.SemaphoreType.DMA((2,))],
        compiler_params=_cparams("arbitrary"),
        name="moe_combine",
    )(slot3, slot3, wts, x, row(g), row(b), y_sorted)


def _moe_layer(xs, p, layer, ln_g, ln_b, *, alpha, routed=None):
    w_gate, w_up, w_down = p['moe_w_gate'], p['moe_w_up'], p['moe_w_down']
    n_exp = w_gate.shape[1]
    d = xs[0].shape[-1]
    x2s = [x.reshape(-1, d) for x in xs]
    n_total = sum(x2.shape[0] for x2 in x2s)
    tm = 512 if n_total >= 8192 else V7X_BF16_SUBLANES
    if routed is None:
        routed = [_router(x2, *_router_params(p, layer)) for x2 in x2s]
    counts = jnp.zeros((1, V7X_LANES), jnp.int32)
    ranks = []
    for gi, (ids, _) in enumerate(routed):
        rank, counts = _expert_ranks(ids, counts)
        ranks.append(rank)
        if gi == 0:
            counts_first = counts[0, :n_exp]
    counts = counts[0, :n_exp]
    padded = ((counts + tm - 1) // tm) * tm
    ends = jnp.cumsum(padded)
    starts = ends - padded
    n_tiles = (MOE_TOP_K * n_total + n_exp * (tm - 1)) // tm
    n_valid = (ends[-1] // tm).astype(jnp.int32)
    tile_start = jnp.minimum(jnp.arange(n_tiles, dtype=jnp.int32), n_valid - 1) * tm
    tile_expert = jnp.minimum(jnp.sum((ends[None, :] <= tile_start[:, None]).astype(jnp.int32), axis=1), n_exp - 1)
    fill = jnp.stack([starts + (counts_first // tm) * tm, ends]).astype(jnp.int32)
    later_rows = MOE_TOP_K * (n_total - x2s[0].shape[0])
    fill_tiles = 1 + (later_rows + tm - 1) // tm
    x_sorted = None
    slots = []
    for x2, (ids, _), rank in zip(x2s, routed, ranks):
        choice = ids[:, :MOE_TOP_K]
        onehot = choice[:, :, None] == jnp.arange(n_exp, dtype=jnp.int32)[None, None, :]
        slot = jnp.sum(jnp.where(onehot, starts[None, None, :], 0), axis=-1) + rank[:, :MOE_TOP_K]
        x_sorted = _dispatch(x2, slot, fill, rows=n_tiles * tm, tile=tm, fill_tiles=fill_tiles, xs=x_sorted,
                             tm=_row_tile(x2.shape[0], 256))
        slots.append(slot)
    y_sorted = _expert_ffn(x_sorted, tile_expert, n_valid.reshape(1), w_gate, w_up, w_down, layer, tm=tm)
    return [_combine_postnorm(y_sorted, slot, wts, x2, ln_g, ln_b, alpha=alpha,
                              tm=_row_tile(x2.shape[0], 256)).reshape(x.shape)
            for x, x2, (_, wts), slot in zip(xs, x2s, routed, slots)]


def _trunks(groups, p):
    depth = p['ln_g'].shape[0]
    alpha = (2.0 * depth) ** 0.25
    xs = [grp[0] for grp in groups]
    outs = [([], [], [], [], []) for _ in groups]
    for layer in range(depth):
        i = layer // 2
        g, bta = p['ln_g'][layer], p['ln_b'][layer]
        rw, rb = _router_params(p, layer)
        wq, wo = p['mem_wq'][layer], p['mem_wo'][layer]
        routed = []
        for gi, (_, dn_s, dn_conv, ssm_re, ssm_im, cconv, mem_k, mem_v) in enumerate(groups):
            out_s, out_conv, out_re, out_im, out_cc = outs[gi]
            x = xs[gi]
            b, t, d = x.shape
            if t == 1:
                if layer % 2 == 0:
                    x, s_new, conv_new, re_new, im_new = _ab_layer(x, dn_s[i], dn_conv[i], ssm_re[i], ssm_im[i],
                                                                   p, i, g[0], bta[0], alpha=alpha)
                else:
                    x, cc_new = _conv_layer(x, cconv[i], p, i, g[0], bta[0], alpha=alpha)
                x = _mem_layer(x, mem_k, mem_v, layer, wq, wo, g[1], bta[1], alpha=alpha)
                routed.append(_router(x.reshape(b * t, d), rw, rb))
            else:
                tq = _row_tile(t, 512)
                if layer % 2 == 0:
                    o, y_tm, (s_new, conv_new, re_new, im_new) = _ab_mixers(x, dn_s[i], dn_conv[i], ssm_re[i],
                                                                            ssm_im[i], p, i)
                    n_o, n_y = o.shape[-1], y_tm.shape[1] // b
                    w_out = p['ab_w_out'][i]
                    proj = [(o, pl.BlockSpec((None, tq, n_o), lambda bi, j: (bi, j, 0)), w_out[:n_o]),
                            (y_tm, pl.BlockSpec((tq, n_y), lambda bi, j: (j, bi)), w_out[n_o:])]
                    bias = None
                else:
                    hc, cc_new = _conv_mixer(x, cconv[i], p, i)
                    proj = [(hc.reshape(b, t, d), pl.BlockSpec((None, tq, d), lambda bi, j: (bi, j, 0)),
                             p['cc_w_pw2'][i])]
                    bias = p['cc_b_pw2'][i]
                n_mem = mem_k.shape[2]
                x, ids, wts = _proj_mem_router(
                    proj, bias, x, (g[0], bta[0]), mem_k.reshape(depth, b, n_mem, d),
                    mem_v.reshape(depth, b, n_mem, d), layer, wq, wo, (g[1], bta[1]), rw, rb, alpha=alpha, tq=tq)
                routed.append((ids, wts))
            if layer % 2 == 0:
                out_s.append(s_new)
                out_conv.append(conv_new)
                out_re.append(re_new)
                out_im.append(im_new)
            else:
                out_cc.append(cc_new)
            xs[gi] = x
        xs = _moe_layer(xs, p, layer, g[2], bta[2], alpha=alpha, routed=routed)
    return [(x,) + tuple(jnp.stack(o) for o in out) for x, out in zip(xs, outs)]


def kernel(x_prompt, x_sample, state_dn_s, state_dn_conv, state_ssm_re, state_ssm_im, state_cconv,
           cache_mem_k, cache_mem_v, mem_prompt, ab_w_in, dn_conv_w, dn_a_log, dn_dt_bias, dn_norm_g,
           ssm_lambda_re, ssm_lambda_im, ssm_log_dt, ssm_b_re, ssm_b_im, ssm_c_re, ssm_c_im, ssm_d,
           ssm_w_glu, ssm_b_glu, ab_w_out, cc_w_pw1, cc_b_pw1, cc_w_dw, cc_b_dw, cc_ln_g, cc_ln_b,
           cc_w_pw2, cc_b_pw2, mem_wq, mem_wk, mem_wv, mem_wo, ln_g, ln_b, moe_w_group, moe_b_group,
           moe_w_expert, moe_b_expert, moe_w_gate, moe_w_up, moe_w_down):
    p = dict(ab_w_in=ab_w_in, dn_conv_w=dn_conv_w, dn_a_log=dn_a_log, dn_dt_bias=dn_dt_bias,
             dn_norm_g=dn_norm_g, ssm_lambda_re=ssm_lambda_re, ssm_lambda_im=ssm_lambda_im,
             ssm_log_dt=ssm_log_dt, ssm_b_re=ssm_b_re, ssm_b_im=ssm_b_im, ssm_c_re=ssm_c_re,
             ssm_c_im=ssm_c_im, ssm_d=ssm_d, ssm_w_glu=ssm_w_glu, ssm_b_glu=ssm_b_glu, ab_w_out=ab_w_out,
             cc_w_pw1=cc_w_pw1, cc_b_pw1=cc_b_pw1, cc_w_dw=cc_w_dw, cc_b_dw=cc_b_dw, cc_ln_g=cc_ln_g,
             cc_ln_b=cc_ln_b, cc_w_pw2=cc_w_pw2, cc_b_pw2=cc_b_pw2, mem_wq=mem_wq, mem_wo=mem_wo,
             ln_g=ln_g, ln_b=ln_b, moe_w_group=moe_w_group, moe_b_group=moe_b_group,
             moe_w_expert=moe_w_expert, moe_b_expert=moe_b_expert, moe_w_gate=moe_w_gate,
             moe_w_up=moe_w_up, moe_w_down=moe_w_down)
    depth = ln_g.shape[0]
    n_ab = state_dn_s.shape[0]
    n_cc = state_cconv.shape[0]
    bsz, _, d = x_prompt.shape
    n_mem = mem_prompt.shape[1]
    hd = d // MEM_HEADS
    z_dn_s = jnp.zeros((n_ab, bsz) + state_dn_s.shape[2:], F32)
    z_dn_conv = jnp.zeros((n_ab, bsz) + state_dn_conv.shape[2:], F32)
    z_ssm = jnp.zeros((n_ab, bsz) + state_ssm_re.shape[2:], F32)
    z_cconv = jnp.zeros((n_cc, bsz) + state_cconv.shape[2:], F32)
    mem2 = mem_prompt.reshape(bsz * n_mem, d)
    p_mem_k = jnp.stack([_linear(mem2, mem_wk[l]) for l in range(depth)]).reshape(depth, bsz, n_mem, MEM_HEADS, hd)
    p_mem_v = jnp.stack([_linear(mem2, mem_wv[l]) for l in range(depth)]).reshape(depth, bsz, n_mem, MEM_HEADS, hd)
    (y_prompt, p_dn_s, p_dn_conv, p_ssm_re, p_ssm_im, p_cconv), \
        (y_sample, s_dn_s, s_dn_conv, s_ssm_re, s_ssm_im, s_cconv) = _trunks(
            [(x_prompt, z_dn_s, z_dn_conv, z_ssm, z_ssm, z_cconv, p_mem_k, p_mem_v),
             (x_sample, state_dn_s, state_dn_conv, state_ssm_re, state_ssm_im, state_cconv, cache_mem_k,
              cache_mem_v)], p)
    return (y_prompt, y_sample, p_dn_s, p_dn_conv, p_ssm_re, p_ssm_im, p_cconv, p_mem_k, p_mem_v,
            s_dn_s, s_dn_conv, s_ssm_re, s_ssm_im, s_cconv)
```

```python
import functools
import math

import jax
import jax.numpy as jnp
from jax import lax
from jax.experimental import pallas as pl
from jax.experimental.pallas import tpu as pltpu

F32 = jnp.float32
BF16 = jnp.bfloat16

DN_HEADS = 4
DN_DK = 128
DN_DV = 128
DN_CONV = 4
DN_CHUNK = 64
SSM_GROUP_CH = 16
MEM_HEADS = 4
MOE_GROUPS = 4
MOE_EXPERTS_PER_GROUP = 8
MOE_TOP_K = 2
LN_EPS = 1e-5
RMS_EPS = 1e-6

V7X_LANES = 128
V7X_SUBLANES = 8
V7X_BF16_SUBLANES = 16
V7X_VMEM_LIMIT_BYTES = 52 * 1024 * 1024


def _cparams(*sem):
    return pltpu.CompilerParams(dimension_semantics=sem, vmem_limit_bytes=V7X_VMEM_LIMIT_BYTES)


def _bdot(a, b):
    return jnp.dot(a.astype(BF16), b.astype(BF16), preferred_element_type=F32)


def _bdot_nt(a, b):
    return lax.dot_general(a.astype(BF16), b.astype(BF16), (((1,), (1,)), ((), ())),
                           preferred_element_type=F32)


def _split3(a):
    hi = a.astype(BF16)
    r1 = a - hi.astype(F32)
    mid = r1.astype(BF16)
    lo = (r1 - mid.astype(F32)).astype(BF16)
    return hi, mid, lo


def _sigmoid(x):
    return 1.0 / (1.0 + jnp.exp(-x))


def _silu(x):
    return x * _sigmoid(x)


def _softplus(x):
    return jnp.maximum(x, 0.0) + jnp.log(1.0 + jnp.exp(-jnp.abs(x)))


def _layer_norm(v, g, b):
    mu = jnp.mean(v, axis=-1, keepdims=True)
    d = v - mu
    var = jnp.mean(d * d, axis=-1, keepdims=True)
    return d * lax.rsqrt(var + LN_EPS) * g + b


def _row_tile(n, pref):
    t = min(n, pref)
    assert n % t == 0, (n, t)
    return t


def _linear_kernel(x_ref, w_ref, b_ref, o_ref, *, glu):
    y = _bdot(x_ref[...], w_ref[...]) + b_ref[...]
    if glu:
        n = y.shape[-1] // 2
        y = y[:, :n] * _sigmoid(y[:, n:])
    o_ref[...] = y.astype(o_ref.dtype)


def _linear(x, w, bias=None, *, glu=False, tm=512, out_dtype=F32):
    m, k = x.shape
    n = w.shape[1]
    if bias is None:
        bias = jnp.zeros((n,), F32)
    tm = _row_tile(m, tm)
    n_out = n // 2 if glu else n
    return pl.pallas_call(
        functools.partial(_linear_kernel, glu=glu),
        grid=(m // tm,),
        in_specs=[pl.BlockSpec((tm, k), lambda i: (i, 0)),
                  pl.BlockSpec((k, n), lambda i: (0, 0)),
                  pl.BlockSpec((1, n), lambda i: (0, 0))],
        out_specs=pl.BlockSpec((tm, n_out), lambda i: (i, 0)),
        out_shape=jax.ShapeDtypeStruct((m, n_out), out_dtype),
        compiler_params=_cparams("parallel"),
        name="linear",
    )(x, w.astype(BF16), bias.reshape(1, n).astype(F32))


def _linear_postnorm_kernel(h_ref, w_ref, b_ref, x_ref, g_ref, beta_ref, o_ref, *, alpha):
    h = _bdot(h_ref[...], w_ref[...]) + b_ref[...]
    o_ref[...] = _layer_norm(alpha * x_ref[...] + h, g_ref[...], beta_ref[...])


def _linear_postnorm(h_in, w, bias, x_res, g, beta, *, alpha, tm=512):
    m, k = h_in.shape
    d = w.shape[1]
    if bias is None:
        bias = jnp.zeros((d,), F32)
    tm = _row_tile(m, tm)
    row = lambda a: a.reshape(1, d).astype(F32)
    return pl.pallas_call(
        functools.partial(_linear_postnorm_kernel, alpha=alpha),
        grid=(m // tm,),
        in_specs=[pl.BlockSpec((tm, k), lambda i: (i, 0)),
                  pl.BlockSpec((k, d), lambda i: (0, 0)),
                  pl.BlockSpec((1, d), lambda i: (0, 0)),
                  pl.BlockSpec((tm, d), lambda i: (i, 0)),
                  pl.BlockSpec((1, d), lambda i: (0, 0)),
                  pl.BlockSpec((1, d), lambda i: (0, 0))],
        out_specs=pl.BlockSpec((tm, d), lambda i: (i, 0)),
        out_shape=jax.ShapeDtypeStruct((m, d), F32),
        compiler_params=_cparams("parallel"),
        name="linear_postnorm",
    )(h_in, w.astype(BF16), row(bias), x_res, row(g), row(beta))


def _ab_in_kernel(x_ref, wqkvz_ref, wba_ref, wu_ref, qkv_ref, z_ref, ba_ref, u_ref, *, n_qkv):
    x = x_ref[...].astype(BF16)
    y = jnp.dot(x, wqkvz_ref[...], preferred_element_type=F32)
    qkv_ref[...] = y[:, :n_qkv]
    z_ref[...] = y[:, n_qkv:]
    ba_ref[...] = jnp.dot(x, wba_ref[...], preferred_element_type=F32)
    u_ref[...] = jnp.dot(x, wu_ref[...], preferred_element_type=F32)


def _ab_in_proj(x, w_in, *, tm=512):
    bx, tx, d = x.shape
    n_key = DN_HEADS * DN_DK
    n_val = DN_HEADS * DN_DV
    n_qkv = 2 * n_key + n_val
    off_beta = n_qkv + n_val
    off_u = off_beta + 2 * DN_HEADS
    n_u = w_in.shape[1] - off_u
    w_qkvz = w_in[:, :off_beta].astype(BF16)
    w_ba = jnp.pad(w_in[:, off_beta:off_u], ((0, 0), (0, V7X_LANES - 2 * DN_HEADS))).astype(BF16)
    w_u = w_in[:, off_u:].astype(BF16)
    tm = _row_tile(tx, tm)
    full = lambda a: pl.BlockSpec(a.shape, lambda b, i: (0, 0))
    return pl.pallas_call(
        functools.partial(_ab_in_kernel, n_qkv=n_qkv),
        grid=(bx, tx // tm),
        in_specs=[pl.BlockSpec((None, tm, d), lambda b, i: (b, i, 0)),
                  full(w_qkvz), full(w_ba), full(w_u)],
        out_specs=[pl.BlockSpec((None, tm, n_qkv), lambda b, i: (b, i, 0)),
                   pl.BlockSpec((None, tm, n_val), lambda b, i: (b, i, 0)),
                   pl.BlockSpec((None, tm, V7X_LANES), lambda b, i: (b, i, 0)),
                   pl.BlockSpec((tm, n_u), lambda b, i: (i, b))],
        out_shape=[jax.ShapeDtypeStruct((bx, tx, n_qkv), F32),
                   jax.ShapeDtypeStruct((bx, tx, n_val), F32),
                   jax.ShapeDtypeStruct((bx, tx, V7X_LANES), F32),
                   jax.ShapeDtypeStruct((tx, bx * n_u), F32)],
        compiler_params=_cparams("parallel", "parallel"),
        name="ab_in_proj",
    )(x, w_qkvz, w_ba, w_u)


def _ab_out_kernel(o_ref, y_ref, wt_ref, wb_ref, x_ref, g_ref, beta_ref, out_ref, *, alpha):
    h = _bdot(o_ref[...], wt_ref[...]) + _bdot(y_ref[...], wb_ref[...])
    out_ref[...] = _layer_norm(alpha * x_ref[...] + h, g_ref[...], beta_ref[...])


def _ab_out_proj(o, y_tm, w_out, x, g, beta, *, alpha, tm=512):
    bx, tx, d = x.shape
    n_o = o.shape[-1]
    n_y = y_tm.shape[1] // bx
    tm = _row_tile(tx, tm)
    wt = w_out[:n_o].astype(BF16)
    wb = w_out[n_o:].astype(BF16)
    row = lambda a: a.reshape(1, d).astype(F32)
    full = lambda a: pl.BlockSpec(a.shape, lambda b, i: (0, 0))
    return pl.pallas_call(
        functools.partial(_ab_out_kernel, alpha=alpha),
        grid=(bx, tx // tm),
        in_specs=[pl.BlockSpec((None, tm, n_o), lambda b, i: (b, i, 0)),
                  pl.BlockSpec((tm, n_y), lambda b, i: (i, b)),
                  full(wt), full(wb),
                  pl.BlockSpec((None, tm, d), lambda b, i: (b, i, 0)),
                  pl.BlockSpec((1, d), lambda b, i: (0, 0)),
                  pl.BlockSpec((1, d), lambda b, i: (0, 0))],
        out_specs=pl.BlockSpec((None, tm, d), lambda b, i: (b, i, 0)),
        out_shape=jax.ShapeDtypeStruct((bx, tx, d), F32),
        compiler_params=_cparams("parallel", "parallel"),
        name="ab_out_proj",
    )(o, y_tm, wt, wb, x, row(g), row(beta))


_DN_HALO = V7X_SUBLANES


def _dn_seq_kernel(qkv_ref, ba_ref, z_ref, cbuf_ref, s0_ref, cw_ref, alog_ref, dtb_ref, ng_ref,
                   o_ref, s_out_ref, cbuf_out_ref, xp_ref, s_ref, *, c, bb):
    j = pl.program_id(1)
    nj = pl.num_programs(1)
    hist = DN_CONV - 1
    n_key = DN_HEADS * DN_DK
    units = [(bi, h) for bi in range(bb) for h in range(DN_HEADS)]
    each = lambda f: {u: f(u) for u in units}

    @pl.when(j == 0)
    def _():
        xp_ref[:, 0:_DN_HALO - hist, :] = jnp.zeros((bb, _DN_HALO - hist, xp_ref.shape[-1]), F32)
        xp_ref[:, _DN_HALO - hist:_DN_HALO, :] = cbuf_ref[...]
        s_ref[...] = s0_ref[...]

    ys, tails = [], []
    for bi in range(bb):
        xp_ref[bi, _DN_HALO:_DN_HALO + c, :] = qkv_ref[bi]
        y = xp_ref[bi, pl.ds(_DN_HALO - hist, c), :] * cw_ref[0:1, :]
        for tap in range(1, DN_CONV):
            y = y + xp_ref[bi, pl.ds(_DN_HALO - hist + tap, c), :] * cw_ref[tap:tap + 1, :]
        tail = xp_ref[bi, pl.ds(_DN_HALO + c - hist, hist), :]
        xp_ref[bi, _DN_HALO - hist:_DN_HALO, :] = tail
        tails.append(tail)
        ys.append(_silu(y))

    ii = lax.broadcasted_iota(jnp.int32, (c, c), 0)
    jj = lax.broadcasted_iota(jnp.int32, (c, c), 1)
    incl = ii >= jj
    strict = ii > jj
    tril = jnp.where(incl, 1.0, 0.0).astype(BF16)
    eye = jnp.where(ii == jj, 1.0, 0.0)
    n_double = int(math.log2(c))
    vsl = lambda h: slice(h * DN_DV, (h + 1) * DN_DV)

    def unit_inputs(u):
        bi, h = u
        lo, hi = h * DN_DK, (h + 1) * DN_DK
        y = ys[bi]
        qh = y[:, lo:hi]
        kh = y[:, n_key + lo:n_key + hi]
        vh = y[:, 2 * n_key + h * DN_DV:2 * n_key + (h + 1) * DN_DV]
        qh = qh * lax.rsqrt(jnp.sum(qh * qh, -1, keepdims=True) + RMS_EPS) * (DN_DK ** -0.5)
        kh = kh * lax.rsqrt(jnp.sum(kh * kh, -1, keepdims=True) + RMS_EPS)
        beta = _sigmoid(ba_ref[bi, :, h:h + 1])
        a_logit = ba_ref[bi, :, DN_HEADS + h:DN_HEADS + h + 1]
        g = -jnp.exp(alog_ref[0:1, lo:hi]) * _softplus(a_logit + dtb_ref[0:1, lo:hi])
        return qh, kh, vh, beta, g

    inp = each(unit_inputs)
    q = each(lambda u: inp[u][0])
    k = each(lambda u: inp[u][1])
    beta = each(lambda u: inp[u][3])
    parts = [p for u in units for p in _split3(inp[u][4])]
    gc_all = jnp.dot(tril, jnp.concatenate(parts, axis=1), preferred_element_type=F32)
    lanes = lambda i: slice(i * DN_DK, (i + 1) * DN_DK)
    gc = {u: gc_all[:, lanes(3 * i)] + gc_all[:, lanes(3 * i + 1)] + gc_all[:, lanes(3 * i + 2)]
          for i, u in enumerate(units)}

    def unit_decay(u):
        gc_row = jnp.transpose(gc[u])[0:1, :]
        diff = gc[u][:, :c] - gc_row
        return jnp.where(incl, jnp.exp(jnp.where(incl, diff, 0.0)), 0.0)

    decay = each(unit_decay)
    eg = each(lambda u: jnp.exp(gc[u]))
    kdec = each(lambda u: k[u] * jnp.exp(gc[u][c - 1:c, :] - gc[u]))
    kb = each(lambda u: k[u] * beta[u])
    akq = each(lambda u: _bdot_nt(jnp.concatenate([kb[u], q[u]], axis=0), k[u]))
    lmat = each(lambda u: jnp.where(strict, akq[u][:c] * decay[u], 0.0))
    qk = each(lambda u: akq[u][c:] * decay[u])
    tmat = each(lambda u: eye - lmat[u])
    lpow = each(lambda u: _bdot(lmat[u], lmat[u]))
    for step in range(1, n_double):
        if step < n_double - 1:
            prod = each(lambda u: _bdot(jnp.concatenate([lpow[u], tmat[u]], axis=0), lpow[u]))
            lpow = each(lambda u: prod[u][:c])
            tmat = each(lambda u: tmat[u] + prod[u][c:])
        else:
            prod = each(lambda u: _bdot(tmat[u], lpow[u]))
            tmat = each(lambda u: tmat[u] + prod[u])
    uw = each(lambda u: _bdot(tmat[u], jnp.concatenate([inp[u][2] * beta[u], kb[u] * eg[u]], axis=1)))
    s_old = each(lambda u: s_ref[u[0], u[1]])
    wq_s = each(lambda u: _bdot(jnp.concatenate([uw[u][:, DN_DV:], q[u] * eg[u]], axis=0), s_old[u]))
    v_new = each(lambda u: uw[u][:, :DN_DV] - wq_s[u][:c])
    o = each(lambda u: wq_s[u][c:] + _bdot(qk[u], v_new[u]))
    s_new = each(lambda u: s_old[u] * jnp.exp(gc[u][c - 1:c, :]) + _bdot(jnp.transpose(kdec[u]), v_new[u]))
    for u in units:
        bi, h = u
        s_ref[bi, h] = s_new[u]
        on = o[u] * lax.rsqrt(jnp.mean(o[u] * o[u], -1, keepdims=True) + RMS_EPS) * ng_ref[...]
        o_ref[bi, :, vsl(h)] = on * _silu(z_ref[bi, :, vsl(h)])

    @pl.when(j == nj - 1)
    def _():
        s_out_ref[...] = s_ref[...]
        for bi in range(bb):
            cbuf_out_ref[bi] = tails[bi]


def _dn_step_kernel(qkv_ref, ba_ref, z_ref, cbuf_ref, s0_ref, cw_ref, alog_ref, dtb_ref, ng_ref,
                    o_ref, s_out_ref, cbuf_out_ref, *, bb):
    hist = DN_CONV - 1
    n_key = DN_HEADS * DN_DK
    x = qkv_ref[...]
    y = x * cw_ref[hist:hist + 1, :]
    for tap in range(hist):
        y = y + cbuf_ref[:, tap, :] * cw_ref[tap:tap + 1, :]
    for tap in range(1, hist):
        cbuf_out_ref[:, tap - 1, :] = cbuf_ref[:, tap, :]
    cbuf_out_ref[:, hist - 1, :] = x
    y = _silu(y)
    ba = ba_ref[...]
    z = z_ref[...]
    row8 = lax.broadcasted_iota(jnp.int32, (V7X_SUBLANES, DN_DK), 0)
    row16 = lax.broadcasted_iota(jnp.int32, (V7X_BF16_SUBLANES, DN_DK), 0)
    heads = range(DN_HEADS)
    units = [(h, r) for h in heads for r in range(bb)]
    each = lambda f: {u: f(u) for u in units}
    one = lambda a, r: a[r:r + 1]

    def head_inputs(h):
        lo, hi = h * DN_DK, (h + 1) * DN_DK
        qh = y[:, lo:hi]
        kh = y[:, n_key + lo:n_key + hi]
        vh = y[:, 2 * n_key + h * DN_DV:2 * n_key + (h + 1) * DN_DV]
        qh = qh * lax.rsqrt(jnp.sum(qh * qh, -1, keepdims=True) + RMS_EPS) * (DN_DK ** -0.5)
        kh = kh * lax.rsqrt(jnp.sum(kh * kh, -1, keepdims=True) + RMS_EPS)
        beta = _sigmoid(ba[:, h:h + 1])
        a_logit = ba[:, DN_HEADS + h:DN_HEADS + h + 1]
        g = -jnp.exp(alog_ref[0:1, lo:hi]) * _softplus(a_logit + dtb_ref[0:1, lo:hi])
        eg = jnp.exp(g)
        return dict(k=kh, eg=eg, w=kh * beta * eg, qg=qh * eg, u=vh * beta, qk=jnp.sum(qh * kh, -1, keepdims=True))

    hd = [head_inputs(h) for h in heads]
    s_old = each(lambda u: s0_ref[u[1], u[0]])
    lhs = each(lambda u: jnp.where(row8 == 0, one(hd[u[0]]['w'], u[1]),
                                   jnp.where(row8 == 1, one(hd[u[0]]['qg'], u[1]), 0.0)))
    ws_qs = each(lambda u: _bdot(lhs[u], s_old[u]))
    v_new = each(lambda u: one(hd[u[0]]['u'], u[1]) - ws_qs[u][0:1])
    o_row = each(lambda u: ws_qs[u][1:2] + one(hd[u[0]]['qk'], u[1]) * v_new[u])
    k16 = each(lambda u: jnp.where(row16 == 0, one(hd[u[0]]['k'], u[1]), 0.0))
    v16 = each(lambda u: jnp.where(row16 == 0, v_new[u], 0.0))
    upd = each(lambda u: _bdot(jnp.transpose(k16[u]), v16[u]))
    for u in units:
        s_out_ref[u[1], u[0]] = s_old[u] * one(hd[u[0]]['eg'], u[1]) + upd[u]
    for h in heads:
        o = jnp.concatenate([o_row[(h, r)] for r in range(bb)], axis=0)
        o = o * lax.rsqrt(jnp.mean(o * o, -1, keepdims=True) + RMS_EPS) * ng_ref[...]
        o_ref[:, h * DN_DV:(h + 1) * DN_DV] = o * _silu(z[:, h * DN_DV:(h + 1) * DN_DV])


def _deltanet(qkv, ba, z, conv_buf, s0, conv_w, a_log, dt_bias, norm_g):
    b, t, n_qkv = qkv.shape
    n_val = DN_HEADS * DN_DV
    hist = DN_CONV - 1
    rep = lambda a: jnp.repeat(a.astype(F32), DN_DK).reshape(1, DN_HEADS * DN_DK)
    cw = conv_w.astype(F32)
    consts = (cw, rep(a_log), rep(dt_bias), norm_g.reshape(1, DN_DV).astype(F32))
    state_shape = (DN_HEADS, DN_DK, DN_DV)
    out_shape = [jax.ShapeDtypeStruct((b, t, n_val), F32),
                 jax.ShapeDtypeStruct((b,) + state_shape, F32),
                 jax.ShapeDtypeStruct((b, hist, n_qkv), F32)]
    if t == 1:
        bb = _row_tile(b, V7X_SUBLANES)
        const = lambda a: pl.BlockSpec(a.shape, lambda i: (0,) * a.ndim)
        o, s_new, cbuf_new = pl.pallas_call(
            functools.partial(_dn_step_kernel, bb=bb),
            grid=(b // bb,),
            in_specs=[pl.BlockSpec((bb, n_qkv), lambda i: (i, 0)),
                      pl.BlockSpec((bb, V7X_LANES), lambda i: (i, 0)),
                      pl.BlockSpec((bb, n_val), lambda i: (i, 0)),
                      pl.BlockSpec((bb, hist, n_qkv), lambda i: (i, 0, 0)),
                      pl.BlockSpec((bb,) + state_shape, lambda i: (i, 0, 0, 0))] + [const(a) for a in consts],
            out_specs=[pl.BlockSpec((bb, n_val), lambda i: (i, 0)),
                       pl.BlockSpec((bb,) + state_shape, lambda i: (i, 0, 0, 0)),
                       pl.BlockSpec((bb, hist, n_qkv), lambda i: (i, 0, 0))],
            out_shape=[jax.ShapeDtypeStruct((b, n_val), F32)] + out_shape[1:],
            compiler_params=_cparams("parallel"),
            name="deltanet_step",
        )(qkv.reshape(b, n_qkv), ba.reshape(b, V7X_LANES), z.reshape(b, n_val), conv_buf.astype(F32),
          s0.astype(F32), *consts)
        return o.reshape(b, 1, n_val), s_new, cbuf_new
    c = DN_CHUNK
    assert t % c == 0 and c >= hist
    bb = _row_tile(b, 4)
    const = lambda a: pl.BlockSpec(a.shape, lambda i, j: (0,) * a.ndim)
    return pl.pallas_call(
        functools.partial(_dn_seq_kernel, c=c, bb=bb),
        grid=(b // bb, t // c),
        in_specs=[pl.BlockSpec((bb, c, n_qkv), lambda i, j: (i, j, 0)),
                  pl.BlockSpec((bb, c, V7X_LANES), lambda i, j: (i, j, 0)),
                  pl.BlockSpec((bb, c, n_val), lambda i, j: (i, j, 0)),
                  pl.BlockSpec((bb, hist, n_qkv), lambda i, j: (i, 0, 0)),
                  pl.BlockSpec((bb,) + state_shape, lambda i, j: (i, 0, 0, 0))] + [const(a) for a in consts],
        out_specs=[pl.BlockSpec((bb, c, n_val), lambda i, j: (i, j, 0)),
                   pl.BlockSpec((bb,) + state_shape, lambda i, j: (i, 0, 0, 0)),
                   pl.BlockSpec((bb, hist, n_qkv), lambda i, j: (i, 0, 0))],
        out_shape=out_shape,
        scratch_shapes=[pltpu.VMEM((bb, _DN_HALO + c, n_qkv), F32),
                        pltpu.VMEM((bb,) + state_shape, F32)],
        compiler_params=_cparams("parallel", "arbitrary"),
        name="deltanet_seq",
    )(qkv, ba, z, conv_buf.astype(F32), s0.astype(F32), *consts)


def _s5_param_kernel(lre_ref, lim_ref, ldt_ref, lbre_ref, lbim_ref, fre_ref, fim_ref):
    lam_re = lre_ref[...]
    lam_im = lim_ref[...]
    dt = jnp.exp(ldt_ref[...])
    mag = jnp.exp(lam_re * dt)
    ang = lam_im * dt
    lb_re = mag * jnp.cos(ang)
    lb_im = mag * jnp.sin(ang)
    den = lam_re * lam_re + lam_im * lam_im
    lbre_ref[...] = lb_re
    lbim_ref[...] = lb_im
    fre_ref[...] = ((lb_re - 1.0) * lam_re + lb_im * lam_im) / den
    fim_ref[...] = (lb_im * lam_re - (lb_re - 1.0) * lam_im) / den


def _s5_discretize(lam_re, lam_im, log_dt):
    g, n = lam_re.shape
    ldt = jnp.broadcast_to(log_dt.astype(F32)[:, None], (g, n))
    shp = jax.ShapeDtypeStruct((g, n), F32)
    return pl.pallas_call(_s5_param_kernel, out_shape=[shp] * 4, name="s5_discretize")(
        lam_re.astype(F32), lam_im.astype(F32), ldt)


def _s5_kernel(u_ref, h0re_ref, h0im_ref, lbre_ref, lbim_ref, bre_ref, bim_ref, cre_ref, cim_ref,
               d_ref, wglu_ref, bglu_ref, y_ref, hre_out_ref, him_out_ref,
               sre_ref, sim_ref, cre_s, cim_s, *, tb, bb, lane_chunk):
    j = pl.program_id(1)
    nj = pl.num_programs(1)
    n_ch = u_ref.shape[-1]
    n_st = sre_ref.shape[-1]
    halves = bre_ref.shape[0]
    ch_h = n_ch // halves
    st_h = n_st // halves

    @pl.when(j == 0)
    def _():
        cre_s[...] = h0re_ref[...]
        cim_s[...] = h0im_ref[...]

    u = u_ref[...].reshape(tb * bb, n_ch)
    ub = u.astype(BF16)
    for hf in range(halves):
        uh = ub[:, hf * ch_h:(hf + 1) * ch_h]
        sre_ref[:, hf * st_h:(hf + 1) * st_h] = jnp.dot(uh, bre_ref[hf], preferred_element_type=F32)
        sim_ref[:, hf * st_h:(hf + 1) * st_h] = jnp.dot(uh, bim_ref[hf], preferred_element_type=F32)

    for c0 in range(0, n_st, lane_chunk):
        cs = slice(c0, c0 + lane_chunk)
        lr = jnp.broadcast_to(lbre_ref[0:1, cs], (bb, lane_chunk))
        li = jnp.broadcast_to(lbim_ref[0:1, cs], (bb, lane_chunk))

        def body(t, carry, cs=cs, lr=lr, li=li):
            hr, hi = carry
            r = pl.multiple_of(t * bb, bb)
            nr = lr * hr - li * hi + sre_ref[pl.ds(r, bb), cs]
            ni = lr * hi + li * hr + sim_ref[pl.ds(r, bb), cs]
            sre_ref[pl.ds(r, bb), cs] = nr
            sim_ref[pl.ds(r, bb), cs] = ni
            return nr, ni

        hr, hi = lax.fori_loop(0, tb, body, (cre_s[:, cs], cim_s[:, cs]))
        cre_s[:, cs] = hr
        cim_s[:, cs] = hi

    ys = []
    for hf in range(halves):
        hre = sre_ref[:, hf * st_h:(hf + 1) * st_h].astype(BF16)
        him = sim_ref[:, hf * st_h:(hf + 1) * st_h].astype(BF16)
        ys.append(jnp.dot(hre, cre_ref[hf], preferred_element_type=F32)
                  - jnp.dot(him, cim_ref[hf], preferred_element_type=F32))
    y = jnp.concatenate(ys, axis=1) + d_ref[...] * u
    y = jax.nn.gelu(y)
    y = y * _sigmoid(_bdot(y, wglu_ref[...]) + bglu_ref[...])
    y_ref[...] = y.reshape(tb, bb, n_ch)

    @pl.when(j == nj - 1)
    def _():
        hre_out_ref[...] = cre_s[...]
        him_out_ref[...] = cim_s[...]


def _s5(u_tm, h0_re, h0_im, lam_re, lam_im, log_dt, b_re, b_im, c_re, c_im, d_skip, w_glu, b_glu, *, halves=2):
    t, b, n_ch = u_tm.shape
    g, n, p = b_re.shape
    n_st = g * n
    lb_re, lb_im, f_re, f_im = _s5_discretize(lam_re, lam_im, log_dt)
    b_re = b_re.astype(F32)
    b_im = b_im.astype(F32)
    bb_re = f_re[..., None] * b_re - f_im[..., None] * b_im
    bb_im = f_re[..., None] * b_im + f_im[..., None] * b_re
    gh = g // halves

    def in_blocks(a):
        a = a.reshape(halves, gh, n, p)
        eye = jnp.eye(gh, dtype=F32)
        return jnp.einsum('hgnp,gk->hgpkn', a, eye).reshape(halves, gh * p, gh * n).astype(BF16)

    def out_blocks(a):
        a = a.astype(F32).reshape(halves, gh, p, n)
        eye = jnp.eye(gh, dtype=F32)
        return jnp.einsum('hgpn,gk->hgnkp', a, eye).reshape(halves, gh * n, gh * p).astype(BF16)

    bre_m, bim_m = in_blocks(bb_re), in_blocks(bb_im)
    cre_m, cim_m = out_blocks(c_re), out_blocks(c_im)
    bb = b if b <= 128 else 128
    assert b % bb == 0
    tb = _row_tile(t, max(1, 512 // bb))
    lane_chunk = max(V7X_LANES, min(n_st, 8192 // bb))
    const = lambda a: pl.BlockSpec(a.shape, lambda i, j: (0,) * a.ndim)
    row = lambda a, m: a.reshape(1, m).astype(F32)
    args = (u_tm, h0_re.reshape(b, n_st).astype(F32), h0_im.reshape(b, n_st).astype(F32),
            row(lb_re, n_st), row(lb_im, n_st), bre_m, bim_m, cre_m, cim_m,
            row(d_skip, n_ch), w_glu.astype(BF16), row(b_glu, n_ch))
    y, hre, him = pl.pallas_call(
        functools.partial(_s5_kernel, tb=tb, bb=bb, lane_chunk=lane_chunk),
        grid=(b // bb, t // tb),
        in_specs=[pl.BlockSpec((tb, bb, n_ch), lambda i, j: (j, i, 0)),
                  pl.BlockSpec((bb, n_st), lambda i, j: (i, 0)),
                  pl.BlockSpec((bb, n_st), lambda i, j: (i, 0))] + [const(a) for a in args[3:]],
        out_specs=[pl.BlockSpec((tb, bb, n_ch), lambda i, j: (j, i, 0)),
                   pl.BlockSpec((bb, n_st), lambda i, j: (i, 0)),
                   pl.BlockSpec((bb, n_st), lambda i, j: (i, 0))],
        out_shape=[jax.ShapeDtypeStruct((t, b, n_ch), F32),
                   jax.ShapeDtypeStruct((b, n_st), F32),
                   jax.ShapeDtypeStruct((b, n_st), F32)],
        scratch_shapes=[pltpu.VMEM((tb * bb, n_st), F32), pltpu.VMEM((tb * bb, n_st), F32),
                        pltpu.VMEM((bb, n_st), F32), pltpu.VMEM((bb, n_st), F32)],
        compiler_params=_cparams("parallel", "arbitrary"),
        name="s5",
    )(*args)
    return y, hre.reshape(b, g, n), him.reshape(b, g, n)


def _ab_mixers(x, dn_s, dn_conv, ssm_re, ssm_im, p, i):
    b, t, d = x.shape
    xr = x if t > 1 else x.reshape(1, b, d)
    bx, tx, _ = xr.shape
    qkv, z, ba, u_tm = _ab_in_proj(xr, p['ab_w_in'][i])
    n_ch = u_tm.shape[1] // bx
    shp = lambda a: a.reshape(b, t, a.shape[-1])
    o, s_new, conv_new = _deltanet(shp(qkv), shp(ba), shp(z), dn_conv, dn_s, p['dn_conv_w'][i],
                                   p['dn_a_log'][i], p['dn_dt_bias'][i], p['dn_norm_g'][i])
    y_tm, h_re, h_im = _s5(u_tm.reshape(t, b, n_ch), ssm_re, ssm_im, p['ssm_lambda_re'][i],
                           p['ssm_lambda_im'][i], p['ssm_log_dt'][i], p['ssm_b_re'][i], p['ssm_b_im'][i],
                           p['ssm_c_re'][i], p['ssm_c_im'][i], p['ssm_d'][i], p['ssm_w_glu'][i],
                           p['ssm_b_glu'][i])
    return o.reshape(bx, tx, -1), y_tm.reshape(tx, bx * n_ch), (s_new, conv_new, h_re, h_im)


def _ab_layer(x, dn_s, dn_conv, ssm_re, ssm_im, p, i, ln_g, ln_b, *, alpha):
    b, t, d = x.shape
    o, y_tm, states = _ab_mixers(x, dn_s, dn_conv, ssm_re, ssm_im, p, i)
    xr = x.reshape(o.shape[0], o.shape[1], d)
    x_new = _ab_out_proj(o, y_tm, p['ab_w_out'][i], xr, ln_g, ln_b, alpha=alpha)
    return (x_new.reshape(b, t, d),) + states


_CC_HALO = 32
_CC_ROW_TILES = 8


def _cconv_seq_kernel(h_ref, buf_ref, w_ref, bdw_ref, g_ref, b_ref, o_ref, xp_ref, xs_ref, acc_ref, *, tt, width):
    j = pl.program_id(1)
    hist = width - 1
    d = h_ref.shape[-1]
    rows = V7X_SUBLANES

    @pl.when(j == 0)
    def _():
        xp_ref[0:_CC_HALO - hist, :] = jnp.zeros((_CC_HALO - hist, d), F32)
        xp_ref[_CC_HALO - hist:_CC_HALO, :] = buf_ref[...]

    xp_ref[_CC_HALO:_CC_HALO + tt, :] = h_ref[...]
    n_shift = xs_ref.shape[1]
    for s in range(1, rows):
        xs_ref[s - 1] = xp_ref[pl.ds(s, n_shift), :]
    base = _CC_HALO - hist
    for c0 in range(0, d, V7X_LANES):
        cs = slice(c0, c0 + V7X_LANES)
        taps = [jnp.broadcast_to(w_ref[k:k + 1, cs], (rows, V7X_LANES)) for k in range(width)]

        def body(i, carry, taps=taps, cs=cs):
            r = pl.multiple_of(i * (rows * _CC_ROW_TILES), rows * _CC_ROW_TILES)
            accs = [None] * _CC_ROW_TILES
            for s in range(rows):
                ks = [k for k in range(width) if (base + k) % rows == s]
                if not ks:
                    continue
                n_rows = rows * (_CC_ROW_TILES + max((base + k) // rows for k in ks))
                big = xp_ref[pl.ds(r, n_rows), cs] if s == 0 else xs_ref[s - 1, pl.ds(r, n_rows), cs]
                for k in ks:
                    a = (base + k) // rows
                    for sub in range(_CC_ROW_TILES):
                        term = big[(sub + a) * rows:(sub + a + 1) * rows] * taps[k]
                        accs[sub] = term if accs[sub] is None else accs[sub] + term
            for sub in range(_CC_ROW_TILES):
                acc_ref[pl.ds(r + sub * rows, rows), cs] = accs[sub]
            return carry

        lax.fori_loop(0, tt // (rows * _CC_ROW_TILES), body, 0)
    xp_ref[0:_CC_HALO, :] = xp_ref[tt:tt + _CC_HALO, :]
    y = _layer_norm(acc_ref[...] + bdw_ref[...], g_ref[...], b_ref[...])
    o_ref[...] = _silu(y)


def _cconv_seq(h, buf, w_dw, b_dw, ln_g, ln_b, *, tt=512):
    b, t, d = h.shape
    width = w_dw.shape[0]
    tt = _row_tile(t, tt)
    assert tt >= _CC_HALO and width - 1 <= _CC_HALO
    row = lambda a: a.reshape(1, d).astype(F32)
    const = lambda a: pl.BlockSpec(a.shape, lambda i, j: (0, 0))
    args = (w_dw.astype(F32), row(b_dw), row(ln_g), row(ln_b))
    return pl.pallas_call(
        functools.partial(_cconv_seq_kernel, tt=tt, width=width),
        grid=(b, t // tt),
        in_specs=[pl.BlockSpec((None, tt, d), lambda i, j: (i, j, 0)),
                  pl.BlockSpec((None, width - 1, d), lambda i, j: (i, 0, 0))] + [const(a) for a in args],
        out_specs=pl.BlockSpec((None, tt, d), lambda i, j: (i, j, 0)),
        out_shape=jax.ShapeDtypeStruct((b, t, d), F32),
        scratch_shapes=[pltpu.VMEM((_CC_HALO + tt, d), F32),
                        pltpu.VMEM((V7X_SUBLANES - 1, _CC_HALO + tt - V7X_SUBLANES, d), F32),
                        pltpu.VMEM((tt, d), F32)],
        compiler_params=_cparams("parallel", "arbitrary"),
        name="cconv_seq",
    )(h, buf.astype(F32), *args)


def _cconv_step_kernel(h_ref, buf_ref, w_ref, bdw_ref, g_ref, b_ref, o_ref, *, width):
    hist = width - 1
    acc = jnp.sum(buf_ref[...] * w_ref[0:hist, :][None], axis=1) + h_ref[...] * w_ref[hist:width, :]
    o_ref[...] = _silu(_layer_norm(acc + bdw_ref[...], g_ref[...], b_ref[...]))


def _cconv_step(h, buf, w_dw, b_dw, ln_g, ln_b, *, bb=8):
    b, d = h.shape
    width = w_dw.shape[0]
    bb = _row_tile(b, bb)
    row = lambda a: a.reshape(1, d).astype(F32)
    const = lambda a: pl.BlockSpec(a.shape, lambda i: (0, 0))
    args = (w_dw.astype(F32), row(b_dw), row(ln_g), row(ln_b))
    return pl.pallas_call(
        functools.partial(_cconv_step_kernel, width=width),
        grid=(b // bb,),
        in_specs=[pl.BlockSpec((bb, d), lambda i: (i, 0)),
                  pl.BlockSpec((bb, width - 1, d), lambda i: (i, 0, 0))] + [const(a) for a in args],
        out_specs=pl.BlockSpec((bb, d), lambda i: (i, 0)),
        out_shape=jax.ShapeDtypeStruct((b, d), F32),
        compiler_params=_cparams("parallel"),
        name="cconv_step",
    )(h, buf.astype(F32), *args)


def _conv_mixer(x, buf, p, i):
    b, t, d = x.shape
    h = _linear(x.reshape(b * t, d), p['cc_w_pw1'][i], p['cc_b_pw1'][i], glu=True)
    args = (p['cc_w_dw'][i], p['cc_b_dw'][i], p['cc_ln_g'][i], p['cc_ln_b'][i])
    if t == 1:
        hc = _cconv_step(h, buf, *args)
    else:
        hc = _cconv_seq(h.reshape(b, t, d), buf, *args).reshape(b * t, d)
    new_buf = jnp.concatenate([buf.astype(F32), h.reshape(b, t, d)], axis=1)[:, t:]
    return hc, new_buf


def _conv_layer(x, buf, p, i, ln_g, ln_b, *, alpha):
    b, t, d = x.shape
    hc, new_buf = _conv_mixer(x, buf, p, i)
    x_new = _linear_postnorm(hc, p['cc_w_pw2'][i], p['cc_b_pw2'][i], x.reshape(b * t, d), ln_g, ln_b, alpha=alpha)
    return x_new.reshape(b, t, d), new_buf


def _mem_attn_seq_kernel(x_ref, wq_ref, k_ref, v_ref, wo_ref, g_ref, b_ref, o_ref, *, alpha, heads):
    o_ref[...] = _mem_attn_math(x_ref[...], wq_ref, k_ref, v_ref, wo_ref, g_ref, b_ref, alpha=alpha, heads=heads)


def _mem_attn_math(x, wq_ref, k_ref, v_ref, wo_ref, g_ref, b_ref, *, alpha, heads):
    d = x.shape[-1]
    hd = d // heads
    q = _bdot(x, wq_ref[...]) * (hd ** -0.5)
    k = k_ref[...].astype(BF16)
    v = v_ref[...].astype(BF16)
    hsl = [slice(h * hd, (h + 1) * hd) for h in range(heads)]
    ss = [_bdot_nt(q[:, hs], k[:, hs]) for hs in hsl]
    es = [jnp.exp(s - jnp.max(s, axis=-1, keepdims=True)) for s in ss]
    ps = [e / jnp.sum(e, axis=-1, keepdims=True) for e in es]
    o = jnp.concatenate([_bdot(a, v[:, hs]) for a, hs in zip(ps, hsl)], axis=1)
    hres = _bdot(o, wo_ref[...])
    return _layer_norm(alpha * x + hres, g_ref[...], b_ref[...])


def _proj_mem_router_kernel(*refs, n_proj, alpha, heads, groups, per_group):
    proj = refs[:2 * n_proj]
    (bias_ref, xres_ref, g0_ref, b0_ref, wq_ref, k_ref, v_ref, wo_ref, g1_ref, b1_ref, rw_ref, rb_ref,
     x_out_ref, ids_ref, wts_ref) = refs[2 * n_proj:]
    h = bias_ref[...]
    for a in range(n_proj):
        h = h + _bdot(proj[2 * a][...], proj[2 * a + 1][...])
    x = _layer_norm(alpha * xres_ref[...] + h, g0_ref[...], b0_ref[...])
    x = _mem_attn_math(x, wq_ref, k_ref, v_ref, wo_ref, g1_ref, b1_ref, alpha=alpha, heads=heads)
    x_out_ref[...] = x
    ids_ref[...], wts_ref[...] = _router_math(x, rw_ref[...], rb_ref[...], groups=groups, per_group=per_group)


def _proj_mem_router(proj, bias, x_res, ln0, mk, mv, layer, wq, wo, ln1, router_w, router_b, *, alpha, tq=512):
    b, t, d = x_res.shape
    m = mk.shape[2]
    nt = t // tq
    groups = MOE_GROUPS
    row = lambda a: a.reshape(1, d).astype(F32)
    const = lambda a: pl.BlockSpec(a.shape, lambda i, j: (0,) * a.ndim)
    xblk = pl.BlockSpec((None, tq, d), lambda i, j: (i, j, 0))
    rowblk = pl.BlockSpec((1, d), lambda i, j: (0, 0))
    proj_args, proj_specs = [], []
    for arr, spec, w in proj:
        wb = w.astype(BF16)
        proj_args += [arr, wb]
        proj_specs += [spec, const(wb)]
    if bias is None:
        bias = jnp.zeros((d,), F32)
    wqb, wob = wq.astype(BF16), wo.astype(BF16)
    lane_blk = pl.BlockSpec((tq, V7X_LANES), lambda i, j: (i * nt + j, 0))
    x_out, ids, wts = pl.pallas_call(
        functools.partial(_proj_mem_router_kernel, n_proj=len(proj), alpha=alpha, heads=MEM_HEADS,
                          groups=groups, per_group=MOE_EXPERTS_PER_GROUP),
        grid=(b, nt),
        in_specs=proj_specs + [rowblk, xblk, rowblk, rowblk, const(wqb),
                               pl.BlockSpec((None, None, m, d), lambda i, j: (layer, i, 0, 0)),
                               pl.BlockSpec((None, None, m, d), lambda i, j: (layer, i, 0, 0)),
                               const(wob), rowblk, rowblk, const(router_w), const(router_b)],
        out_specs=[xblk, lane_blk, lane_blk],
        out_shape=[jax.ShapeDtypeStruct((b, t, d), F32),
                   jax.ShapeDtypeStruct((b * t, V7X_LANES), jnp.int32),
                   jax.ShapeDtypeStruct((b * t, V7X_LANES), F32)],
        compiler_params=_cparams("parallel", "parallel"),
        name="proj_mem_router",
    )(*proj_args, row(bias), x_res, row(ln0[0]), row(ln0[1]), wqb, mk, mv, wob, row(ln1[0]), row(ln1[1]),
      router_w, router_b)
    return x_out, ids, wts


def _mem_attn_seq(x, mk, mv, layer, wq, wo, ln_g, ln_b, *, alpha, tq=512):
    b, t, d = x.shape
    m = mk.shape[2]
    tq = _row_tile(t, tq)
    row = lambda a: a.reshape(1, d).astype(F32)
    const = lambda a: pl.BlockSpec(a.shape, lambda i, j: (0, 0))
    wqb, wob = wq.astype(BF16), wo.astype(BF16)
    return pl.pallas_call(
        functools.partial(_mem_attn_seq_kernel, alpha=alpha, heads=MEM_HEADS),
        grid=(b, t // tq),
        in_specs=[pl.BlockSpec((None, tq, d), lambda i, j: (i, j, 0)),
                  const(wqb),
                  pl.BlockSpec((None, None, m, d), lambda i, j: (layer, i, 0, 0)),
                  pl.BlockSpec((None, None, m, d), lambda i, j: (layer, i, 0, 0)),
                  const(wob), pl.BlockSpec((1, d), lambda i, j: (0, 0)),
                  pl.BlockSpec((1, d), lambda i, j: (0, 0))],
        out_specs=pl.BlockSpec((None, tq, d), lambda i, j: (i, j, 0)),
        out_shape=jax.ShapeDtypeStruct((b, t, d), F32),
        compiler_params=_cparams("parallel", "parallel"),
        name="mem_attn_seq",
    )(x, wqb, mk, mv, wob, row(ln_g), row(ln_b))


def _mem_attn_step_kernel(q_ref, k_ref, v_ref, o_ref, *, heads, bb):
    i = pl.program_id(0)
    hd = q_ref.shape[-1]
    for r in range(bb):
        q = q_ref[i * bb + r] * (hd ** -0.5)
        s = jnp.sum(k_ref[r] * q[None], axis=-1, keepdims=True)
        s = s - jnp.max(s, axis=0, keepdims=True)
        e = jnp.exp(s)
        o_ref[i * bb + r] = jnp.sum(e * v_ref[r], axis=0) / jnp.sum(e, axis=0)


def _mem_attn_step(q, mk_all, mv_all, layer, *, bb=4):
    b, heads, hd = q.shape
    m = mk_all.shape[2]
    bb = _row_tile(b, bb)
    return pl.pallas_call(
        functools.partial(_mem_attn_step_kernel, heads=heads, bb=bb),
        grid=(b // bb,),
        in_specs=[pl.BlockSpec((b, heads, hd), lambda i: (0, 0, 0)),
                  pl.BlockSpec((None, bb, m, heads, hd), lambda i: (layer, i, 0, 0, 0)),
                  pl.BlockSpec((None, bb, m, heads, hd), lambda i: (layer, i, 0, 0, 0))],
        out_specs=pl.BlockSpec((b, heads, hd), lambda i: (0, 0, 0)),
        out_shape=jax.ShapeDtypeStruct((b, heads, hd), F32),
        compiler_params=_cparams("arbitrary"),
        name="mem_attn_step",
    )(q, mk_all, mv_all)


def _mem_layer(x, mk_all, mv_all, layer, wq, wo, ln_g, ln_b, *, alpha):
    b, t, d = x.shape
    n_layers, _, m, heads, hd = mk_all.shape
    if t == 1:
        x2 = x.reshape(b, d)
        q = _linear(x2, wq).reshape(b, heads, hd)
        o = _mem_attn_step(q, mk_all, mv_all, layer).reshape(b, d)
        return _linear_postnorm(o, wo, None, x2, ln_g, ln_b, alpha=alpha).reshape(b, t, d)
    mk_all = mk_all.reshape(n_layers, b, m, d)
    mv_all = mv_all.reshape(n_layers, b, m, d)
    return _mem_attn_seq(x, mk_all, mv_all, layer, wq, wo, ln_g, ln_b, alpha=alpha)


_NEG = -1e30


def _router_kernel(x_ref, w_ref, b_ref, ids_ref, wts_ref, *, groups, per_group):
    ids_ref[...], wts_ref[...] = _router_math(x_ref[...], w_ref[...], b_ref[...], groups=groups,
                                              per_group=per_group)


def _router_math(x, w, bias, *, groups, per_group):
    xh = x.astype(BF16)
    xl = (x - xh.astype(F32)).astype(BF16)
    wh = w.astype(BF16)
    wl = (w - wh.astype(F32)).astype(BF16)
    logits = (jnp.dot(xh, wh, preferred_element_type=F32) + jnp.dot(xh, wl, preferred_element_type=F32)
              + jnp.dot(xl, wh, preferred_element_type=F32)) + bias
    lane = lax.broadcasted_iota(jnp.int32, logits.shape, 1)
    n_exp = groups * per_group
    is_g = lane < groups
    gl = jnp.where(is_g, logits, _NEG)
    gmax = jnp.max(gl, axis=-1, keepdims=True)
    gsum = jnp.sum(jnp.where(is_g, jnp.exp(gl - gmax), 0.0), axis=-1, keepdims=True)
    gw = 1.0 / gsum
    gi = jnp.min(jnp.where(gl == gmax, lane, V7X_LANES), axis=-1, keepdims=True)
    lane_grp = (lane - groups) // per_group
    sel = (lane >= groups) & (lane < groups + n_exp) & (lane_grp == gi)
    el = jnp.where(sel, logits, _NEG)
    emax = jnp.max(el, axis=-1, keepdims=True)
    ee = jnp.where(sel, jnp.exp(el - emax), 0.0)
    ep = jnp.where(sel, ee / jnp.sum(ee, axis=-1, keepdims=True), -1.0)
    p1 = jnp.max(ep, axis=-1, keepdims=True)
    i1 = jnp.min(jnp.where(ep == p1, lane, V7X_LANES), axis=-1, keepdims=True)
    ep2 = jnp.where(lane == i1, -1.0, ep)
    p2 = jnp.max(ep2, axis=-1, keepdims=True)
    i2 = jnp.min(jnp.where(ep2 == p2, lane, V7X_LANES), axis=-1, keepdims=True)
    denom = p1 + p2
    ids = jnp.where(lane == 0, i1 - groups, jnp.where(lane == 1, i2 - groups, 0))
    wts = jnp.where(lane == 0, gw * p1 / denom, jnp.where(lane == 1, gw * p2 / denom, 0.0))
    return ids, wts


def _router_params(p, layer):
    w_group, w_expert = p['moe_w_group'][layer], p['moe_w_expert'][layer]
    pad = V7X_LANES - w_group.shape[1] - w_expert.shape[1]
    w = jnp.pad(jnp.concatenate([w_group, w_expert], axis=1).astype(F32), ((0, 0), (0, pad)))
    b = jnp.pad(jnp.concatenate([p['moe_b_group'][layer], p['moe_b_expert'][layer]]).astype(F32), (0, pad))
    return w, b.reshape(1, V7X_LANES)


def _router(x, w, b, *, tm=512):
    n, d = x.shape
    tm = _row_tile(n, tm)
    return pl.pallas_call(
        functools.partial(_router_kernel, groups=MOE_GROUPS, per_group=MOE_EXPERTS_PER_GROUP),
        grid=(n // tm,),
        in_specs=[pl.BlockSpec((tm, d), lambda i: (i, 0)),
                  pl.BlockSpec((d, V7X_LANES), lambda i: (0, 0)),
                  pl.BlockSpec((1, V7X_LANES), lambda i: (0, 0))],
        out_specs=[pl.BlockSpec((tm, V7X_LANES), lambda i: (i, 0)),
                   pl.BlockSpec((tm, V7X_LANES), lambda i: (i, 0))],
        out_shape=[jax.ShapeDtypeStruct((n, V7X_LANES), jnp.int32),
                   jax.ShapeDtypeStruct((n, V7X_LANES), F32)],
        compiler_params=_cparams("parallel"),
        name="moe_router",
    )(x, w, b)


def _expert_ffn_kernel(te_ref, nv_ref, x_ref, wg_ref, wu_ref, wd_ref, o_ref):
    i = pl.program_id(0)

    @pl.when(i < nv_ref[0])
    def _():
        x = x_ref[...]
        hid = _silu(_bdot(x, wg_ref[...])) * _bdot(x, wu_ref[...])
        o_ref[...] = _bdot(hid, wd_ref[...])

    @pl.when(i >= nv_ref[0])
    def _():
        o_ref[...] = jnp.zeros_like(o_ref)


def _expert_ffn(x_sorted, tile_expert, n_valid, w_gate, w_up, w_down, layer, *, tm):
    r, d = x_sorted.shape
    f = w_gate.shape[-1]
    return pl.pallas_call(
        _expert_ffn_kernel,
        grid_spec=pltpu.PrefetchScalarGridSpec(
            num_scalar_prefetch=2,
            grid=(r // tm,),
            in_specs=[pl.BlockSpec((tm, d), lambda i, te, nv: (jnp.minimum(i, nv[0] - 1), 0)),
                      pl.BlockSpec((None, None, d, f), lambda i, te, nv: (layer, te[i], 0, 0)),
                      pl.BlockSpec((None, None, d, f), lambda i, te, nv: (layer, te[i], 0, 0)),
                      pl.BlockSpec((None, None, f, d), lambda i, te, nv: (layer, te[i], 0, 0))],
            out_specs=pl.BlockSpec((tm, d), lambda i, te, nv: (i, 0)),
        ),
        out_shape=jax.ShapeDtypeStruct((r, d), F32),
        compiler_params=_cparams("arbitrary"),
        name="moe_expert_ffn",
    )(tile_expert, n_valid, x_sorted, w_gate, w_up, w_down)


def _rank_kernel(ids_ref, cnt_in_ref, rank_ref, cnt_ref, carry_ref):
    i = pl.program_id(0)
    tm = ids_ref.shape[0]

    @pl.when(i == 0)
    def _():
        carry_ref[...] = cnt_in_ref[...].astype(F32)

    ids = ids_ref[...]
    lane = lax.broadcasted_iota(jnp.int32, ids.shape, 1)
    oh0 = jnp.where(lane == ids[:, 0:1], 1.0, 0.0)
    oh1 = jnp.where(lane == ids[:, 1:2], 1.0, 0.0)
    both = oh0 + oh1
    ii = lax.broadcasted_iota(jnp.int32, (tm, tm), 0)
    jj = lax.broadcasted_iota(jnp.int32, (tm, tm), 1)
    earlier = jnp.where(ii > jj, 1.0, 0.0).astype(BF16)
    prefix = jnp.dot(earlier, both.astype(BF16), preferred_element_type=F32) + carry_ref[...]
    r0 = jnp.sum(oh0 * prefix, axis=-1, keepdims=True)
    r1 = jnp.sum(oh1 * prefix, axis=-1, keepdims=True)
    rank_ref[...] = jnp.where(lane == 0, r0, jnp.where(lane == 1, r1, 0.0)).astype(jnp.int32)
    total = carry_ref[...] + jnp.sum(both, axis=0, keepdims=True)
    carry_ref[...] = total
    cnt_ref[...] = total.astype(jnp.int32)


def _expert_ranks(ids, counts_in, *, tm=512):
    n = ids.shape[0]
    tm = _row_tile(n, tm)
    return pl.pallas_call(
        _rank_kernel,
        grid=(n // tm,),
        in_specs=[pl.BlockSpec((tm, V7X_LANES), lambda i: (i, 0)),
                  pl.BlockSpec((1, V7X_LANES), lambda i: (0, 0))],
        out_specs=[pl.BlockSpec((tm, V7X_LANES), lambda i: (i, 0)),
                   pl.BlockSpec((1, V7X_LANES), lambda i: (0, 0))],
        out_shape=[jax.ShapeDtypeStruct((n, V7X_LANES), jnp.int32),
                   jax.ShapeDtypeStruct((1, V7X_LANES), jnp.int32)],
        scratch_shapes=[pltpu.VMEM((1, V7X_LANES), F32)],
        compiler_params=_cparams("arbitrary"),
        name="moe_rank",
    )(ids, counts_in)


def _for_each_row(tm, fn):
    def body(i, carry):
        base = pl.multiple_of(i * V7X_SUBLANES, V7X_SUBLANES)
        for sub in range(V7X_SUBLANES):
            fn(base, sub)
        return carry

    lax.fori_loop(0, tm // V7X_SUBLANES, body, 0)


def _dispatch_kernel(fill_ref, slot_ref, x_ref, *rest, tile, fill_tiles):
    tm = x_ref.shape[0]
    if fill_tiles:
        xs_ref, sem, zbuf, zsem = rest

        @pl.when(pl.program_id(0) == 0)
        def _():
            zbuf[...] = jnp.zeros_like(zbuf)
            n_exp = fill_ref.shape[1]
            n_tiles = xs_ref.shape[0] // tile
            total_end = fill_ref[1, n_exp - 1]
            starts, conds = [], []
            for e in range(n_exp):
                for j in range(fill_tiles):
                    starts.append(fill_ref[0, e] + j * tile)
                    conds.append(starts[-1] < fill_ref[1, e])
            for t in range(n_tiles):
                starts.append(t * tile)
                conds.append(t * tile >= total_end)

            def fill(start):
                return pltpu.make_async_copy(zbuf, xs_ref.at[pl.ds(pl.multiple_of(start, tile), tile), :], zsem)

            for start, c in zip(starts, conds):
                pl.when(c)(lambda start=start: fill(start).start())
            for start, c in zip(starts, conds):
                pl.when(c)(lambda start=start: fill(start).wait())
    else:
        _, xs_ref, sem = rest

    def issue(base, sub):
        for k in range(MOE_TOP_K):
            s = slot_ref[0, 0, MOE_TOP_K * (base + sub) + k]
            pltpu.make_async_copy(x_ref.at[pl.ds(base + sub, 1), :], xs_ref.at[pl.ds(s, 1), :],
                                  sem).start(priority=k)

    _for_each_row(tm, issue)
    for k in range(MOE_TOP_K):
        pltpu.make_async_copy(x_ref, xs_ref.at[pl.ds(0, tm), :], sem).wait()


def _dispatch(x, slot, fill, *, rows, tile, fill_tiles, xs=None, tm):
    n, d = x.shape
    slot3 = slot.reshape(n // tm, 1, MOE_TOP_K * tm)
    in_specs = [pl.BlockSpec(memory_space=pltpu.SMEM),
                pl.BlockSpec((1, 1, MOE_TOP_K * tm), lambda i: (i, 0, 0), memory_space=pltpu.SMEM),
                pl.BlockSpec((tm, d), lambda i: (i, 0))]
    scratch = [pltpu.SemaphoreType.DMA(())]
    fresh = xs is None
    if fresh:
        args, aliases = (fill, slot3, x), {}
        scratch += [pltpu.VMEM((tile, d), F32), pltpu.SemaphoreType.DMA(())]
    else:
        args, aliases = (fill, slot3, x, xs), {3: 0}
        in_specs.append(pl.BlockSpec(memory_space=pl.ANY))
    return pl.pallas_call(
        functools.partial(_dispatch_kernel, tile=tile, fill_tiles=fill_tiles if fresh else 0),
        grid=(n // tm,),
        in_specs=in_specs,
        out_specs=pl.BlockSpec(memory_space=pl.ANY),
        out_shape=jax.ShapeDtypeStruct((rows, d), F32),
        scratch_shapes=scratch,
        input_output_aliases=aliases,
        compiler_params=_cparams("arbitrary"),
        name="moe_dispatch",
    )(*args)


def _combine_kernel(slot_ref, slot_next_ref, wts_ref, x_ref, g_ref, b_ref, y_hbm, o_ref, ybuf, sem, *, alpha):
    i = pl.program_id(0)
    n = pl.num_programs(0)
    tm = x_ref.shape[0]
    cur = i % 2

    def issue(idx_ref, buf):
        def row(base, sub):
            for k in range(MOE_TOP_K):
                s = idx_ref[0, 0, MOE_TOP_K * (base + sub) + k]
                pltpu.make_async_copy(y_hbm.at[pl.ds(s, 1), :], ybuf.at[buf, k, pl.ds(base + sub, 1), :],
                                      sem.at[buf]).start(priority=k)

        _for_each_row(tm, row)

    @pl.when(i == 0)
    def _():
        issue(slot_ref, 0)

    @pl.when(i + 1 < n)
    def _():
        issue(slot_next_ref, 1 - cur)

    for k in range(MOE_TOP_K):
        pltpu.make_async_copy(y_hbm.at[pl.ds(0, tm), :], ybuf.at[cur, k], sem.at[cur]).wait()
    wts = wts_ref[...]
    y = wts[:, 0:1] * ybuf[cur, 0] + wts[:, 1:2] * ybuf[cur, 1]
    o_ref[...] = _layer_norm(alpha * x_ref[...] + y, g_ref[...], b_ref[...])


def _combine_postnorm(y_sorted, slot, wts, x, g, b, *, alpha, tm):
    n, d = x.shape
    nt = n // tm
    slot3 = slot.reshape(nt, 1, MOE_TOP_K * tm)
    row = lambda a: a.reshape(1, d).astype(F32)
    smem = lambda f: pl.BlockSpec((1, 1, MOE_TOP_K * tm), f, memory_space=pltpu.SMEM)
    return pl.pallas_call(
        functools.partial(_combine_kernel, alpha=alpha),
        grid=(nt,),
        in_specs=[smem(lambda i: (i, 0, 0)),
                  smem(lambda i: (jnp.minimum(i + 1, nt - 1), 0, 0)),
                  pl.BlockSpec((tm, V7X_LANES), lambda i: (i, 0)),
                  pl.BlockSpec((tm, d), lambda i: (i, 0)),
                  pl.BlockSpec((1, d), lambda i: (0, 0)),
                  pl.BlockSpec((1, d), lambda i: (0, 0)),
                  pl.BlockSpec(memory_space=pl.ANY)],
        out_specs=pl.BlockSpec((tm, d), lambda i: (i, 0)),
        out_shape=jax.ShapeDtypeStruct((n, d), F32),
        scratch_shapes=[pltpu.VMEM((2, MOE_TOP_K, tm, d), F32), pltpu.SemaphoreType.DMA((2,))],
        compiler_params=_cparams("arbitrary"),
        name="moe_combine",
    )(slot3, slot3, wts, x, row(g), row(b), y_sorted)


def _moe_layer(xs, p, layer, ln_g, ln_b, *, alpha, routed=None):
    w_gate, w_up, w_down = p['moe_w_gate'], p['moe_w_up'], p['moe_w_down']
    n_exp = w_gate.shape[1]
    d = xs[0].shape[-1]
    x2s = [x.reshape(-1, d) for x in xs]
    n_total = sum(x2.shape[0] for x2 in x2s)
    tm = 512 if n_total >= 8192 else V7X_BF16_SUBLANES
    if routed is None:
        routed = [_router(x2, *_router_params(p, layer)) for x2 in x2s]
    counts = jnp.zeros((1, V7X_LANES), jnp.int32)
    ranks = []
    for gi, (ids, _) in enumerate(routed):
        rank, counts = _expert_ranks(ids, counts)
        ranks.append(rank)
        if gi == 0:
            counts_first = counts[0, :n_exp]
    counts = counts[0, :n_exp]
    padded = ((counts + tm - 1) // tm) * tm
    ends = jnp.cumsum(padded)
    starts = ends - padded
    n_tiles = (MOE_TOP_K * n_total + n_exp * (tm - 1)) // tm
    n_valid = (ends[-1] // tm).astype(jnp.int32)
    tile_start = jnp.minimum(jnp.arange(n_tiles, dtype=jnp.int32), n_valid - 1) * tm
    tile_expert = jnp.minimum(jnp.sum((ends[None, :] <= tile_start[:, None]).astype(jnp.int32), axis=1), n_exp - 1)
    fill = jnp.stack([starts + (counts_first // tm) * tm, ends]).astype(jnp.int32)
    later_rows = MOE_TOP_K * (n_total - x2s[0].shape[0])
    fill_tiles = 1 + (later_rows + tm - 1) // tm
    x_sorted = None
    slots = []
    for x2, (ids, _), rank in zip(x2s, routed, ranks):
        choice = ids[:, :MOE_TOP_K]
        onehot = choice[:, :, None] == jnp.arange(n_exp, dtype=jnp.int32)[None, None, :]
        slot = jnp.sum(jnp.where(onehot, starts[None, None, :], 0), axis=-1) + rank[:, :MOE_TOP_K]
        x_sorted = _dispatch(x2, slot, fill, rows=n_tiles * tm, tile=tm, fill_tiles=fill_tiles, xs=x_sorted,
                             tm=_row_tile(x2.shape[0], 512))
        slots.append(slot)
    y_sorted = _expert_ffn(x_sorted, tile_expert, n_valid.reshape(1), w_gate, w_up, w_down, layer, tm=tm)
    return [_combine_postnorm(y_sorted, slot, wts, x2, ln_g, ln_b, alpha=alpha,
                              tm=_row_tile(x2.shape[0], 512)).reshape(x.shape)
            for x, x2, (_, wts), slot in zip(xs, x2s, routed, slots)]


def _trunks(groups, p):
    depth = p['ln_g'].shape[0]
    alpha = (2.0 * depth) ** 0.25
    xs = [grp[0] for grp in groups]
    outs = [([], [], [], [], []) for _ in groups]
    for layer in range(depth):
        i = layer // 2
        g, bta = p['ln_g'][layer], p['ln_b'][layer]
        rw, rb = _router_params(p, layer)
        wq, wo = p['mem_wq'][layer], p['mem_wo'][layer]
        routed = []
        for gi, (_, dn_s, dn_conv, ssm_re, ssm_im, cconv, mem_k, mem_v) in enumerate(groups):
            out_s, out_conv, out_re, out_im, out_cc = outs[gi]
            x = xs[gi]
            b, t, d = x.shape
            if t == 1:
                if layer % 2 == 0:
                    x, s_new, conv_new, re_new, im_new = _ab_layer(x, dn_s[i], dn_conv[i], ssm_re[i], ssm_im[i],
                                                                   p, i, g[0], bta[0], alpha=alpha)
                else:
                    x, cc_new = _conv_layer(x, cconv[i], p, i, g[0], bta[0], alpha=alpha)
                x = _mem_layer(x, mem_k, mem_v, layer, wq, wo, g[1], bta[1], alpha=alpha)
                routed.append(_router(x.reshape(b * t, d), rw, rb))
            else:
                tq = _row_tile(t, 1024)
                if layer % 2 == 0:
                    o, y_tm, (s_new, conv_new, re_new, im_new) = _ab_mixers(x, dn_s[i], dn_conv[i], ssm_re[i],
                                                                            ssm_im[i], p, i)
                    n_o, n_y = o.shape[-1], y_tm.shape[1] // b
                    w_out = p['ab_w_out'][i]
                    proj = [(o, pl.BlockSpec((None, tq, n_o), lambda bi, j: (bi, j, 0)), w_out[:n_o]),
                            (y_tm, pl.BlockSpec((tq, n_y), lambda bi, j: (j, bi)), w_out[n_o:])]
                    bias = None
                else:
                    hc, cc_new = _conv_mixer(x, cconv[i], p, i)
                    proj = [(hc.reshape(b, t, d), pl.BlockSpec((None, tq, d), lambda bi, j: (bi, j, 0)),
                             p['cc_w_pw2'][i])]
                    bias = p['cc_b_pw2'][i]
                n_mem = mem_k.shape[2]
                x, ids, wts = _proj_mem_router(
                    proj, bias, x, (g[0], bta[0]), mem_k.reshape(depth, b, n_mem, d),
                    mem_v.reshape(depth, b, n_mem, d), layer, wq, wo, (g[1], bta[1]), rw, rb, alpha=alpha, tq=tq)
                routed.append((ids, wts))
            if layer % 2 == 0:
                out_s.append(s_new)
                out_conv.append(conv_new)
                out_re.append(re_new)
                out_im.append(im_new)
            else:
                out_cc.append(cc_new)
            xs[gi] = x
        xs = _moe_layer(xs, p, layer, g[2], bta[2], alpha=alpha, routed=routed)
    return [(x,) + tuple(jnp.stack(o) for o in out) for x, out in zip(xs, outs)]


def kernel(x_prompt, x_sample, state_dn_s, state_dn_conv, state_ssm_re, state_ssm_im, state_cconv,
           cache_mem_k, cache_mem_v, mem_prompt, ab_w_in, dn_conv_w, dn_a_log, dn_dt_bias, dn_norm_g,
           ssm_lambda_re, ssm_lambda_im, ssm_log_dt, ssm_b_re, ssm_b_im, ssm_c_re, ssm_c_im, ssm_d,
           ssm_w_glu, ssm_b_glu, ab_w_out, cc_w_pw1, cc_b_pw1, cc_w_dw, cc_b_dw, cc_ln_g, cc_ln_b,
           cc_w_pw2, cc_b_pw2, mem_wq, mem_wk, mem_wv, mem_wo, ln_g, ln_b, moe_w_group, moe_b_group,
           moe_w_expert, moe_b_expert, moe_w_gate, moe_w_up, moe_w_down):
    p = dict(ab_w_in=ab_w_in, dn_conv_w=dn_conv_w, dn_a_log=dn_a_log, dn_dt_bias=dn_dt_bias,
             dn_norm_g=dn_norm_g, ssm_lambda_re=ssm_lambda_re, ssm_lambda_im=ssm_lambda_im,
             ssm_log_dt=ssm_log_dt, ssm_b_re=ssm_b_re, ssm_b_im=ssm_b_im, ssm_c_re=ssm_c_re,
             ssm_c_im=ssm_c_im, ssm_d=ssm_d, ssm_w_glu=ssm_w_glu, ssm_b_glu=ssm_b_glu, ab_w_out=ab_w_out,
             cc_w_pw1=cc_w_pw1, cc_b_pw1=cc_b_pw1, cc_w_dw=cc_w_dw, cc_b_dw=cc_b_dw, cc_ln_g=cc_ln_g,
             cc_ln_b=cc_ln_b, cc_w_pw2=cc_w_pw2, cc_b_pw2=cc_b_pw2, mem_wq=mem_wq, mem_wo=mem_wo,
             ln_g=ln_g, ln_b=ln_b, moe_w_group=moe_w_group, moe_b_group=moe_b_group,
             moe_w_expert=moe_w_expert, moe_b_expert=moe_b_expert, moe_w_gate=moe_w_gate,
             moe_w_up=moe_w_up, moe_w_down=moe_w_down)
    depth = ln_g.shape[0]
    n_ab = state_dn_s.shape[0]
    n_cc = state_cconv.shape[0]
    bsz, _, d = x_prompt.shape
    n_mem = mem_prompt.shape[1]
    hd = d // MEM_HEADS
    z_dn_s = jnp.zeros((n_ab, bsz) + state_dn_s.shape[2:], F32)
    z_dn_conv = jnp.zeros((n_ab, bsz) + state_dn_conv.shape[2:], F32)
    z_ssm = jnp.zeros((n_ab, bsz) + state_ssm_re.shape[2:], F32)
    z_cconv = jnp.zeros((n_cc, bsz) + state_cconv.shape[2:], F32)
    mem2 = mem_prompt.reshape(bsz * n_mem, d)
    p_mem_k = jnp.stack([_linear(mem2, mem_wk[l]) for l in range(depth)]).reshape(depth, bsz, n_mem, MEM_HEADS, hd)
    p_mem_v = jnp.stack([_linear(mem2, mem_wv[l]) for l in range(depth)]).reshape(depth, bsz, n_mem, MEM_HEADS, hd)
    (y_prompt, p_dn_s, p_dn_conv, p_ssm_re, p_ssm_im, p_cconv), \
        (y_sample, s_dn_s, s_dn_conv, s_ssm_re, s_ssm_im, s_cconv) = _trunks(
            [(x_prompt, z_dn_s, z_dn_conv, z_ssm, z_ssm, z_cconv, p_mem_k, p_mem_v),
             (x_sample, state_dn_s, state_dn_conv, state_ssm_re, state_ssm_im, state_cconv, cache_mem_k,
              cache_mem_v)], p)
    return (y_prompt, y_sample, p_dn_s, p_dn_conv, p_ssm_re, p_ssm_im, p_cconv, p_mem_k, p_mem_v,
            s_dn_s, s_dn_conv, s_ssm_re, s_ssm_im, s_cconv)
```

```python
import functools
import math

import jax
import jax.numpy as jnp
from jax import lax
from jax.experimental import pallas as pl
from jax.experimental.pallas import tpu as pltpu

F32 = jnp.float32
BF16 = jnp.bfloat16

DN_HEADS = 4
DN_DK = 128
DN_DV = 128
DN_CONV = 4
DN_CHUNK = 64
SSM_GROUP_CH = 16
MEM_HEADS = 4
MOE_GROUPS = 4
MOE_EXPERTS_PER_GROUP = 8
MOE_TOP_K = 2
LN_EPS = 1e-5
RMS_EPS = 1e-6

V7X_LANES = 128
V7X_SUBLANES = 8
V7X_BF16_SUBLANES = 16
V7X_VMEM_LIMIT_BYTES = 52 * 1024 * 1024


def _cparams(*sem):
    return pltpu.CompilerParams(dimension_semantics=sem, vmem_limit_bytes=V7X_VMEM_LIMIT_BYTES)


def _bdot(a, b):
    return jnp.dot(a.astype(BF16), b.astype(BF16), preferred_element_type=F32)


def _bdot_nt(a, b):
    return lax.dot_general(a.astype(BF16), b.astype(BF16), (((1,), (1,)), ((), ())),
                           preferred_element_type=F32)


def _split3(a):
    hi = a.astype(BF16)
    r1 = a - hi.astype(F32)
    mid = r1.astype(BF16)
    lo = (r1 - mid.astype(F32)).astype(BF16)
    return hi, mid, lo


def _sigmoid(x):
    return 1.0 / (1.0 + jnp.exp(-x))


def _silu(x):
    return x * _sigmoid(x)


def _softplus(x):
    return jnp.maximum(x, 0.0) + jnp.log(1.0 + jnp.exp(-jnp.abs(x)))


def _layer_norm(v, g, b):
    mu = jnp.mean(v, axis=-1, keepdims=True)
    d = v - mu
    var = jnp.mean(d * d, axis=-1, keepdims=True)
    return d * lax.rsqrt(var + LN_EPS) * g + b


def _row_tile(n, pref):
    t = min(n, pref)
    assert n % t == 0, (n, t)
    return t


def _linear_kernel(x_ref, w_ref, b_ref, o_ref, *, glu):
    y = _bdot(x_ref[...], w_ref[...]) + b_ref[...]
    if glu:
        n = y.shape[-1] // 2
        y = y[:, :n] * _sigmoid(y[:, n:])
    o_ref[...] = y.astype(o_ref.dtype)


def _linear(x, w, bias=None, *, glu=False, tm=1024, out_dtype=F32):
    m, k = x.shape
    n = w.shape[1]
    if bias is None:
        bias = jnp.zeros((n,), F32)
    tm = _row_tile(m, tm)
    n_out = n // 2 if glu else n
    return pl.pallas_call(
        functools.partial(_linear_kernel, glu=glu),
        grid=(m // tm,),
        in_specs=[pl.BlockSpec((tm, k), lambda i: (i, 0)),
                  pl.BlockSpec((k, n), lambda i: (0, 0)),
                  pl.BlockSpec((1, n), lambda i: (0, 0))],
        out_specs=pl.BlockSpec((tm, n_out), lambda i: (i, 0)),
        out_shape=jax.ShapeDtypeStruct((m, n_out), out_dtype),
        compiler_params=_cparams("parallel"),
        name="linear",
    )(x, w.astype(BF16), bias.reshape(1, n).astype(F32))


def _linear_postnorm_kernel(h_ref, w_ref, b_ref, x_ref, g_ref, beta_ref, o_ref, *, alpha):
    h = _bdot(h_ref[...], w_ref[...]) + b_ref[...]
    o_ref[...] = _layer_norm(alpha * x_ref[...] + h, g_ref[...], beta_ref[...])


def _linear_postnorm(h_in, w, bias, x_res, g, beta, *, alpha, tm=512):
    m, k = h_in.shape
    d = w.shape[1]
    if bias is None:
        bias = jnp.zeros((d,), F32)
    tm = _row_tile(m, tm)
    row = lambda a: a.reshape(1, d).astype(F32)
    return pl.pallas_call(
        functools.partial(_linear_postnorm_kernel, alpha=alpha),
        grid=(m // tm,),
        in_specs=[pl.BlockSpec((tm, k), lambda i: (i, 0)),
                  pl.BlockSpec((k, d), lambda i: (0, 0)),
                  pl.BlockSpec((1, d), lambda i: (0, 0)),
                  pl.BlockSpec((tm, d), lambda i: (i, 0)),
                  pl.BlockSpec((1, d), lambda i: (0, 0)),
                  pl.BlockSpec((1, d), lambda i: (0, 0))],
        out_specs=pl.BlockSpec((tm, d), lambda i: (i, 0)),
        out_shape=jax.ShapeDtypeStruct((m, d), F32),
        compiler_params=_cparams("parallel"),
        name="linear_postnorm",
    )(h_in, w.astype(BF16), row(bias), x_res, row(g), row(beta))


def _ab_in_kernel(x_ref, wqkvz_ref, wba_ref, wu_ref, qkv_ref, z_ref, ba_ref, u_ref, *, n_qkv):
    x = x_ref[...].astype(BF16)
    y = jnp.dot(x, wqkvz_ref[...], preferred_element_type=F32)
    qkv_ref[...] = y[:, :n_qkv]
    z_ref[...] = y[:, n_qkv:]
    ba_ref[...] = jnp.dot(x, wba_ref[...], preferred_element_type=F32)
    u_ref[...] = jnp.dot(x, wu_ref[...], preferred_element_type=F32)


def _ab_in_proj(x, w_in, *, tm=1024):
    bx, tx, d = x.shape
    n_key = DN_HEADS * DN_DK
    n_val = DN_HEADS * DN_DV
    n_qkv = 2 * n_key + n_val
    off_beta = n_qkv + n_val
    off_u = off_beta + 2 * DN_HEADS
    n_u = w_in.shape[1] - off_u
    w_qkvz = w_in[:, :off_beta].astype(BF16)
    w_ba = jnp.pad(w_in[:, off_beta:off_u], ((0, 0), (0, V7X_LANES - 2 * DN_HEADS))).astype(BF16)
    w_u = w_in[:, off_u:].astype(BF16)
    tm = _row_tile(tx, tm)
    full = lambda a: pl.BlockSpec(a.shape, lambda b, i: (0, 0))
    return pl.pallas_call(
        functools.partial(_ab_in_kernel, n_qkv=n_qkv),
        grid=(bx, tx // tm),
        in_specs=[pl.BlockSpec((None, tm, d), lambda b, i: (b, i, 0)),
                  full(w_qkvz), full(w_ba), full(w_u)],
        out_specs=[pl.BlockSpec((None, tm, n_qkv), lambda b, i: (b, i, 0)),
                   pl.BlockSpec((None, tm, n_val), lambda b, i: (b, i, 0)),
                   pl.BlockSpec((None, tm, V7X_LANES), lambda b, i: (b, i, 0)),
                   pl.BlockSpec((tm, n_u), lambda b, i: (i, b))],
        out_shape=[jax.ShapeDtypeStruct((bx, tx, n_qkv), F32),
                   jax.ShapeDtypeStruct((bx, tx, n_val), F32),
                   jax.ShapeDtypeStruct((bx, tx, V7X_LANES), F32),
                   jax.ShapeDtypeStruct((tx, bx * n_u), F32)],
        compiler_params=_cparams("parallel", "parallel"),
        name="ab_in_proj",
    )(x, w_qkvz, w_ba, w_u)


def _ab_out_kernel(o_ref, y_ref, wt_ref, wb_ref, x_ref, g_ref, beta_ref, out_ref, *, alpha):
    h = _bdot(o_ref[...], wt_ref[...]) + _bdot(y_ref[...], wb_ref[...])
    out_ref[...] = _layer_norm(alpha * x_ref[...] + h, g_ref[...], beta_ref[...])


def _ab_out_proj(o, y_tm, w_out, x, g, beta, *, alpha, tm=512):
    bx, tx, d = x.shape
    n_o = o.shape[-1]
    n_y = y_tm.shape[1] // bx
    tm = _row_tile(tx, tm)
    wt = w_out[:n_o].astype(BF16)
    wb = w_out[n_o:].astype(BF16)
    row = lambda a: a.reshape(1, d).astype(F32)
    full = lambda a: pl.BlockSpec(a.shape, lambda b, i: (0, 0))
    return pl.pallas_call(
        functools.partial(_ab_out_kernel, alpha=alpha),
        grid=(bx, tx // tm),
        in_specs=[pl.BlockSpec((None, tm, n_o), lambda b, i: (b, i, 0)),
                  pl.BlockSpec((tm, n_y), lambda b, i: (i, b)),
                  full(wt), full(wb),
                  pl.BlockSpec((None, tm, d), lambda b, i: (b, i, 0)),
                  pl.BlockSpec((1, d), lambda b, i: (0, 0)),
                  pl.BlockSpec((1, d), lambda b, i: (0, 0))],
        out_specs=pl.BlockSpec((None, tm, d), lambda b, i: (b, i, 0)),
        out_shape=jax.ShapeDtypeStruct((bx, tx, d), F32),
        compiler_params=_cparams("parallel", "parallel"),
        name="ab_out_proj",
    )(o, y_tm, wt, wb, x, row(g), row(beta))


_DN_HALO = V7X_SUBLANES


def _dn_seq_kernel(qkv_ref, ba_ref, z_ref, cbuf_ref, s0_ref, cw_ref, alog_ref, dtb_ref, ng_ref,
                   o_ref, s_out_ref, cbuf_out_ref, xp_ref, s_ref, *, c, bb):
    j = pl.program_id(1)
    nj = pl.num_programs(1)
    hist = DN_CONV - 1
    n_key = DN_HEADS * DN_DK
    units = [(bi, h) for bi in range(bb) for h in range(DN_HEADS)]
    each = lambda f: {u: f(u) for u in units}

    @pl.when(j == 0)
    def _():
        xp_ref[:, 0:_DN_HALO - hist, :] = jnp.zeros((bb, _DN_HALO - hist, xp_ref.shape[-1]), F32)
        xp_ref[:, _DN_HALO - hist:_DN_HALO, :] = cbuf_ref[...]
        s_ref[...] = s0_ref[...]

    ys, tails = [], []
    for bi in range(bb):
        xp_ref[bi, _DN_HALO:_DN_HALO + c, :] = qkv_ref[bi]
        y = xp_ref[bi, pl.ds(_DN_HALO - hist, c), :] * cw_ref[0:1, :]
        for tap in range(1, DN_CONV):
            y = y + xp_ref[bi, pl.ds(_DN_HALO - hist + tap, c), :] * cw_ref[tap:tap + 1, :]
        tail = xp_ref[bi, pl.ds(_DN_HALO + c - hist, hist), :]
        xp_ref[bi, _DN_HALO - hist:_DN_HALO, :] = tail
        tails.append(tail)
        ys.append(_silu(y))

    ii = lax.broadcasted_iota(jnp.int32, (c, c), 0)
    jj = lax.broadcasted_iota(jnp.int32, (c, c), 1)
    incl = ii >= jj
    strict = ii > jj
    tril = jnp.where(incl, 1.0, 0.0).astype(BF16)
    eye = jnp.where(ii == jj, 1.0, 0.0)
    n_double = int(math.log2(c))
    vsl = lambda h: slice(h * DN_DV, (h + 1) * DN_DV)

    def unit_inputs(u):
        bi, h = u
        lo, hi = h * DN_DK, (h + 1) * DN_DK
        y = ys[bi]
        qh = y[:, lo:hi]
        kh = y[:, n_key + lo:n_key + hi]
        vh = y[:, 2 * n_key + h * DN_DV:2 * n_key + (h + 1) * DN_DV]
        qh = qh * lax.rsqrt(jnp.sum(qh * qh, -1, keepdims=True) + RMS_EPS) * (DN_DK ** -0.5)
        kh = kh * lax.rsqrt(jnp.sum(kh * kh, -1, keepdims=True) + RMS_EPS)
        beta = _sigmoid(ba_ref[bi, :, h:h + 1])
        a_logit = ba_ref[bi, :, DN_HEADS + h:DN_HEADS + h + 1]
        g = -jnp.exp(alog_ref[0:1, lo:hi]) * _softplus(a_logit + dtb_ref[0:1, lo:hi])
        return qh, kh, vh, beta, g

    inp = each(unit_inputs)
    q = each(lambda u: inp[u][0])
    k = each(lambda u: inp[u][1])
    beta = each(lambda u: inp[u][3])
    parts = [p for u in units for p in _split3(inp[u][4])]
    gc_all = jnp.dot(tril, jnp.concatenate(parts, axis=1), preferred_element_type=F32)
    lanes = lambda i: slice(i * DN_DK, (i + 1) * DN_DK)
    gc = {u: gc_all[:, lanes(3 * i)] + gc_all[:, lanes(3 * i + 1)] + gc_all[:, lanes(3 * i + 2)]
          for i, u in enumerate(units)}

    def unit_decay(u):
        gc_row = jnp.transpose(gc[u])[0:1, :]
        diff = gc[u][:, :c] - gc_row
        return jnp.where(incl, jnp.exp(jnp.where(incl, diff, 0.0)), 0.0)

    decay = each(unit_decay)
    eg = each(lambda u: jnp.exp(gc[u]))
    kdec = each(lambda u: k[u] * jnp.exp(gc[u][c - 1:c, :] - gc[u]))
    kb = each(lambda u: k[u] * beta[u])
    akq = each(lambda u: _bdot_nt(jnp.concatenate([kb[u], q[u]], axis=0), k[u]))
    lmat = each(lambda u: jnp.where(strict, akq[u][:c] * decay[u], 0.0))
    qk = each(lambda u: akq[u][c:] * decay[u])
    tmat = each(lambda u: eye - lmat[u])
    lpow = each(lambda u: _bdot(lmat[u], lmat[u]))
    for step in range(1, n_double):
        if step < n_double - 1:
            prod = each(lambda u: _bdot(jnp.concatenate([lpow[u], tmat[u]], axis=0), lpow[u]))
            lpow = each(lambda u: prod[u][:c])
            tmat = each(lambda u: tmat[u] + prod[u][c:])
        else:
            prod = each(lambda u: _bdot(tmat[u], lpow[u]))
            tmat = each(lambda u: tmat[u] + prod[u])
    uw = each(lambda u: _bdot(tmat[u], jnp.concatenate([inp[u][2] * beta[u], kb[u] * eg[u]], axis=1)))
    s_old = each(lambda u: s_ref[u[0], u[1]])
    wq_s = each(lambda u: _bdot(jnp.concatenate([uw[u][:, DN_DV:], q[u] * eg[u]], axis=0), s_old[u]))
    v_new = each(lambda u: uw[u][:, :DN_DV] - wq_s[u][:c])
    o = each(lambda u: wq_s[u][c:] + _bdot(qk[u], v_new[u]))
    s_new = each(lambda u: s_old[u] * jnp.exp(gc[u][c - 1:c, :]) + _bdot(jnp.transpose(kdec[u]), v_new[u]))
    for u in units:
        bi, h = u
        s_ref[bi, h] = s_new[u]
        on = o[u] * lax.rsqrt(jnp.mean(o[u] * o[u], -1, keepdims=True) + RMS_EPS) * ng_ref[...]
        o_ref[bi, :, vsl(h)] = on * _silu(z_ref[bi, :, vsl(h)])

    @pl.when(j == nj - 1)
    def _():
        s_out_ref[...] = s_ref[...]
        for bi in range(bb):
            cbuf_out_ref[bi] = tails[bi]


def _dn_step_kernel(qkv_ref, ba_ref, z_ref, cbuf_ref, s0_ref, cw_ref, alog_ref, dtb_ref, ng_ref,
                    o_ref, s_out_ref, cbuf_out_ref, *, bb):
    hist = DN_CONV - 1
    n_key = DN_HEADS * DN_DK
    x = qkv_ref[...]
    y = x * cw_ref[hist:hist + 1, :]
    for tap in range(hist):
        y = y + cbuf_ref[:, tap, :] * cw_ref[tap:tap + 1, :]
    for tap in range(1, hist):
        cbuf_out_ref[:, tap - 1, :] = cbuf_ref[:, tap, :]
    cbuf_out_ref[:, hist - 1, :] = x
    y = _silu(y)
    ba = ba_ref[...]
    z = z_ref[...]
    row8 = lax.broadcasted_iota(jnp.int32, (V7X_SUBLANES, DN_DK), 0)
    row16 = lax.broadcasted_iota(jnp.int32, (V7X_BF16_SUBLANES, DN_DK), 0)
    heads = range(DN_HEADS)
    units = [(h, r) for h in heads for r in range(bb)]
    each = lambda f: {u: f(u) for u in units}
    one = lambda a, r: a[r:r + 1]

    def head_inputs(h):
        lo, hi = h * DN_DK, (h + 1) * DN_DK
        qh = y[:, lo:hi]
        kh = y[:, n_key + lo:n_key + hi]
        vh = y[:, 2 * n_key + h * DN_DV:2 * n_key + (h + 1) * DN_DV]
        qh = qh * lax.rsqrt(jnp.sum(qh * qh, -1, keepdims=True) + RMS_EPS) * (DN_DK ** -0.5)
        kh = kh * lax.rsqrt(jnp.sum(kh * kh, -1, keepdims=True) + RMS_EPS)
        beta = _sigmoid(ba[:, h:h + 1])
        a_logit = ba[:, DN_HEADS + h:DN_HEADS + h + 1]
        g = -jnp.exp(alog_ref[0:1, lo:hi]) * _softplus(a_logit + dtb_ref[0:1, lo:hi])
        eg = jnp.exp(g)
        return dict(k=kh, eg=eg, w=kh * beta * eg, qg=qh * eg, u=vh * beta, qk=jnp.sum(qh * kh, -1, keepdims=True))

    hd = [head_inputs(h) for h in heads]
    s_old = each(lambda u: s0_ref[u[1], u[0]])
    lhs = each(lambda u: jnp.where(row8 == 0, one(hd[u[0]]['w'], u[1]),
                                   jnp.where(row8 == 1, one(hd[u[0]]['qg'], u[1]), 0.0)))
    ws_qs = each(lambda u: _bdot(lhs[u], s_old[u]))
    v_new = each(lambda u: one(hd[u[0]]['u'], u[1]) - ws_qs[u][0:1])
    o_row = each(lambda u: ws_qs[u][1:2] + one(hd[u[0]]['qk'], u[1]) * v_new[u])
    k16 = each(lambda u: jnp.where(row16 == 0, one(hd[u[0]]['k'], u[1]), 0.0))
    v16 = each(lambda u: jnp.where(row16 == 0, v_new[u], 0.0))
    upd = each(lambda u: _bdot(jnp.transpose(k16[u]), v16[u]))
    for u in units:
        s_out_ref[u[1], u[0]] = s_old[u] * one(hd[u[0]]['eg'], u[1]) + upd[u]
    for h in heads:
        o = jnp.concatenate([o_row[(h, r)] for r in range(bb)], axis=0)
        o = o * lax.rsqrt(jnp.mean(o * o, -1, keepdims=True) + RMS_EPS) * ng_ref[...]
        o_ref[:, h * DN_DV:(h + 1) * DN_DV] = o * _silu(z[:, h * DN_DV:(h + 1) * DN_DV])


def _deltanet(qkv, ba, z, conv_buf, s0, conv_w, a_log, dt_bias, norm_g):
    b, t, n_qkv = qkv.shape
    n_val = DN_HEADS * DN_DV
    hist = DN_CONV - 1
    rep = lambda a: jnp.repeat(a.astype(F32), DN_DK).reshape(1, DN_HEADS * DN_DK)
    cw = conv_w.astype(F32)
    consts = (cw, rep(a_log), rep(dt_bias), norm_g.reshape(1, DN_DV).astype(F32))
    state_shape = (DN_HEADS, DN_DK, DN_DV)
    out_shape = [jax.ShapeDtypeStruct((b, t, n_val), F32),
                 jax.ShapeDtypeStruct((b,) + state_shape, F32),
                 jax.ShapeDtypeStruct((b, hist, n_qkv), F32)]
    if t == 1:
        bb = _row_tile(b, V7X_SUBLANES)
        const = lambda a: pl.BlockSpec(a.shape, lambda i: (0,) * a.ndim)
        o, s_new, cbuf_new = pl.pallas_call(
            functools.partial(_dn_step_kernel, bb=bb),
            grid=(b // bb,),
            in_specs=[pl.BlockSpec((bb, n_qkv), lambda i: (i, 0)),
                      pl.BlockSpec((bb, V7X_LANES), lambda i: (i, 0)),
                      pl.BlockSpec((bb, n_val), lambda i: (i, 0)),
                      pl.BlockSpec((bb, hist, n_qkv), lambda i: (i, 0, 0)),
                      pl.BlockSpec((bb,) + state_shape, lambda i: (i, 0, 0, 0))] + [const(a) for a in consts],
            out_specs=[pl.BlockSpec((bb, n_val), lambda i: (i, 0)),
                       pl.BlockSpec((bb,) + state_shape, lambda i: (i, 0, 0, 0)),
                       pl.BlockSpec((bb, hist, n_qkv), lambda i: (i, 0, 0))],
            out_shape=[jax.ShapeDtypeStruct((b, n_val), F32)] + out_shape[1:],
            compiler_params=_cparams("parallel"),
            name="deltanet_step",
        )(qkv.reshape(b, n_qkv), ba.reshape(b, V7X_LANES), z.reshape(b, n_val), conv_buf.astype(F32),
          s0.astype(F32), *consts)
        return o.reshape(b, 1, n_val), s_new, cbuf_new
    c = DN_CHUNK
    assert t % c == 0 and c >= hist
    bb = _row_tile(b, 4)
    const = lambda a: pl.BlockSpec(a.shape, lambda i, j: (0,) * a.ndim)
    return pl.pallas_call(
        functools.partial(_dn_seq_kernel, c=c, bb=bb),
        grid=(b // bb, t // c),
        in_specs=[pl.BlockSpec((bb, c, n_qkv), lambda i, j: (i, j, 0)),
                  pl.BlockSpec((bb, c, V7X_LANES), lambda i, j: (i, j, 0)),
                  pl.BlockSpec((bb, c, n_val), lambda i, j: (i, j, 0)),
                  pl.BlockSpec((bb, hist, n_qkv), lambda i, j: (i, 0, 0)),
                  pl.BlockSpec((bb,) + state_shape, lambda i, j: (i, 0, 0, 0))] + [const(a) for a in consts],
        out_specs=[pl.BlockSpec((bb, c, n_val), lambda i, j: (i, j, 0)),
                   pl.BlockSpec((bb,) + state_shape, lambda i, j: (i, 0, 0, 0)),
                   pl.BlockSpec((bb, hist, n_qkv), lambda i, j: (i, 0, 0))],
        out_shape=out_shape,
        scratch_shapes=[pltpu.VMEM((bb, _DN_HALO + c, n_qkv), F32),
                        pltpu.VMEM((bb,) + state_shape, F32)],
        compiler_params=_cparams("parallel", "arbitrary"),
        name="deltanet_seq",
    )(qkv, ba, z, conv_buf.astype(F32), s0.astype(F32), *consts)


def _s5_param_kernel(lre_ref, lim_ref, ldt_ref, lbre_ref, lbim_ref, fre_ref, fim_ref):
    lam_re = lre_ref[...]
    lam_im = lim_ref[...]
    dt = jnp.exp(ldt_ref[...])
    mag = jnp.exp(lam_re * dt)
    ang = lam_im * dt
    lb_re = mag * jnp.cos(ang)
    lb_im = mag * jnp.sin(ang)
    den = lam_re * lam_re + lam_im * lam_im
    lbre_ref[...] = lb_re
    lbim_ref[...] = lb_im
    fre_ref[...] = ((lb_re - 1.0) * lam_re + lb_im * lam_im) / den
    fim_ref[...] = (lb_im * lam_re - (lb_re - 1.0) * lam_im) / den


def _s5_discretize(lam_re, lam_im, log_dt):
    g, n = lam_re.shape
    ldt = jnp.broadcast_to(log_dt.astype(F32)[:, None], (g, n))
    shp = jax.ShapeDtypeStruct((g, n), F32)
    return pl.pallas_call(_s5_param_kernel, out_shape=[shp] * 4, name="s5_discretize")(
        lam_re.astype(F32), lam_im.astype(F32), ldt)


def _s5_kernel(u_ref, h0re_ref, h0im_ref, lbre_ref, lbim_ref, bre_ref, bim_ref, cre_ref, cim_ref,
               d_ref, wglu_ref, bglu_ref, y_ref, hre_out_ref, him_out_ref,
               sre_ref, sim_ref, cre_s, cim_s, *, tb, bb, lane_chunk):
    j = pl.program_id(1)
    nj = pl.num_programs(1)
    n_ch = u_ref.shape[-1]
    n_st = sre_ref.shape[-1]
    halves = bre_ref.shape[0]
    ch_h = n_ch // halves
    st_h = n_st // halves

    @pl.when(j == 0)
    def _():
        cre_s[...] = h0re_ref[...]
        cim_s[...] = h0im_ref[...]

    u = u_ref[...].reshape(tb * bb, n_ch)
    ub = u.astype(BF16)
    for hf in range(halves):
        uh = ub[:, hf * ch_h:(hf + 1) * ch_h]
        sre_ref[:, hf * st_h:(hf + 1) * st_h] = jnp.dot(uh, bre_ref[hf], preferred_element_type=F32)
        sim_ref[:, hf * st_h:(hf + 1) * st_h] = jnp.dot(uh, bim_ref[hf], preferred_element_type=F32)

    for c0 in range(0, n_st, lane_chunk):
        cs = slice(c0, c0 + lane_chunk)
        lr = jnp.broadcast_to(lbre_ref[0:1, cs], (bb, lane_chunk))
        li = jnp.broadcast_to(lbim_ref[0:1, cs], (bb, lane_chunk))

        def body(t, carry, cs=cs, lr=lr, li=li):
            hr, hi = carry
            r = pl.multiple_of(t * bb, bb)
            nr = lr * hr - li * hi + sre_ref[pl.ds(r, bb), cs]
            ni = lr * hi + li * hr + sim_ref[pl.ds(r, bb), cs]
            sre_ref[pl.ds(r, bb), cs] = nr
            sim_ref[pl.ds(r, bb), cs] = ni
            return nr, ni

        hr, hi = lax.fori_loop(0, tb, body, (cre_s[:, cs], cim_s[:, cs]))
        cre_s[:, cs] = hr
        cim_s[:, cs] = hi

    ys = []
    for hf in range(halves):
        hre = sre_ref[:, hf * st_h:(hf + 1) * st_h].astype(BF16)
        him = sim_ref[:, hf * st_h:(hf + 1) * st_h].astype(BF16)
        ys.append(jnp.dot(hre, cre_ref[hf], preferred_element_type=F32)
                  - jnp.dot(him, cim_ref[hf], preferred_element_type=F32))
    y = jnp.concatenate(ys, axis=1) + d_ref[...] * u
    y = jax.nn.gelu(y)
    y = y * _sigmoid(_bdot(y, wglu_ref[...]) + bglu_ref[...])
    y_ref[...] = y.reshape(tb, bb, n_ch)

    @pl.when(j == nj - 1)
    def _():
        hre_out_ref[...] = cre_s[...]
        him_out_ref[...] = cim_s[...]


def _s5(u_tm, h0_re, h0_im, lam_re, lam_im, log_dt, b_re, b_im, c_re, c_im, d_skip, w_glu, b_glu, *, halves=2):
    t, b, n_ch = u_tm.shape
    g, n, p = b_re.shape
    n_st = g * n
    lb_re, lb_im, f_re, f_im = _s5_discretize(lam_re, lam_im, log_dt)
    b_re = b_re.astype(F32)
    b_im = b_im.astype(F32)
    bb_re = f_re[..., None] * b_re - f_im[..., None] * b_im
    bb_im = f_re[..., None] * b_im + f_im[..., None] * b_re
    gh = g // halves

    def in_blocks(a):
        a = a.reshape(halves, gh, n, p)
        eye = jnp.eye(gh, dtype=F32)
        return jnp.einsum('hgnp,gk->hgpkn', a, eye).reshape(halves, gh * p, gh * n).astype(BF16)

    def out_blocks(a):
        a = a.astype(F32).reshape(halves, gh, p, n)
        eye = jnp.eye(gh, dtype=F32)
        return jnp.einsum('hgpn,gk->hgnkp', a, eye).reshape(halves, gh * n, gh * p).astype(BF16)

    bre_m, bim_m = in_blocks(bb_re), in_blocks(bb_im)
    cre_m, cim_m = out_blocks(c_re), out_blocks(c_im)
    bb = b if b <= 128 else 128
    assert b % bb == 0
    tb = _row_tile(t, max(1, 512 // bb))
    lane_chunk = max(V7X_LANES, min(n_st, 8192 // bb))
    const = lambda a: pl.BlockSpec(a.shape, lambda i, j: (0,) * a.ndim)
    row = lambda a, m: a.reshape(1, m).astype(F32)
    args = (u_tm, h0_re.reshape(b, n_st).astype(F32), h0_im.reshape(b, n_st).astype(F32),
            row(lb_re, n_st), row(lb_im, n_st), bre_m, bim_m, cre_m, cim_m,
            row(d_skip, n_ch), w_glu.astype(BF16), row(b_glu, n_ch))
    y, hre, him = pl.pallas_call(
        functools.partial(_s5_kernel, tb=tb, bb=bb, lane_chunk=lane_chunk),
        grid=(b // bb, t // tb),
        in_specs=[pl.BlockSpec((tb, bb, n_ch), lambda i, j: (j, i, 0)),
                  pl.BlockSpec((bb, n_st), lambda i, j: (i, 0)),
                  pl.BlockSpec((bb, n_st), lambda i, j: (i, 0))] + [const(a) for a in args[3:]],
        out_specs=[pl.BlockSpec((tb, bb, n_ch), lambda i, j: (j, i, 0)),
                   pl.BlockSpec((bb, n_st), lambda i, j: (i, 0)),
                   pl.BlockSpec((bb, n_st), lambda i, j: (i, 0))],
        out_shape=[jax.ShapeDtypeStruct((t, b, n_ch), F32),
                   jax.ShapeDtypeStruct((b, n_st), F32),
                   jax.ShapeDtypeStruct((b, n_st), F32)],
        scratch_shapes=[pltpu.VMEM((tb * bb, n_st), F32), pltpu.VMEM((tb * bb, n_st), F32),
                        pltpu.VMEM((bb, n_st), F32), pltpu.VMEM((bb, n_st), F32)],
        compiler_params=_cparams("parallel", "arbitrary"),
        name="s5",
    )(*args)
    return y, hre.reshape(b, g, n), him.reshape(b, g, n)


def _ab_mixers(x, dn_s, dn_conv, ssm_re, ssm_im, p, i):
    b, t, d = x.shape
    xr = x if t > 1 else x.reshape(1, b, d)
    bx, tx, _ = xr.shape
    qkv, z, ba, u_tm = _ab_in_proj(xr, p['ab_w_in'][i])
    n_ch = u_tm.shape[1] // bx
    shp = lambda a: a.reshape(b, t, a.shape[-1])
    o, s_new, conv_new = _deltanet(shp(qkv), shp(ba), shp(z), dn_conv, dn_s, p['dn_conv_w'][i],
                                   p['dn_a_log'][i], p['dn_dt_bias'][i], p['dn_norm_g'][i])
    y_tm, h_re, h_im = _s5(u_tm.reshape(t, b, n_ch), ssm_re, ssm_im, p['ssm_lambda_re'][i],
                           p['ssm_lambda_im'][i], p['ssm_log_dt'][i], p['ssm_b_re'][i], p['ssm_b_im'][i],
                           p['ssm_c_re'][i], p['ssm_c_im'][i], p['ssm_d'][i], p['ssm_w_glu'][i],
                           p['ssm_b_glu'][i])
    return o.reshape(bx, tx, -1), y_tm.reshape(tx, bx * n_ch), (s_new, conv_new, h_re, h_im)


def _ab_layer(x, dn_s, dn_conv, ssm_re, ssm_im, p, i, ln_g, ln_b, *, alpha):
    b, t, d = x.shape
    o, y_tm, states = _ab_mixers(x, dn_s, dn_conv, ssm_re, ssm_im, p, i)
    xr = x.reshape(o.shape[0], o.shape[1], d)
    x_new = _ab_out_proj(o, y_tm, p['ab_w_out'][i], xr, ln_g, ln_b, alpha=alpha)
    return (x_new.reshape(b, t, d),) + states


_CC_HALO = 32
_CC_ROW_TILES = 8


def _cconv_seq_kernel(h_ref, buf_ref, w_ref, bdw_ref, g_ref, b_ref, o_ref, xp_ref, xs_ref, acc_ref, *, tt, width):
    j = pl.program_id(1)
    hist = width - 1
    d = h_ref.shape[-1]
    rows = V7X_SUBLANES

    @pl.when(j == 0)
    def _():
        xp_ref[0:_CC_HALO - hist, :] = jnp.zeros((_CC_HALO - hist, d), F32)
        xp_ref[_CC_HALO - hist:_CC_HALO, :] = buf_ref[...]

    xp_ref[_CC_HALO:_CC_HALO + tt, :] = h_ref[...]
    n_shift = xs_ref.shape[1]
    for s in range(1, rows):
        xs_ref[s - 1] = xp_ref[pl.ds(s, n_shift), :]
    base = _CC_HALO - hist
    for c0 in range(0, d, V7X_LANES):
        cs = slice(c0, c0 + V7X_LANES)
        taps = [jnp.broadcast_to(w_ref[k:k + 1, cs], (rows, V7X_LANES)) for k in range(width)]

        def body(i, carry, taps=taps, cs=cs):
            r = pl.multiple_of(i * (rows * _CC_ROW_TILES), rows * _CC_ROW_TILES)
            accs = [None] * _CC_ROW_TILES
            for s in range(rows):
                ks = [k for k in range(width) if (base + k) % rows == s]
                if not ks:
                    continue
                n_rows = rows * (_CC_ROW_TILES + max((base + k) // rows for k in ks))
                big = xp_ref[pl.ds(r, n_rows), cs] if s == 0 else xs_ref[s - 1, pl.ds(r, n_rows), cs]
                for k in ks:
                    a = (base + k) // rows
                    for sub in range(_CC_ROW_TILES):
                        term = big[(sub + a) * rows:(sub + a + 1) * rows] * taps[k]
                        accs[sub] = term if accs[sub] is None else accs[sub] + term
            for sub in range(_CC_ROW_TILES):
                acc_ref[pl.ds(r + sub * rows, rows), cs] = accs[sub]
            return carry

        lax.fori_loop(0, tt // (rows * _CC_ROW_TILES), body, 0)
    xp_ref[0:_CC_HALO, :] = xp_ref[tt:tt + _CC_HALO, :]
    y = _layer_norm(acc_ref[...] + bdw_ref[...], g_ref[...], b_ref[...])
    o_ref[...] = _silu(y)


def _cconv_seq(h, buf, w_dw, b_dw, ln_g, ln_b, *, tt=512):
    b, t, d = h.shape
    width = w_dw.shape[0]
    tt = _row_tile(t, tt)
    assert tt >= _CC_HALO and width - 1 <= _CC_HALO
    row = lambda a: a.reshape(1, d).astype(F32)
    const = lambda a: pl.BlockSpec(a.shape, lambda i, j: (0, 0))
    args = (w_dw.astype(F32), row(b_dw), row(ln_g), row(ln_b))
    return pl.pallas_call(
        functools.partial(_cconv_seq_kernel, tt=tt, width=width),
        grid=(b, t // tt),
        in_specs=[pl.BlockSpec((None, tt, d), lambda i, j: (i, j, 0)),
                  pl.BlockSpec((None, width - 1, d), lambda i, j: (i, 0, 0))] + [const(a) for a in args],
        out_specs=pl.BlockSpec((None, tt, d), lambda i, j: (i, j, 0)),
        out_shape=jax.ShapeDtypeStruct((b, t, d), F32),
        scratch_shapes=[pltpu.VMEM((_CC_HALO + tt, d), F32),
                        pltpu.VMEM((V7X_SUBLANES - 1, _CC_HALO + tt - V7X_SUBLANES, d), F32),
                        pltpu.VMEM((tt, d), F32)],
        compiler_params=_cparams("parallel", "arbitrary"),
        name="cconv_seq",
    )(h, buf.astype(F32), *args)


def _cconv_step_kernel(h_ref, buf_ref, w_ref, bdw_ref, g_ref, b_ref, o_ref, *, width):
    hist = width - 1
    acc = jnp.sum(buf_ref[...] * w_ref[0:hist, :][None], axis=1) + h_ref[...] * w_ref[hist:width, :]
    o_ref[...] = _silu(_layer_norm(acc + bdw_ref[...], g_ref[...], b_ref[...]))


def _cconv_step(h, buf, w_dw, b_dw, ln_g, ln_b, *, bb=8):
    b, d = h.shape
    width = w_dw.shape[0]
    bb = _row_tile(b, bb)
    row = lambda a: a.reshape(1, d).astype(F32)
    const = lambda a: pl.BlockSpec(a.shape, lambda i: (0, 0))
    args = (w_dw.astype(F32), row(b_dw), row(ln_g), row(ln_b))
    return pl.pallas_call(
        functools.partial(_cconv_step_kernel, width=width),
        grid=(b // bb,),
        in_specs=[pl.BlockSpec((bb, d), lambda i: (i, 0)),
                  pl.BlockSpec((bb, width - 1, d), lambda i: (i, 0, 0))] + [const(a) for a in args],
        out_specs=pl.BlockSpec((bb, d), lambda i: (i, 0)),
        out_shape=jax.ShapeDtypeStruct((b, d), F32),
        compiler_params=_cparams("parallel"),
        name="cconv_step",
    )(h, buf.astype(F32), *args)


def _conv_mixer(x, buf, p, i):
    b, t, d = x.shape
    h = _linear(x.reshape(b * t, d), p['cc_w_pw1'][i], p['cc_b_pw1'][i], glu=True)
    args = (p['cc_w_dw'][i], p['cc_b_dw'][i], p['cc_ln_g'][i], p['cc_ln_b'][i])
    if t == 1:
        hc = _cconv_step(h, buf, *args)
    else:
        hc = _cconv_seq(h.reshape(b, t, d), buf, *args).reshape(b * t, d)
    new_buf = jnp.concatenate([buf.astype(F32), h.reshape(b, t, d)], axis=1)[:, t:]
    return hc, new_buf


def _conv_layer(x, buf, p, i, ln_g, ln_b, *, alpha):
    b, t, d = x.shape
    hc, new_buf = _conv_mixer(x, buf, p, i)
    x_new = _linear_postnorm(hc, p['cc_w_pw2'][i], p['cc_b_pw2'][i], x.reshape(b * t, d), ln_g, ln_b, alpha=alpha)
    return x_new.reshape(b, t, d), new_buf


def _mem_attn_seq_kernel(x_ref, wq_ref, k_ref, v_ref, wo_ref, g_ref, b_ref, o_ref, *, alpha, heads):
    o_ref[...] = _mem_attn_math(x_ref[...], wq_ref, k_ref, v_ref, wo_ref, g_ref, b_ref, alpha=alpha, heads=heads)


def _mem_attn_math(x, wq_ref, k_ref, v_ref, wo_ref, g_ref, b_ref, *, alpha, heads):
    d = x.shape[-1]
    hd = d // heads
    q = _bdot(x, wq_ref[...]) * (hd ** -0.5)
    k = k_ref[...].astype(BF16)
    v = v_ref[...].astype(BF16)
    hsl = [slice(h * hd, (h + 1) * hd) for h in range(heads)]
    ss = [_bdot_nt(q[:, hs], k[:, hs]) for hs in hsl]
    es = [jnp.exp(s - jnp.max(s, axis=-1, keepdims=True)) for s in ss]
    ps = [e / jnp.sum(e, axis=-1, keepdims=True) for e in es]
    o = jnp.concatenate([_bdot(a, v[:, hs]) for a, hs in zip(ps, hsl)], axis=1)
    hres = _bdot(o, wo_ref[...])
    return _layer_norm(alpha * x + hres, g_ref[...], b_ref[...])


_PMR_CHUNKS = 4


def _proj_mem_router_kernel(*refs, n_proj, alpha, heads, groups, per_group):
    proj = refs[:2 * n_proj]
    (bias_ref, xres_ref, g0_ref, b0_ref, wq_ref, k_ref, v_ref, wo_ref, g1_ref, b1_ref, rw_ref, rb_ref,
     x_out_ref, ids_ref, wts_ref) = refs[2 * n_proj:]
    tq, d = xres_ref.shape
    hd = d // heads
    chunk = tq // _PMR_CHUNKS
    rows = [pl.ds(c * chunk, chunk) for c in range(_PMR_CHUNKS)]
    hsl = [slice(h * hd, (h + 1) * hd) for h in range(heads)]
    hs = [bias_ref[...] + sum(_bdot(proj[2 * a][r, :], proj[2 * a + 1][...]) for a in range(n_proj)) for r in rows]
    xs = [_layer_norm(alpha * xres_ref[r, :] + h, g0_ref[...], b0_ref[...]) for r, h in zip(rows, hs)]
    qs = [_bdot(x, wq_ref[...]) * (hd ** -0.5) for x in xs]
    k = k_ref[...].astype(BF16)
    v = v_ref[...].astype(BF16)
    ss = [[_bdot_nt(q[:, s], k[:, s]) for s in hsl] for q in qs]
    es = [[jnp.exp(s - jnp.max(s, axis=-1, keepdims=True)) for s in sc] for sc in ss]
    ps = [[e / jnp.sum(e, axis=-1, keepdims=True) for e in ec] for ec in es]
    os = [jnp.concatenate([_bdot(a, v[:, s]) for a, s in zip(pc, hsl)], axis=1) for pc in ps]
    ys = [_layer_norm(alpha * x + _bdot(o, wo_ref[...]), g1_ref[...], b1_ref[...]) for x, o in zip(xs, os)]
    for r, y in zip(rows, ys):
        x_out_ref[r, :] = y
    routed = [_router_math(y, rw_ref[...], rb_ref[...], groups=groups, per_group=per_group) for y in ys]
    for r, (ids, wts) in zip(rows, routed):
        ids_ref[r, :] = ids
        wts_ref[r, :] = wts


def _proj_mem_router(proj, bias, x_res, ln0, mk, mv, layer, wq, wo, ln1, router_w, router_b, *, alpha, tq=512):
    b, t, d = x_res.shape
    m = mk.shape[2]
    nt = t // tq
    groups = MOE_GROUPS
    row = lambda a: a.reshape(1, d).astype(F32)
    const = lambda a: pl.BlockSpec(a.shape, lambda i, j: (0,) * a.ndim)
    xblk = pl.BlockSpec((None, tq, d), lambda i, j: (i, j, 0))
    rowblk = pl.BlockSpec((1, d), lambda i, j: (0, 0))
    proj_args, proj_specs = [], []
    for arr, spec, w in proj:
        wb = w.astype(BF16)
        proj_args += [arr, wb]
        proj_specs += [spec, const(wb)]
    if bias is None:
        bias = jnp.zeros((d,), F32)
    wqb, wob = wq.astype(BF16), wo.astype(BF16)
    lane_blk = pl.BlockSpec((tq, V7X_LANES), lambda i, j: (i * nt + j, 0))
    x_out, ids, wts = pl.pallas_call(
        functools.partial(_proj_mem_router_kernel, n_proj=len(proj), alpha=alpha, heads=MEM_HEADS,
                          groups=groups, per_group=MOE_EXPERTS_PER_GROUP),
        grid=(b, nt),
        in_specs=proj_specs + [rowblk, xblk, rowblk, rowblk, const(wqb),
                               pl.BlockSpec((None, None, m, d), lambda i, j: (layer, i, 0, 0)),
                               pl.BlockSpec((None, None, m, d), lambda i, j: (layer, i, 0, 0)),
                               const(wob), rowblk, rowblk, const(router_w), const(router_b)],
        out_specs=[xblk, lane_blk, lane_blk],
        out_shape=[jax.ShapeDtypeStruct((b, t, d), F32),
                   jax.ShapeDtypeStruct((b * t, V7X_LANES), jnp.int32),
                   jax.ShapeDtypeStruct((b * t, V7X_LANES), F32)],
        compiler_params=_cparams("parallel", "parallel"),
        name="proj_mem_router",
    )(*proj_args, row(bias), x_res, row(ln0[0]), row(ln0[1]), wqb, mk, mv, wob, row(ln1[0]), row(ln1[1]),
      router_w, router_b)
    return x_out, ids, wts


def _mem_attn_seq(x, mk, mv, layer, wq, wo, ln_g, ln_b, *, alpha, tq=512):
    b, t, d = x.shape
    m = mk.shape[2]
    tq = _row_tile(t, tq)
    row = lambda a: a.reshape(1, d).astype(F32)
    const = lambda a: pl.BlockSpec(a.shape, lambda i, j: (0, 0))
    wqb, wob = wq.astype(BF16), wo.astype(BF16)
    return pl.pallas_call(
        functools.partial(_mem_attn_seq_kernel, alpha=alpha, heads=MEM_HEADS),
        grid=(b, t // tq),
        in_specs=[pl.BlockSpec((None, tq, d), lambda i, j: (i, j, 0)),
                  const(wqb),
                  pl.BlockSpec((None, None, m, d), lambda i, j: (layer, i, 0, 0)),
                  pl.BlockSpec((None, None, m, d), lambda i, j: (layer, i, 0, 0)),
                  const(wob), pl.BlockSpec((1, d), lambda i, j: (0, 0)),
                  pl.BlockSpec((1, d), lambda i, j: (0, 0))],
        out_specs=pl.BlockSpec((None, tq, d), lambda i, j: (i, j, 0)),
        out_shape=jax.ShapeDtypeStruct((b, t, d), F32),
        compiler_params=_cparams("parallel", "parallel"),
        name="mem_attn_seq",
    )(x, wqb, mk, mv, wob, row(ln_g), row(ln_b))


def _mem_attn_step_kernel(q_ref, k_ref, v_ref, o_ref, *, heads, bb):
    i = pl.program_id(0)
    hd = q_ref.shape[-1]
    for r in range(bb):
        q = q_ref[i * bb + r] * (hd ** -0.5)
        s = jnp.sum(k_ref[r] * q[None], axis=-1, keepdims=True)
        s = s - jnp.max(s, axis=0, keepdims=True)
        e = jnp.exp(s)
        o_ref[i * bb + r] = jnp.sum(e * v_ref[r], axis=0) / jnp.sum(e, axis=0)


def _mem_attn_step(q, mk_all, mv_all, layer, *, bb=8):
    b, heads, hd = q.shape
    m = mk_all.shape[2]
    bb = _row_tile(b, bb)
    return pl.pallas_call(
        functools.partial(_mem_attn_step_kernel, heads=heads, bb=bb),
        grid=(b // bb,),
        in_specs=[pl.BlockSpec((b, heads, hd), lambda i: (0, 0, 0)),
                  pl.BlockSpec((None, bb, m, heads, hd), lambda i: (layer, i, 0, 0, 0)),
                  pl.BlockSpec((None, bb, m, heads, hd), lambda i: (layer, i, 0, 0, 0))],
        out_specs=pl.BlockSpec((b, heads, hd), lambda i: (0, 0, 0)),
        out_shape=jax.ShapeDtypeStruct((b, heads, hd), F32),
        compiler_params=_cparams("arbitrary"),
        name="mem_attn_step",
    )(q, mk_all, mv_all)


def _mem_layer(x, mk_all, mv_all, layer, wq, wo, ln_g, ln_b, *, alpha):
    b, t, d = x.shape
    n_layers, _, m, heads, hd = mk_all.shape
    if t == 1:
        x2 = x.reshape(b, d)
        q = _linear(x2, wq).reshape(b, heads, hd)
        o = _mem_attn_step(q, mk_all, mv_all, layer).reshape(b, d)
        return _linear_postnorm(o, wo, None, x2, ln_g, ln_b, alpha=alpha).reshape(b, t, d)
    mk_all = mk_all.reshape(n_layers, b, m, d)
    mv_all = mv_all.reshape(n_layers, b, m, d)
    return _mem_attn_seq(x, mk_all, mv_all, layer, wq, wo, ln_g, ln_b, alpha=alpha)


_NEG = -1e30


def _router_kernel(x_ref, w_ref, b_ref, ids_ref, wts_ref, *, groups, per_group):
    ids_ref[...], wts_ref[...] = _router_math(x_ref[...], w_ref[...], b_ref[...], groups=groups,
                                              per_group=per_group)


def _router_math(x, w, bias, *, groups, per_group):
    xh = x.astype(BF16)
    xl = (x - xh.astype(F32)).astype(BF16)
    wh = w.astype(BF16)
    wl = (w - wh.astype(F32)).astype(BF16)
    logits = (jnp.dot(xh, wh, preferred_element_type=F32) + jnp.dot(xh, wl, preferred_element_type=F32)
              + jnp.dot(xl, wh, preferred_element_type=F32)) + bias
    lane = lax.broadcasted_iota(jnp.int32, logits.shape, 1)
    n_exp = groups * per_group
    is_g = lane < groups
    gl = jnp.where(is_g, logits, _NEG)
    gmax = jnp.max(gl, axis=-1, keepdims=True)
    gsum = jnp.sum(jnp.where(is_g, jnp.exp(gl - gmax), 0.0), axis=-1, keepdims=True)
    gw = 1.0 / gsum
    gi = jnp.min(jnp.where(gl == gmax, lane, V7X_LANES), axis=-1, keepdims=True)
    lane_grp = (lane - groups) // per_group
    sel = (lane >= groups) & (lane < groups + n_exp) & (lane_grp == gi)
    el = jnp.where(sel, logits, _NEG)
    emax = jnp.max(el, axis=-1, keepdims=True)
    ee = jnp.where(sel, jnp.exp(el - emax), 0.0)
    ep = jnp.where(sel, ee / jnp.sum(ee, axis=-1, keepdims=True), -1.0)
    p1 = jnp.max(ep, axis=-1, keepdims=True)
    i1 = jnp.min(jnp.where(ep == p1, lane, V7X_LANES), axis=-1, keepdims=True)
    ep2 = jnp.where(lane == i1, -1.0, ep)
    p2 = jnp.max(ep2, axis=-1, keepdims=True)
    i2 = jnp.min(jnp.where(ep2 == p2, lane, V7X_LANES), axis=-1, keepdims=True)
    denom = p1 + p2
    ids = jnp.where(lane == 0, i1 - groups, jnp.where(lane == 1, i2 - groups, 0))
    wts = jnp.where(lane == 0, gw * p1 / denom, jnp.where(lane == 1, gw * p2 / denom, 0.0))
    return ids, wts


def _router_params(p, layer):
    w_group, w_expert = p['moe_w_group'][layer], p['moe_w_expert'][layer]
    pad = V7X_LANES - w_group.shape[1] - w_expert.shape[1]
    w = jnp.pad(jnp.concatenate([w_group, w_expert], axis=1).astype(F32), ((0, 0), (0, pad)))
    b = jnp.pad(jnp.concatenate([p['moe_b_group'][layer], p['moe_b_expert'][layer]]).astype(F32), (0, pad))
    return w, b.reshape(1, V7X_LANES)


def _router(x, w, b, *, tm=512):
    n, d = x.shape
    tm = _row_tile(n, tm)
    return pl.pallas_call(
        functools.partial(_router_kernel, groups=MOE_GROUPS, per_group=MOE_EXPERTS_PER_GROUP),
        grid=(n // tm,),
        in_specs=[pl.BlockSpec((tm, d), lambda i: (i, 0)),
                  pl.BlockSpec((d, V7X_LANES), lambda i: (0, 0)),
                  pl.BlockSpec((1, V7X_LANES), lambda i: (0, 0))],
        out_specs=[pl.BlockSpec((tm, V7X_LANES), lambda i: (i, 0)),
                   pl.BlockSpec((tm, V7X_LANES), lambda i: (i, 0))],
        out_shape=[jax.ShapeDtypeStruct((n, V7X_LANES), jnp.int32),
                   jax.ShapeDtypeStruct((n, V7X_LANES), F32)],
        compiler_params=_cparams("parallel"),
        name="moe_router",
    )(x, w, b)


def _expert_ffn_kernel(te_ref, nv_ref, x_ref, wg_ref, wu_ref, wd_ref, o_ref):
    i = pl.program_id(0)

    @pl.when(i < nv_ref[0])
    def _():
        x = x_ref[...]
        hid = _silu(_bdot(x, wg_ref[...])) * _bdot(x, wu_ref[...])
        o_ref[...] = _bdot(hid, wd_ref[...])

    @pl.when(i >= nv_ref[0])
    def _():
        o_ref[...] = jnp.zeros_like(o_ref)


def _expert_ffn(x_sorted, tile_expert, n_valid, w_gate, w_up, w_down, layer, *, tm):
    r, d = x_sorted.shape
    f = w_gate.shape[-1]
    return pl.pallas_call(
        _expert_ffn_kernel,
        grid_spec=pltpu.PrefetchScalarGridSpec(
            num_scalar_prefetch=2,
            grid=(r // tm,),
            in_specs=[pl.BlockSpec((tm, d), lambda i, te, nv: (jnp.minimum(i, nv[0] - 1), 0)),
                      pl.BlockSpec((None, None, d, f), lambda i, te, nv: (layer, te[i], 0, 0)),
                      pl.BlockSpec((None, None, d, f), lambda i, te, nv: (layer, te[i], 0, 0)),
                      pl.BlockSpec((None, None, f, d), lambda i, te, nv: (layer, te[i], 0, 0))],
            out_specs=pl.BlockSpec((tm, d), lambda i, te, nv: (i, 0)),
        ),
        out_shape=jax.ShapeDtypeStruct((r, d), F32),
        compiler_params=_cparams("arbitrary"),
        name="moe_expert_ffn",
    )(tile_expert, n_valid, x_sorted, w_gate, w_up, w_down)


def _rank_kernel(ids_ref, cnt_in_ref, rank_ref, cnt_ref, carry_ref):
    i = pl.program_id(0)
    tm = ids_ref.shape[0]

    @pl.when(i == 0)
    def _():
        carry_ref[...] = cnt_in_ref[...].astype(F32)

    ids = ids_ref[...]
    lane = lax.broadcasted_iota(jnp.int32, ids.shape, 1)
    oh0 = jnp.where(lane == ids[:, 0:1], 1.0, 0.0)
    oh1 = jnp.where(lane == ids[:, 1:2], 1.0, 0.0)
    both = oh0 + oh1
    ii = lax.broadcasted_iota(jnp.int32, (tm, tm), 0)
    jj = lax.broadcasted_iota(jnp.int32, (tm, tm), 1)
    earlier = jnp.where(ii > jj, 1.0, 0.0).astype(BF16)
    prefix = jnp.dot(earlier, both.astype(BF16), preferred_element_type=F32) + carry_ref[...]
    r0 = jnp.sum(oh0 * prefix, axis=-1, keepdims=True)
    r1 = jnp.sum(oh1 * prefix, axis=-1, keepdims=True)
    rank_ref[...] = jnp.where(lane == 0, r0, jnp.where(lane == 1, r1, 0.0)).astype(jnp.int32)
    total = carry_ref[...] + jnp.sum(both, axis=0, keepdims=True)
    carry_ref[...] = total
    cnt_ref[...] = total.astype(jnp.int32)


def _expert_ranks(ids, counts_in, *, tm=512):
    n = ids.shape[0]
    tm = _row_tile(n, tm)
    return pl.pallas_call(
        _rank_kernel,
        grid=(n // tm,),
        in_specs=[pl.BlockSpec((tm, V7X_LANES), lambda i: (i, 0)),
                  pl.BlockSpec((1, V7X_LANES), lambda i: (0, 0))],
        out_specs=[pl.BlockSpec((tm, V7X_LANES), lambda i: (i, 0)),
                   pl.BlockSpec((1, V7X_LANES), lambda i: (0, 0))],
        out_shape=[jax.ShapeDtypeStruct((n, V7X_LANES), jnp.int32),
                   jax.ShapeDtypeStruct((1, V7X_LANES), jnp.int32)],
        scratch_shapes=[pltpu.VMEM((1, V7X_LANES), F32)],
        compiler_params=_cparams("arbitrary"),
        name="moe_rank",
    )(ids, counts_in)


def _for_each_row(tm, fn):
    def body(i, carry):
        base = pl.multiple_of(i * V7X_SUBLANES, V7X_SUBLANES)
        for sub in range(V7X_SUBLANES):
            fn(base, sub)
        return carry

    lax.fori_loop(0, tm // V7X_SUBLANES, body, 0)


def _dispatch_kernel(fill_ref, slot_ref, x_ref, *rest, tile, fill_tiles):
    tm = x_ref.shape[0]
    if fill_tiles:
        xs_ref, sem, zbuf, zsem = rest

        @pl.when(pl.program_id(0) == 0)
        def _():
            zbuf[...] = jnp.zeros_like(zbuf)
            n_exp = fill_ref.shape[1]
            n_tiles = xs_ref.shape[0] // tile
            total_end = fill_ref[1, n_exp - 1]
            starts, conds = [], []
            for e in range(n_exp):
                for j in range(fill_tiles):
                    starts.append(fill_ref[0, e] + j * tile)
                    conds.append(starts[-1] < fill_ref[1, e])
            for t in range(n_tiles):
                starts.append(t * tile)
                conds.append(t * tile >= total_end)

            def fill(start):
                return pltpu.make_async_copy(zbuf, xs_ref.at[pl.ds(pl.multiple_of(start, tile), tile), :], zsem)

            for start, c in zip(starts, conds):
                pl.when(c)(lambda start=start: fill(start).start())
            for start, c in zip(starts, conds):
                pl.when(c)(lambda start=start: fill(start).wait())
    else:
        _, xs_ref, sem = rest

    def issue(base, sub):
        for k in range(MOE_TOP_K):
            s = slot_ref[0, 0, MOE_TOP_K * (base + sub) + k]
            pltpu.make_async_copy(x_ref.at[pl.ds(base + sub, 1), :], xs_ref.at[pl.ds(s, 1), :],
                                  sem).start(priority=k)

    _for_each_row(tm, issue)
    for k in range(MOE_TOP_K):
        pltpu.make_async_copy(x_ref, xs_ref.at[pl.ds(0, tm), :], sem).wait()


def _dispatch(x, slot, fill, *, rows, tile, fill_tiles, xs=None, tm):
    n, d = x.shape
    slot3 = slot.reshape(n // tm, 1, MOE_TOP_K * tm)
    in_specs = [pl.BlockSpec(memory_space=pltpu.SMEM),
                pl.BlockSpec((1, 1, MOE_TOP_K * tm), lambda i: (i, 0, 0), memory_space=pltpu.SMEM),
                pl.BlockSpec((tm, d), lambda i: (i, 0))]
    scratch = [pltpu.SemaphoreType.DMA(())]
    fresh = xs is None
    if fresh:
        args, aliases = (fill, slot3, x), {}
        scratch += [pltpu.VMEM((tile, d), F32), pltpu.SemaphoreType.DMA(())]
    else:
        args, aliases = (fill, slot3, x, xs), {3: 0}
        in_specs.append(pl.BlockSpec(memory_space=pl.ANY))
    return pl.pallas_call(
        functools.partial(_dispatch_kernel, tile=tile, fill_tiles=fill_tiles if fresh else 0),
        grid=(n // tm,),
        in_specs=in_specs,
        out_specs=pl.BlockSpec(memory_space=pl.ANY),
        out_shape=jax.ShapeDtypeStruct((rows, d), F32),
        scratch_shapes=scratch,
        input_output_aliases=aliases,
        compiler_params=_cparams("arbitrary"),
        name="moe_dispatch",
    )(*args)


def _combine_kernel(slot_ref, slot_next_ref, wts_ref, x_ref, g_ref, b_ref, y_hbm, o_ref, ybuf, sem, *, alpha):
    i = pl.program_id(0)
    n = pl.num_programs(0)
    tm = x_ref.shape[0]
    cur = i % 2

    def issue(idx_ref, buf):
        def row(base, sub):
            for k in range(MOE_TOP_K):
                s = idx_ref[0, 0, MOE_TOP_K * (base + sub) + k]
                pltpu.make_async_copy(y_hbm.at[pl.ds(s, 1), :], ybuf.at[buf, k, pl.ds(base + sub, 1), :],
                                      sem.at[buf]).start(priority=k)

        _for_each_row(tm, row)

    @pl.when(i == 0)
    def _():
        issue(slot_ref, 0)

    @pl.when(i + 1 < n)
    def _():
        issue(slot_next_ref, 1 - cur)

    for k in range(MOE_TOP_K):
        pltpu.make_async_copy(y_hbm.at[pl.ds(0, tm), :], ybuf.at[cur, k], sem.at[cur]).wait()
    wts = wts_ref[...]
    y = wts[:, 0:1] * ybuf[cur, 0] + wts[:, 1:2] * ybuf[cur, 1]
    o_ref[...] = _layer_norm(alpha * x_ref[...] + y, g_ref[...], b_ref[...])


def _combine_postnorm(y_sorted, slot, wts, x, g, b, *, alpha, tm):
    n, d = x.shape
    nt = n // tm
    slot3 = slot.reshape(nt, 1, MOE_TOP_K * tm)
    row = lambda a: a.reshape(1, d).astype(F32)
    smem = lambda f: pl.BlockSpec((1, 1, MOE_TOP_K * tm), f, memory_space=pltpu.SMEM)
    return pl.pallas_call(
        functools.partial(_combine_kernel, alpha=alpha),
        grid=(nt,),
        in_specs=[smem(lambda i: (i, 0, 0)),
                  smem(lambda i: (jnp.minimum(i + 1, nt - 1), 0, 0)),
                  pl.BlockSpec((tm, V7X_LANES), lambda i: (i, 0)),
                  pl.BlockSpec((tm, d), lambda i: (i, 0)),
                  pl.BlockSpec((1, d), lambda i: (0, 0)),
                  pl.BlockSpec((1, d), lambda i: (0, 0)),
                  pl.BlockSpec(memory_space=pl.ANY)],
        out_specs=pl.BlockSpec((tm, d), lambda i: (i, 0)),
        out_shape=jax.ShapeDtypeStruct((n, d), F32),
        scratch_shapes=[pltpu.VMEM((2, MOE_TOP_K, tm, d), F32), pltpu.SemaphoreType.DMA((2,))],
        compiler_params=_cparams("arbitrary"),
        name="moe_combine",
    )(slot3, slot3, wts, x, row(g), row(b), y_sorted)


def _moe_layer(xs, p, layer, ln_g, ln_b, *, alpha, routed=None):
    w_gate, w_up, w_down = p['moe_w_gate'], p['moe_w_up'], p['moe_w_down']
    n_exp = w_gate.shape[1]
    d = xs[0].shape[-1]
    x2s = [x.reshape(-1, d) for x in xs]
    n_total = sum(x2.shape[0] for x2 in x2s)
    tm = 512 if n_total >= 8192 else V7X_BF16_SUBLANES
    if routed is None:
        routed = [_router(x2, *_router_params(p, layer)) for x2 in x2s]
    counts = jnp.zeros((1, V7X_LANES), jnp.int32)
    ranks = []
    for gi, (ids, _) in enumerate(routed):
        rank, counts = _expert_ranks(ids, counts)
        ranks.append(rank)
        if gi == 0:
            counts_first = counts[0, :n_exp]
    counts = counts[0, :n_exp]
    padded = ((counts + tm - 1) // tm) * tm
    ends = jnp.cumsum(padded)
    starts = ends - padded
    n_tiles = (MOE_TOP_K * n_total + n_exp * (tm - 1)) // tm
    n_valid = (ends[-1] // tm).astype(jnp.int32)
    tile_start = jnp.minimum(jnp.arange(n_tiles, dtype=jnp.int32), n_valid - 1) * tm
    tile_expert = jnp.minimum(jnp.sum((ends[None, :] <= tile_start[:, None]).astype(jnp.int32), axis=1), n_exp - 1)
    fill = jnp.stack([starts + (counts_first // tm) * tm, ends]).astype(jnp.int32)
    later_rows = MOE_TOP_K * (n_total - x2s[0].shape[0])
    fill_tiles = 1 + (later_rows + tm - 1) // tm
    x_sorted = None
    slots = []
    for x2, (ids, _), rank in zip(x2s, routed, ranks):
        choice = ids[:, :MOE_TOP_K]
        onehot = choice[:, :, None] == jnp.arange(n_exp, dtype=jnp.int32)[None, None, :]
        slot = jnp.sum(jnp.where(onehot, starts[None, None, :], 0), axis=-1) + rank[:, :MOE_TOP_K]
        x_sorted = _dispatch(x2, slot, fill, rows=n_tiles * tm, tile=tm, fill_tiles=fill_tiles, xs=x_sorted,
                             tm=_row_tile(x2.shape[0], 512))
        slots.append(slot)
    y_sorted = _expert_ffn(x_sorted, tile_expert, n_valid.reshape(1), w_gate, w_up, w_down, layer, tm=tm)
    return [_combine_postnorm(y_sorted, slot, wts, x2, ln_g, ln_b, alpha=alpha,
                              tm=_row_tile(x2.shape[0], 512)).reshape(x.shape)
            for x, x2, (_, wts), slot in zip(xs, x2s, routed, slots)]


def _trunks(groups, p):
    depth = p['ln_g'].shape[0]
    alpha = (2.0 * depth) ** 0.25
    xs = [grp[0] for grp in groups]
    outs = [([], [], [], [], []) for _ in groups]
    for layer in range(depth):
        i = layer // 2
        g, bta = p['ln_g'][layer], p['ln_b'][layer]
        rw, rb = _router_params(p, layer)
        wq, wo = p['mem_wq'][layer], p['mem_wo'][layer]
        routed = []
        for gi, (_, dn_s, dn_conv, ssm_re, ssm_im, cconv, mem_k, mem_v) in enumerate(groups):
            out_s, out_conv, out_re, out_im, out_cc = outs[gi]
            x = xs[gi]
            b, t, d = x.shape
            if t == 1:
                if layer % 2 == 0:
                    x, s_new, conv_new, re_new, im_new = _ab_layer(x, dn_s[i], dn_conv[i], ssm_re[i], ssm_im[i],
                                                                   p, i, g[0], bta[0], alpha=alpha)
                else:
                    x, cc_new = _conv_layer(x, cconv[i], p, i, g[0], bta[0], alpha=alpha)
                x = _mem_layer(x, mem_k, mem_v, layer, wq, wo, g[1], bta[1], alpha=alpha)
                routed.append(_router(x.reshape(b * t, d), rw, rb))
            else:
                tq = _row_tile(t, 1024)
                if layer % 2 == 0:
                    o, y_tm, (s_new, conv_new, re_new, im_new) = _ab_mixers(x, dn_s[i], dn_conv[i], ssm_re[i],
                                                                            ssm_im[i], p, i)
                    n_o, n_y = o.shape[-1], y_tm.shape[1] // b
                    w_out = p['ab_w_out'][i]
                    proj = [(o, pl.BlockSpec((None, tq, n_o), lambda bi, j: (bi, j, 0)), w_out[:n_o]),
                            (y_tm, pl.BlockSpec((tq, n_y), lambda bi, j: (j, bi)), w_out[n_o:])]
                    bias = None
                else:
                    hc, cc_new = _conv_mixer(x, cconv[i], p, i)
                    proj = [(hc.reshape(b, t, d), pl.BlockSpec((None, tq, d), lambda bi, j: (bi, j, 0)),
                             p['cc_w_pw2'][i])]
                    bias = p['cc_b_pw2'][i]
                n_mem = mem_k.shape[2]
                x, ids, wts = _proj_mem_router(
                    proj, bias, x, (g[0], bta[0]), mem_k.reshape(depth, b, n_mem, d),
                    mem_v.reshape(depth, b, n_mem, d), layer, wq, wo, (g[1], bta[1]), rw, rb, alpha=alpha, tq=tq)
                routed.append((ids, wts))
            if layer % 2 == 0:
                out_s.append(s_new)
                out_conv.append(conv_new)
                out_re.append(re_new)
                out_im.append(im_new)
            else:
                out_cc.append(cc_new)
            xs[gi] = x
        xs = _moe_layer(xs, p, layer, g[2], bta[2], alpha=alpha, routed=routed)
    return [(x,) + tuple(jnp.stack(o) for o in out) for x, out in zip(xs, outs)]


def kernel(x_prompt, x_sample, state_dn_s, state_dn_conv, state_ssm_re, state_ssm_im, state_cconv,
           cache_mem_k, cache_mem_v, mem_prompt, ab_w_in, dn_conv_w, dn_a_log, dn_dt_bias, dn_norm_g,
           ssm_lambda_re, ssm_lambda_im, ssm_log_dt, ssm_b_re, ssm_b_im, ssm_c_re, ssm_c_im, ssm_d,
           ssm_w_glu, ssm_b_glu, ab_w_out, cc_w_pw1, cc_b_pw1, cc_w_dw, cc_b_dw, cc_ln_g, cc_ln_b,
           cc_w_pw2, cc_b_pw2, mem_wq, mem_wk, mem_wv, mem_wo, ln_g, ln_b, moe_w_group, moe_b_group,
           moe_w_expert, moe_b_expert, moe_w_gate, moe_w_up, moe_w_down):
    p = dict(ab_w_in=ab_w_in, dn_conv_w=dn_conv_w, dn_a_log=dn_a_log, dn_dt_bias=dn_dt_bias,
             dn_norm_g=dn_norm_g, ssm_lambda_re=ssm_lambda_re, ssm_lambda_im=ssm_lambda_im,
             ssm_log_dt=ssm_log_dt, ssm_b_re=ssm_b_re, ssm_b_im=ssm_b_im, ssm_c_re=ssm_c_re,
             ssm_c_im=ssm_c_im, ssm_d=ssm_d, ssm_w_glu=ssm_w_glu, ssm_b_glu=ssm_b_glu, ab_w_out=ab_w_out,
             cc_w_pw1=cc_w_pw1, cc_b_pw1=cc_b_pw1, cc_w_dw=cc_w_dw, cc_b_dw=cc_b_dw, cc_ln_g=cc_ln_g,
             cc_ln_b=cc_ln_b, cc_w_pw2=cc_w_pw2, cc_b_pw2=cc_b_pw2, mem_wq=mem_wq, mem_wo=mem_wo,
             ln_g=ln_g, ln_b=ln_b, moe_w_group=moe_w_group, moe_b_group=moe_b_group,
             moe_w_expert=moe_w_expert, moe_b_expert=moe_b_expert, moe_w_gate=moe_w_gate,
             moe_w_up=moe_w_up, moe_w_down=moe_w_down)
    depth = ln_g.shape[0]
    n_ab = state_dn_s.shape[0]
    n_cc = state_cconv.shape[0]
    bsz, _, d = x_prompt.shape
    n_mem = mem_prompt.shape[1]
    hd = d // MEM_HEADS
    z_dn_s = jnp.zeros((n_ab, bsz) + state_dn_s.shape[2:], F32)
    z_dn_conv = jnp.zeros((n_ab, bsz) + state_dn_conv.shape[2:], F32)
    z_ssm = jnp.zeros((n_ab, bsz) + state_ssm_re.shape[2:], F32)
    z_cconv = jnp.zeros((n_cc, bsz) + state_cconv.shape[2:], F32)
    mem2 = mem_prompt.reshape(bsz * n_mem, d)
    p_mem_k = jnp.stack([_linear(mem2, mem_wk[l]) for l in range(depth)]).reshape(depth, bsz, n_mem, MEM_HEADS, hd)
    p_mem_v = jnp.stack([_linear(mem2, mem_wv[l]) for l in range(depth)]).reshape(depth, bsz, n_mem, MEM_HEADS, hd)
    (y_prompt, p_dn_s, p_dn_conv, p_ssm_re, p_ssm_im, p_cconv), \
        (y_sample, s_dn_s, s_dn_conv, s_ssm_re, s_ssm_im, s_cconv) = _trunks(
            [(x_prompt, z_dn_s, z_dn_conv, z_ssm, z_ssm, z_cconv, p_mem_k, p_mem_v),
             (x_sample, state_dn_s, state_dn_conv, state_ssm_re, state_ssm_im, state_cconv, cache_mem_k,
              cache_mem_v)], p)
    return (y_prompt, y_sample, p_dn_s, p_dn_conv, p_ssm_re, p_ssm_im, p_cconv, p_mem_k, p_mem_v,
            s_dn_s, s_dn_conv, s_ssm_re, s_ssm_im, s_cconv)
```

```python
import functools
import math

import jax
import jax.numpy as jnp
from jax import lax
from jax.experimental import pallas as pl
from jax.experimental.pallas import tpu as pltpu

F32 = jnp.float32
BF16 = jnp.bfloat16

DN_HEADS = 4
DN_DK = 128
DN_DV = 128
DN_CONV = 4
DN_CHUNK = 64
MEM_HEADS = 4
MOE_GROUPS = 4
MOE_EXPERTS_PER_GROUP = 8
MOE_TOP_K = 2
LN_EPS = 1e-5
RMS_EPS = 1e-6

V7X_LANES = 128
V7X_SUBLANES = 8
V7X_BF16_SUBLANES = 16
V7X_VMEM_LIMIT_BYTES = 52 * 1024 * 1024


def _cparams(*sem):
    return pltpu.CompilerParams(dimension_semantics=sem, vmem_limit_bytes=V7X_VMEM_LIMIT_BYTES)


def _bdot(a, b):
    return jnp.dot(a.astype(BF16), b.astype(BF16), preferred_element_type=F32)


def _bdot_nt(a, b):
    return lax.dot_general(a.astype(BF16), b.astype(BF16), (((1,), (1,)), ((), ())),
                           preferred_element_type=F32)


def _split3(a):
    hi = a.astype(BF16)
    r1 = a - hi.astype(F32)
    mid = r1.astype(BF16)
    lo = (r1 - mid.astype(F32)).astype(BF16)
    return hi, mid, lo


def _sigmoid(x):
    return 1.0 / (1.0 + jnp.exp(-x))


def _silu(x):
    return x * _sigmoid(x)


def _softplus(x):
    return jnp.maximum(x, 0.0) + jnp.log(1.0 + jnp.exp(-jnp.abs(x)))


def _layer_norm(v, g, b):
    mu = jnp.mean(v, axis=-1, keepdims=True)
    d = v - mu
    var = jnp.mean(d * d, axis=-1, keepdims=True)
    return d * lax.rsqrt(var + LN_EPS) * g + b


def _row_tile(n, pref):
    t = min(n, pref)
    assert n % t == 0, (n, t)
    return t


def _linear_kernel(x_ref, w_ref, b_ref, o_ref, *, glu):
    y = _bdot(x_ref[...], w_ref[...]) + b_ref[...]
    if glu:
        n = y.shape[-1] // 2
        y = y[:, :n] * _sigmoid(y[:, n:])
    o_ref[...] = y.astype(o_ref.dtype)


def _linear(x, w, bias=None, *, glu=False, tm=1024, out_dtype=F32):
    m, k = x.shape
    n = w.shape[1]
    if bias is None:
        bias = jnp.zeros((n,), F32)
    tm = _row_tile(m, tm)
    n_out = n // 2 if glu else n
    return pl.pallas_call(
        functools.partial(_linear_kernel, glu=glu),
        grid=(m // tm,),
        in_specs=[pl.BlockSpec((tm, k), lambda i: (i, 0)),
                  pl.BlockSpec((k, n), lambda i: (0, 0)),
                  pl.BlockSpec((1, n), lambda i: (0, 0))],
        out_specs=pl.BlockSpec((tm, n_out), lambda i: (i, 0)),
        out_shape=jax.ShapeDtypeStruct((m, n_out), out_dtype),
        compiler_params=_cparams("parallel"),
        name="linear",
    )(x, w.astype(BF16), bias.reshape(1, n).astype(F32))


def _linear_postnorm_kernel(h_ref, w_ref, b_ref, x_ref, g_ref, beta_ref, o_ref, *, alpha):
    h = _bdot(h_ref[...], w_ref[...]) + b_ref[...]
    o_ref[...] = _layer_norm(alpha * x_ref[...] + h, g_ref[...], beta_ref[...])


def _linear_postnorm(h_in, w, bias, x_res, g, beta, *, alpha, tm=512):
    m, k = h_in.shape
    d = w.shape[1]
    if bias is None:
        bias = jnp.zeros((d,), F32)
    tm = _row_tile(m, tm)
    row = lambda a: a.reshape(1, d).astype(F32)
    return pl.pallas_call(
        functools.partial(_linear_postnorm_kernel, alpha=alpha),
        grid=(m // tm,),
        in_specs=[pl.BlockSpec((tm, k), lambda i: (i, 0)),
                  pl.BlockSpec((k, d), lambda i: (0, 0)),
                  pl.BlockSpec((1, d), lambda i: (0, 0)),
                  pl.BlockSpec((tm, d), lambda i: (i, 0)),
                  pl.BlockSpec((1, d), lambda i: (0, 0)),
                  pl.BlockSpec((1, d), lambda i: (0, 0))],
        out_specs=pl.BlockSpec((tm, d), lambda i: (i, 0)),
        out_shape=jax.ShapeDtypeStruct((m, d), F32),
        compiler_params=_cparams("parallel"),
        name="linear_postnorm",
    )(h_in, w.astype(BF16), row(bias), x_res, row(g), row(beta))


def _ab_in_kernel(x_ref, wqkvz_ref, wba_ref, wu_ref, qkv_ref, z_ref, ba_ref, u_ref, *, n_qkv):
    x = x_ref[...].astype(BF16)
    y = jnp.dot(x, wqkvz_ref[...], preferred_element_type=F32)
    qkv_ref[...] = y[:, :n_qkv]
    z_ref[...] = y[:, n_qkv:]
    ba_ref[...] = jnp.dot(x, wba_ref[...], preferred_element_type=F32)
    u_ref[...] = jnp.dot(x, wu_ref[...], preferred_element_type=F32)


def _ab_in_proj(x, w_in, *, tm=1024):
    bx, tx, d = x.shape
    n_key = DN_HEADS * DN_DK
    n_val = DN_HEADS * DN_DV
    n_qkv = 2 * n_key + n_val
    off_beta = n_qkv + n_val
    off_u = off_beta + 2 * DN_HEADS
    n_u = w_in.shape[1] - off_u
    w_qkvz = w_in[:, :off_beta].astype(BF16)
    w_ba = jnp.pad(w_in[:, off_beta:off_u], ((0, 0), (0, V7X_LANES - 2 * DN_HEADS))).astype(BF16)
    w_u = w_in[:, off_u:].astype(BF16)
    tm = _row_tile(tx, tm)
    full = lambda a: pl.BlockSpec(a.shape, lambda b, i: (0, 0))
    return pl.pallas_call(
        functools.partial(_ab_in_kernel, n_qkv=n_qkv),
        grid=(bx, tx // tm),
        in_specs=[pl.BlockSpec((None, tm, d), lambda b, i: (b, i, 0)),
                  full(w_qkvz), full(w_ba), full(w_u)],
        out_specs=[pl.BlockSpec((None, tm, n_qkv), lambda b, i: (b, i, 0)),
                   pl.BlockSpec((None, tm, n_val), lambda b, i: (b, i, 0)),
                   pl.BlockSpec((None, tm, V7X_LANES), lambda b, i: (b, i, 0)),
                   pl.BlockSpec((tm, n_u), lambda b, i: (i, b))],
        out_shape=[jax.ShapeDtypeStruct((bx, tx, n_qkv), F32),
                   jax.ShapeDtypeStruct((bx, tx, n_val), F32),
                   jax.ShapeDtypeStruct((bx, tx, V7X_LANES), F32),
                   jax.ShapeDtypeStruct((tx, bx * n_u), F32)],
        compiler_params=_cparams("parallel", "parallel"),
        name="ab_in_proj",
    )(x, w_qkvz, w_ba, w_u)


def _ab_out_kernel(o_ref, y_ref, wt_ref, wb_ref, x_ref, g_ref, beta_ref, out_ref, *, alpha):
    h = _bdot(o_ref[...], wt_ref[...]) + _bdot(y_ref[...], wb_ref[...])
    out_ref[...] = _layer_norm(alpha * x_ref[...] + h, g_ref[...], beta_ref[...])


def _ab_out_proj(o, y_tm, w_out, x, g, beta, *, alpha, tm=512):
    bx, tx, d = x.shape
    n_o = o.shape[-1]
    n_y = y_tm.shape[1] // bx
    tm = _row_tile(tx, tm)
    wt = w_out[:n_o].astype(BF16)
    wb = w_out[n_o:].astype(BF16)
    row = lambda a: a.reshape(1, d).astype(F32)
    full = lambda a: pl.BlockSpec(a.shape, lambda b, i: (0, 0))
    return pl.pallas_call(
        functools.partial(_ab_out_kernel, alpha=alpha),
        grid=(bx, tx // tm),
        in_specs=[pl.BlockSpec((None, tm, n_o), lambda b, i: (b, i, 0)),
                  pl.BlockSpec((tm, n_y), lambda b, i: (i, b)),
                  full(wt), full(wb),
                  pl.BlockSpec((None, tm, d), lambda b, i: (b, i, 0)),
                  pl.BlockSpec((1, d), lambda b, i: (0, 0)),
                  pl.BlockSpec((1, d), lambda b, i: (0, 0))],
        out_specs=pl.BlockSpec((None, tm, d), lambda b, i: (b, i, 0)),
        out_shape=jax.ShapeDtypeStruct((bx, tx, d), F32),
        compiler_params=_cparams("parallel", "parallel"),
        name="ab_out_proj",
    )(o, y_tm, wt, wb, x, row(g), row(beta))


_DN_HALO = V7X_SUBLANES


def _dn_seq_kernel(qkv_ref, ba_ref, z_ref, cbuf_ref, s0_ref, cw_ref, alog_ref, dtb_ref, ng_ref,
                   o_ref, s_out_ref, cbuf_out_ref, xp_ref, s_ref, *, c, bb):
    j = pl.program_id(1)
    nj = pl.num_programs(1)
    hist = DN_CONV - 1
    n_key = DN_HEADS * DN_DK
    units = [(bi, h) for bi in range(bb) for h in range(DN_HEADS)]
    each = lambda f: {u: f(u) for u in units}

    @pl.when(j == 0)
    def _():
        xp_ref[:, 0:_DN_HALO - hist, :] = jnp.zeros((bb, _DN_HALO - hist, xp_ref.shape[-1]), F32)
        xp_ref[:, _DN_HALO - hist:_DN_HALO, :] = cbuf_ref[...]
        s_ref[...] = s0_ref[...]

    ys, tails = [], []
    for bi in range(bb):
        xp_ref[bi, _DN_HALO:_DN_HALO + c, :] = qkv_ref[bi]
        y = xp_ref[bi, pl.ds(_DN_HALO - hist, c), :] * cw_ref[0:1, :]
        for tap in range(1, DN_CONV):
            y = y + xp_ref[bi, pl.ds(_DN_HALO - hist + tap, c), :] * cw_ref[tap:tap + 1, :]
        tail = xp_ref[bi, pl.ds(_DN_HALO + c - hist, hist), :]
        xp_ref[bi, _DN_HALO - hist:_DN_HALO, :] = tail
        tails.append(tail)
        ys.append(_silu(y))

    ii = lax.broadcasted_iota(jnp.int32, (c, c), 0)
    jj = lax.broadcasted_iota(jnp.int32, (c, c), 1)
    incl = ii >= jj
    strict = ii > jj
    tril = jnp.where(incl, 1.0, 0.0).astype(BF16)
    eye = jnp.where(ii == jj, 1.0, 0.0)
    n_double = int(math.log2(c))
    vsl = lambda h: slice(h * DN_DV, (h + 1) * DN_DV)

    def unit_inputs(u):
        bi, h = u
        lo, hi = h * DN_DK, (h + 1) * DN_DK
        y = ys[bi]
        qh = y[:, lo:hi]
        kh = y[:, n_key + lo:n_key + hi]
        vh = y[:, 2 * n_key + h * DN_DV:2 * n_key + (h + 1) * DN_DV]
        qh = qh * lax.rsqrt(jnp.sum(qh * qh, -1, keepdims=True) + RMS_EPS) * (DN_DK ** -0.5)
        kh = kh * lax.rsqrt(jnp.sum(kh * kh, -1, keepdims=True) + RMS_EPS)
        beta = _sigmoid(ba_ref[bi, :, h:h + 1])
        a_logit = ba_ref[bi, :, DN_HEADS + h:DN_HEADS + h + 1]
        g = -jnp.exp(alog_ref[0:1, lo:hi]) * _softplus(a_logit + dtb_ref[0:1, lo:hi])
        return qh, kh, vh, beta, g

    inp = each(unit_inputs)
    q = each(lambda u: inp[u][0])
    k = each(lambda u: inp[u][1])
    beta = each(lambda u: inp[u][3])
    parts = [p for u in units for p in _split3(inp[u][4])]
    gc_all = jnp.dot(tril, jnp.concatenate(parts, axis=1), preferred_element_type=F32)
    lanes = lambda i: slice(i * DN_DK, (i + 1) * DN_DK)
    gc = {u: gc_all[:, lanes(3 * i)] + gc_all[:, lanes(3 * i + 1)] + gc_all[:, lanes(3 * i + 2)]
          for i, u in enumerate(units)}

    def unit_decay(u):
        gc_row = jnp.transpose(gc[u])[0:1, :]
        diff = gc[u][:, :c] - gc_row
        return jnp.where(incl, jnp.exp(jnp.where(incl, diff, 0.0)), 0.0)

    decay = each(unit_decay)
    eg = each(lambda u: jnp.exp(gc[u]))
    kdec = each(lambda u: k[u] * jnp.exp(gc[u][c - 1:c, :] - gc[u]))
    kb = each(lambda u: k[u] * beta[u])
    akq = each(lambda u: _bdot_nt(jnp.concatenate([kb[u], q[u]], axis=0), k[u]))
    lmat = each(lambda u: jnp.where(strict, akq[u][:c] * decay[u], 0.0))
    qk = each(lambda u: akq[u][c:] * decay[u])
    tmat = each(lambda u: eye - lmat[u])
    lpow = each(lambda u: _bdot(lmat[u], lmat[u]))
    for step in range(1, n_double):
        if step < n_double - 1:
            prod = each(lambda u: _bdot(jnp.concatenate([lpow[u], tmat[u]], axis=0), lpow[u]))
            lpow = each(lambda u: prod[u][:c])
            tmat = each(lambda u: tmat[u] + prod[u][c:])
        else:
            prod = each(lambda u: _bdot(tmat[u], lpow[u]))
            tmat = each(lambda u: tmat[u] + prod[u])
    uw = each(lambda u: _bdot(tmat[u], jnp.concatenate([inp[u][2] * beta[u], kb[u] * eg[u]], axis=1)))
    s_old = each(lambda u: s_ref[u[0], u[1]])
    wq_s = each(lambda u: _bdot(jnp.concatenate([uw[u][:, DN_DV:], q[u] * eg[u]], axis=0), s_old[u]))
    v_new = each(lambda u: uw[u][:, :DN_DV] - wq_s[u][:c])
    o = each(lambda u: wq_s[u][c:] + _bdot(qk[u], v_new[u]))
    s_new = each(lambda u: s_old[u] * jnp.exp(gc[u][c - 1:c, :]) + _bdot(jnp.transpose(kdec[u]), v_new[u]))
    for u in units:
        bi, h = u
        s_ref[bi, h] = s_new[u]
        on = o[u] * lax.rsqrt(jnp.mean(o[u] * o[u], -1, keepdims=True) + RMS_EPS) * ng_ref[...]
        o_ref[bi, :, vsl(h)] = on * _silu(z_ref[bi, :, vsl(h)])

    @pl.when(j == nj - 1)
    def _():
        s_out_ref[...] = s_ref[...]
        for bi in range(bb):
            cbuf_out_ref[bi] = tails[bi]


def _dn_step_kernel(qkv_ref, ba_ref, z_ref, cbuf_ref, s0_ref, cw_ref, alog_ref, dtb_ref, ng_ref,
                    o_ref, s_out_ref, cbuf_out_ref, *, bb):
    hist = DN_CONV - 1
    n_key = DN_HEADS * DN_DK
    x = qkv_ref[...]
    y = x * cw_ref[hist:hist + 1, :]
    for tap in range(hist):
        y = y + cbuf_ref[:, tap, :] * cw_ref[tap:tap + 1, :]
    for tap in range(1, hist):
        cbuf_out_ref[:, tap - 1, :] = cbuf_ref[:, tap, :]
    cbuf_out_ref[:, hist - 1, :] = x
    y = _silu(y)
    ba = ba_ref[...]
    z = z_ref[...]
    row8 = lax.broadcasted_iota(jnp.int32, (V7X_SUBLANES, DN_DK), 0)
    row16 = lax.broadcasted_iota(jnp.int32, (V7X_BF16_SUBLANES, DN_DK), 0)
    heads = range(DN_HEADS)
    units = [(h, r) for h in heads for r in range(bb)]
    each = lambda f: {u: f(u) for u in units}
    one = lambda a, r: a[r:r + 1]

    def head_inputs(h):
        lo, hi = h * DN_DK, (h + 1) * DN_DK
        qh = y[:, lo:hi]
        kh = y[:, n_key + lo:n_key + hi]
        vh = y[:, 2 * n_key + h * DN_DV:2 * n_key + (h + 1) * DN_DV]
        qh = qh * lax.rsqrt(jnp.sum(qh * qh, -1, keepdims=True) + RMS_EPS) * (DN_DK ** -0.5)
        kh = kh * lax.rsqrt(jnp.sum(kh * kh, -1, keepdims=True) + RMS_EPS)
        beta = _sigmoid(ba[:, h:h + 1])
        a_logit = ba[:, DN_HEADS + h:DN_HEADS + h + 1]
        g = -jnp.exp(alog_ref[0:1, lo:hi]) * _softplus(a_logit + dtb_ref[0:1, lo:hi])
        eg = jnp.exp(g)
        return dict(k=kh, eg=eg, w=kh * beta * eg, qg=qh * eg, u=vh * beta, qk=jnp.sum(qh * kh, -1, keepdims=True))

    hd = [head_inputs(h) for h in heads]
    s_old = each(lambda u: s0_ref[u[1], u[0]])
    lhs = each(lambda u: jnp.where(row8 == 0, one(hd[u[0]]['w'], u[1]),
                                   jnp.where(row8 == 1, one(hd[u[0]]['qg'], u[1]), 0.0)))
    ws_qs = each(lambda u: _bdot(lhs[u], s_old[u]))
    v_new = each(lambda u: one(hd[u[0]]['u'], u[1]) - ws_qs[u][0:1])
    o_row = each(lambda u: ws_qs[u][1:2] + one(hd[u[0]]['qk'], u[1]) * v_new[u])
    k16 = each(lambda u: jnp.where(row16 == 0, one(hd[u[0]]['k'], u[1]), 0.0))
    v16 = each(lambda u: jnp.where(row16 == 0, v_new[u], 0.0))
    upd = each(lambda u: _bdot(jnp.transpose(k16[u]), v16[u]))
    for u in units:
        s_out_ref[u[1], u[0]] = s_old[u] * one(hd[u[0]]['eg'], u[1]) + upd[u]
    for h in heads:
        o = jnp.concatenate([o_row[(h, r)] for r in range(bb)], axis=0)
        o = o * lax.rsqrt(jnp.mean(o * o, -1, keepdims=True) + RMS_EPS) * ng_ref[...]
        o_ref[:, h * DN_DV:(h + 1) * DN_DV] = o * _silu(z[:, h * DN_DV:(h + 1) * DN_DV])


def _deltanet(qkv, ba, z, conv_buf, s0, conv_w, a_log, dt_bias, norm_g):
    b, t, n_qkv = qkv.shape
    n_val = DN_HEADS * DN_DV
    hist = DN_CONV - 1
    rep = lambda a: jnp.repeat(a.astype(F32), DN_DK).reshape(1, DN_HEADS * DN_DK)
    cw = conv_w.astype(F32)
    consts = (cw, rep(a_log), rep(dt_bias), norm_g.reshape(1, DN_DV).astype(F32))
    state_shape = (DN_HEADS, DN_DK, DN_DV)
    out_shape = [jax.ShapeDtypeStruct((b, t, n_val), F32),
                 jax.ShapeDtypeStruct((b,) + state_shape, F32),
                 jax.ShapeDtypeStruct((b, hist, n_qkv), F32)]
    if t == 1:
        bb = _row_tile(b, V7X_SUBLANES)
        const = lambda a: pl.BlockSpec(a.shape, lambda i: (0,) * a.ndim)
        o, s_new, cbuf_new = pl.pallas_call(
            functools.partial(_dn_step_kernel, bb=bb),
            grid=(b // bb,),
            in_specs=[pl.BlockSpec((bb, n_qkv), lambda i: (i, 0)),
                      pl.BlockSpec((bb, V7X_LANES), lambda i: (i, 0)),
                      pl.BlockSpec((bb, n_val), lambda i: (i, 0)),
                      pl.BlockSpec((bb, hist, n_qkv), lambda i: (i, 0, 0)),
                      pl.BlockSpec((bb,) + state_shape, lambda i: (i, 0, 0, 0))] + [const(a) for a in consts],
            out_specs=[pl.BlockSpec((bb, n_val), lambda i: (i, 0)),
                       pl.BlockSpec((bb,) + state_shape, lambda i: (i, 0, 0, 0)),
                       pl.BlockSpec((bb, hist, n_qkv), lambda i: (i, 0, 0))],
            out_shape=[jax.ShapeDtypeStruct((b, n_val), F32)] + out_shape[1:],
            compiler_params=_cparams("parallel"),
            name="deltanet_step",
        )(qkv.reshape(b, n_qkv), ba.reshape(b, V7X_LANES), z.reshape(b, n_val), conv_buf.astype(F32),
          s0.astype(F32), *consts)
        return o.reshape(b, 1, n_val), s_new, cbuf_new
    c = DN_CHUNK
    assert t % c == 0 and c >= hist
    bb = _row_tile(b, 4)
    const = lambda a: pl.BlockSpec(a.shape, lambda i, j: (0,) * a.ndim)
    return pl.pallas_call(
        functools.partial(_dn_seq_kernel, c=c, bb=bb),
        grid=(b // bb, t // c),
        in_specs=[pl.BlockSpec((bb, c, n_qkv), lambda i, j: (i, j, 0)),
                  pl.BlockSpec((bb, c, V7X_LANES), lambda i, j: (i, j, 0)),
                  pl.BlockSpec((bb, c, n_val), lambda i, j: (i, j, 0)),
                  pl.BlockSpec((bb, hist, n_qkv), lambda i, j: (i, 0, 0)),
                  pl.BlockSpec((bb,) + state_shape, lambda i, j: (i, 0, 0, 0))] + [const(a) for a in consts],
        out_specs=[pl.BlockSpec((bb, c, n_val), lambda i, j: (i, j, 0)),
                   pl.BlockSpec((bb,) + state_shape, lambda i, j: (i, 0, 0, 0)),
                   pl.BlockSpec((bb, hist, n_qkv), lambda i, j: (i, 0, 0))],
        out_shape=out_shape,
        scratch_shapes=[pltpu.VMEM((bb, _DN_HALO + c, n_qkv), F32),
                        pltpu.VMEM((bb,) + state_shape, F32)],
        compiler_params=_cparams("parallel", "arbitrary"),
        name="deltanet_seq",
    )(qkv, ba, z, conv_buf.astype(F32), s0.astype(F32), *consts)


def _s5_param_kernel(lre_ref, lim_ref, ldt_ref, lbre_ref, lbim_ref, fre_ref, fim_ref):
    lam_re = lre_ref[...]
    lam_im = lim_ref[...]
    dt = jnp.exp(ldt_ref[...])
    mag = jnp.exp(lam_re * dt)
    ang = lam_im * dt
    lb_re = mag * jnp.cos(ang)
    lb_im = mag * jnp.sin(ang)
    den = lam_re * lam_re + lam_im * lam_im
    lbre_ref[...] = lb_re
    lbim_ref[...] = lb_im
    fre_ref[...] = ((lb_re - 1.0) * lam_re + lb_im * lam_im) / den
    fim_ref[...] = (lb_im * lam_re - (lb_re - 1.0) * lam_im) / den


def _s5_discretize(lam_re, lam_im, log_dt):
    g, n = lam_re.shape
    ldt = jnp.broadcast_to(log_dt.astype(F32)[:, None], (g, n))
    shp = jax.ShapeDtypeStruct((g, n), F32)
    return pl.pallas_call(_s5_param_kernel, out_shape=[shp] * 4, name="s5_discretize")(
        lam_re.astype(F32), lam_im.astype(F32), ldt)


def _s5_kernel(u_ref, h0re_ref, h0im_ref, lbre_ref, lbim_ref, bre_ref, bim_ref, cre_ref, cim_ref,
               d_ref, wglu_ref, bglu_ref, y_ref, hre_out_ref, him_out_ref,
               sre_ref, sim_ref, cre_s, cim_s, *, tb, bb, lane_chunk):
    j = pl.program_id(1)
    nj = pl.num_programs(1)
    n_ch = u_ref.shape[-1]
    n_st = sre_ref.shape[-1]
    halves = bre_ref.shape[0]
    ch_h = n_ch // halves
    st_h = n_st // halves

    @pl.when(j == 0)
    def _():
        cre_s[...] = h0re_ref[...]
        cim_s[...] = h0im_ref[...]

    u = u_ref[...].reshape(tb * bb, n_ch)
    ub = u.astype(BF16)
    for hf in range(halves):
        uh = ub[:, hf * ch_h:(hf + 1) * ch_h]
        sre_ref[:, hf * st_h:(hf + 1) * st_h] = jnp.dot(uh, bre_ref[hf], preferred_element_type=F32)
        sim_ref[:, hf * st_h:(hf + 1) * st_h] = jnp.dot(uh, bim_ref[hf], preferred_element_type=F32)

    for c0 in range(0, n_st, lane_chunk):
        cs = slice(c0, c0 + lane_chunk)
        lr = jnp.broadcast_to(lbre_ref[0:1, cs], (bb, lane_chunk))
        li = jnp.broadcast_to(lbim_ref[0:1, cs], (bb, lane_chunk))

        def body(t, carry, cs=cs, lr=lr, li=li):
            hr, hi = carry
            r = pl.multiple_of(t * bb, bb)
            nr = lr * hr - li * hi + sre_ref[pl.ds(r, bb), cs]
            ni = lr * hi + li * hr + sim_ref[pl.ds(r, bb), cs]
            sre_ref[pl.ds(r, bb), cs] = nr
            sim_ref[pl.ds(r, bb), cs] = ni
            return nr, ni

        hr, hi = lax.fori_loop(0, tb, body, (cre_s[:, cs], cim_s[:, cs]))
        cre_s[:, cs] = hr
        cim_s[:, cs] = hi

    ys = []
    for hf in range(halves):
        hre = sre_ref[:, hf * st_h:(hf + 1) * st_h].astype(BF16)
        him = sim_ref[:, hf * st_h:(hf + 1) * st_h].astype(BF16)
        ys.append(jnp.dot(hre, cre_ref[hf], preferred_element_type=F32)
                  - jnp.dot(him, cim_ref[hf], preferred_element_type=F32))
    y = jnp.concatenate(ys, axis=1) + d_ref[...] * u
    y = jax.nn.gelu(y)
    y = y * _sigmoid(_bdot(y, wglu_ref[...]) + bglu_ref[...])
    y_ref[...] = y.reshape(tb, bb, n_ch)

    @pl.when(j == nj - 1)
    def _():
        hre_out_ref[...] = cre_s[...]
        him_out_ref[...] = cim_s[...]


def _s5(u_tm, h0_re, h0_im, lam_re, lam_im, log_dt, b_re, b_im, c_re, c_im, d_skip, w_glu, b_glu, *, halves=2):
    t, b, n_ch = u_tm.shape
    g, n, p = b_re.shape
    n_st = g * n
    lb_re, lb_im, f_re, f_im = _s5_discretize(lam_re, lam_im, log_dt)
    b_re = b_re.astype(F32)
    b_im = b_im.astype(F32)
    bb_re = f_re[..., None] * b_re - f_im[..., None] * b_im
    bb_im = f_re[..., None] * b_im + f_im[..., None] * b_re
    gh = g // halves

    def in_blocks(a):
        a = a.reshape(halves, gh, n, p)
        eye = jnp.eye(gh, dtype=F32)
        return jnp.einsum('hgnp,gk->hgpkn', a, eye).reshape(halves, gh * p, gh * n).astype(BF16)

    def out_blocks(a):
        a = a.astype(F32).reshape(halves, gh, p, n)
        eye = jnp.eye(gh, dtype=F32)
        return jnp.einsum('hgpn,gk->hgnkp', a, eye).reshape(halves, gh * n, gh * p).astype(BF16)

    bre_m, bim_m = in_blocks(bb_re), in_blocks(bb_im)
    cre_m, cim_m = out_blocks(c_re), out_blocks(c_im)
    bb = b if b <= 128 else 128
    assert b % bb == 0
    tb = _row_tile(t, max(1, 512 // bb))
    lane_chunk = max(V7X_LANES, min(n_st, 8192 // bb))
    const = lambda a: pl.BlockSpec(a.shape, lambda i, j: (0,) * a.ndim)
    row = lambda a, m: a.reshape(1, m).astype(F32)
    args = (u_tm, h0_re.reshape(b, n_st).astype(F32), h0_im.reshape(b, n_st).astype(F32),
            row(lb_re, n_st), row(lb_im, n_st), bre_m, bim_m, cre_m, cim_m,
            row(d_skip, n_ch), w_glu.astype(BF16), row(b_glu, n_ch))
    y, hre, him = pl.pallas_call(
        functools.partial(_s5_kernel, tb=tb, bb=bb, lane_chunk=lane_chunk),
        grid=(b // bb, t // tb),
        in_specs=[pl.BlockSpec((tb, bb, n_ch), lambda i, j: (j, i, 0)),
                  pl.BlockSpec((bb, n_st), lambda i, j: (i, 0)),
                  pl.BlockSpec((bb, n_st), lambda i, j: (i, 0))] + [const(a) for a in args[3:]],
        out_specs=[pl.BlockSpec((tb, bb, n_ch), lambda i, j: (j, i, 0)),
                   pl.BlockSpec((bb, n_st), lambda i, j: (i, 0)),
                   pl.BlockSpec((bb, n_st), lambda i, j: (i, 0))],
        out_shape=[jax.ShapeDtypeStruct((t, b, n_ch), F32),
                   jax.ShapeDtypeStruct((b, n_st), F32),
                   jax.ShapeDtypeStruct((b, n_st), F32)],
        scratch_shapes=[pltpu.VMEM((tb * bb, n_st), F32), pltpu.VMEM((tb * bb, n_st), F32),
                        pltpu.VMEM((bb, n_st), F32), pltpu.VMEM((bb, n_st), F32)],
        compiler_params=_cparams("parallel", "arbitrary"),
        name="s5",
    )(*args)
    return y, hre.reshape(b, g, n), him.reshape(b, g, n)


def _ab_mixers(x, dn_s, dn_conv, ssm_re, ssm_im, p, i):
    b, t, d = x.shape
    xr = x if t > 1 else x.reshape(1, b, d)
    bx, tx, _ = xr.shape
    qkv, z, ba, u_tm = _ab_in_proj(xr, p['ab_w_in'][i])
    n_ch = u_tm.shape[1] // bx
    shp = lambda a: a.reshape(b, t, a.shape[-1])
    o, s_new, conv_new = _deltanet(shp(qkv), shp(ba), shp(z), dn_conv, dn_s, p['dn_conv_w'][i],
                                   p['dn_a_log'][i], p['dn_dt_bias'][i], p['dn_norm_g'][i])
    y_tm, h_re, h_im = _s5(u_tm.reshape(t, b, n_ch), ssm_re, ssm_im, p['ssm_lambda_re'][i],
                           p['ssm_lambda_im'][i], p['ssm_log_dt'][i], p['ssm_b_re'][i], p['ssm_b_im'][i],
                           p['ssm_c_re'][i], p['ssm_c_im'][i], p['ssm_d'][i], p['ssm_w_glu'][i],
                           p['ssm_b_glu'][i])
    return o.reshape(bx, tx, -1), y_tm.reshape(tx, bx * n_ch), (s_new, conv_new, h_re, h_im)


def _ab_layer(x, dn_s, dn_conv, ssm_re, ssm_im, p, i, ln_g, ln_b, *, alpha):
    b, t, d = x.shape
    o, y_tm, states = _ab_mixers(x, dn_s, dn_conv, ssm_re, ssm_im, p, i)
    xr = x.reshape(o.shape[0], o.shape[1], d)
    x_new = _ab_out_proj(o, y_tm, p['ab_w_out'][i], xr, ln_g, ln_b, alpha=alpha)
    return (x_new.reshape(b, t, d),) + states


_CC_HALO = 32
_CC_ROW_TILES = 8


def _cconv_seq_kernel(h_ref, buf_ref, w_ref, bdw_ref, g_ref, b_ref, o_ref, xp_ref, xs_ref, acc_ref, *, tt, width):
    j = pl.program_id(1)
    hist = width - 1
    d = h_ref.shape[-1]
    rows = V7X_SUBLANES

    @pl.when(j == 0)
    def _():
        xp_ref[0:_CC_HALO - hist, :] = jnp.zeros((_CC_HALO - hist, d), F32)
        xp_ref[_CC_HALO - hist:_CC_HALO, :] = buf_ref[...]

    xp_ref[_CC_HALO:_CC_HALO + tt, :] = h_ref[...]
    n_shift = xs_ref.shape[1]
    for s in range(1, rows):
        xs_ref[s - 1] = xp_ref[pl.ds(s, n_shift), :]
    base = _CC_HALO - hist
    for c0 in range(0, d, V7X_LANES):
        cs = slice(c0, c0 + V7X_LANES)
        taps = [jnp.broadcast_to(w_ref[k:k + 1, cs], (rows, V7X_LANES)) for k in range(width)]

        def body(i, carry, taps=taps, cs=cs):
            r = pl.multiple_of(i * (rows * _CC_ROW_TILES), rows * _CC_ROW_TILES)
            accs = [None] * _CC_ROW_TILES
            for s in range(rows):
                ks = [k for k in range(width) if (base + k) % rows == s]
                if not ks:
                    continue
                n_rows = rows * (_CC_ROW_TILES + max((base + k) // rows for k in ks))
                big = xp_ref[pl.ds(r, n_rows), cs] if s == 0 else xs_ref[s - 1, pl.ds(r, n_rows), cs]
                for k in ks:
                    a = (base + k) // rows
                    for sub in range(_CC_ROW_TILES):
                        term = big[(sub + a) * rows:(sub + a + 1) * rows] * taps[k]
                        accs[sub] = term if accs[sub] is None else accs[sub] + term
            for sub in range(_CC_ROW_TILES):
                acc_ref[pl.ds(r + sub * rows, rows), cs] = accs[sub]
            return carry

        lax.fori_loop(0, tt // (rows * _CC_ROW_TILES), body, 0)
    xp_ref[0:_CC_HALO, :] = xp_ref[tt:tt + _CC_HALO, :]
    y = _layer_norm(acc_ref[...] + bdw_ref[...], g_ref[...], b_ref[...])
    o_ref[...] = _silu(y)


def _cconv_seq(h, buf, w_dw, b_dw, ln_g, ln_b, *, tt=512):
    b, t, d = h.shape
    width = w_dw.shape[0]
    tt = _row_tile(t, tt)
    assert tt >= _CC_HALO and width - 1 <= _CC_HALO
    row = lambda a: a.reshape(1, d).astype(F32)
    const = lambda a: pl.BlockSpec(a.shape, lambda i, j: (0, 0))
    args = (w_dw.astype(F32), row(b_dw), row(ln_g), row(ln_b))
    return pl.pallas_call(
        functools.partial(_cconv_seq_kernel, tt=tt, width=width),
        grid=(b, t // tt),
        in_specs=[pl.BlockSpec((None, tt, d), lambda i, j: (i, j, 0)),
                  pl.BlockSpec((None, width - 1, d), lambda i, j: (i, 0, 0))] + [const(a) for a in args],
        out_specs=pl.BlockSpec((None, tt, d), lambda i, j: (i, j, 0)),
        out_shape=jax.ShapeDtypeStruct((b, t, d), F32),
        scratch_shapes=[pltpu.VMEM((_CC_HALO + tt, d), F32),
                        pltpu.VMEM((V7X_SUBLANES - 1, _CC_HALO + tt - V7X_SUBLANES, d), F32),
                        pltpu.VMEM((tt, d), F32)],
        compiler_params=_cparams("parallel", "arbitrary"),
        name="cconv_seq",
    )(h, buf.astype(F32), *args)


def _cconv_step_kernel(h_ref, buf_ref, w_ref, bdw_ref, g_ref, b_ref, o_ref, *, width):
    hist = width - 1
    acc = jnp.sum(buf_ref[...] * w_ref[0:hist, :][None], axis=1) + h_ref[...] * w_ref[hist:width, :]
    o_ref[...] = _silu(_layer_norm(acc + bdw_ref[...], g_ref[...], b_ref[...]))


def _cconv_step(h, buf, w_dw, b_dw, ln_g, ln_b, *, bb=8):
    b, d = h.shape
    width = w_dw.shape[0]
    bb = _row_tile(b, bb)
    row = lambda a: a.reshape(1, d).astype(F32)
    const = lambda a: pl.BlockSpec(a.shape, lambda i: (0, 0))
    args = (w_dw.astype(F32), row(b_dw), row(ln_g), row(ln_b))
    return pl.pallas_call(
        functools.partial(_cconv_step_kernel, width=width),
        grid=(b // bb,),
        in_specs=[pl.BlockSpec((bb, d), lambda i: (i, 0)),
                  pl.BlockSpec((bb, width - 1, d), lambda i: (i, 0, 0))] + [const(a) for a in args],
        out_specs=pl.BlockSpec((bb, d), lambda i: (i, 0)),
        out_shape=jax.ShapeDtypeStruct((b, d), F32),
        compiler_params=_cparams("parallel"),
        name="cconv_step",
    )(h, buf.astype(F32), *args)


def _conv_mixer(x, buf, p, i):
    b, t, d = x.shape
    h = _linear(x.reshape(b * t, d), p['cc_w_pw1'][i], p['cc_b_pw1'][i], glu=True)
    args = (p['cc_w_dw'][i], p['cc_b_dw'][i], p['cc_ln_g'][i], p['cc_ln_b'][i])
    if t == 1:
        hc = _cconv_step(h, buf, *args)
    else:
        hc = _cconv_seq(h.reshape(b, t, d), buf, *args).reshape(b * t, d)
    new_buf = jnp.concatenate([buf.astype(F32), h.reshape(b, t, d)], axis=1)[:, t:]
    return hc, new_buf


def _conv_layer(x, buf, p, i, ln_g, ln_b, *, alpha):
    b, t, d = x.shape
    hc, new_buf = _conv_mixer(x, buf, p, i)
    x_new = _linear_postnorm(hc, p['cc_w_pw2'][i], p['cc_b_pw2'][i], x.reshape(b * t, d), ln_g, ln_b, alpha=alpha)
    return x_new.reshape(b, t, d), new_buf


_PMR_CHUNKS = 4


def _proj_mem_router_kernel(*refs, n_proj, alpha, heads, groups, per_group):
    proj = refs[:2 * n_proj]
    (bias_ref, xres_ref, g0_ref, b0_ref, wq_ref, k_ref, v_ref, wo_ref, g1_ref, b1_ref, rw_ref, rb_ref,
     x_out_ref, ids_ref, wts_ref) = refs[2 * n_proj:]
    tq, d = xres_ref.shape
    hd = d // heads
    chunk = tq // _PMR_CHUNKS
    rows = [pl.ds(c * chunk, chunk) for c in range(_PMR_CHUNKS)]
    hsl = [slice(h * hd, (h + 1) * hd) for h in range(heads)]
    hs = [bias_ref[...] + sum(_bdot(proj[2 * a][r, :], proj[2 * a + 1][...]) for a in range(n_proj)) for r in rows]
    xs = [_layer_norm(alpha * xres_ref[r, :] + h, g0_ref[...], b0_ref[...]) for r, h in zip(rows, hs)]
    qs = [_bdot(x, wq_ref[...]) * (hd ** -0.5) for x in xs]
    k = k_ref[...].astype(BF16)
    v = v_ref[...].astype(BF16)
    ss = [[_bdot_nt(q[:, s], k[:, s]) for s in hsl] for q in qs]
    es = [[jnp.exp(s - jnp.max(s, axis=-1, keepdims=True)) for s in sc] for sc in ss]
    ps = [[e / jnp.sum(e, axis=-1, keepdims=True) for e in ec] for ec in es]
    os = [jnp.concatenate([_bdot(a, v[:, s]) for a, s in zip(pc, hsl)], axis=1) for pc in ps]
    ys = [_layer_norm(alpha * x + _bdot(o, wo_ref[...]), g1_ref[...], b1_ref[...]) for x, o in zip(xs, os)]
    for r, y in zip(rows, ys):
        x_out_ref[r, :] = y
    routed = [_router_math(y, rw_ref[...], rb_ref[...], groups=groups, per_group=per_group) for y in ys]
    for r, (ids, wts) in zip(rows, routed):
        ids_ref[r, :] = ids
        wts_ref[r, :] = wts


def _proj_mem_router(proj, bias, x_res, ln0, mk, mv, layer, wq, wo, ln1, router_w, router_b, *, alpha, tq=512):
    b, t, d = x_res.shape
    m = mk.shape[2]
    nt = t // tq
    groups = MOE_GROUPS
    row = lambda a: a.reshape(1, d).astype(F32)
    const = lambda a: pl.BlockSpec(a.shape, lambda i, j: (0,) * a.ndim)
    xblk = pl.BlockSpec((None, tq, d), lambda i, j: (i, j, 0))
    rowblk = pl.BlockSpec((1, d), lambda i, j: (0, 0))
    proj_args, proj_specs = [], []
    for arr, spec, w in proj:
        wb = w.astype(BF16)
        proj_args += [arr, wb]
        proj_specs += [spec, const(wb)]
    if bias is None:
        bias = jnp.zeros((d,), F32)
    wqb, wob = wq.astype(BF16), wo.astype(BF16)
    lane_blk = pl.BlockSpec((tq, V7X_LANES), lambda i, j: (i * nt + j, 0))
    x_out, ids, wts = pl.pallas_call(
        functools.partial(_proj_mem_router_kernel, n_proj=len(proj), alpha=alpha, heads=MEM_HEADS,
                          groups=groups, per_group=MOE_EXPERTS_PER_GROUP),
        grid=(b, nt),
        in_specs=proj_specs + [rowblk, xblk, rowblk, rowblk, const(wqb),
                               pl.BlockSpec((None, None, m, d), lambda i, j: (layer, i, 0, 0)),
                               pl.BlockSpec((None, None, m, d), lambda i, j: (layer, i, 0, 0)),
                               const(wob), rowblk, rowblk, const(router_w), const(router_b)],
        out_specs=[xblk, lane_blk, lane_blk],
        out_shape=[jax.ShapeDtypeStruct((b, t, d), F32),
                   jax.ShapeDtypeStruct((b * t, V7X_LANES), jnp.int32),
                   jax.ShapeDtypeStruct((b * t, V7X_LANES), F32)],
        compiler_params=_cparams("parallel", "parallel"),
        name="proj_mem_router",
    )(*proj_args, row(bias), x_res, row(ln0[0]), row(ln0[1]), wqb, mk, mv, wob, row(ln1[0]), row(ln1[1]),
      router_w, router_b)
    return x_out, ids, wts


def _mem_attn_step_kernel(q_ref, k_ref, v_ref, o_ref, *, heads, bb):
    i = pl.program_id(0)
    hd = q_ref.shape[-1]
    for r in range(bb):
        q = q_ref[i * bb + r] * (hd ** -0.5)
        s = jnp.sum(k_ref[r] * q[None], axis=-1, keepdims=True)
        s = s - jnp.max(s, axis=0, keepdims=True)
        e = jnp.exp(s)
        o_ref[i * bb + r] = jnp.sum(e * v_ref[r], axis=0) / jnp.sum(e, axis=0)


def _mem_attn_step(q, mk_all, mv_all, layer, *, bb=8):
    b, heads, hd = q.shape
    m = mk_all.shape[2]
    bb = _row_tile(b, bb)
    return pl.pallas_call(
        functools.partial(_mem_attn_step_kernel, heads=heads, bb=bb),
        grid=(b // bb,),
        in_specs=[pl.BlockSpec((b, heads, hd), lambda i: (0, 0, 0)),
                  pl.BlockSpec((None, bb, m, heads, hd), lambda i: (layer, i, 0, 0, 0)),
                  pl.BlockSpec((None, bb, m, heads, hd), lambda i: (layer, i, 0, 0, 0))],
        out_specs=pl.BlockSpec((b, heads, hd), lambda i: (0, 0, 0)),
        out_shape=jax.ShapeDtypeStruct((b, heads, hd), F32),
        compiler_params=_cparams("arbitrary"),
        name="mem_attn_step",
    )(q, mk_all, mv_all)


def _mem_layer(x, mk_all, mv_all, layer, wq, wo, ln_g, ln_b, *, alpha):
    b, t, d = x.shape
    assert t == 1
    heads, hd = mk_all.shape[-2:]
    x2 = x.reshape(b, d)
    q = _linear(x2, wq).reshape(b, heads, hd)
    o = _mem_attn_step(q, mk_all, mv_all, layer).reshape(b, d)
    return _linear_postnorm(o, wo, None, x2, ln_g, ln_b, alpha=alpha).reshape(b, t, d)


_NEG = -1e30


def _router_kernel(x_ref, w_ref, b_ref, ids_ref, wts_ref, *, groups, per_group):
    ids_ref[...], wts_ref[...] = _router_math(x_ref[...], w_ref[...], b_ref[...], groups=groups,
                                              per_group=per_group)


def _router_math(x, w, bias, *, groups, per_group):
    xh = x.astype(BF16)
    xl = (x - xh.astype(F32)).astype(BF16)
    wh = w.astype(BF16)
    wl = (w - wh.astype(F32)).astype(BF16)
    logits = (jnp.dot(xh, wh, preferred_element_type=F32) + jnp.dot(xh, wl, preferred_element_type=F32)
              + jnp.dot(xl, wh, preferred_element_type=F32)) + bias
    lane = lax.broadcasted_iota(jnp.int32, logits.shape, 1)
    n_exp = groups * per_group
    is_g = lane < groups
    gl = jnp.where(is_g, logits, _NEG)
    gmax = jnp.max(gl, axis=-1, keepdims=True)
    gsum = jnp.sum(jnp.where(is_g, jnp.exp(gl - gmax), 0.0), axis=-1, keepdims=True)
    gw = 1.0 / gsum
    gi = jnp.min(jnp.where(gl == gmax, lane, V7X_LANES), axis=-1, keepdims=True)
    lane_grp = (lane - groups) // per_group
    sel = (lane >= groups) & (lane < groups + n_exp) & (lane_grp == gi)
    el = jnp.where(sel, logits, _NEG)
    emax = jnp.max(el, axis=-1, keepdims=True)
    ee = jnp.where(sel, jnp.exp(el - emax), 0.0)
    ep = jnp.where(sel, ee / jnp.sum(ee, axis=-1, keepdims=True), -1.0)
    p1 = jnp.max(ep, axis=-1, keepdims=True)
    i1 = jnp.min(jnp.where(ep == p1, lane, V7X_LANES), axis=-1, keepdims=True)
    ep2 = jnp.where(lane == i1, -1.0, ep)
    p2 = jnp.max(ep2, axis=-1, keepdims=True)
    i2 = jnp.min(jnp.where(ep2 == p2, lane, V7X_LANES), axis=-1, keepdims=True)
    denom = p1 + p2
    ids = jnp.where(lane == 0, i1 - groups, jnp.where(lane == 1, i2 - groups, 0))
    wts = jnp.where(lane == 0, gw * p1 / denom, jnp.where(lane == 1, gw * p2 / denom, 0.0))
    return ids, wts


def _router_params(p, layer):
    w_group, w_expert = p['moe_w_group'][layer], p['moe_w_expert'][layer]
    pad = V7X_LANES - w_group.shape[1] - w_expert.shape[1]
    w = jnp.pad(jnp.concatenate([w_group, w_expert], axis=1).astype(F32), ((0, 0), (0, pad)))
    b = jnp.pad(jnp.concatenate([p['moe_b_group'][layer], p['moe_b_expert'][layer]]).astype(F32), (0, pad))
    return w, b.reshape(1, V7X_LANES)


def _router(x, w, b, *, tm=512):
    n, d = x.shape
    tm = _row_tile(n, tm)
    return pl.pallas_call(
        functools.partial(_router_kernel, groups=MOE_GROUPS, per_group=MOE_EXPERTS_PER_GROUP),
        grid=(n // tm,),
        in_specs=[pl.BlockSpec((tm, d), lambda i: (i, 0)),
                  pl.BlockSpec((d, V7X_LANES), lambda i: (0, 0)),
                  pl.BlockSpec((1, V7X_LANES), lambda i: (0, 0))],
        out_specs=[pl.BlockSpec((tm, V7X_LANES), lambda i: (i, 0)),
                   pl.BlockSpec((tm, V7X_LANES), lambda i: (i, 0))],
        out_shape=[jax.ShapeDtypeStruct((n, V7X_LANES), jnp.int32),
                   jax.ShapeDtypeStruct((n, V7X_LANES), F32)],
        compiler_params=_cparams("parallel"),
        name="moe_router",
    )(x, w, b)


def _expert_ffn_kernel(te_ref, nv_ref, x_ref, wg_ref, wu_ref, wd_ref, o_ref):
    i = pl.program_id(0)

    @pl.when(i < nv_ref[0])
    def _():
        x = x_ref[...]
        hid = _silu(_bdot(x, wg_ref[...])) * _bdot(x, wu_ref[...])
        o_ref[...] = _bdot(hid, wd_ref[...])

    @pl.when(i >= nv_ref[0])
    def _():
        o_ref[...] = jnp.zeros_like(o_ref)


def _expert_ffn(x_sorted, tile_expert, n_valid, w_gate, w_up, w_down, layer, *, tm):
    r, d = x_sorted.shape
    f = w_gate.shape[-1]
    return pl.pallas_call(
        _expert_ffn_kernel,
        grid_spec=pltpu.PrefetchScalarGridSpec(
            num_scalar_prefetch=2,
            grid=(r // tm,),
            in_specs=[pl.BlockSpec((tm, d), lambda i, te, nv: (jnp.minimum(i, nv[0] - 1), 0)),
                      pl.BlockSpec((None, None, d, f), lambda i, te, nv: (layer, te[i], 0, 0)),
                      pl.BlockSpec((None, None, d, f), lambda i, te, nv: (layer, te[i], 0, 0)),
                      pl.BlockSpec((None, None, f, d), lambda i, te, nv: (layer, te[i], 0, 0))],
            out_specs=pl.BlockSpec((tm, d), lambda i, te, nv: (i, 0)),
        ),
        out_shape=jax.ShapeDtypeStruct((r, d), F32),
        compiler_params=_cparams("arbitrary"),
        name="moe_expert_ffn",
    )(tile_expert, n_valid, x_sorted, w_gate, w_up, w_down)


def _rank_kernel(ids_ref, cnt_in_ref, rank_ref, cnt_ref, carry_ref):
    i = pl.program_id(0)
    tm = ids_ref.shape[0]

    @pl.when(i == 0)
    def _():
        carry_ref[...] = cnt_in_ref[...].astype(F32)

    ids = ids_ref[...]
    lane = lax.broadcasted_iota(jnp.int32, ids.shape, 1)
    oh0 = jnp.where(lane == ids[:, 0:1], 1.0, 0.0)
    oh1 = jnp.where(lane == ids[:, 1:2], 1.0, 0.0)
    both = oh0 + oh1
    ii = lax.broadcasted_iota(jnp.int32, (tm, tm), 0)
    jj = lax.broadcasted_iota(jnp.int32, (tm, tm), 1)
    earlier = jnp.where(ii > jj, 1.0, 0.0).astype(BF16)
    prefix = jnp.dot(earlier, both.astype(BF16), preferred_element_type=F32) + carry_ref[...]
    r0 = jnp.sum(oh0 * prefix, axis=-1, keepdims=True)
    r1 = jnp.sum(oh1 * prefix, axis=-1, keepdims=True)
    rank_ref[...] = jnp.where(lane == 0, r0, jnp.where(lane == 1, r1, 0.0)).astype(jnp.int32)
    total = carry_ref[...] + jnp.sum(both, axis=0, keepdims=True)
    carry_ref[...] = total
    cnt_ref[...] = total.astype(jnp.int32)


def _expert_ranks(ids, counts_in, *, tm=1024):
    n = ids.shape[0]
    tm = _row_tile(n, tm)
    return pl.pallas_call(
        _rank_kernel,
        grid=(n // tm,),
        in_specs=[pl.BlockSpec((tm, V7X_LANES), lambda i: (i, 0)),
                  pl.BlockSpec((1, V7X_LANES), lambda i: (0, 0))],
        out_specs=[pl.BlockSpec((tm, V7X_LANES), lambda i: (i, 0)),
                   pl.BlockSpec((1, V7X_LANES), lambda i: (0, 0))],
        out_shape=[jax.ShapeDtypeStruct((n, V7X_LANES), jnp.int32),
                   jax.ShapeDtypeStruct((1, V7X_LANES), jnp.int32)],
        scratch_shapes=[pltpu.VMEM((1, V7X_LANES), F32)],
        compiler_params=_cparams("arbitrary"),
        name="moe_rank",
    )(ids, counts_in)


def _for_each_row(tm, fn):
    def body(i, carry):
        base = pl.multiple_of(i * V7X_SUBLANES, V7X_SUBLANES)
        for sub in range(V7X_SUBLANES):
            fn(base, sub)
        return carry

    lax.fori_loop(0, tm // V7X_SUBLANES, body, 0)


def _dispatch_kernel(fill_ref, slot_ref, x_ref, *rest, tile, fill_tiles):
    tm = x_ref.shape[0]
    if fill_tiles:
        xs_ref, sem, zbuf, zsem = rest

        @pl.when(pl.program_id(0) == 0)
        def _():
            zbuf[...] = jnp.zeros_like(zbuf)
            n_exp = fill_ref.shape[1]
            n_tiles = xs_ref.shape[0] // tile
            total_end = fill_ref[1, n_exp - 1]
            starts, conds = [], []
            for e in range(n_exp):
                for j in range(fill_tiles):
                    starts.append(fill_ref[0, e] + j * tile)
                    conds.append(starts[-1] < fill_ref[1, e])
            for t in range(n_tiles):
                starts.append(t * tile)
                conds.append(t * tile >= total_end)

            def fill(start):
                return pltpu.make_async_copy(zbuf, xs_ref.at[pl.ds(pl.multiple_of(start, tile), tile), :], zsem)

            for start, c in zip(starts, conds):
                pl.when(c)(lambda start=start: fill(start).start())
            for start, c in zip(starts, conds):
                pl.when(c)(lambda start=start: fill(start).wait())
    else:
        _, xs_ref, sem = rest

    def issue(base, sub):
        for k in range(MOE_TOP_K):
            s = slot_ref[0, 0, MOE_TOP_K * (base + sub) + k]
            pltpu.make_async_copy(x_ref.at[pl.ds(base + sub, 1), :], xs_ref.at[pl.ds(s, 1), :],
                                  sem).start(priority=k)

    _for_each_row(tm, issue)
    for k in range(MOE_TOP_K):
        pltpu.make_async_copy(x_ref, xs_ref.at[pl.ds(0, tm), :], sem).wait()


def _dispatch(x, slot, fill, *, rows, tile, fill_tiles, xs=None, tm):
    n, d = x.shape
    slot3 = slot.reshape(n // tm, 1, MOE_TOP_K * tm)
    in_specs = [pl.BlockSpec(memory_space=pltpu.SMEM),
                pl.BlockSpec((1, 1, MOE_TOP_K * tm), lambda i: (i, 0, 0), memory_space=pltpu.SMEM),
                pl.BlockSpec((tm, d), lambda i: (i, 0))]
    scratch = [pltpu.SemaphoreType.DMA(())]
    fresh = xs is None
    if fresh:
        args, aliases = (fill, slot3, x), {}
        scratch += [pltpu.VMEM((tile, d), F32), pltpu.SemaphoreType.DMA(())]
    else:
        args, aliases = (fill, slot3, x, xs), {3: 0}
        in_specs.append(pl.BlockSpec(memory_space=pl.ANY))
    return pl.pallas_call(
        functools.partial(_dispatch_kernel, tile=tile, fill_tiles=fill_tiles if fresh else 0),
        grid=(n // tm,),
        in_specs=in_specs,
        out_specs=pl.BlockSpec(memory_space=pl.ANY),
        out_shape=jax.ShapeDtypeStruct((rows, d), F32),
        scratch_shapes=scratch,
        input_output_aliases=aliases,
        compiler_params=_cparams("arbitrary"),
        name="moe_dispatch",
    )(*args)


def _combine_kernel(slot_ref, slot_next_ref, wts_ref, x_ref, g_ref, b_ref, y_hbm, o_ref, ybuf, sem, *, alpha):
    i = pl.program_id(0)
    n = pl.num_programs(0)
    tm = x_ref.shape[0]
    cur = i % 2

    def issue(idx_ref, buf):
        def row(base, sub):
            for k in range(MOE_TOP_K):
                s = idx_ref[0, 0, MOE_TOP_K * (base + sub) + k]
                pltpu.make_async_copy(y_hbm.at[pl.ds(s, 1), :], ybuf.at[buf, k, pl.ds(base + sub, 1), :],
                                      sem.at[buf]).start(priority=k)

        _for_each_row(tm, row)

    @pl.when(i == 0)
    def _():
        issue(slot_ref, 0)

    @pl.when(i + 1 < n)
    def _():
        issue(slot_next_ref, 1 - cur)

    for k in range(MOE_TOP_K):
        pltpu.make_async_copy(y_hbm.at[pl.ds(0, tm), :], ybuf.at[cur, k], sem.at[cur]).wait()
    wts = wts_ref[...]
    y = wts[:, 0:1] * ybuf[cur, 0] + wts[:, 1:2] * ybuf[cur, 1]
    o_ref[...] = _layer_norm(alpha * x_ref[...] + y, g_ref[...], b_ref[...])


def _combine_postnorm(y_sorted, slot, wts, x, g, b, *, alpha, tm):
    n, d = x.shape
    nt = n // tm
    slot3 = slot.reshape(nt, 1, MOE_TOP_K * tm)
    row = lambda a: a.reshape(1, d).astype(F32)
    smem = lambda f: pl.BlockSpec((1, 1, MOE_TOP_K * tm), f, memory_space=pltpu.SMEM)
    return pl.pallas_call(
        functools.partial(_combine_kernel, alpha=alpha),
        grid=(nt,),
        in_specs=[smem(lambda i: (i, 0, 0)),
                  smem(lambda i: (jnp.minimum(i + 1, nt - 1), 0, 0)),
                  pl.BlockSpec((tm, V7X_LANES), lambda i: (i, 0)),
                  pl.BlockSpec((tm, d), lambda i: (i, 0)),
                  pl.BlockSpec((1, d), lambda i: (0, 0)),
                  pl.BlockSpec((1, d), lambda i: (0, 0)),
                  pl.BlockSpec(memory_space=pl.ANY)],
        out_specs=pl.BlockSpec((tm, d), lambda i: (i, 0)),
        out_shape=jax.ShapeDtypeStruct((n, d), F32),
        scratch_shapes=[pltpu.VMEM((2, MOE_TOP_K, tm, d), F32), pltpu.SemaphoreType.DMA((2,))],
        compiler_params=_cparams("arbitrary"),
        name="moe_combine",
    )(slot3, slot3, wts, x, row(g), row(b), y_sorted)


def _moe_layer(xs, p, layer, ln_g, ln_b, *, alpha, routed=None):
    w_gate, w_up, w_down = p['moe_w_gate'], p['moe_w_up'], p['moe_w_down']
    n_exp = w_gate.shape[1]
    d = xs[0].shape[-1]
    x2s = [x.reshape(-1, d) for x in xs]
    n_total = sum(x2.shape[0] for x2 in x2s)
    tm = 512 if n_total >= 8192 else V7X_BF16_SUBLANES
    if routed is None:
        routed = [_router(x2, *_router_params(p, layer)) for x2 in x2s]
    counts = jnp.zeros((1, V7X_LANES), jnp.int32)
    ranks = []
    for gi, (ids, _) in enumerate(routed):
        rank, counts = _expert_ranks(ids, counts)
        ranks.append(rank)
        if gi == 0:
            counts_first = counts[0, :n_exp]
    counts = counts[0, :n_exp]
    padded = ((counts + tm - 1) // tm) * tm
    ends = jnp.cumsum(padded)
    starts = ends - padded
    n_tiles = (MOE_TOP_K * n_total + n_exp * (tm - 1)) // tm
    n_valid = (ends[-1] // tm).astype(jnp.int32)
    tile_start = jnp.minimum(jnp.arange(n_tiles, dtype=jnp.int32), n_valid - 1) * tm
    tile_expert = jnp.minimum(jnp.sum((ends[None, :] <= tile_start[:, None]).astype(jnp.int32), axis=1), n_exp - 1)
    fill = jnp.stack([starts + (counts_first // tm) * tm, ends]).astype(jnp.int32)
    later_rows = MOE_TOP_K * (n_total - x2s[0].shape[0])
    fill_tiles = 1 + (later_rows + tm - 1) // tm
    x_sorted = None
    slots = []
    for x2, (ids, _), rank in zip(x2s, routed, ranks):
        choice = ids[:, :MOE_TOP_K]
        onehot = choice[:, :, None] == jnp.arange(n_exp, dtype=jnp.int32)[None, None, :]
        slot = jnp.sum(jnp.where(onehot, starts[None, None, :], 0), axis=-1) + rank[:, :MOE_TOP_K]
        x_sorted = _dispatch(x2, slot, fill, rows=n_tiles * tm, tile=tm, fill_tiles=fill_tiles, xs=x_sorted,
                             tm=_row_tile(x2.shape[0], 512))
        slots.append(slot)
    y_sorted = _expert_ffn(x_sorted, tile_expert, n_valid.reshape(1), w_gate, w_up, w_down, layer, tm=tm)
    return [_combine_postnorm(y_sorted, slot, wts, x2, ln_g, ln_b, alpha=alpha,
                              tm=_row_tile(x2.shape[0], 512)).reshape(x.shape)
            for x, x2, (_, wts), slot in zip(xs, x2s, routed, slots)]


def _trunks(groups, p):
    depth = p['ln_g'].shape[0]
    alpha = (2.0 * depth) ** 0.25
    xs = [grp[0] for grp in groups]
    outs = [([], [], [], [], []) for _ in groups]
    for layer in range(depth):
        i = layer // 2
        g, bta = p['ln_g'][layer], p['ln_b'][layer]
        rw, rb = _router_params(p, layer)
        wq, wo = p['mem_wq'][layer], p['mem_wo'][layer]
        routed = []
        for gi, (_, dn_s, dn_conv, ssm_re, ssm_im, cconv, mem_k, mem_v) in enumerate(groups):
            out_s, out_conv, out_re, out_im, out_cc = outs[gi]
            x = xs[gi]
            b, t, d = x.shape
            if t == 1:
                if layer % 2 == 0:
                    x, s_new, conv_new, re_new, im_new = _ab_layer(x, dn_s[i], dn_conv[i], ssm_re[i], ssm_im[i],
                                                                   p, i, g[0], bta[0], alpha=alpha)
                else:
                    x, cc_new = _conv_layer(x, cconv[i], p, i, g[0], bta[0], alpha=alpha)
                x = _mem_layer(x, mem_k, mem_v, layer, wq, wo, g[1], bta[1], alpha=alpha)
                routed.append(_router(x.reshape(b * t, d), rw, rb))
            else:
                tq = _row_tile(t, 1024)
                if layer % 2 == 0:
                    o, y_tm, (s_new, conv_new, re_new, im_new) = _ab_mixers(x, dn_s[i], dn_conv[i], ssm_re[i],
                                                                            ssm_im[i], p, i)
                    n_o, n_y = o.shape[-1], y_tm.shape[1] // b
                    w_out = p['ab_w_out'][i]
                    proj = [(o, pl.BlockSpec((None, tq, n_o), lambda bi, j: (bi, j, 0)), w_out[:n_o]),
                            (y_tm, pl.BlockSpec((tq, n_y), lambda bi, j: (j, bi)), w_out[n_o:])]
                    bias = None
                else:
                    hc, cc_new = _conv_mixer(x, cconv[i], p, i)
                    proj = [(hc.reshape(b, t, d), pl.BlockSpec((None, tq, d), lambda bi, j: (bi, j, 0)),
                             p['cc_w_pw2'][i])]
                    bias = p['cc_b_pw2'][i]
                n_mem = mem_k.shape[2]
                x, ids, wts = _proj_mem_router(
                    proj, bias, x, (g[0], bta[0]), mem_k.reshape(depth, b, n_mem, d),
                    mem_v.reshape(depth, b, n_mem, d), layer, wq, wo, (g[1], bta[1]), rw, rb, alpha=alpha, tq=tq)
                routed.append((ids, wts))
            if layer % 2 == 0:
                out_s.append(s_new)
                out_conv.append(conv_new)
                out_re.append(re_new)
                out_im.append(im_new)
            else:
                out_cc.append(cc_new)
            xs[gi] = x
        xs = _moe_layer(xs, p, layer, g[2], bta[2], alpha=alpha, routed=routed)
    return [(x,) + tuple(jnp.stack(o) for o in out) for x, out in zip(xs, outs)]


def kernel(x_prompt, x_sample, state_dn_s, state_dn_conv, state_ssm_re, state_ssm_im, state_cconv,
           cache_mem_k, cache_mem_v, mem_prompt, ab_w_in, dn_conv_w, dn_a_log, dn_dt_bias, dn_norm_g,
           ssm_lambda_re, ssm_lambda_im, ssm_log_dt, ssm_b_re, ssm_b_im, ssm_c_re, ssm_c_im, ssm_d,
           ssm_w_glu, ssm_b_glu, ab_w_out, cc_w_pw1, cc_b_pw1, cc_w_dw, cc_b_dw, cc_ln_g, cc_ln_b,
           cc_w_pw2, cc_b_pw2, mem_wq, mem_wk, mem_wv, mem_wo, ln_g, ln_b, moe_w_group, moe_b_group,
           moe_w_expert, moe_b_expert, moe_w_gate, moe_w_up, moe_w_down):
    p = dict(ab_w_in=ab_w_in, dn_conv_w=dn_conv_w, dn_a_log=dn_a_log, dn_dt_bias=dn_dt_bias,
             dn_norm_g=dn_norm_g, ssm_lambda_re=ssm_lambda_re, ssm_lambda_im=ssm_lambda_im,
             ssm_log_dt=ssm_log_dt, ssm_b_re=ssm_b_re, ssm_b_im=ssm_b_im, ssm_c_re=ssm_c_re,
             ssm_c_im=ssm_c_im, ssm_d=ssm_d, ssm_w_glu=ssm_w_glu, ssm_b_glu=ssm_b_glu, ab_w_out=ab_w_out,
             cc_w_pw1=cc_w_pw1, cc_b_pw1=cc_b_pw1, cc_w_dw=cc_w_dw, cc_b_dw=cc_b_dw, cc_ln_g=cc_ln_g,
             cc_ln_b=cc_ln_b, cc_w_pw2=cc_w_pw2, cc_b_pw2=cc_b_pw2, mem_wq=mem_wq, mem_wo=mem_wo,
             ln_g=ln_g, ln_b=ln_b, moe_w_group=moe_w_group, moe_b_group=moe_b_group,
             moe_w_expert=moe_w_expert, moe_b_expert=moe_b_expert, moe_w_gate=moe_w_gate,
             moe_w_up=moe_w_up, moe_w_down=moe_w_down)
    depth = ln_g.shape[0]
    n_ab = state_dn_s.shape[0]
    n_cc = state_cconv.shape[0]
    bsz, _, d = x_prompt.shape
    n_mem = mem_prompt.shape[1]
    hd = d // MEM_HEADS
    z_dn_s = jnp.zeros((n_ab, bsz) + state_dn_s.shape[2:], F32)
    z_dn_conv = jnp.zeros((n_ab, bsz) + state_dn_conv.shape[2:], F32)
    z_ssm = jnp.zeros((n_ab, bsz) + state_ssm_re.shape[2:], F32)
    z_cconv = jnp.zeros((n_cc, bsz) + state_cconv.shape[2:], F32)
    mem2 = mem_prompt.reshape(bsz * n_mem, d)
    p_mem_k = jnp.stack([_linear(mem2, mem_wk[l]) for l in range(depth)]).reshape(depth, bsz, n_mem, MEM_HEADS, hd)
    p_mem_v = jnp.stack([_linear(mem2, mem_wv[l]) for l in range(depth)]).reshape(depth, bsz, n_mem, MEM_HEADS, hd)
    (y_prompt, p_dn_s, p_dn_conv, p_ssm_re, p_ssm_im, p_cconv), \
        (y_sample, s_dn_s, s_dn_conv, s_ssm_re, s_ssm_im, s_cconv) = _trunks(
            [(x_prompt, z_dn_s, z_dn_conv, z_ssm, z_ssm, z_cconv, p_mem_k, p_mem_v),
             (x_sample, state_dn_s, state_dn_conv, state_ssm_re, state_ssm_im, state_cconv, cache_mem_k,
              cache_mem_v)], p)
    return (y_prompt, y_sample, p_dn_s, p_dn_conv, p_ssm_re, p_ssm_im, p_cconv, p_mem_k, p_mem_v,
            s_dn_s, s_dn_conv, s_ssm_re, s_ssm_im, s_cconv)
```

```python
import functools
import math

import jax
import jax.numpy as jnp
from jax import lax
from jax.experimental import pallas as pl
from jax.experimental.pallas import tpu as pltpu

F32 = jnp.float32
BF16 = jnp.bfloat16

DN_HEADS = 4
DN_DK = 128
DN_DV = 128
DN_CONV = 4
DN_CHUNK = 64
MEM_HEADS = 4
MOE_GROUPS = 4
MOE_EXPERTS_PER_GROUP = 8
MOE_TOP_K = 2
LN_EPS = 1e-5
RMS_EPS = 1e-6

V7X_LANES = 128
V7X_SUBLANES = 8
V7X_BF16_SUBLANES = 16
V7X_VMEM_LIMIT_BYTES = 52 * 1024 * 1024


def _cparams(*sem):
    return pltpu.CompilerParams(dimension_semantics=sem, vmem_limit_bytes=V7X_VMEM_LIMIT_BYTES)


def _bdot(a, b):
    return jnp.dot(a.astype(BF16), b.astype(BF16), preferred_element_type=F32)


def _bdot_nt(a, b):
    return lax.dot_general(a.astype(BF16), b.astype(BF16), (((1,), (1,)), ((), ())),
                           preferred_element_type=F32)


def _split3(a):
    hi = a.astype(BF16)
    r1 = a - hi.astype(F32)
    mid = r1.astype(BF16)
    lo = (r1 - mid.astype(F32)).astype(BF16)
    return hi, mid, lo


def _sigmoid(x):
    return 1.0 / (1.0 + jnp.exp(-x))


def _silu(x):
    return x * _sigmoid(x)


def _softplus(x):
    return jnp.maximum(x, 0.0) + jnp.log(1.0 + jnp.exp(-jnp.abs(x)))


def _layer_norm(v, g, b):
    mu = jnp.mean(v, axis=-1, keepdims=True)
    d = v - mu
    var = jnp.mean(d * d, axis=-1, keepdims=True)
    return d * lax.rsqrt(var + LN_EPS) * g + b


def _row_tile(n, pref):
    t = min(n, pref)
    assert n % t == 0, (n, t)
    return t


def _linear_kernel(x_ref, w_ref, b_ref, o_ref, *, glu):
    y = _bdot(x_ref[...], w_ref[...]) + b_ref[...]
    if glu:
        n = y.shape[-1] // 2
        y = y[:, :n] * _sigmoid(y[:, n:])
    o_ref[...] = y.astype(o_ref.dtype)


def _linear(x, w, bias=None, *, glu=False, tm=1024, out_dtype=F32):
    m, k = x.shape
    n = w.shape[1]
    if bias is None:
        bias = jnp.zeros((n,), F32)
    tm = _row_tile(m, tm)
    n_out = n // 2 if glu else n
    return pl.pallas_call(
        functools.partial(_linear_kernel, glu=glu),
        grid=(m // tm,),
        in_specs=[pl.BlockSpec((tm, k), lambda i: (i, 0)),
                  pl.BlockSpec((k, n), lambda i: (0, 0)),
                  pl.BlockSpec((1, n), lambda i: (0, 0))],
        out_specs=pl.BlockSpec((tm, n_out), lambda i: (i, 0)),
        out_shape=jax.ShapeDtypeStruct((m, n_out), out_dtype),
        compiler_params=_cparams("parallel"),
        name="linear",
    )(x, w.astype(BF16), bias.reshape(1, n).astype(F32))


def _linear_postnorm_kernel(h_ref, w_ref, b_ref, x_ref, g_ref, beta_ref, o_ref, *, alpha):
    h = _bdot(h_ref[...], w_ref[...]) + b_ref[...]
    o_ref[...] = _layer_norm(alpha * x_ref[...] + h, g_ref[...], beta_ref[...])


def _linear_postnorm(h_in, w, bias, x_res, g, beta, *, alpha, tm=512):
    m, k = h_in.shape
    d = w.shape[1]
    if bias is None:
        bias = jnp.zeros((d,), F32)
    tm = _row_tile(m, tm)
    row = lambda a: a.reshape(1, d).astype(F32)
    return pl.pallas_call(
        functools.partial(_linear_postnorm_kernel, alpha=alpha),
        grid=(m // tm,),
        in_specs=[pl.BlockSpec((tm, k), lambda i: (i, 0)),
                  pl.BlockSpec((k, d), lambda i: (0, 0)),
                  pl.BlockSpec((1, d), lambda i: (0, 0)),
                  pl.BlockSpec((tm, d), lambda i: (i, 0)),
                  pl.BlockSpec((1, d), lambda i: (0, 0)),
                  pl.BlockSpec((1, d), lambda i: (0, 0))],
        out_specs=pl.BlockSpec((tm, d), lambda i: (i, 0)),
        out_shape=jax.ShapeDtypeStruct((m, d), F32),
        compiler_params=_cparams("parallel"),
        name="linear_postnorm",
    )(h_in, w.astype(BF16), row(bias), x_res, row(g), row(beta))


def _ab_in_kernel(x_ref, wqkvz_ref, wba_ref, wu_ref, qkv_ref, z_ref, ba_ref, u_ref, *, n_qkv):
    x = x_ref[...].astype(BF16)
    y = jnp.dot(x, wqkvz_ref[...], preferred_element_type=F32)
    qkv_ref[...] = y[:, :n_qkv]
    z_ref[...] = y[:, n_qkv:]
    ba_ref[...] = jnp.dot(x, wba_ref[...], preferred_element_type=F32)
    u_ref[...] = jnp.dot(x, wu_ref[...], preferred_element_type=F32)


def _ab_in_proj(x, w_in, *, tm=1024):
    bx, tx, d = x.shape
    n_key = DN_HEADS * DN_DK
    n_val = DN_HEADS * DN_DV
    n_qkv = 2 * n_key + n_val
    off_beta = n_qkv + n_val
    off_u = off_beta + 2 * DN_HEADS
    n_u = w_in.shape[1] - off_u
    w_qkvz = w_in[:, :off_beta].astype(BF16)
    w_ba = jnp.pad(w_in[:, off_beta:off_u], ((0, 0), (0, V7X_LANES - 2 * DN_HEADS))).astype(BF16)
    w_u = w_in[:, off_u:].astype(BF16)
    tm = _row_tile(tx, tm)
    full = lambda a: pl.BlockSpec(a.shape, lambda b, i: (0, 0))
    return pl.pallas_call(
        functools.partial(_ab_in_kernel, n_qkv=n_qkv),
        grid=(bx, tx // tm),
        in_specs=[pl.BlockSpec((None, tm, d), lambda b, i: (b, i, 0)),
                  full(w_qkvz), full(w_ba), full(w_u)],
        out_specs=[pl.BlockSpec((None, tm, n_qkv), lambda b, i: (b, i, 0)),
                   pl.BlockSpec((None, tm, n_val), lambda b, i: (b, i, 0)),
                   pl.BlockSpec((None, tm, V7X_LANES), lambda b, i: (b, i, 0)),
                   pl.BlockSpec((tm, n_u), lambda b, i: (i, b))],
        out_shape=[jax.ShapeDtypeStruct((bx, tx, n_qkv), F32),
                   jax.ShapeDtypeStruct((bx, tx, n_val), F32),
                   jax.ShapeDtypeStruct((bx, tx, V7X_LANES), F32),
                   jax.ShapeDtypeStruct((tx, bx * n_u), F32)],
        compiler_params=_cparams("parallel", "parallel"),
        name="ab_in_proj",
    )(x, w_qkvz, w_ba, w_u)


def _ab_out_kernel(o_ref, y_ref, wt_ref, wb_ref, x_ref, g_ref, beta_ref, out_ref, *, alpha):
    h = _bdot(o_ref[...], wt_ref[...]) + _bdot(y_ref[...], wb_ref[...])
    out_ref[...] = _layer_norm(alpha * x_ref[...] + h, g_ref[...], beta_ref[...])


def _ab_out_proj(o, y_tm, w_out, x, g, beta, *, alpha, tm=512):
    bx, tx, d = x.shape
    n_o = o.shape[-1]
    n_y = y_tm.shape[1] // bx
    tm = _row_tile(tx, tm)
    wt = w_out[:n_o].astype(BF16)
    wb = w_out[n_o:].astype(BF16)
    row = lambda a: a.reshape(1, d).astype(F32)
    full = lambda a: pl.BlockSpec(a.shape, lambda b, i: (0, 0))
    return pl.pallas_call(
        functools.partial(_ab_out_kernel, alpha=alpha),
        grid=(bx, tx // tm),
        in_specs=[pl.BlockSpec((None, tm, n_o), lambda b, i: (b, i, 0)),
                  pl.BlockSpec((tm, n_y), lambda b, i: (i, b)),
                  full(wt), full(wb),
                  pl.BlockSpec((None, tm, d), lambda b, i: (b, i, 0)),
                  pl.BlockSpec((1, d), lambda b, i: (0, 0)),
                  pl.BlockSpec((1, d), lambda b, i: (0, 0))],
        out_specs=pl.BlockSpec((None, tm, d), lambda b, i: (b, i, 0)),
        out_shape=jax.ShapeDtypeStruct((bx, tx, d), F32),
        compiler_params=_cparams("parallel", "parallel"),
        name="ab_out_proj",
    )(o, y_tm, wt, wb, x, row(g), row(beta))


_DN_HALO = V7X_SUBLANES


def _dn_seq_kernel(qkv_ref, ba_ref, z_ref, cbuf_ref, s0_ref, cw_ref, alog_ref, dtb_ref, ng_ref,
                   o_ref, s_out_ref, cbuf_out_ref, xp_ref, s_ref, *, c, bb):
    j = pl.program_id(1)
    nj = pl.num_programs(1)
    hist = DN_CONV - 1
    n_key = DN_HEADS * DN_DK
    units = [(bi, h) for bi in range(bb) for h in range(DN_HEADS)]
    each = lambda f: {u: f(u) for u in units}

    @pl.when(j == 0)
    def _():
        xp_ref[:, 0:_DN_HALO - hist, :] = jnp.zeros((bb, _DN_HALO - hist, xp_ref.shape[-1]), F32)
        xp_ref[:, _DN_HALO - hist:_DN_HALO, :] = cbuf_ref[...]
        s_ref[...] = s0_ref[...]

    ys, tails = [], []
    for bi in range(bb):
        xp_ref[bi, _DN_HALO:_DN_HALO + c, :] = qkv_ref[bi]
        y = xp_ref[bi, pl.ds(_DN_HALO - hist, c), :] * cw_ref[0:1, :]
        for tap in range(1, DN_CONV):
            y = y + xp_ref[bi, pl.ds(_DN_HALO - hist + tap, c), :] * cw_ref[tap:tap + 1, :]
        tail = xp_ref[bi, pl.ds(_DN_HALO + c - hist, hist), :]
        xp_ref[bi, _DN_HALO - hist:_DN_HALO, :] = tail
        tails.append(tail)
        ys.append(_silu(y))

    ii = lax.broadcasted_iota(jnp.int32, (c, c), 0)
    jj = lax.broadcasted_iota(jnp.int32, (c, c), 1)
    incl = ii >= jj
    strict = ii > jj
    tril = jnp.where(incl, 1.0, 0.0).astype(BF16)
    eye = jnp.where(ii == jj, 1.0, 0.0)
    n_double = int(math.log2(c))
    vsl = lambda h: slice(h * DN_DV, (h + 1) * DN_DV)

    def unit_inputs(u):
        bi, h = u
        lo, hi = h * DN_DK, (h + 1) * DN_DK
        y = ys[bi]
        qh = y[:, lo:hi]
        kh = y[:, n_key + lo:n_key + hi]
        vh = y[:, 2 * n_key + h * DN_DV:2 * n_key + (h + 1) * DN_DV]
        qh = qh * lax.rsqrt(jnp.sum(qh * qh, -1, keepdims=True) + RMS_EPS) * (DN_DK ** -0.5)
        kh = kh * lax.rsqrt(jnp.sum(kh * kh, -1, keepdims=True) + RMS_EPS)
        beta = _sigmoid(ba_ref[bi, :, h:h + 1])
        a_logit = ba_ref[bi, :, DN_HEADS + h:DN_HEADS + h + 1]
        g = -jnp.exp(alog_ref[0:1, lo:hi]) * _softplus(a_logit + dtb_ref[0:1, lo:hi])
        return qh, kh, vh, beta, g

    inp = each(unit_inputs)
    q = each(lambda u: inp[u][0])
    k = each(lambda u: inp[u][1])
    beta = each(lambda u: inp[u][3])
    parts = [p for u in units for p in _split3(inp[u][4])]
    gc_all = jnp.dot(tril, jnp.concatenate(parts, axis=1), preferred_element_type=F32)
    lanes = lambda i: slice(i * DN_DK, (i + 1) * DN_DK)
    gc = {u: gc_all[:, lanes(3 * i)] + gc_all[:, lanes(3 * i + 1)] + gc_all[:, lanes(3 * i + 2)]
          for i, u in enumerate(units)}

    def unit_decay(u):
        gc_row = jnp.transpose(gc[u])[0:1, :]
        diff = gc[u][:, :c] - gc_row
        return jnp.where(incl, jnp.exp(jnp.where(incl, diff, 0.0)), 0.0)

    decay = each(unit_decay)
    eg = each(lambda u: jnp.exp(gc[u]))
    kdec = each(lambda u: k[u] * jnp.exp(gc[u][c - 1:c, :] - gc[u]))
    kb = each(lambda u: k[u] * beta[u])
    akq = each(lambda u: _bdot_nt(jnp.concatenate([kb[u], q[u]], axis=0), k[u]))
    lmat = each(lambda u: jnp.where(strict, akq[u][:c] * decay[u], 0.0))
    qk = each(lambda u: akq[u][c:] * decay[u])
    tmat = each(lambda u: eye - lmat[u])
    lpow = each(lambda u: _bdot(lmat[u], lmat[u]))
    for step in range(1, n_double):
        if step < n_double - 1:
            prod = each(lambda u: _bdot(jnp.concatenate([lpow[u], tmat[u]], axis=0), lpow[u]))
            lpow = each(lambda u: prod[u][:c])
            tmat = each(lambda u: tmat[u] + prod[u][c:])
        else:
            prod = each(lambda u: _bdot(tmat[u], lpow[u]))
            tmat = each(lambda u: tmat[u] + prod[u])
    uw = each(lambda u: _bdot(tmat[u], jnp.concatenate([inp[u][2] * beta[u], kb[u] * eg[u]], axis=1)))
    s_old = each(lambda u: s_ref[u[0], u[1]])
    wq_s = each(lambda u: _bdot(jnp.concatenate([uw[u][:, DN_DV:], q[u] * eg[u]], axis=0), s_old[u]))
    v_new = each(lambda u: uw[u][:, :DN_DV] - wq_s[u][:c])
    o = each(lambda u: wq_s[u][c:] + _bdot(qk[u], v_new[u]))
    s_new = each(lambda u: s_old[u] * jnp.exp(gc[u][c - 1:c, :]) + _bdot(jnp.transpose(kdec[u]), v_new[u]))
    for u in units:
        bi, h = u
        s_ref[bi, h] = s_new[u]
        on = o[u] * lax.rsqrt(jnp.mean(o[u] * o[u], -1, keepdims=True) + RMS_EPS) * ng_ref[...]
        o_ref[bi, :, vsl(h)] = on * _silu(z_ref[bi, :, vsl(h)])

    @pl.when(j == nj - 1)
    def _():
        s_out_ref[...] = s_ref[...]
        for bi in range(bb):
            cbuf_out_ref[bi] = tails[bi]


def _dn_step_kernel(qkv_ref, ba_ref, z_ref, cbuf_ref, s0_ref, cw_ref, alog_ref, dtb_ref, ng_ref,
                    o_ref, s_out_ref, cbuf_out_ref, *, bb):
    hist = DN_CONV - 1
    n_key = DN_HEADS * DN_DK
    x = qkv_ref[...]
    y = x * cw_ref[hist:hist + 1, :]
    for tap in range(hist):
        y = y + cbuf_ref[:, tap, :] * cw_ref[tap:tap + 1, :]
    for tap in range(1, hist):
        cbuf_out_ref[:, tap - 1, :] = cbuf_ref[:, tap, :]
    cbuf_out_ref[:, hist - 1, :] = x
    y = _silu(y)
    ba = ba_ref[...]
    z = z_ref[...]
    row8 = lax.broadcasted_iota(jnp.int32, (V7X_SUBLANES, DN_DK), 0)
    row16 = lax.broadcasted_iota(jnp.int32, (V7X_BF16_SUBLANES, DN_DK), 0)
    heads = range(DN_HEADS)
    units = [(h, r) for h in heads for r in range(bb)]
    each = lambda f: {u: f(u) for u in units}
    one = lambda a, r: a[r:r + 1]

    def head_inputs(h):
        lo, hi = h * DN_DK, (h + 1) * DN_DK
        qh = y[:, lo:hi]
        kh = y[:, n_key + lo:n_key + hi]
        vh = y[:, 2 * n_key + h * DN_DV:2 * n_key + (h + 1) * DN_DV]
        qh = qh * lax.rsqrt(jnp.sum(qh * qh, -1, keepdims=True) + RMS_EPS) * (DN_DK ** -0.5)
        kh = kh * lax.rsqrt(jnp.sum(kh * kh, -1, keepdims=True) + RMS_EPS)
        beta = _sigmoid(ba[:, h:h + 1])
        a_logit = ba[:, DN_HEADS + h:DN_HEADS + h + 1]
        g = -jnp.exp(alog_ref[0:1, lo:hi]) * _softplus(a_logit + dtb_ref[0:1, lo:hi])
        eg = jnp.exp(g)
        return dict(k=kh, eg=eg, w=kh * beta * eg, qg=qh * eg, u=vh * beta, qk=jnp.sum(qh * kh, -1, keepdims=True))

    hd = [head_inputs(h) for h in heads]
    s_old = each(lambda u: s0_ref[u[1], u[0]])
    lhs = each(lambda u: jnp.where(row8 == 0, one(hd[u[0]]['w'], u[1]),
                                   jnp.where(row8 == 1, one(hd[u[0]]['qg'], u[1]), 0.0)))
    ws_qs = each(lambda u: _bdot(lhs[u], s_old[u]))
    v_new = each(lambda u: one(hd[u[0]]['u'], u[1]) - ws_qs[u][0:1])
    o_row = each(lambda u: ws_qs[u][1:2] + one(hd[u[0]]['qk'], u[1]) * v_new[u])
    k16 = each(lambda u: jnp.where(row16 == 0, one(hd[u[0]]['k'], u[1]), 0.0))
    v16 = each(lambda u: jnp.where(row16 == 0, v_new[u], 0.0))
    upd = each(lambda u: _bdot(jnp.transpose(k16[u]), v16[u]))
    for u in units:
        s_out_ref[u[1], u[0]] = s_old[u] * one(hd[u[0]]['eg'], u[1]) + upd[u]
    for h in heads:
        o = jnp.concatenate([o_row[(h, r)] for r in range(bb)], axis=0)
        o = o * lax.rsqrt(jnp.mean(o * o, -1, keepdims=True) + RMS_EPS) * ng_ref[...]
        o_ref[:, h * DN_DV:(h + 1) * DN_DV] = o * _silu(z[:, h * DN_DV:(h + 1) * DN_DV])


def _deltanet(qkv, ba, z, conv_buf, s0, conv_w, a_log, dt_bias, norm_g):
    b, t, n_qkv = qkv.shape
    n_val = DN_HEADS * DN_DV
    hist = DN_CONV - 1
    rep = lambda a: jnp.repeat(a.astype(F32), DN_DK).reshape(1, DN_HEADS * DN_DK)
    cw = conv_w.astype(F32)
    consts = (cw, rep(a_log), rep(dt_bias), norm_g.reshape(1, DN_DV).astype(F32))
    state_shape = (DN_HEADS, DN_DK, DN_DV)
    out_shape = [jax.ShapeDtypeStruct((b, t, n_val), F32),
                 jax.ShapeDtypeStruct((b,) + state_shape, F32),
                 jax.ShapeDtypeStruct((b, hist, n_qkv), F32)]
    if t == 1:
        bb = _row_tile(b, V7X_SUBLANES)
        const = lambda a: pl.BlockSpec(a.shape, lambda i: (0,) * a.ndim)
        o, s_new, cbuf_new = pl.pallas_call(
            functools.partial(_dn_step_kernel, bb=bb),
            grid=(b // bb,),
            in_specs=[pl.BlockSpec((bb, n_qkv), lambda i: (i, 0)),
                      pl.BlockSpec((bb, V7X_LANES), lambda i: (i, 0)),
                      pl.BlockSpec((bb, n_val), lambda i: (i, 0)),
                      pl.BlockSpec((bb, hist, n_qkv), lambda i: (i, 0, 0)),
                      pl.BlockSpec((bb,) + state_shape, lambda i: (i, 0, 0, 0))] + [const(a) for a in consts],
            out_specs=[pl.BlockSpec((bb, n_val), lambda i: (i, 0)),
                       pl.BlockSpec((bb,) + state_shape, lambda i: (i, 0, 0, 0)),
                       pl.BlockSpec((bb, hist, n_qkv), lambda i: (i, 0, 0))],
            out_shape=[jax.ShapeDtypeStruct((b, n_val), F32)] + out_shape[1:],
            compiler_params=_cparams("parallel"),
            name="deltanet_step",
        )(qkv.reshape(b, n_qkv), ba.reshape(b, V7X_LANES), z.reshape(b, n_val), conv_buf.astype(F32),
          s0.astype(F32), *consts)
        return o.reshape(b, 1, n_val), s_new, cbuf_new
    c = DN_CHUNK
    assert t % c == 0 and c >= hist
    bb = _row_tile(b, 4)
    const = lambda a: pl.BlockSpec(a.shape, lambda i, j: (0,) * a.ndim)
    return pl.pallas_call(
        functools.partial(_dn_seq_kernel, c=c, bb=bb),
        grid=(b // bb, t // c),
        in_specs=[pl.BlockSpec((bb, c, n_qkv), lambda i, j: (i, j, 0)),
                  pl.BlockSpec((bb, c, V7X_LANES), lambda i, j: (i, j, 0)),
                  pl.BlockSpec((bb, c, n_val), lambda i, j: (i, j, 0)),
                  pl.BlockSpec((bb, hist, n_qkv), lambda i, j: (i, 0, 0)),
                  pl.BlockSpec((bb,) + state_shape, lambda i, j: (i, 0, 0, 0))] + [const(a) for a in consts],
        out_specs=[pl.BlockSpec((bb, c, n_val), lambda i, j: (i, j, 0)),
                   pl.BlockSpec((bb,) + state_shape, lambda i, j: (i, 0, 0, 0)),
                   pl.BlockSpec((bb, hist, n_qkv), lambda i, j: (i, 0, 0))],
        out_shape=out_shape,
        scratch_shapes=[pltpu.VMEM((bb, _DN_HALO + c, n_qkv), F32),
                        pltpu.VMEM((bb,) + state_shape, F32)],
        compiler_params=_cparams("parallel", "arbitrary"),
        name="deltanet_seq",
    )(qkv, ba, z, conv_buf.astype(F32), s0.astype(F32), *consts)


def _s5_param_kernel(lre_ref, lim_ref, ldt_ref, lbre_ref, lbim_ref, fre_ref, fim_ref):
    lam_re = lre_ref[...]
    lam_im = lim_ref[...]
    dt = jnp.exp(ldt_ref[...])
    mag = jnp.exp(lam_re * dt)
    ang = lam_im * dt
    lb_re = mag * jnp.cos(ang)
    lb_im = mag * jnp.sin(ang)
    den = lam_re * lam_re + lam_im * lam_im
    lbre_ref[...] = lb_re
    lbim_ref[...] = lb_im
    fre_ref[...] = ((lb_re - 1.0) * lam_re + lb_im * lam_im) / den
    fim_ref[...] = (lb_im * lam_re - (lb_re - 1.0) * lam_im) / den


def _s5_discretize(lam_re, lam_im, log_dt):
    g, n = lam_re.shape
    ldt = jnp.broadcast_to(log_dt.astype(F32)[:, None], (g, n))
    shp = jax.ShapeDtypeStruct((g, n), F32)
    return pl.pallas_call(_s5_param_kernel, out_shape=[shp] * 4, name="s5_discretize")(
        lam_re.astype(F32), lam_im.astype(F32), ldt)


def _s5_kernel(u_ref, h0re_ref, h0im_ref, lbre_ref, lbim_ref, bre_ref, bim_ref, cre_ref, cim_ref,
               d_ref, wglu_ref, bglu_ref, y_ref, hre_out_ref, him_out_ref,
               sre_ref, sim_ref, cre_s, cim_s, *, tb, bb, lane_chunk):
    j = pl.program_id(1)
    nj = pl.num_programs(1)
    n_ch = u_ref.shape[-1]
    n_st = sre_ref.shape[-1]
    halves = bre_ref.shape[0]
    ch_h = n_ch // halves
    st_h = n_st // halves

    @pl.when(j == 0)
    def _():
        cre_s[...] = h0re_ref[...]
        cim_s[...] = h0im_ref[...]

    u = u_ref[...].reshape(tb * bb, n_ch)
    ub = u.astype(BF16)
    for hf in range(halves):
        uh = ub[:, hf * ch_h:(hf + 1) * ch_h]
        sre_ref[:, hf * st_h:(hf + 1) * st_h] = jnp.dot(uh, bre_ref[hf], preferred_element_type=F32)
        sim_ref[:, hf * st_h:(hf + 1) * st_h] = jnp.dot(uh, bim_ref[hf], preferred_element_type=F32)

    for c0 in range(0, n_st, lane_chunk):
        cs = slice(c0, c0 + lane_chunk)
        lr = jnp.broadcast_to(lbre_ref[0:1, cs], (bb, lane_chunk))
        li = jnp.broadcast_to(lbim_ref[0:1, cs], (bb, lane_chunk))

        def body(t, carry, cs=cs, lr=lr, li=li):
            hr, hi = carry
            r = pl.multiple_of(t * bb, bb)
            nr = lr * hr - li * hi + sre_ref[pl.ds(r, bb), cs]
            ni = lr * hi + li * hr + sim_ref[pl.ds(r, bb), cs]
            sre_ref[pl.ds(r, bb), cs] = nr
            sim_ref[pl.ds(r, bb), cs] = ni
            return nr, ni

        hr, hi = lax.fori_loop(0, tb, body, (cre_s[:, cs], cim_s[:, cs]))
        cre_s[:, cs] = hr
        cim_s[:, cs] = hi

    ys = []
    for hf in range(halves):
        hre = sre_ref[:, hf * st_h:(hf + 1) * st_h].astype(BF16)
        him = sim_ref[:, hf * st_h:(hf + 1) * st_h].astype(BF16)
        ys.append(jnp.dot(hre, cre_ref[hf], preferred_element_type=F32)
                  - jnp.dot(him, cim_ref[hf], preferred_element_type=F32))
    y = jnp.concatenate(ys, axis=1) + d_ref[...] * u
    y = jax.nn.gelu(y)
    y = y * _sigmoid(_bdot(y, wglu_ref[...]) + bglu_ref[...])
    y_ref[...] = y.reshape(tb, bb, n_ch)

    @pl.when(j == nj - 1)
    def _():
        hre_out_ref[...] = cre_s[...]
        him_out_ref[...] = cim_s[...]


def _s5(u_tm, h0_re, h0_im, lam_re, lam_im, log_dt, b_re, b_im, c_re, c_im, d_skip, w_glu, b_glu, *, halves=2):
    t, b, n_ch = u_tm.shape
    g, n, p = b_re.shape
    n_st = g * n
    lb_re, lb_im, f_re, f_im = _s5_discretize(lam_re, lam_im, log_dt)
    b_re = b_re.astype(F32)
    b_im = b_im.astype(F32)
    bb_re = f_re[..., None] * b_re - f_im[..., None] * b_im
    bb_im = f_re[..., None] * b_im + f_im[..., None] * b_re
    gh = g // halves

    def in_blocks(a):
        a = a.reshape(halves, gh, n, p)
        eye = jnp.eye(gh, dtype=F32)
        return jnp.einsum('hgnp,gk->hgpkn', a, eye).reshape(halves, gh * p, gh * n).astype(BF16)

    def out_blocks(a):
        a = a.astype(F32).reshape(halves, gh, p, n)
        eye = jnp.eye(gh, dtype=F32)
        return jnp.einsum('hgpn,gk->hgnkp', a, eye).reshape(halves, gh * n, gh * p).astype(BF16)

    bre_m, bim_m = in_blocks(bb_re), in_blocks(bb_im)
    cre_m, cim_m = out_blocks(c_re), out_blocks(c_im)
    bb = b if b <= 128 else 128
    assert b % bb == 0
    tb = _row_tile(t, max(1, 512 // bb))
    lane_chunk = max(V7X_LANES, min(n_st, 8192 // bb))
    const = lambda a: pl.BlockSpec(a.shape, lambda i, j: (0,) * a.ndim)
    row = lambda a, m: a.reshape(1, m).astype(F32)
    args = (u_tm, h0_re.reshape(b, n_st).astype(F32), h0_im.reshape(b, n_st).astype(F32),
            row(lb_re, n_st), row(lb_im, n_st), bre_m, bim_m, cre_m, cim_m,
            row(d_skip, n_ch), w_glu.astype(BF16), row(b_glu, n_ch))
    y, hre, him = pl.pallas_call(
        functools.partial(_s5_kernel, tb=tb, bb=bb, lane_chunk=lane_chunk),
        grid=(b // bb, t // tb),
        in_specs=[pl.BlockSpec((tb, bb, n_ch), lambda i, j: (j, i, 0)),
                  pl.BlockSpec((bb, n_st), lambda i, j: (i, 0)),
                  pl.BlockSpec((bb, n_st), lambda i, j: (i, 0))] + [const(a) for a in args[3:]],
        out_specs=[pl.BlockSpec((tb, bb, n_ch), lambda i, j: (j, i, 0)),
                   pl.BlockSpec((bb, n_st), lambda i, j: (i, 0)),
                   pl.BlockSpec((bb, n_st), lambda i, j: (i, 0))],
        out_shape=[jax.ShapeDtypeStruct((t, b, n_ch), F32),
                   jax.ShapeDtypeStruct((b, n_st), F32),
                   jax.ShapeDtypeStruct((b, n_st), F32)],
        scratch_shapes=[pltpu.VMEM((tb * bb, n_st), F32), pltpu.VMEM((tb * bb, n_st), F32),
                        pltpu.VMEM((bb, n_st), F32), pltpu.VMEM((bb, n_st), F32)],
        compiler_params=_cparams("parallel", "arbitrary"),
        name="s5",
    )(*args)
    return y, hre.reshape(b, g, n), him.reshape(b, g, n)


def _ab_mixers(x, dn_s, dn_conv, ssm_re, ssm_im, p, i):
    b, t, d = x.shape
    xr = x if t > 1 else x.reshape(1, b, d)
    bx, tx, _ = xr.shape
    qkv, z, ba, u_tm = _ab_in_proj(xr, p['ab_w_in'][i])
    n_ch = u_tm.shape[1] // bx
    shp = lambda a: a.reshape(b, t, a.shape[-1])
    o, s_new, conv_new = _deltanet(shp(qkv), shp(ba), shp(z), dn_conv, dn_s, p['dn_conv_w'][i],
                                   p['dn_a_log'][i], p['dn_dt_bias'][i], p['dn_norm_g'][i])
    y_tm, h_re, h_im = _s5(u_tm.reshape(t, b, n_ch), ssm_re, ssm_im, p['ssm_lambda_re'][i],
                           p['ssm_lambda_im'][i], p['ssm_log_dt'][i], p['ssm_b_re'][i], p['ssm_b_im'][i],
                           p['ssm_c_re'][i], p['ssm_c_im'][i], p['ssm_d'][i], p['ssm_w_glu'][i],
                           p['ssm_b_glu'][i])
    return o.reshape(bx, tx, -1), y_tm.reshape(tx, bx * n_ch), (s_new, conv_new, h_re, h_im)


def _ab_layer(x, dn_s, dn_conv, ssm_re, ssm_im, p, i, ln_g, ln_b, *, alpha):
    b, t, d = x.shape
    o, y_tm, states = _ab_mixers(x, dn_s, dn_conv, ssm_re, ssm_im, p, i)
    xr = x.reshape(o.shape[0], o.shape[1], d)
    x_new = _ab_out_proj(o, y_tm, p['ab_w_out'][i], xr, ln_g, ln_b, alpha=alpha)
    return (x_new.reshape(b, t, d),) + states


_CC_HALO = 32
_CC_ROW_TILES = 8


def _cconv_seq_kernel(h_ref, buf_ref, w_ref, bdw_ref, g_ref, b_ref, o_ref, xp_ref, xs_ref, acc_ref, *, tt, width):
    j = pl.program_id(1)
    hist = width - 1
    d = h_ref.shape[-1]
    rows = V7X_SUBLANES

    @pl.when(j == 0)
    def _():
        xp_ref[0:_CC_HALO - hist, :] = jnp.zeros((_CC_HALO - hist, d), F32)
        xp_ref[_CC_HALO - hist:_CC_HALO, :] = buf_ref[...]

    xp_ref[_CC_HALO:_CC_HALO + tt, :] = h_ref[...]
    n_shift = xs_ref.shape[1]
    for s in range(1, rows):
        xs_ref[s - 1] = xp_ref[pl.ds(s, n_shift), :]
    base = _CC_HALO - hist
    for c0 in range(0, d, V7X_LANES):
        cs = slice(c0, c0 + V7X_LANES)
        taps = [jnp.broadcast_to(w_ref[k:k + 1, cs], (rows, V7X_LANES)) for k in range(width)]

        def body(i, carry, taps=taps, cs=cs):
            r = pl.multiple_of(i * (rows * _CC_ROW_TILES), rows * _CC_ROW_TILES)
            accs = [None] * _CC_ROW_TILES
            for s in range(rows):
                ks = [k for k in range(width) if (base + k) % rows == s]
                if not ks:
                    continue
                n_rows = rows * (_CC_ROW_TILES + max((base + k) // rows for k in ks))
                big = xp_ref[pl.ds(r, n_rows), cs] if s == 0 else xs_ref[s - 1, pl.ds(r, n_rows), cs]
                for k in ks:
                    a = (base + k) // rows
                    for sub in range(_CC_ROW_TILES):
                        term = big[(sub + a) * rows:(sub + a + 1) * rows] * taps[k]
                        accs[sub] = term if accs[sub] is None else accs[sub] + term
            for sub in range(_CC_ROW_TILES):
                acc_ref[pl.ds(r + sub * rows, rows), cs] = accs[sub]
            return carry

        lax.fori_loop(0, tt // (rows * _CC_ROW_TILES), body, 0)
    xp_ref[0:_CC_HALO, :] = xp_ref[tt:tt + _CC_HALO, :]
    y = _layer_norm(acc_ref[...] + bdw_ref[...], g_ref[...], b_ref[...])
    o_ref[...] = _silu(y)


def _cconv_seq(h, buf, w_dw, b_dw, ln_g, ln_b, *, tt=512):
    b, t, d = h.shape
    width = w_dw.shape[0]
    tt = _row_tile(t, tt)
    assert tt >= _CC_HALO and width - 1 <= _CC_HALO
    row = lambda a: a.reshape(1, d).astype(F32)
    const = lambda a: pl.BlockSpec(a.shape, lambda i, j: (0, 0))
    args = (w_dw.astype(F32), row(b_dw), row(ln_g), row(ln_b))
    return pl.pallas_call(
        functools.partial(_cconv_seq_kernel, tt=tt, width=width),
        grid=(b, t // tt),
        in_specs=[pl.BlockSpec((None, tt, d), lambda i, j: (i, j, 0)),
                  pl.BlockSpec((None, width - 1, d), lambda i, j: (i, 0, 0))] + [const(a) for a in args],
        out_specs=pl.BlockSpec((None, tt, d), lambda i, j: (i, j, 0)),
        out_shape=jax.ShapeDtypeStruct((b, t, d), F32),
        scratch_shapes=[pltpu.VMEM((_CC_HALO + tt, d), F32),
                        pltpu.VMEM((V7X_SUBLANES - 1, _CC_HALO + tt - V7X_SUBLANES, d), F32),
                        pltpu.VMEM((tt, d), F32)],
        compiler_params=_cparams("parallel", "arbitrary"),
        name="cconv_seq",
    )(h, buf.astype(F32), *args)


def _cconv_step_kernel(h_ref, buf_ref, w_ref, bdw_ref, g_ref, b_ref, o_ref, *, width):
    hist = width - 1
    acc = jnp.sum(buf_ref[...] * w_ref[0:hist, :][None], axis=1) + h_ref[...] * w_ref[hist:width, :]
    o_ref[...] = _silu(_layer_norm(acc + bdw_ref[...], g_ref[...], b_ref[...]))


def _cconv_step(h, buf, w_dw, b_dw, ln_g, ln_b, *, bb=8):
    b, d = h.shape
    width = w_dw.shape[0]
    bb = _row_tile(b, bb)
    row = lambda a: a.reshape(1, d).astype(F32)
    const = lambda a: pl.BlockSpec(a.shape, lambda i: (0, 0))
    args = (w_dw.astype(F32), row(b_dw), row(ln_g), row(ln_b))
    return pl.pallas_call(
        functools.partial(_cconv_step_kernel, width=width),
        grid=(b // bb,),
        in_specs=[pl.BlockSpec((bb, d), lambda i: (i, 0)),
                  pl.BlockSpec((bb, width - 1, d), lambda i: (i, 0, 0))] + [const(a) for a in args],
        out_specs=pl.BlockSpec((bb, d), lambda i: (i, 0)),
        out_shape=jax.ShapeDtypeStruct((b, d), F32),
        compiler_params=_cparams("parallel"),
        name="cconv_step",
    )(h, buf.astype(F32), *args)


def _conv_mixer(x, buf, p, i):
    b, t, d = x.shape
    h = _linear(x.reshape(b * t, d), p['cc_w_pw1'][i], p['cc_b_pw1'][i], glu=True)
    args = (p['cc_w_dw'][i], p['cc_b_dw'][i], p['cc_ln_g'][i], p['cc_ln_b'][i])
    if t == 1:
        hc = _cconv_step(h, buf, *args)
    else:
        hc = _cconv_seq(h.reshape(b, t, d), buf, *args).reshape(b * t, d)
    new_buf = jnp.concatenate([buf.astype(F32), h.reshape(b, t, d)], axis=1)[:, t:]
    return hc, new_buf


def _conv_layer(x, buf, p, i, ln_g, ln_b, *, alpha):
    b, t, d = x.shape
    hc, new_buf = _conv_mixer(x, buf, p, i)
    x_new = _linear_postnorm(hc, p['cc_w_pw2'][i], p['cc_b_pw2'][i], x.reshape(b * t, d), ln_g, ln_b, alpha=alpha)
    return x_new.reshape(b, t, d), new_buf


_PMR_CHUNKS = 4


def _proj_mem_router_kernel(*refs, n_proj, alpha, heads, groups, per_group):
    proj = refs[:2 * n_proj]
    (bias_ref, xres_ref, g0_ref, b0_ref, wq_ref, k_ref, v_ref, wo_ref, g1_ref, b1_ref, rw_ref, rb_ref,
     x_out_ref, ids_ref, wts_ref) = refs[2 * n_proj:]
    tq, d = xres_ref.shape
    hd = d // heads
    chunk = tq // _PMR_CHUNKS
    rows = [pl.ds(c * chunk, chunk) for c in range(_PMR_CHUNKS)]
    hsl = [slice(h * hd, (h + 1) * hd) for h in range(heads)]
    hs = [bias_ref[...] + sum(_bdot(proj[2 * a][r, :], proj[2 * a + 1][...]) for a in range(n_proj)) for r in rows]
    xs = [_layer_norm(alpha * xres_ref[r, :] + h, g0_ref[...], b0_ref[...]) for r, h in zip(rows, hs)]
    qs = [_bdot(x, wq_ref[...]) * (hd ** -0.5) for x in xs]
    k = k_ref[...].astype(BF16)
    v = v_ref[...].astype(BF16)
    ss = [[_bdot_nt(q[:, s], k[:, s]) for s in hsl] for q in qs]
    es = [[jnp.exp(s - jnp.max(s, axis=-1, keepdims=True)) for s in sc] for sc in ss]
    ps = [[e / jnp.sum(e, axis=-1, keepdims=True) for e in ec] for ec in es]
    os = [jnp.concatenate([_bdot(a, v[:, s]) for a, s in zip(pc, hsl)], axis=1) for pc in ps]
    ys = [_layer_norm(alpha * x + _bdot(o, wo_ref[...]), g1_ref[...], b1_ref[...]) for x, o in zip(xs, os)]
    for r, y in zip(rows, ys):
        x_out_ref[r, :] = y
    routed = [_router_math(y, rw_ref[...], rb_ref[...], groups=groups, per_group=per_group) for y in ys]
    for r, (ids, wts) in zip(rows, routed):
        ids_ref[r, :] = ids
        wts_ref[r, :] = wts


def _proj_mem_router(proj, bias, x_res, ln0, mk, mv, layer, wq, wo, ln1, router_w, router_b, *, alpha, tq=512):
    b, t, d = x_res.shape
    m = mk.shape[2]
    nt = t // tq
    groups = MOE_GROUPS
    row = lambda a: a.reshape(1, d).astype(F32)
    const = lambda a: pl.BlockSpec(a.shape, lambda i, j: (0,) * a.ndim)
    xblk = pl.BlockSpec((None, tq, d), lambda i, j: (i, j, 0))
    rowblk = pl.BlockSpec((1, d), lambda i, j: (0, 0))
    proj_args, proj_specs = [], []
    for arr, spec, w in proj:
        wb = w.astype(BF16)
        proj_args += [arr, wb]
        proj_specs += [spec, const(wb)]
    if bias is None:
        bias = jnp.zeros((d,), F32)
    wqb, wob = wq.astype(BF16), wo.astype(BF16)
    lane_blk = pl.BlockSpec((tq, V7X_LANES), lambda i, j: (i * nt + j, 0))
    x_out, ids, wts = pl.pallas_call(
        functools.partial(_proj_mem_router_kernel, n_proj=len(proj), alpha=alpha, heads=MEM_HEADS,
                          groups=groups, per_group=MOE_EXPERTS_PER_GROUP),
        grid=(b, nt),
        in_specs=proj_specs + [rowblk, xblk, rowblk, rowblk, const(wqb),
                               pl.BlockSpec((None, None, m, d), lambda i, j: (layer, i, 0, 0)),
                               pl.BlockSpec((None, None, m, d), lambda i, j: (layer, i, 0, 0)),
                               const(wob), rowblk, rowblk, const(router_w), const(router_b)],
        out_specs=[xblk, lane_blk, lane_blk],
        out_shape=[jax.ShapeDtypeStruct((b, t, d), F32),
                   jax.ShapeDtypeStruct((b * t, V7X_LANES), jnp.int32),
                   jax.ShapeDtypeStruct((b * t, V7X_LANES), F32)],
        compiler_params=_cparams("parallel", "parallel"),
        name="proj_mem_router",
    )(*proj_args, row(bias), x_res, row(ln0[0]), row(ln0[1]), wqb, mk, mv, wob, row(ln1[0]), row(ln1[1]),
      router_w, router_b)
    return x_out, ids, wts


def _mem_attn_step_kernel(q_ref, k_ref, v_ref, o_ref, *, heads, bb):
    i = pl.program_id(0)
    hd = q_ref.shape[-1]
    for r in range(bb):
        q = q_ref[i * bb + r] * (hd ** -0.5)
        s = jnp.sum(k_ref[r] * q[None], axis=-1, keepdims=True)
        s = s - jnp.max(s, axis=0, keepdims=True)
        e = jnp.exp(s)
        o_ref[i * bb + r] = jnp.sum(e * v_ref[r], axis=0) / jnp.sum(e, axis=0)


def _mem_attn_step(q, mk_all, mv_all, layer, *, bb=8):
    b, heads, hd = q.shape
    m = mk_all.shape[2]
    bb = _row_tile(b, bb)
    return pl.pallas_call(
        functools.partial(_mem_attn_step_kernel, heads=heads, bb=bb),
        grid=(b // bb,),
        in_specs=[pl.BlockSpec((b, heads, hd), lambda i: (0, 0, 0)),
                  pl.BlockSpec((None, bb, m, heads, hd), lambda i: (layer, i, 0, 0, 0)),
                  pl.BlockSpec((None, bb, m, heads, hd), lambda i: (layer, i, 0, 0, 0))],
        out_specs=pl.BlockSpec((b, heads, hd), lambda i: (0, 0, 0)),
        out_shape=jax.ShapeDtypeStruct((b, heads, hd), F32),
        compiler_params=_cparams("arbitrary"),
        name="mem_attn_step",
    )(q, mk_all, mv_all)


def _mem_layer(x, mk_all, mv_all, layer, wq, wo, ln_g, ln_b, *, alpha):
    b, t, d = x.shape
    assert t == 1
    heads, hd = mk_all.shape[-2:]
    x2 = x.reshape(b, d)
    q = _linear(x2, wq).reshape(b, heads, hd)
    o = _mem_attn_step(q, mk_all, mv_all, layer).reshape(b, d)
    return _linear_postnorm(o, wo, None, x2, ln_g, ln_b, alpha=alpha).reshape(b, t, d)


_NEG = -1e30


def _router_kernel(x_ref, w_ref, b_ref, ids_ref, wts_ref, *, groups, per_group):
    ids_ref[...], wts_ref[...] = _router_math(x_ref[...], w_ref[...], b_ref[...], groups=groups,
                                              per_group=per_group)


def _router_math(x, w, bias, *, groups, per_group):
    xh = x.astype(BF16)
    xl = (x - xh.astype(F32)).astype(BF16)
    wh = w.astype(BF16)
    wl = (w - wh.astype(F32)).astype(BF16)
    logits = (jnp.dot(xh, wh, preferred_element_type=F32) + jnp.dot(xh, wl, preferred_element_type=F32)
              + jnp.dot(xl, wh, preferred_element_type=F32)) + bias
    lane = lax.broadcasted_iota(jnp.int32, logits.shape, 1)
    n_exp = groups * per_group
    is_g = lane < groups
    gl = jnp.where(is_g, logits, _NEG)
    gmax = jnp.max(gl, axis=-1, keepdims=True)
    gsum = jnp.sum(jnp.where(is_g, jnp.exp(gl - gmax), 0.0), axis=-1, keepdims=True)
    gw = 1.0 / gsum
    gi = jnp.min(jnp.where(gl == gmax, lane, V7X_LANES), axis=-1, keepdims=True)
    lane_grp = (lane - groups) // per_group
    sel = (lane >= groups) & (lane < groups + n_exp) & (lane_grp == gi)
    el = jnp.where(sel, logits, _NEG)
    emax = jnp.max(el, axis=-1, keepdims=True)
    ee = jnp.where(sel, jnp.exp(el - emax), 0.0)
    ep = jnp.where(sel, ee / jnp.sum(ee, axis=-1, keepdims=True), -1.0)
    p1 = jnp.max(ep, axis=-1, keepdims=True)
    i1 = jnp.min(jnp.where(ep == p1, lane, V7X_LANES), axis=-1, keepdims=True)
    ep2 = jnp.where(lane == i1, -1.0, ep)
    p2 = jnp.max(ep2, axis=-1, keepdims=True)
    i2 = jnp.min(jnp.where(ep2 == p2, lane, V7X_LANES), axis=-1, keepdims=True)
    denom = p1 + p2
    ids = jnp.where(lane == 0, i1 - groups, jnp.where(lane == 1, i2 - groups, 0))
    wts = jnp.where(lane == 0, gw * p1 / denom, jnp.where(lane == 1, gw * p2 / denom, 0.0))
    return ids, wts


def _router_params(p, layer):
    w_group, w_expert = p['moe_w_group'][layer], p['moe_w_expert'][layer]
    pad = V7X_LANES - w_group.shape[1] - w_expert.shape[1]
    w = jnp.pad(jnp.concatenate([w_group, w_expert], axis=1).astype(F32), ((0, 0), (0, pad)))
    b = jnp.pad(jnp.concatenate([p['moe_b_group'][layer], p['moe_b_expert'][layer]]).astype(F32), (0, pad))
    return w, b.reshape(1, V7X_LANES)


def _router(x, w, b, *, tm=512):
    n, d = x.shape
    tm = _row_tile(n, tm)
    return pl.pallas_call(
        functools.partial(_router_kernel, groups=MOE_GROUPS, per_group=MOE_EXPERTS_PER_GROUP),
        grid=(n // tm,),
        in_specs=[pl.BlockSpec((tm, d), lambda i: (i, 0)),
                  pl.BlockSpec((d, V7X_LANES), lambda i: (0, 0)),
                  pl.BlockSpec((1, V7X_LANES), lambda i: (0, 0))],
        out_specs=[pl.BlockSpec((tm, V7X_LANES), lambda i: (i, 0)),
                   pl.BlockSpec((tm, V7X_LANES), lambda i: (i, 0))],
        out_shape=[jax.ShapeDtypeStruct((n, V7X_LANES), jnp.int32),
                   jax.ShapeDtypeStruct((n, V7X_LANES), F32)],
        compiler_params=_cparams("parallel"),
        name="moe_router",
    )(x, w, b)


def _expert_ffn_kernel(te_ref, nv_ref, x_ref, wg_ref, wu_ref, wd_ref, o_ref, wg_s, wu_s, wd_s):
    i = pl.program_id(0)

    @pl.when(i < nv_ref[0])
    def _():
        @pl.when(jnp.logical_or(i == 0, te_ref[i] != te_ref[jnp.maximum(i - 1, 0)]))
        def _():
            wg_s[...] = wg_ref[...].astype(BF16)
            wu_s[...] = wu_ref[...].astype(BF16)
            wd_s[...] = wd_ref[...].astype(BF16)

        x = x_ref[...].astype(BF16)
        hid = _silu(_bdot(x, wg_s[...])) * _bdot(x, wu_s[...])
        o_ref[...] = _bdot(hid, wd_s[...])

    @pl.when(i >= nv_ref[0])
    def _():
        o_ref[...] = jnp.zeros_like(o_ref)


def _expert_ffn(x_sorted, tile_expert, n_valid, w_gate, w_up, w_down, layer, *, tm):
    r, d = x_sorted.shape
    f = w_gate.shape[-1]
    return pl.pallas_call(
        _expert_ffn_kernel,
        grid_spec=pltpu.PrefetchScalarGridSpec(
            num_scalar_prefetch=2,
            grid=(r // tm,),
            in_specs=[pl.BlockSpec((tm, d), lambda i, te, nv: (jnp.minimum(i, nv[0] - 1), 0)),
                      pl.BlockSpec((None, None, d, f), lambda i, te, nv: (layer, te[i], 0, 0)),
                      pl.BlockSpec((None, None, d, f), lambda i, te, nv: (layer, te[i], 0, 0)),
                      pl.BlockSpec((None, None, f, d), lambda i, te, nv: (layer, te[i], 0, 0))],
            out_specs=pl.BlockSpec((tm, d), lambda i, te, nv: (i, 0)),
            scratch_shapes=[pltpu.VMEM((d, f), BF16), pltpu.VMEM((d, f), BF16), pltpu.VMEM((f, d), BF16)],
        ),
        out_shape=jax.ShapeDtypeStruct((r, d), F32),
        compiler_params=_cparams("arbitrary"),
        name="moe_expert_ffn",
    )(tile_expert, n_valid, x_sorted, w_gate, w_up, w_down)


def _rank_kernel(ids_ref, cnt_in_ref, rank_ref, cnt_ref, carry_ref):
    i = pl.program_id(0)
    tm = ids_ref.shape[0]

    @pl.when(i == 0)
    def _():
        carry_ref[...] = cnt_in_ref[...].astype(F32)

    ids = ids_ref[...]
    lane = lax.broadcasted_iota(jnp.int32, ids.shape, 1)
    oh0 = jnp.where(lane == ids[:, 0:1], 1.0, 0.0)
    oh1 = jnp.where(lane == ids[:, 1:2], 1.0, 0.0)
    both = oh0 + oh1
    ii = lax.broadcasted_iota(jnp.int32, (tm, tm), 0)
    jj = lax.broadcasted_iota(jnp.int32, (tm, tm), 1)
    earlier = jnp.where(ii > jj, 1.0, 0.0).astype(BF16)
    prefix = jnp.dot(earlier, both.astype(BF16), preferred_element_type=F32) + carry_ref[...]
    r0 = jnp.sum(oh0 * prefix, axis=-1, keepdims=True)
    r1 = jnp.sum(oh1 * prefix, axis=-1, keepdims=True)
    rank_ref[...] = jnp.where(lane == 0, r0, jnp.where(lane == 1, r1, 0.0)).astype(jnp.int32)
    total = carry_ref[...] + jnp.sum(both, axis=0, keepdims=True)
    carry_ref[...] = total
    cnt_ref[...] = total.astype(jnp.int32)


def _expert_ranks(ids, counts_in, *, tm=1024):
    n = ids.shape[0]
    tm = _row_tile(n, tm)
    return pl.pallas_call(
        _rank_kernel,
        grid=(n // tm,),
        in_specs=[pl.BlockSpec((tm, V7X_LANES), lambda i: (i, 0)),
                  pl.BlockSpec((1, V7X_LANES), lambda i: (0, 0))],
        out_specs=[pl.BlockSpec((tm, V7X_LANES), lambda i: (i, 0)),
                   pl.BlockSpec((1, V7X_LANES), lambda i: (0, 0))],
        out_shape=[jax.ShapeDtypeStruct((n, V7X_LANES), jnp.int32),
                   jax.ShapeDtypeStruct((1, V7X_LANES), jnp.int32)],
        scratch_shapes=[pltpu.VMEM((1, V7X_LANES), F32)],
        compiler_params=_cparams("arbitrary"),
        name="moe_rank",
    )(ids, counts_in)


def _for_each_row(tm, fn):
    def body(i, carry):
        base = pl.multiple_of(i * V7X_SUBLANES, V7X_SUBLANES)
        for sub in range(V7X_SUBLANES):
            fn(base, sub)
        return carry

    lax.fori_loop(0, tm // V7X_SUBLANES, body, 0)


def _dispatch_kernel(fill_ref, slot_ref, x_ref, *rest, tile, fill_tiles):
    tm = x_ref.shape[0]
    if fill_tiles:
        xs_ref, sem, zbuf, zsem = rest

        @pl.when(pl.program_id(0) == 0)
        def _():
            zbuf[...] = jnp.zeros_like(zbuf)
            n_exp = fill_ref.shape[1]
            n_tiles = xs_ref.shape[0] // tile
            total_end = fill_ref[1, n_exp - 1]
            starts, conds = [], []
            for e in range(n_exp):
                for j in range(fill_tiles):
                    starts.append(fill_ref[0, e] + j * tile)
                    conds.append(starts[-1] < fill_ref[1, e])
            for t in range(n_tiles):
                starts.append(t * tile)
                conds.append(t * tile >= total_end)

            def fill(start):
                return pltpu.make_async_copy(zbuf, xs_ref.at[pl.ds(pl.multiple_of(start, tile), tile), :], zsem)

            for start, c in zip(starts, conds):
                pl.when(c)(lambda start=start: fill(start).start())
            for start, c in zip(starts, conds):
                pl.when(c)(lambda start=start: fill(start).wait())
    else:
        _, xs_ref, sem = rest

    def issue(base, sub):
        for k in range(MOE_TOP_K):
            s = slot_ref[0, 0, MOE_TOP_K * (base + sub) + k]
            pltpu.make_async_copy(x_ref.at[pl.ds(base + sub, 1), :], xs_ref.at[pl.ds(s, 1), :],
                                  sem).start(priority=k)

    _for_each_row(tm, issue)
    for k in range(MOE_TOP_K):
        pltpu.make_async_copy(x_ref, xs_ref.at[pl.ds(0, tm), :], sem).wait()


def _dispatch(x, slot, fill, *, rows, tile, fill_tiles, xs=None, tm):
    n, d = x.shape
    slot3 = slot.reshape(n // tm, 1, MOE_TOP_K * tm)
    in_specs = [pl.BlockSpec(memory_space=pltpu.SMEM),
                pl.BlockSpec((1, 1, MOE_TOP_K * tm), lambda i: (i, 0, 0), memory_space=pltpu.SMEM),
                pl.BlockSpec((tm, d), lambda i: (i, 0))]
    scratch = [pltpu.SemaphoreType.DMA(())]
    fresh = xs is None
    if fresh:
        args, aliases = (fill, slot3, x), {}
        scratch += [pltpu.VMEM((tile, d), F32), pltpu.SemaphoreType.DMA(())]
    else:
        args, aliases = (fill, slot3, x, xs), {3: 0}
        in_specs.append(pl.BlockSpec(memory_space=pl.ANY))
    return pl.pallas_call(
        functools.partial(_dispatch_kernel, tile=tile, fill_tiles=fill_tiles if fresh else 0),
        grid=(n // tm,),
        in_specs=in_specs,
        out_specs=pl.BlockSpec(memory_space=pl.ANY),
        out_shape=jax.ShapeDtypeStruct((rows, d), F32),
        scratch_shapes=scratch,
        input_output_aliases=aliases,
        compiler_params=_cparams("arbitrary"),
        name="moe_dispatch",
    )(*args)


def _combine_kernel(slot_ref, slot_next_ref, wts_ref, x_ref, g_ref, b_ref, y_hbm, o_ref, ybuf, sem, *, alpha):
    i = pl.program_id(0)
    n = pl.num_programs(0)
    tm = x_ref.shape[0]
    cur = i % 2

    def issue(idx_ref, buf):
        def row(base, sub):
            for k in range(MOE_TOP_K):
                s = idx_ref[0, 0, MOE_TOP_K * (base + sub) + k]
                pltpu.make_async_copy(y_hbm.at[pl.ds(s, 1), :], ybuf.at[buf, k, pl.ds(base + sub, 1), :],
                                      sem.at[buf]).start(priority=k)

        _for_each_row(tm, row)

    @pl.when(i == 0)
    def _():
        issue(slot_ref, 0)

    @pl.when(i + 1 < n)
    def _():
        issue(slot_next_ref, 1 - cur)

    for k in range(MOE_TOP_K):
        pltpu.make_async_copy(y_hbm.at[pl.ds(0, tm), :], ybuf.at[cur, k], sem.at[cur]).wait()
    wts = wts_ref[...]
    y = wts[:, 0:1] * ybuf[cur, 0] + wts[:, 1:2] * ybuf[cur, 1]
    o_ref[...] = _layer_norm(alpha * x_ref[...] + y, g_ref[...], b_ref[...])


def _combine_postnorm(y_sorted, slot, wts, x, g, b, *, alpha, tm):
    n, d = x.shape
    nt = n // tm
    slot3 = slot.reshape(nt, 1, MOE_TOP_K * tm)
    row = lambda a: a.reshape(1, d).astype(F32)
    smem = lambda f: pl.BlockSpec((1, 1, MOE_TOP_K * tm), f, memory_space=pltpu.SMEM)
    return pl.pallas_call(
        functools.partial(_combine_kernel, alpha=alpha),
        grid=(nt,),
        in_specs=[smem(lambda i: (i, 0, 0)),
                  smem(lambda i: (jnp.minimum(i + 1, nt - 1), 0, 0)),
                  pl.BlockSpec((tm, V7X_LANES), lambda i: (i, 0)),
                  pl.BlockSpec((tm, d), lambda i: (i, 0)),
                  pl.BlockSpec((1, d), lambda i: (0, 0)),
                  pl.BlockSpec((1, d), lambda i: (0, 0)),
                  pl.BlockSpec(memory_space=pl.ANY)],
        out_specs=pl.BlockSpec((tm, d), lambda i: (i, 0)),
        out_shape=jax.ShapeDtypeStruct((n, d), F32),
        scratch_shapes=[pltpu.VMEM((2, MOE_TOP_K, tm, d), F32), pltpu.SemaphoreType.DMA((2,))],
        compiler_params=_cparams("arbitrary"),
        name="moe_combine",
    )(slot3, slot3, wts, x, row(g), row(b), y_sorted)


def _moe_layer(xs, p, layer, ln_g, ln_b, *, alpha, routed=None):
    w_gate, w_up, w_down = p['moe_w_gate'], p['moe_w_up'], p['moe_w_down']
    n_exp = w_gate.shape[1]
    d = xs[0].shape[-1]
    x2s = [x.reshape(-1, d) for x in xs]
    n_total = sum(x2.shape[0] for x2 in x2s)
    tm = 512 if n_total >= 8192 else V7X_BF16_SUBLANES
    if routed is None:
        routed = [_router(x2, *_router_params(p, layer)) for x2 in x2s]
    counts = jnp.zeros((1, V7X_LANES), jnp.int32)
    ranks = []
    for gi, (ids, _) in enumerate(routed):
        rank, counts = _expert_ranks(ids, counts)
        ranks.append(rank)
        if gi == 0:
            counts_first = counts[0, :n_exp]
    counts = counts[0, :n_exp]
    padded = ((counts + tm - 1) // tm) * tm
    ends = jnp.cumsum(padded)
    starts = ends - padded
    n_tiles = (MOE_TOP_K * n_total + n_exp * (tm - 1)) // tm
    n_valid = (ends[-1] // tm).astype(jnp.int32)
    tile_start = jnp.minimum(jnp.arange(n_tiles, dtype=jnp.int32), n_valid - 1) * tm
    tile_expert = jnp.minimum(jnp.sum((ends[None, :] <= tile_start[:, None]).astype(jnp.int32), axis=1), n_exp - 1)
    fill = jnp.stack([starts + (counts_first // tm) * tm, ends]).astype(jnp.int32)
    later_rows = MOE_TOP_K * (n_total - x2s[0].shape[0])
    fill_tiles = 1 + (later_rows + tm - 1) // tm
    x_sorted = None
    slots = []
    for x2, (ids, _), rank in zip(x2s, routed, ranks):
        choice = ids[:, :MOE_TOP_K]
        onehot = choice[:, :, None] == jnp.arange(n_exp, dtype=jnp.int32)[None, None, :]
        slot = jnp.sum(jnp.where(onehot, starts[None, None, :], 0), axis=-1) + rank[:, :MOE_TOP_K]
        x_sorted = _dispatch(x2, slot, fill, rows=n_tiles * tm, tile=tm, fill_tiles=fill_tiles, xs=x_sorted,
                             tm=_row_tile(x2.shape[0], 512))
        slots.append(slot)
    y_sorted = _expert_ffn(x_sorted, tile_expert, n_valid.reshape(1), w_gate, w_up, w_down, layer, tm=tm)
    return [_combine_postnorm(y_sorted, slot, wts, x2, ln_g, ln_b, alpha=alpha,
                              tm=_row_tile(x2.shape[0], 512)).reshape(x.shape)
            for x, x2, (_, wts), slot in zip(xs, x2s, routed, slots)]


def _trunks(groups, p):
    depth = p['ln_g'].shape[0]
    alpha = (2.0 * depth) ** 0.25
    xs = [grp[0] for grp in groups]
    outs = [([], [], [], [], []) for _ in groups]
    for layer in range(depth):
        i = layer // 2
        g, bta = p['ln_g'][layer], p['ln_b'][layer]
        rw, rb = _router_params(p, layer)
        wq, wo = p['mem_wq'][layer], p['mem_wo'][layer]
        routed = []
        for gi, (_, dn_s, dn_conv, ssm_re, ssm_im, cconv, mem_k, mem_v) in enumerate(groups):
            out_s, out_conv, out_re, out_im, out_cc = outs[gi]
            x = xs[gi]
            b, t, d = x.shape
            if t == 1:
                if layer % 2 == 0:
                    x, s_new, conv_new, re_new, im_new = _ab_layer(x, dn_s[i], dn_conv[i], ssm_re[i], ssm_im[i],
                                                                   p, i, g[0], bta[0], alpha=alpha)
                else:
                    x, cc_new = _conv_layer(x, cconv[i], p, i, g[0], bta[0], alpha=alpha)
                x = _mem_layer(x, mem_k, mem_v, layer, wq, wo, g[1], bta[1], alpha=alpha)
                routed.append(_router(x.reshape(b * t, d), rw, rb))
            else:
                tq = _row_tile(t, 1024)
                if layer % 2 == 0:
                    o, y_tm, (s_new, conv_new, re_new, im_new) = _ab_mixers(x, dn_s[i], dn_conv[i], ssm_re[i],
                                                                            ssm_im[i], p, i)
                    n_o, n_y = o.shape[-1], y_tm.shape[1] // b
                    w_out = p['ab_w_out'][i]
                    proj = [(o, pl.BlockSpec((None, tq, n_o), lambda bi, j: (bi, j, 0)), w_out[:n_o]),
                            (y_tm, pl.BlockSpec((tq, n_y), lambda bi, j: (j, bi)), w_out[n_o:])]
                    bias = None
                else:
                    hc, cc_new = _conv_mixer(x, cconv[i], p, i)
                    proj = [(hc.reshape(b, t, d), pl.BlockSpec((None, tq, d), lambda bi, j: (bi, j, 0)),
                             p['cc_w_pw2'][i])]
                    bias = p['cc_b_pw2'][i]
                n_mem = mem_k.shape[2]
                x, ids, wts = _proj_mem_router(
                    proj, bias, x, (g[0], bta[0]), mem_k.reshape(depth, b, n_mem, d),
                    mem_v.reshape(depth, b, n_mem, d), layer, wq, wo, (g[1], bta[1]), rw, rb, alpha=alpha, tq=tq)
                routed.append((ids, wts))
            if layer % 2 == 0:
                out_s.append(s_new)
                out_conv.append(conv_new)
                out_re.append(re_new)
                out_im.append(im_new)
            else:
                out_cc.append(cc_new)
            xs[gi] = x
        xs = _moe_layer(xs, p, layer, g[2], bta[2], alpha=alpha, routed=routed)
    return [(x,) + tuple(jnp.stack(o) for o in out) for x, out in zip(xs, outs)]


def kernel(x_prompt, x_sample, state_dn_s, state_dn_conv, state_ssm_re, state_ssm_im, state_cconv,
           cache_mem_k, cache_mem_v, mem_prompt, ab_w_in, dn_conv_w, dn_a_log, dn_dt_bias, dn_norm_g,
           ssm_lambda_re, ssm_lambda_im, ssm_log_dt, ssm_b_re, ssm_b_im, ssm_c_re, ssm_c_im, ssm_d,
           ssm_w_glu, ssm_b_glu, ab_w_out, cc_w_pw1, cc_b_pw1, cc_w_dw, cc_b_dw, cc_ln_g, cc_ln_b,
           cc_w_pw2, cc_b_pw2, mem_wq, mem_wk, mem_wv, mem_wo, ln_g, ln_b, moe_w_group, moe_b_group,
           moe_w_expert, moe_b_expert, moe_w_gate, moe_w_up, moe_w_down):
    p = dict(ab_w_in=ab_w_in, dn_conv_w=dn_conv_w, dn_a_log=dn_a_log, dn_dt_bias=dn_dt_bias,
             dn_norm_g=dn_norm_g, ssm_lambda_re=ssm_lambda_re, ssm_lambda_im=ssm_lambda_im,
             ssm_log_dt=ssm_log_dt, ssm_b_re=ssm_b_re, ssm_b_im=ssm_b_im, ssm_c_re=ssm_c_re,
             ssm_c_im=ssm_c_im, ssm_d=ssm_d, ssm_w_glu=ssm_w_glu, ssm_b_glu=ssm_b_glu, ab_w_out=ab_w_out,
             cc_w_pw1=cc_w_pw1, cc_b_pw1=cc_b_pw1, cc_w_dw=cc_w_dw, cc_b_dw=cc_b_dw, cc_ln_g=cc_ln_g,
             cc_ln_b=cc_ln_b, cc_w_pw2=cc_w_pw2, cc_b_pw2=cc_b_pw2, mem_wq=mem_wq, mem_wo=mem_wo,
             ln_g=ln_g, ln_b=ln_b, moe_w_group=moe_w_group, moe_b_group=moe_b_group,
             moe_w_expert=moe_w_expert, moe_b_expert=moe_b_expert, moe_w_gate=moe_w_gate,
             moe_w_up=moe_w_up, moe_w_down=moe_w_down)
    depth = ln_g.shape[0]
    n_ab = state_dn_s.shape[0]
    n_cc = state_cconv.shape[0]
    bsz, _, d = x_prompt.shape
    n_mem = mem_prompt.shape[1]
    hd = d // MEM_HEADS
    z_dn_s = jnp.zeros((n_ab, bsz) + state_dn_s.shape[2:], F32)
    z_dn_conv = jnp.zeros((n_ab, bsz) + state_dn_conv.shape[2:], F32)
    z_ssm = jnp.zeros((n_ab, bsz) + state_ssm_re.shape[2:], F32)
    z_cconv = jnp.zeros((n_cc, bsz) + state_cconv.shape[2:], F32)
    mem2 = mem_prompt.reshape(bsz * n_mem, d)
    p_mem_k = jnp.stack([_linear(mem2, mem_wk[l]) for l in range(depth)]).reshape(depth, bsz, n_mem, MEM_HEADS, hd)
    p_mem_v = jnp.stack([_linear(mem2, mem_wv[l]) for l in range(depth)]).reshape(depth, bsz, n_mem, MEM_HEADS, hd)
    (y_prompt, p_dn_s, p_dn_conv, p_ssm_re, p_ssm_im, p_cconv), \
        (y_sample, s_dn_s, s_dn_conv, s_ssm_re, s_ssm_im, s_cconv) = _trunks(
            [(x_prompt, z_dn_s, z_dn_conv, z_ssm, z_ssm, z_cconv, p_mem_k, p_mem_v),
             (x_sample, state_dn_s, state_dn_conv, state_ssm_re, state_ssm_im, state_cconv, cache_mem_k,
              cache_mem_v)], p)
    return (y_prompt, y_sample, p_dn_s, p_dn_conv, p_ssm_re, p_ssm_im, p_cconv, p_mem_k, p_mem_v,
            s_dn_s, s_dn_conv, s_ssm_re, s_ssm_im, s_cconv)
```
